```python
import math
import jax, jax.numpy as jnp
from jax import lax
import numpy as np

D_MODEL = 2048
BATCH = 2
SEQ = 4096
DEPTH = 2
DEC_BATCH = 128
DEC_SEQ = 4
PAST_LEN = 8192
PAGE_SIZE = 128

N_HEADS = 32
N_KV_HEADS = 4
HEAD_DIM = 64
WINDOW = 128
ROPE_THETA = 10000.0
CHUNK = 128
GM_GROUPS = 16
GM_GROUP_DIM = 128
SSM_HEADS = 32
SSM_HEAD_DIM = 64
SSM_GROUPS = 4
SSM_STATE = 128
CONV_K = 4
SSD_CHUNK = 128
MEM_LEN = 256
MEM_HEADS = 4
MEM_HEAD_DIM = 128
N_EGROUPS = 4
EXPERTS_PER_GROUP = 8
N_EXPERTS = N_EGROUPS * EXPERTS_PER_GROUP
TOP_K = 2
EXPERT_FF = D_MODEL // 2
Q_W = N_HEADS * HEAD_DIM
KV_W = N_KV_HEADS * HEAD_DIM
GM_W = GM_GROUPS * GM_GROUP_DIM
SSM_INNER = SSM_HEADS * SSM_HEAD_DIM
SSM_BC = SSM_GROUPS * SSM_STATE
CONV_DIM = SSM_INNER + 2 * SSM_BC
MEM_W = MEM_HEADS * MEM_HEAD_DIM
SPLIT_SIZES = (Q_W, KV_W, KV_W, GM_W, GM_W, SSM_INNER, CONV_DIM, SSM_HEADS, 3 * D_MODEL)
IN_W = sum(SPLIT_SIZES)
DEEPNORM_ALPHA = (2 * DEPTH) ** 0.25
DEEPNORM_BETA = (8 * DEPTH) ** -0.25
LN_EPS = 1e-5

kernel_name = 'hybrid_swa_gmlp_ssd_hmoe_decode_step'

F32 = jnp.float32


def layer_norm(x, g, b):
    xf = x.astype(F32)
    mu = jnp.mean(xf, axis=-1, keepdims=True)
    var = jnp.mean(jnp.square(xf - mu), axis=-1, keepdims=True)
    return ((xf - mu) * lax.rsqrt(var + LN_EPS) * g.astype(F32) + b.astype(F32)).astype(x.dtype)


def rms_norm(xf, g):
    return xf * lax.rsqrt(jnp.mean(xf * xf, axis=-1, keepdims=True) + LN_EPS) * g.astype(F32)


def rope(x, pos):
    half = x.shape[-1] // 2
    inv = ROPE_THETA ** (-jnp.arange(half, dtype=F32) / half)
    ang = pos.astype(F32)[:, None] * inv[None, :]
    cos = jnp.cos(ang)[None, :, None, :]
    sin = jnp.sin(ang)[None, :, None, :]
    xf = x.astype(F32)
    x1, x2 = xf[..., :half], xf[..., half:]
    return jnp.concatenate([x1 * cos - x2 * sin, x2 * cos + x1 * sin], axis=-1).astype(x.dtype)


def sliding_window_attention(q, k, v, kbuf, vbuf, pos0, sinks):
    b, L = q.shape[:2]
    wb = kbuf.shape[1]
    new_k = jnp.concatenate([kbuf.astype(k.dtype), k], axis=1)[:, -wb:]
    new_v = jnp.concatenate([vbuf.astype(v.dtype), v], axis=1)[:, -wb:]
    front = WINDOW - wb
    kb = jnp.pad(kbuf.astype(k.dtype), ((0, 0), (front, 0), (0, 0), (0, 0)))
    vb = jnp.pad(vbuf.astype(v.dtype), ((0, 0), (front, 0), (0, 0), (0, 0)))
    qb = min(WINDOW, L)
    nb = -(-L // qb)
    pad = nb * qb - L
    padl = lambda a: jnp.pad(a, ((0, 0), (0, pad), (0, 0), (0, 0)))
    kf = jnp.concatenate([kb, padl(k)], axis=1)
    vf = jnp.concatenate([vb, padl(v)], axis=1)
    idx = (np.arange(nb) * qb)[:, None] + np.arange(WINDOW + qb)[None, :]
    kblk = kf[:, idx]
    vblk = vf[:, idx]
    qblk = padl(q).reshape(b, nb, qb, N_KV_HEADS, N_HEADS // N_KV_HEADS, HEAD_DIM)
    qpos = pos0 + np.arange(nb * qb).reshape(nb, qb)
    kpos = pos0 - WINDOW + idx
    rel = qpos[:, :, None] - kpos[:, None, :]
    valid = (rel >= 0) & (rel < WINDOW) & (kpos[:, None, :] >= 0)
    s = jnp.einsum('bnqgrd,bnkgd->bngrqk', qblk, kblk).astype(F32) * HEAD_DIM ** -0.5
    s = jnp.where(valid[None, :, None, None], s, -jnp.inf)
    sink = jnp.broadcast_to(sinks.astype(F32).reshape(N_KV_HEADS, -1)[None, None, :, :, None, None], s.shape[:-1] + (1,))
    p = jax.nn.softmax(jnp.concatenate([s, sink], axis=-1), axis=-1)[..., :-1]
    o = jnp.einsum('bngrqk,bnkgd->bnqgrd', p.astype(v.dtype), vblk).reshape(b, nb * qb, Q_W)[:, :L]
    return o, new_k, new_v


def chunk_spatial_mix(v, ws, bs):
    b, L, _ = v.shape
    nc = -(-L // CHUNK)
    pad = nc * CHUNK - L
    vr = jnp.pad(v, ((0, 0), (0, pad), (0, 0))).reshape(b, nc, CHUNK, GM_GROUPS, GM_GROUP_DIM)
    w = jnp.where(jnp.tril(jnp.ones((CHUNK, CHUNK), bool)), ws, 0).astype(v.dtype)
    s = jnp.einsum('gij,bnjgc->bnigc', w, vr) + bs.T[:, :, None].astype(v.dtype)
    return s.reshape(b, nc * CHUNK, GM_W)[:, :L]


def causal_conv(u, w, bias, buf):
    L = u.shape[1]
    up = jnp.concatenate([buf.astype(u.dtype), u], axis=1)
    out = sum((up[:, j:j + L] * w[j] for j in range(CONV_K)), bias.astype(u.dtype))
    return out, up[:, -(CONV_K - 1):]


def ssd_scan(xh, dt, a_log, bm, cm, h0):
    b, L = xh.shape[:2]
    q = min(SSD_CHUNK, L)
    nc = -(-L // q)
    pad = nc * q - L
    padt = lambda a: jnp.pad(a, ((0, 0), (0, pad)) + ((0, 0),) * (a.ndim - 2))
    r = SSM_HEADS // SSM_GROUPS
    A = -jnp.exp(a_log.astype(F32)).reshape(SSM_GROUPS, r)
    x = padt(xh.astype(F32)).reshape(b, nc, q, SSM_GROUPS, r, SSM_HEAD_DIM)
    dtc = padt(dt.astype(F32)).reshape(b, nc, q, SSM_GROUPS, r)
    B = padt(bm.astype(F32)).reshape(b, nc, q, SSM_GROUPS, SSM_STATE)
    C = padt(cm.astype(F32)).reshape(b, nc, q, SSM_GROUPS, SSM_STATE)
    cum = jnp.cumsum(dtc * A, axis=2)
    xdt = x * dtc[..., None]
    causal = jnp.tril(jnp.ones((q, q), bool))[None, None, :, :, None, None]
    seg = cum[:, :, :, None] - cum[:, :, None, :]
    decay = jnp.exp(jnp.where(causal, seg, -jnp.inf))
    cb = jnp.einsum('bcign,bcjgn->bcijg', C, B)
    y_diag = jnp.einsum('bcijg,bcijgr,bcjgrp->bcigrp', cb, decay, xdt)
    to_end = jnp.exp(cum[:, :, -1:] - cum)
    s_chunk = jnp.einsum('bcjgn,bcjgr,bcjgrp->bcgrpn', B, to_end, xdt)
    chunk_decay = jnp.exp(cum[:, :, -1])

    def step(h, inp):
        s_c, d_c = inp
        return h * d_c[..., None, None] + s_c, h

    h0r = h0.astype(F32).reshape(b, SSM_GROUPS, r, SSM_HEAD_DIM, SSM_STATE)
    h_last, h_prev = lax.scan(step, h0r, (jnp.moveaxis(s_chunk, 1, 0), jnp.moveaxis(chunk_decay, 1, 0)))
    h_prev = jnp.moveaxis(h_prev, 0, 1)
    y_off = jnp.einsum('bcign,bcigr,bcgrpn->bcigrp', C, jnp.exp(cum), h_prev)
    y = (y_diag + y_off).reshape(b, nc * q, SSM_HEADS, SSM_HEAD_DIM)[:, :L]
    return y, h_last.reshape(b, SSM_HEADS, SSM_HEAD_DIM, SSM_STATE)


def token_mixers(h, pos0, kbuf, vbuf, conv_buf, ssm_h, w_in, sinks, gm_ln_g, gm_ln_b, gm_ws, gm_bs,
                 conv_w, conv_b, dt_bias, a_log, d_skip, ssm_norm_g, w_pa, w_pb, w_pc, w_o):
    b, L, _ = h.shape
    pos = pos0 + jnp.arange(L)
    q, k, v, gu, gv, z, xbc, dt_raw, gates = jnp.split(h @ w_in, list(np.cumsum(SPLIT_SIZES)[:-1]), axis=-1)
    q = rope(q.reshape(b, L, N_HEADS, HEAD_DIM), pos)
    k = rope(k.reshape(b, L, N_KV_HEADS, HEAD_DIM), pos)
    v = v.reshape(b, L, N_KV_HEADS, HEAD_DIM)
    att, new_k, new_v = sliding_window_attention(q, k, v, kbuf, vbuf, pos0, sinks)
    vg = layer_norm(jax.nn.gelu(gv), gm_ln_g, gm_ln_b)
    gm = jax.nn.gelu(gu) * chunk_spatial_mix(vg, gm_ws, gm_bs)
    new_gv = vg[:, ((L - 1) // CHUNK) * CHUNK:].reshape(b, -1, GM_GROUPS, GM_GROUP_DIM)
    xbc, new_conv = causal_conv(xbc, conv_w, conv_b, conv_buf)
    xbc = jax.nn.silu(xbc)
    xs, bm, cm = jnp.split(xbc, [SSM_INNER, SSM_INNER + SSM_BC], axis=-1)
    xs = xs.reshape(b, L, SSM_HEADS, SSM_HEAD_DIM)
    dt = jax.nn.softplus(dt_raw.astype(F32) + dt_bias.astype(F32))
    y, new_h = ssd_scan(xs, dt, a_log, bm.reshape(b, L, SSM_GROUPS, SSM_STATE),
                        cm.reshape(b, L, SSM_GROUPS, SSM_STATE), ssm_h)
    y = (y + d_skip.astype(F32)[:, None] * xs.astype(F32)).reshape(b, L, SSM_INNER)
    ssm = rms_norm(y * jax.nn.silu(z.astype(F32)), ssm_norm_g).astype(h.dtype)
    ga, gb, gc = jnp.split(jax.nn.sigmoid(gates), 3, axis=-1)
    merged = ga * (att @ w_pa) + gb * (gm @ w_pb) + gc * (ssm @ w_pc)
    return merged @ w_o, new_k, new_v, new_conv, new_h, new_gv


def memory_cross_attention(h, mk, mv, w_cq, w_co):
    b, L, _ = h.shape
    q = (h @ w_cq).reshape(b, L, MEM_HEADS, MEM_HEAD_DIM)
    s = jnp.einsum('blhd,bmhd->bhlm', q, mk.astype(q.dtype)).astype(F32) * MEM_HEAD_DIM ** -0.5
    p = jax.nn.softmax(s, axis=-1)
    o = jnp.einsum('bhlm,bmhd->blhd', p.astype(h.dtype), mv.astype(h.dtype)).reshape(b, L, MEM_W)
    return o @ w_co


def hierarchical_moe(h, w_rg, b_rg, w_re, b_re, w_gate, w_up, w_down):
    shp = h.shape
    t = h.reshape(-1, D_MODEL)
    T = t.shape[0]
    pg = jax.nn.softmax((t @ w_rg).astype(F32) + b_rg.astype(F32), axis=-1)
    gi = jnp.argmax(pg, axis=-1)
    gw = jnp.take_along_axis(pg, gi[:, None], axis=-1)
    le = ((t @ w_re).astype(F32) + b_re.astype(F32)).reshape(T, N_EGROUPS, EXPERTS_PER_GROUP)
    le = jnp.take_along_axis(le, gi[:, None, None], axis=1)[:, 0]
    tw, ti = lax.top_k(jax.nn.softmax(le, axis=-1), TOP_K)
    tw = tw / jnp.sum(tw, axis=-1, keepdims=True)
    eid = (gi[:, None] * EXPERTS_PER_GROUP + ti).reshape(-1)
    wts = (gw * tw).reshape(-1)
    order = jnp.argsort(eid)
    tok = order // TOP_K
    xs = t[tok]
    sizes = jnp.bincount(eid, length=N_EXPERTS).astype(jnp.int32)
    a = lax.ragged_dot(xs, w_gate, sizes)
    u = lax.ragged_dot(xs, w_up, sizes)
    y = lax.ragged_dot(jax.nn.silu(a) * u, w_down, sizes)
    y = y * wts[order][:, None].astype(y.dtype)
    return jax.ops.segment_sum(y, tok, num_segments=T).reshape(shp)


def run_trunk(x, pos0, kbuf, vbuf, conv_buf, ssm_h, mem_k, mem_v, W):
    h = layer_norm(x, W['ln_in_g'], W['ln_in_b'])
    ks, vs, cs, hs, gvs = [], [], [], [], []
    for l in range(DEPTH):
        mix, nk, nv, nc, nh, ngv = token_mixers(
            h, pos0, kbuf[l], vbuf[l], conv_buf[l], ssm_h[l],
            W['w_in'][l], W['attn_sinks'][l], W['gm_ln_g'][l], W['gm_ln_b'][l], W['gm_ws'][l], W['gm_bs'][l],
            W['conv_w'][l], W['conv_b'][l], W['dt_bias'][l], W['a_log'][l], W['d_skip'][l], W['ssm_norm_g'][l],
            W['w_pa'][l], W['w_pb'][l], W['w_pc'][l], W['w_o'][l])
        h = layer_norm(DEEPNORM_ALPHA * h + mix, W['ln1_g'][l], W['ln1_b'][l])
        ca = memory_cross_attention(h, mem_k[l], mem_v[l], W['w_cq'][l], W['w_co'][l])
        h = layer_norm(DEEPNORM_ALPHA * h + ca, W['ln2_g'][l], W['ln2_b'][l])
        ff = hierarchical_moe(h, W['w_rg'][l], W['b_rg'][l], W['w_re'][l], W['b_re'][l],
                              W['w_gate'][l], W['w_up'][l], W['w_down'][l])
        h = layer_norm(DEEPNORM_ALPHA * h + ff, W['ln3_g'][l], W['ln3_b'][l])
        ks.append(nk)
        vs.append(nv)
        cs.append(nc)
        hs.append(nh)
        gvs.append(ngv)
    return h, jnp.stack(ks), jnp.stack(vs), jnp.stack(cs), jnp.stack(hs), jnp.stack(gvs)


def setup_inputs(seed: int = 0) -> dict:
    key = jax.random.key(seed)
    ks = iter(jax.random.split(key, 64))

    def nrm(shape, scale):
        return scale * jax.random.normal(next(ks), shape, F32)

    def gain(shape):
        return 1.0 + nrm(shape, 0.02)

    wb = min(WINDOW, PAST_LEN)
    beta = DEEPNORM_BETA
    d_in = D_MODEL ** -0.5
    a_log = jnp.log(jax.random.uniform(next(ks), (DEPTH, SSM_HEADS), F32, 1.0, 16.0))
    dt0 = jnp.exp(jax.random.uniform(next(ks), (DEPTH, SSM_HEADS), F32, math.log(1e-3), math.log(1e-1)))
    dt_bias = dt0 + jnp.log(-jnp.expm1(-dt0))
    return {
        'x_prompt': nrm((BATCH, SEQ, D_MODEL), 1.0),
        'x_sample': nrm((DEC_BATCH, DEC_SEQ, D_MODEL), 1.0),
        'mem_prompt': nrm((BATCH, MEM_LEN, D_MODEL), 1.0),
        'cache_swa_k': nrm((DEPTH, DEC_BATCH, wb, N_KV_HEADS, HEAD_DIM), 1.0),
        'cache_swa_v': nrm((DEPTH, DEC_BATCH, wb, N_KV_HEADS, HEAD_DIM), 1.0),
        'cache_mem_k': nrm((DEPTH, DEC_BATCH, MEM_LEN, MEM_HEADS, MEM_HEAD_DIM), 1.0),
        'cache_mem_v': nrm((DEPTH, DEC_BATCH, MEM_LEN, MEM_HEADS, MEM_HEAD_DIM), 1.0),
        'state_conv': nrm((DEPTH, DEC_BATCH, CONV_K - 1, CONV_DIM), 1.0),
        'state_ssm': nrm((DEPTH, DEC_BATCH, SSM_HEADS, SSM_HEAD_DIM, SSM_STATE), 0.5),
        'ln_in_g': gain((D_MODEL,)),
        'ln_in_b': nrm((D_MODEL,), 0.02),
        'w_in': nrm((DEPTH, D_MODEL, IN_W), d_in),
        'attn_sinks': nrm((DEPTH, N_HEADS), 0.5),
        'gm_ln_g': gain((DEPTH, GM_W)),
        'gm_ln_b': nrm((DEPTH, GM_W), 0.02),
        'gm_ws': nrm((DEPTH, GM_GROUPS, CHUNK, CHUNK), 0.5 * CHUNK ** -0.5),
        'gm_bs': 1.0 + nrm((DEPTH, GM_GROUPS, CHUNK), 0.02),
        'conv_w': nrm((DEPTH, CONV_K, CONV_DIM), CONV_K ** -0.5),
        'conv_b': nrm((DEPTH, CONV_DIM), 0.02),
        'dt_bias': dt_bias,
        'a_log': a_log,
        'd_skip': gain((DEPTH, SSM_HEADS)),
        'ssm_norm_g': gain((DEPTH, SSM_INNER)),
        'w_pa': nrm((DEPTH, Q_W, D_MODEL), Q_W ** -0.5),
        'w_pb': nrm((DEPTH, GM_W, D_MODEL), GM_W ** -0.5),
        'w_pc': nrm((DEPTH, SSM_INNER, D_MODEL), SSM_INNER ** -0.5),
        'w_o': nrm((DEPTH, D_MODEL, D_MODEL), beta * d_in),
        'ln1_g': gain((DEPTH, D_MODEL)),
        'ln1_b': nrm((DEPTH, D_MODEL), 0.02),
        'w_cq': nrm((DEPTH, D_MODEL, MEM_W), d_in),
        'w_ck': nrm((DEPTH, D_MODEL, MEM_W), d_in),
        'w_cv': nrm((DEPTH, D_MODEL, MEM_W), beta * d_in),
        'w_co': nrm((DEPTH, MEM_W, D_MODEL), beta * MEM_W ** -0.5),
        'ln2_g': gain((DEPTH, D_MODEL)),
        'ln2_b': nrm((DEPTH, D_MODEL), 0.02),
        'w_rg': nrm((DEPTH, D_MODEL, N_EGROUPS), d_in),
        'b_rg': nrm((DEPTH, N_EGROUPS), 0.01),
        'w_re': nrm((DEPTH, D_MODEL, N_EXPERTS), d_in),
        'b_re': nrm((DEPTH, N_EXPERTS), 0.01),
        'w_gate': nrm((DEPTH, N_EXPERTS, D_MODEL, EXPERT_FF), d_in),
        'w_up': nrm((DEPTH, N_EXPERTS, D_MODEL, EXPERT_FF), d_in),
        'w_down': nrm((DEPTH, N_EXPERTS, EXPERT_FF, D_MODEL), beta * EXPERT_FF ** -0.5),
        'ln3_g': gain((DEPTH, D_MODEL)),
        'ln3_b': nrm((DEPTH, D_MODEL), 0.02),
    }


def reference(x_prompt, x_sample, mem_prompt, cache_swa_k, cache_swa_v, cache_mem_k, cache_mem_v,
              state_conv, state_ssm, ln_in_g, ln_in_b, w_in, attn_sinks, gm_ln_g, gm_ln_b, gm_ws, gm_bs,
              conv_w, conv_b, dt_bias, a_log, d_skip, ssm_norm_g, w_pa, w_pb, w_pc, w_o, ln1_g, ln1_b,
              w_cq, w_ck, w_cv, w_co, ln2_g, ln2_b, w_rg, b_rg, w_re, b_re, w_gate, w_up, w_down,
              ln3_g, ln3_b):
    W = dict(ln_in_g=ln_in_g, ln_in_b=ln_in_b, w_in=w_in, attn_sinks=attn_sinks, gm_ln_g=gm_ln_g,
             gm_ln_b=gm_ln_b, gm_ws=gm_ws, gm_bs=gm_bs, conv_w=conv_w, conv_b=conv_b, dt_bias=dt_bias,
             a_log=a_log, d_skip=d_skip, ssm_norm_g=ssm_norm_g, w_pa=w_pa, w_pb=w_pb, w_pc=w_pc, w_o=w_o,
             ln1_g=ln1_g, ln1_b=ln1_b, w_cq=w_cq, w_co=w_co, ln2_g=ln2_g, ln2_b=ln2_b, w_rg=w_rg,
             b_rg=b_rg, w_re=w_re, b_re=b_re, w_gate=w_gate, w_up=w_up, w_down=w_down,
             ln3_g=ln3_g, ln3_b=ln3_b)
    bp = x_prompt.shape[0]
    zkv = jnp.zeros((DEPTH, bp, WINDOW, N_KV_HEADS, HEAD_DIM), x_prompt.dtype)
    zconv = jnp.zeros((DEPTH, bp, CONV_K - 1, CONV_DIM), x_prompt.dtype)
    zssm = jnp.zeros((DEPTH, bp, SSM_HEADS, SSM_HEAD_DIM, SSM_STATE), F32)
    p_mem_k = jnp.einsum('bmd,lde->lbme', mem_prompt, w_ck).reshape(DEPTH, bp, -1, MEM_HEADS, MEM_HEAD_DIM)
    p_mem_v = jnp.einsum('bmd,lde->lbme', mem_prompt, w_cv).reshape(DEPTH, bp, -1, MEM_HEADS, MEM_HEAD_DIM)
    y_prompt, p_swa_k, p_swa_v, p_conv, p_ssm, p_gmlp_v = run_trunk(
        x_prompt, 0, zkv, zkv, zconv, zssm, p_mem_k, p_mem_v, W)
    y_sample, s_swa_k, s_swa_v, s_conv, s_ssm, s_gmlp_v = run_trunk(
        x_sample, PAST_LEN, cache_swa_k, cache_swa_v, state_conv, state_ssm, cache_mem_k, cache_mem_v, W)
    return (y_prompt, y_sample, p_swa_k, p_swa_v, p_mem_k, p_mem_v, p_conv, p_ssm, p_gmlp_v,
            s_swa_k, s_swa_v, s_conv, s_ssm, s_gmlp_v)
```

```python
import functools
import math

import numpy as np
import jax
import jax.numpy as jnp
from jax import lax
from jax.experimental import pallas as pl
from jax.experimental.pallas import tpu as pltpu

F32 = jnp.float32
BF16 = jnp.bfloat16

D_MODEL = 2048
N_HEADS = 32
N_KV_HEADS = 4
HEAD_DIM = 64
WINDOW = 128
PAST_LEN = 8192
ROPE_THETA = 10000.0
CHUNK = 128
GM_GROUPS = 16
GM_GROUP_DIM = 128
SSM_HEADS = 32
SSM_HEAD_DIM = 64
SSM_GROUPS = 4
SSM_STATE = 128
CONV_K = 4
MEM_HEADS = 4
MEM_HEAD_DIM = 128
N_EGROUPS = 4
EXPERTS_PER_GROUP = 8
N_EXPERTS = N_EGROUPS * EXPERTS_PER_GROUP
EXPERT_FF = D_MODEL // 2
Q_W = N_HEADS * HEAD_DIM
KV_W = N_KV_HEADS * HEAD_DIM
GM_W = GM_GROUPS * GM_GROUP_DIM
SSM_INNER = SSM_HEADS * SSM_HEAD_DIM
SSM_BC = SSM_GROUPS * SSM_STATE
CONV_DIM = SSM_INNER + 2 * SSM_BC
MEM_W = MEM_HEADS * MEM_HEAD_DIM
LN_EPS = 1e-5
NEG_BIG = -1e30

VMEM_LIMIT_BYTES = 52 * 1024 * 1024
LANES = 128
SUBLANES = 8

COL_Q = 0
COL_GU = 2048
COL_GV = 4096
COL_Z = 6144
COL_GATES = 8192
COL_XS = 14336
COL_BC = 16384
COL_K = 17408
COL_V = 17664
PROJ_W = 17920

MOE_TM = 256
MOE_FCHUNK = 512


def _cparams(n_grid):
    return pltpu.CompilerParams(
        dimension_semantics=("arbitrary",) * n_grid,
        vmem_limit_bytes=VMEM_LIMIT_BYTES,
    )


def _pick(n, prefs):
    for p in prefs:
        if n % p == 0:
            return p
    raise ValueError(f"no tile for {n} in {prefs}")


def _ln_rows(x, g, b):
    mu = jnp.mean(x, axis=-1, keepdims=True)
    xc = x - mu
    var = jnp.mean(xc * xc, axis=-1, keepdims=True)
    return xc * lax.rsqrt(var + LN_EPS) * g + b


def _sigmoid(x):
    return 1.0 / (1.0 + jnp.exp(-x))


def _silu(x):
    return x * _sigmoid(x)


def _softplus(x):
    return jnp.maximum(x, 0.0) + jnp.log1p(jnp.exp(-jnp.abs(x)))


def _gelu(x):
    return jax.nn.gelu(x, approximate=True)


def _split3(x):
    hi = x.astype(BF16)
    r1 = x - hi.astype(F32)
    mid = r1.astype(BF16)
    lo = (r1 - mid.astype(F32)).astype(BF16)
    return hi, mid, lo


def _dot(a, b):
    return jnp.dot(a, b, preferred_element_type=F32)


def _dot_nt(a, b):
    return lax.dot_general(a, b, (((1,), (1,)), ((), ())), preferred_element_type=F32)


def _exact_dot_left(pieces, m):
    acc = _dot(pieces[0], m)
    for p in pieces[1:]:
        acc = acc + _dot(p, m)
    return acc


def _ln_in_kernel(xp_ref, xs_ref, g_ref, b_ref, of_ref, ob_ref, *, n_p):
    i = pl.program_id(0)

    @pl.when(i < n_p)
    def _():
        y = _ln_rows(xp_ref[...], g_ref[...], b_ref[...])
        of_ref[...] = y
        ob_ref[...] = y.astype(BF16)

    @pl.when(i >= n_p)
    def _():
        y = _ln_rows(xs_ref[...], g_ref[...], b_ref[...])
        of_ref[...] = y
        ob_ref[...] = y.astype(BF16)


def ln_in(xp, xs, g, b):
    tp, d = xp.shape
    ts = xs.shape[0]
    tm = _pick(math.gcd(tp, ts), (256, 128, 64, 32, 16, 8))
    n_p, n_s = tp // tm, ts // tm
    t = tp + ts
    return pl.pallas_call(
        functools.partial(_ln_in_kernel, n_p=n_p),
        grid=(n_p + n_s,),
        in_specs=[
            pl.BlockSpec((tm, d), lambda i: (jnp.minimum(i, n_p - 1), 0)),
            pl.BlockSpec((tm, d), lambda i: (jnp.maximum(i - n_p, 0), 0)),
            pl.BlockSpec((1, d), lambda i: (0, 0)),
            pl.BlockSpec((1, d), lambda i: (0, 0)),
        ],
        out_specs=[
            pl.BlockSpec((tm, d), lambda i: (i, 0)),
            pl.BlockSpec((tm, d), lambda i: (i, 0)),
        ],
        out_shape=[jax.ShapeDtypeStruct((t, d), F32), jax.ShapeDtypeStruct((t, d), BF16)],
        compiler_params=_cparams(1),
        name="ln_in",
    )(xp, xs, g.reshape(1, d), b.reshape(1, d))


def _mm_kernel(x_ref, w_ref, o_ref):
    o_ref[...] = _dot(x_ref[...], w_ref[...]).astype(o_ref.dtype)


def matmul(x, w, *, out_dtype=F32, tm_prefs=(1088, 1024, 512, 256, 128, 64, 32, 16, 8),
           tn_prefs=(1280, 1024, 512, 256, 128), name="mm"):
    t, k = x.shape
    n = w.shape[1]
    tm = _pick(t, tm_prefs)
    tn = _pick(n, tn_prefs)
    return pl.pallas_call(
        _mm_kernel,
        grid=(n // tn, t // tm),
        in_specs=[
            pl.BlockSpec((tm, k), lambda j, i: (i, 0)),
            pl.BlockSpec((k, tn), lambda j, i: (0, j)),
        ],
        out_specs=pl.BlockSpec((tm, tn), lambda j, i: (i, j)),
        out_shape=jax.ShapeDtypeStruct((t, n), out_dtype),
        compiler_params=_cparams(2),
        name=name,
    )(x, w)


def _mm_ln_kernel(x_ref, w_ref, r_ref, g_ref, b_ref, of_ref, ob_ref, *, alpha):
    y = _dot(x_ref[...], w_ref[...])
    h = _ln_rows(alpha * r_ref[...] + y, g_ref[...], b_ref[...])
    of_ref[...] = h
    ob_ref[...] = h.astype(BF16)


def matmul_ln(x, w, res, g, b, *, alpha, name="mm_ln"):
    t, k = x.shape
    d = w.shape[1]
    tm = _pick(t, (256, 128, 64, 32, 16, 8))
    return pl.pallas_call(
        functools.partial(_mm_ln_kernel, alpha=alpha),
        grid=(t // tm,),
        in_specs=[
            pl.BlockSpec((tm, k), lambda i: (i, 0)),
            pl.BlockSpec((k, d), lambda i: (0, 0)),
            pl.BlockSpec((tm, d), lambda i: (i, 0)),
            pl.BlockSpec((1, d), lambda i: (0, 0)),
            pl.BlockSpec((1, d), lambda i: (0, 0)),
        ],
        out_specs=[
            pl.BlockSpec((tm, d), lambda i: (i, 0)),
            pl.BlockSpec((tm, d), lambda i: (i, 0)),
        ],
        out_shape=[jax.ShapeDtypeStruct((t, d), F32), jax.ShapeDtypeStruct((t, d), BF16)],
        compiler_params=_cparams(1),
        name=name,
    )(x, w, res, g.reshape(1, d), b.reshape(1, d))


def _merge_kernel(ap_ref, as_ref, bp_ref, bs_ref, cp_ref, cs_ref, wa_ref, wb_ref, wc_ref,
                  ga_ref, gb_ref, gc_ref, o_ref, *, n_p):
    i = pl.program_id(1)
    is_p = i < n_p
    xa = jnp.where(is_p, ap_ref[...], as_ref[...])
    xb = jnp.where(is_p, bp_ref[...], bs_ref[...])
    xc = jnp.where(is_p, cp_ref[...], cs_ref[...])
    acc = _sigmoid(ga_ref[...]) * _dot(xa, wa_ref[...])
    acc = acc + _sigmoid(gb_ref[...]) * _dot(xb, wb_ref[...])
    acc = acc + _sigmoid(gc_ref[...]) * _dot(xc, wc_ref[...])
    o_ref[...] = acc.astype(o_ref.dtype)


def gated_merge(att_p, att_s, gm_p, gm_s, ssm_p, ssm_s, w_pa, w_pb, w_pc, proj):
    tp, k = att_p.shape
    ts = att_s.shape[0]
    d = w_pa.shape[1]
    tm = _pick(math.gcd(tp, ts), (256, 128, 64, 32, 16, 8))
    tn = 512
    n_p, n_s = tp // tm, ts // tm
    gate_blk = COL_GATES // tn
    d_blk = d // tn

    def xp_spec():
        return pl.BlockSpec((tm, k), lambda j, i: (jnp.minimum(i, n_p - 1), 0))

    def xs_spec():
        return pl.BlockSpec((tm, k), lambda j, i: (jnp.maximum(i - n_p, 0), 0))

    def w_spec():
        return pl.BlockSpec((k, tn), lambda j, i: (0, j))

    def g_spec(which):
        return pl.BlockSpec((tm, tn), lambda j, i: (i, gate_blk + which * d_blk + j))

    return pl.pallas_call(
        functools.partial(_merge_kernel, n_p=n_p),
        grid=(d // tn, n_p + n_s),
        in_specs=[xp_spec(), xs_spec(), xp_spec(), xs_spec(), xp_spec(), xs_spec(),
                  w_spec(), w_spec(), w_spec(), g_spec(0), g_spec(1), g_spec(2)],
        out_specs=pl.BlockSpec((tm, tn), lambda j, i: (i, j)),
        out_shape=jax.ShapeDtypeStruct((tp + ts, d), BF16),
        compiler_params=_cparams(2),
        name="gated_merge",
    )(att_p, att_s, gm_p, gm_s, ssm_p, ssm_s, w_pa, w_pb, w_pc, proj, proj, proj)


def _rope_block(x, cos, sin_signed, first_half):
    outs = []
    for c in range(x.shape[1] // LANES):
        xc = x[:, c * LANES:(c + 1) * LANES]
        fwd = pltpu.roll(xc, LANES - HEAD_DIM // 2, axis=1)
        bwd = pltpu.roll(xc, HEAD_DIM // 2, axis=1)
        partner = jnp.where(first_half, fwd, bwd)
        outs.append(xc * cos + partner * sin_signed)
    return outs


def _rope_kernel(q_ref, k_ref, cos_ref, sin_ref, qo_ref, ko_ref):
    cos = cos_ref[...]
    sin_signed = sin_ref[...]
    lane = lax.broadcasted_iota(jnp.int32, cos.shape, 1)
    first_half = (lane % HEAD_DIM) < (HEAD_DIM // 2)
    for c, o in enumerate(_rope_block(q_ref[...], cos, sin_signed, first_half)):
        qo_ref[:, c * LANES:(c + 1) * LANES] = o
    for c, o in enumerate(_rope_block(k_ref[...], cos, sin_signed, first_half)):
        ko_ref[:, c * LANES:(c + 1) * LANES] = o


def rope_qk(proj, cos_t, sin_t):
    t = proj.shape[0]
    tm = _pick(t, (256, 128, 64, 32, 16, 8))
    return pl.pallas_call(
        _rope_kernel,
        grid=(t // tm,),
        in_specs=[
            pl.BlockSpec((tm, Q_W), lambda i: (i, COL_Q // Q_W)),
            pl.BlockSpec((tm, KV_W), lambda i: (i, COL_K // KV_W)),
            pl.BlockSpec((tm, LANES), lambda i: (i, 0)),
            pl.BlockSpec((tm, LANES), lambda i: (i, 0)),
        ],
        out_specs=[
            pl.BlockSpec((tm, Q_W), lambda i: (i, 0)),
            pl.BlockSpec((tm, KV_W), lambda i: (i, 0)),
        ],
        out_shape=[jax.ShapeDtypeStruct((t, Q_W), F32), jax.ShapeDtypeStruct((t, KV_W), F32)],
        compiler_params=_cparams(1),
        name="rope_qk",
    )(proj, proj, cos_t, sin_t)


def rope_tables(tp, seq, ts, nb, past_len):
    half = HEAD_DIM // 2
    inv = ROPE_THETA ** (-jnp.arange(half, dtype=F32) / half)
    pos_p = jnp.arange(tp, dtype=jnp.int32) % seq
    pos_s = past_len + jnp.arange(ts, dtype=jnp.int32) // nb
    pos = jnp.concatenate([pos_p, pos_s]).astype(F32)
    ang = pos[:, None] * inv[None, :]
    cos = jnp.tile(jnp.cos(ang), (1, LANES // half))
    sin = jnp.sin(ang)
    sin_signed = jnp.tile(jnp.concatenate([-sin, sin], axis=1), (1, LANES // HEAD_DIM))
    return cos, sin_signed


def _dup_head(slab, g):
    lane = lax.broadcasted_iota(jnp.int32, slab.shape, 1)
    rolled = pltpu.roll(slab, HEAD_DIM, axis=1)
    if g % 2 == 0:
        return jnp.where(lane < HEAD_DIM, slab, rolled)
    return jnp.where(lane < HEAD_DIM, rolled, slab)


def _swa_kernel(sink_ref, q_ref, kp_ref, kc_ref, vp_ref, vc_ref, o_ref, *, qb, prev_from_block0):
    n = pl.program_id(1)
    rep = N_HEADS // N_KV_HEADS
    kw = WINDOW + qb
    q = q_ref[...]
    kp, kc, vp, vc = kp_ref[...], kc_ref[...], vp_ref[...], vc_ref[...]
    lane = lax.broadcasted_iota(jnp.int32, (qb, LANES), 1)
    ii = lax.broadcasted_iota(jnp.int32, (qb, kw), 0)
    jj = lax.broadcasted_iota(jnp.int32, (qb, kw), 1)
    ok_prev = (jj < WINDOW) & (jj > ii)
    if not prev_from_block0:
        ok_prev = ok_prev & (n > 0)
    ok = ok_prev | ((jj >= WINDOW) & ((jj - WINDOW) <= ii))
    bias = jnp.where(ok, 0.0, NEG_BIG)
    scale = HEAD_DIM ** -0.5
    for g in range(N_KV_HEADS):
        sl = slice((g // 2) * LANES, (g // 2 + 1) * LANES)
        kdup = _dup_head(jnp.concatenate([kp[:, sl], kc[:, sl]], axis=0), g).astype(BF16)
        vdup = _dup_head(jnp.concatenate([vp[:, sl], vc[:, sl]], axis=0), g).astype(BF16)
        rows = []
        for r in range(rep):
            h = g * rep + r
            qc = q[:, (h // 2) * LANES:(h // 2 + 1) * LANES]
            keep = (lane < HEAD_DIM) if h % 2 == 0 else (lane >= HEAD_DIM)
            rows.append(jnp.where(keep, qc * scale, 0.0))
        lhs = jnp.concatenate(rows, axis=0).astype(BF16)
        s = _dot_nt(lhs, kdup)
        ps, dens = [], []
        for r in range(rep):
            sr = s[r * qb:(r + 1) * qb] + bias
            sink = sink_ref[g * rep + r]
            m = jnp.maximum(jnp.max(sr, axis=-1, keepdims=True), sink)
            p = jnp.exp(sr - m)
            dens.append(jnp.sum(p, axis=-1, keepdims=True) + jnp.exp(sink - m))
            ps.append(p)
        o = _dot(jnp.concatenate(ps, axis=0).astype(BF16), vdup)
        for c in range(rep // 2):
            oa = o[(2 * c) * qb:(2 * c + 1) * qb] / dens[2 * c]
            ob = o[(2 * c + 1) * qb:(2 * c + 2) * qb] / dens[2 * c + 1]
            col = (g * rep // 2 + c) * LANES
            o_ref[:, col:col + LANES] = jnp.where(lane < HEAD_DIM, oa, ob).astype(o_ref.dtype)


def swa_attention(sinks, q, kp, kc, vp, vc, *, n_seq, n_blk, qb, prev_from_block0,
                  kp_map, kc_map, vp_map, vc_map, out_dtype, name):
    return pl.pallas_call(
        functools.partial(_swa_kernel, qb=qb, prev_from_block0=prev_from_block0),
        grid=(n_seq, n_blk),
        in_specs=[
            pl.BlockSpec(memory_space=pltpu.SMEM),
            pl.BlockSpec((qb, Q_W), lambda i, n: (i * n_blk + n, 0)),
            pl.BlockSpec((WINDOW, KV_W), kp_map),
            pl.BlockSpec((qb, KV_W), kc_map),
            pl.BlockSpec((WINDOW, KV_W), vp_map),
            pl.BlockSpec((qb, KV_W), vc_map),
        ],
        out_specs=pl.BlockSpec((qb, Q_W), lambda i, n: (i * n_blk + n, 0)),
        out_shape=jax.ShapeDtypeStruct((n_seq * n_blk * qb, Q_W), out_dtype),
        compiler_params=_cparams(2),
        name=name,
    )(sinks, q, kp, kc, vp, vc)


def _gmlp_p_kernel(gu_ref, gv_ref, ws_ref, bst_ref, lg_ref, lb_ref, gm_ref, vg_ref, *, n_chunks):
    n = pl.program_id(1)
    vg = _ln_rows(_gelu(gv_ref[...]), lg_ref[...], lb_ref[...])
    gu = gu_ref[...]
    ri = lax.broadcasted_iota(jnp.int32, (CHUNK, CHUNK), 0)
    ci = lax.broadcasted_iota(jnp.int32, (CHUNK, CHUNK), 1)
    tril = ri >= ci
    bst = bst_ref[...]
    for g in range(GM_GROUPS):
        sl = slice(g * GM_GROUP_DIM, (g + 1) * GM_GROUP_DIM)
        w = jnp.where(tril, ws_ref[g], 0.0).astype(BF16)
        s = _dot(w, vg[:, sl].astype(BF16)) + bst[:, g:g + 1]
        gm_ref[:, sl] = (_gelu(gu[:, sl]) * s).astype(gm_ref.dtype)

    @pl.when(n == n_chunks - 1)
    def _():
        vg_ref[...] = vg


def gmlp_prompt(proj, n_seq, seq, ws, bs, ln_g, ln_b):
    nc = seq // CHUNK
    return pl.pallas_call(
        functools.partial(_gmlp_p_kernel, n_chunks=nc),
        grid=(n_seq, nc),
        in_specs=[
            pl.BlockSpec((CHUNK, GM_W), lambda i, n: (i * nc + n, COL_GU // GM_W)),
            pl.BlockSpec((CHUNK, GM_W), lambda i, n: (i * nc + n, COL_GV // GM_W)),
            pl.BlockSpec((GM_GROUPS, CHUNK, CHUNK), lambda i, n: (0, 0, 0)),
            pl.BlockSpec((CHUNK, GM_GROUPS), lambda i, n: (0, 0)),
            pl.BlockSpec((1, GM_W), lambda i, n: (0, 0)),
            pl.BlockSpec((1, GM_W), lambda i, n: (0, 0)),
        ],
        out_specs=[
            pl.BlockSpec((CHUNK, GM_W), lambda i, n: (i * nc + n, 0)),
            pl.BlockSpec((CHUNK, GM_W), lambda i, n: (i, 0)),
        ],
        out_shape=[jax.ShapeDtypeStruct((n_seq * seq, GM_W), BF16),
                   jax.ShapeDtypeStruct((n_seq * CHUNK, GM_W), F32)],
        compiler_params=_cparams(2),
        name="gmlp_prompt",
    )(proj, proj, ws, bs.T, ln_g.reshape(1, GM_W), ln_b.reshape(1, GM_W))


def _gmlp_s_kernel(*refs, lt):
    gu_refs = refs[:lt]
    gv_refs = refs[lt:2 * lt]
    wrow_ref, brow_ref, lg_ref, lb_ref, gm_ref, vg_ref = refs[2 * lt:]
    nb = gu_refs[0].shape[0]
    vgs = [_ln_rows(_gelu(gv_refs[t][...]), lg_ref[...], lb_ref[...]) for t in range(lt)]
    for i in range(lt):
        s = brow_ref[i:i + 1, :]
        for j in range(i + 1):
            s = s + wrow_ref[i * lt + j:i * lt + j + 1, :] * vgs[j]
        gm_ref[i * nb:(i + 1) * nb, :] = (_gelu(gu_refs[i][...]) * s).astype(gm_ref.dtype)
        vg_ref[i * nb:(i + 1) * nb, :] = vgs[i]


def gmlp_sample(proj, tp, nb, lt, ws, bs, ln_g, ln_b):
    w_small = ws[:, :lt, :lt]
    wrow = jnp.repeat(jnp.transpose(w_small, (1, 2, 0)).reshape(lt * lt, GM_GROUPS), GM_GROUP_DIM, axis=1)
    brow = jnp.repeat(bs[:, :lt].T, GM_GROUP_DIM, axis=1)
    row0 = tp // nb

    def spec(t, col):
        return pl.BlockSpec((nb, GM_W), lambda i: (row0 + t, col // GM_W))

    in_specs = [spec(t, COL_GU) for t in range(lt)] + [spec(t, COL_GV) for t in range(lt)] + [
        pl.BlockSpec((lt * lt, GM_W), lambda i: (0, 0)),
        pl.BlockSpec((lt, GM_W), lambda i: (0, 0)),
        pl.BlockSpec((1, GM_W), lambda i: (0, 0)),
        pl.BlockSpec((1, GM_W), lambda i: (0, 0)),
    ]
    return pl.pallas_call(
        functools.partial(_gmlp_s_kernel, lt=lt),
        grid=(1,),
        in_specs=in_specs,
        out_specs=[pl.BlockSpec((lt * nb, GM_W), lambda i: (0, 0)),
                   pl.BlockSpec((lt * nb, GM_W), lambda i: (0, 0))],
        out_shape=[jax.ShapeDtypeStruct((lt * nb, GM_W), BF16),
                   jax.ShapeDtypeStruct((lt * nb, GM_W), F32)],
        compiler_params=_cparams(1),
        name="gmlp_sample",
    )(*([proj] * (2 * lt)), wrow, brow, ln_g.reshape(1, GM_W), ln_b.reshape(1, GM_W))


def _conv_silu(cur, prev8, w, bias):
    q = cur.shape[0]
    up = jnp.concatenate([prev8, cur], axis=0)
    acc = bias + up[SUBLANES:SUBLANES + q] * w[CONV_K - 1:CONV_K]
    for j in range(CONV_K - 1):
        off = SUBLANES - (CONV_K - 1) + j
        acc = acc + up[off:off + q] * w[j:j + 1]
    return _silu(acc)


def _ssd_p_kernel(z_ref, xs_ref, bc_ref, xsp_ref, bcp_ref, dt_ref, dtt_ref,
                  cwx_ref, cwbc_ref, cbx_ref, cbbc_ref, dtbe_ref, dtbc_ref, aloge_ref, alogc_ref,
                  dske_ref, ng_ref, e_ref, y_ref, st_ref, s_scr, *, n_chunks):
    c = pl.program_id(1)
    q = CHUNK
    rep = SSM_HEADS // SSM_GROUPS
    gw = rep * SSM_HEAD_DIM

    @pl.when(c == 0)
    def _():
        s_scr[...] = jnp.zeros_like(s_scr)

    has_prev = (c > 0).astype(F32)
    xs = _conv_silu(xs_ref[...], xsp_ref[...] * has_prev, cwx_ref[...], cbx_ref[...])
    bcm = _conv_silu(bc_ref[...], bcp_ref[...] * has_prev, cwbc_ref[...], cbbc_ref[...])

    ri = lax.broadcasted_iota(jnp.int32, (q, q), 0)
    ci = lax.broadcasted_iota(jnp.int32, (q, q), 1)
    tril = ri >= ci
    ones_tril = jnp.where(tril, 1.0, 0.0).astype(BF16)
    ones_triu = jnp.where(ri <= ci, 1.0, 0.0).astype(BF16)

    dt_e = _softplus(_exact_dot_left(_split3(dt_ref[...]), e_ref[...]) + dtbe_ref[...])
    a_e = dt_e * (-jnp.exp(aloge_ref[...]))
    a_hi, a_mid, a_lo = _split3(a_e)
    cum_e = _dot(ones_tril, a_hi) + _dot(ones_tril, a_mid) + _dot(ones_tril, a_lo)
    dt_t = _softplus(dtt_ref[...] + dtbc_ref[...])
    a_t = dt_t * (-jnp.exp(alogc_ref[...]))
    cum_t = _exact_dot_left(_split3(a_t), ones_triu)

    xdt = xs * dt_e
    cum_last = cum_e[q - 1:q, :]
    lane = lax.broadcasted_iota(jnp.int32, (q, LANES), 1)
    z = z_ref[...]
    ys = []
    for g in range(SSM_GROUPS):
        gs = slice(g * gw, (g + 1) * gw)
        bg = bcm[:, g * SSM_STATE:(g + 1) * SSM_STATE]
        cg = bcm[:, SSM_BC + g * SSM_STATE:SSM_BC + (g + 1) * SSM_STATE]
        bg16, cg16 = bg.astype(BF16), cg.astype(BF16)
        cb = _dot_nt(cg16, bg16)
        ydiag = []
        for pr in range(rep // 2):
            ms = []
            for hh in (2 * pr, 2 * pr + 1):
                h = g * rep + hh
                col = cum_e[:, h * SSM_HEAD_DIM:h * SSM_HEAD_DIM + 1]
                row = cum_t[h:h + 1, :]
                seg = jnp.where(tril, col - row, NEG_BIG)
                ms.append(cb * jnp.exp(seg))
            lhs = jnp.concatenate(ms, axis=1).astype(BF16)
            xslab = xdt[:, (g * rep + 2 * pr) * SSM_HEAD_DIM:(g * rep + 2 * pr + 2) * SSM_HEAD_DIM]
            xbd = jnp.concatenate([jnp.where(lane < SSM_HEAD_DIM, xslab, 0.0),
                                   jnp.where(lane >= SSM_HEAD_DIM, xslab, 0.0)], axis=0).astype(BF16)
            ydiag.append(_dot(lhs, xbd))
        ydiag = jnp.concatenate(ydiag, axis=1)
        s_old = s_scr[g]
        yoff = _dot(cg16, s_old.astype(BF16)) * jnp.exp(cum_e[:, gs])
        ys.append(ydiag + yoff)
        xw = xdt[:, gs] * jnp.exp(cum_last[:, gs] - cum_e[:, gs])
        s_scr[g] = s_old * jnp.exp(cum_last[:, gs]) + _dot(bg.T.astype(BF16), xw.astype(BF16))
    y = jnp.concatenate(ys, axis=1) + dske_ref[...] * xs
    gated = y * _silu(z)
    out = gated * lax.rsqrt(jnp.mean(gated * gated, axis=-1, keepdims=True) + LN_EPS) * ng_ref[...]
    y_ref[...] = out.astype(y_ref.dtype)

    @pl.when(c == n_chunks - 1)
    def _():
        for g in range(SSM_GROUPS):
            st_ref[0, g * gw:(g + 1) * gw, :] = s_scr[g].T


def _expand_heads(v):
    return jnp.repeat(v.astype(F32), SSM_HEAD_DIM).reshape(1, SSM_INNER)


def _head_expand_matrix():
    e = np.zeros((LANES, SSM_INNER), np.float32)
    for h in range(SSM_HEADS):
        e[h, h * SSM_HEAD_DIM:(h + 1) * SSM_HEAD_DIM] = 1.0
    return jnp.asarray(e, BF16)


def ssd_prompt(proj, dtp, dtt, n_seq, seq, conv_w, conv_b, dt_bias, a_log, d_skip, norm_g):
    b = n_seq
    nc = seq // CHUNK
    blk8 = CHUNK // SUBLANES
    const2 = lambda i, c: (0, 0)
    col_bc = COL_BC // (2 * SSM_BC)
    col_xs = COL_XS // SSM_INNER
    col_z = COL_Z // SSM_INNER
    prev_map_x = lambda i, c: (jnp.maximum((i * nc + c) * blk8 - 1, 0), col_xs)
    prev_map_bc = lambda i, c: (jnp.maximum((i * nc + c) * blk8 - 1, 0), col_bc)
    args = (
        proj, proj, proj, proj, proj, dtp, dtt,
        conv_w[:, :SSM_INNER], conv_w[:, SSM_INNER:], conv_b[:SSM_INNER].reshape(1, -1),
        conv_b[SSM_INNER:].reshape(1, -1),
        _expand_heads(dt_bias), jnp.broadcast_to(dt_bias.astype(F32)[:, None], (SSM_HEADS, CHUNK)),
        _expand_heads(a_log), jnp.broadcast_to(a_log.astype(F32)[:, None], (SSM_HEADS, CHUNK)),
        _expand_heads(d_skip), norm_g.reshape(1, SSM_INNER), _head_expand_matrix(),
    )
    in_specs = [
        pl.BlockSpec((CHUNK, SSM_INNER), lambda i, c: (i * nc + c, col_z)),
        pl.BlockSpec((CHUNK, SSM_INNER), lambda i, c: (i * nc + c, col_xs)),
        pl.BlockSpec((CHUNK, 2 * SSM_BC), lambda i, c: (i * nc + c, col_bc)),
        pl.BlockSpec((SUBLANES, SSM_INNER), prev_map_x),
        pl.BlockSpec((SUBLANES, 2 * SSM_BC), prev_map_bc),
        pl.BlockSpec((CHUNK, LANES), lambda i, c: (i * nc + c, 0)),
        pl.BlockSpec((SSM_HEADS, CHUNK), lambda i, c: (0, i * nc + c)),
        pl.BlockSpec((CONV_K, SSM_INNER), const2),
        pl.BlockSpec((CONV_K, 2 * SSM_BC), const2),
        pl.BlockSpec((1, SSM_INNER), const2),
        pl.BlockSpec((1, 2 * SSM_BC), const2),
        pl.BlockSpec((1, SSM_INNER), const2),
        pl.BlockSpec((SSM_HEADS, CHUNK), const2),
        pl.BlockSpec((1, SSM_INNER), const2),
        pl.BlockSpec((SSM_HEADS, CHUNK), const2),
        pl.BlockSpec((1, SSM_INNER), const2),
        pl.BlockSpec((1, SSM_INNER), const2),
        pl.BlockSpec((LANES, SSM_INNER), const2),
    ]
    return pl.pallas_call(
        functools.partial(_ssd_p_kernel, n_chunks=nc),
        grid=(b, nc),
        in_specs=in_specs,
        out_specs=[
            pl.BlockSpec((CHUNK, SSM_INNER), lambda i, c: (i * nc + c, 0)),
            pl.BlockSpec((1, SSM_INNER, SSM_STATE), lambda i, c: (i, 0, 0)),
        ],
        out_shape=[jax.ShapeDtypeStruct((b * seq, SSM_INNER), BF16),
                   jax.ShapeDtypeStruct((b, SSM_INNER, SSM_STATE), F32)],
        scratch_shapes=[pltpu.VMEM((SSM_GROUPS, SSM_STATE, SSM_INNER // SSM_GROUPS), F32)],
        compiler_params=_cparams(2),
        name="ssd_prompt",
    )(*args)


def _group_expand_matrix():
    gw = SSM_INNER // SSM_GROUPS
    m = np.zeros((SSM_BC, SSM_INNER), np.float32)
    for g in range(SSM_GROUPS):
        m[g * SSM_STATE:(g + 1) * SSM_STATE, g * gw:(g + 1) * gw] = 1.0
    return jnp.asarray(m, BF16)


def _ssd_s_pre_kernel(*refs, lt):
    xs_refs = refs[:lt]
    bc_refs = refs[lt:2 * lt]
    dt_refs = refs[2 * lt:3 * lt]
    (cx_ref, cbc_ref, cwx_ref, cwbc_ref, cbx_ref, cbbc_ref, dtbe_ref, aloge_ref, dske_ref, e_ref,
     gmat_ref, c_ref, b_ref, xw_ref, dec_ref, yd_ref, ec_ref) = refs[3 * lt:]
    nprev = CONV_K - 1
    ux = [cx_ref[j] for j in range(nprev)] + [r[...] for r in xs_refs]
    ub = [cbc_ref[j] for j in range(nprev)] + [r[...] for r in bc_refs]
    cwx, cwbc = cwx_ref[...], cwbc_ref[...]
    neg_a = -jnp.exp(aloge_ref[...])
    xs, bm, cm, xdt, cum = [], [], [], [], []
    run = None
    for t in range(lt):
        ax = cbx_ref[...]
        ab = cbbc_ref[...]
        for j in range(CONV_K):
            ax = ax + ux[t + j] * cwx[j:j + 1]
            ab = ab + ub[t + j] * cwbc[j:j + 1]
        x_t = _silu(ax)
        bc_t = _silu(ab)
        dt_e = _softplus(_exact_dot_left(_split3(dt_refs[t][...]), e_ref[...]) + dtbe_ref[...])
        a_t = dt_e * neg_a
        run = a_t if run is None else run + a_t
        xs.append(x_t)
        bm.append(bc_t[:, :SSM_BC])
        cm.append(bc_t[:, SSM_BC:])
        xdt.append(x_t * dt_e)
        cum.append(run)
    for i in range(lt):
        yd = dske_ref[...] * xs[i]
        for j in range(i + 1):
            hi, mid, _ = _split3(cm[i] * bm[j])
            cbe = _dot(hi, gmat_ref[...]) + _dot(mid, gmat_ref[...])
            yd = yd + cbe * jnp.exp(cum[i] - cum[j]) * xdt[j]
        yd_ref[i] = yd
        ec_ref[i] = jnp.exp(cum[i])
        c_ref[i] = cm[i]
        b_ref[i] = bm[i]
        xw_ref[i] = xdt[i] * jnp.exp(cum[lt - 1] - cum[i])
    dec_ref[...] = jnp.exp(cum[lt - 1])


def _rows_block(rows, total):
    c = rows[0].shape[1]
    rid = lax.broadcasted_iota(jnp.int32, (SUBLANES, c), 0)
    acc = jnp.zeros((SUBLANES, c), F32)
    for j, r in enumerate(rows):
        acc = jnp.where(rid == j, jnp.broadcast_to(r, (SUBLANES, c)), acc)
    if total == SUBLANES:
        return acc
    return jnp.concatenate([acc, jnp.zeros((total - SUBLANES, c), F32)], axis=0)


def _ssd_s_state_kernel(c_ref, b_ref, xw_ref, dec_ref, h0_ref, hn_ref, yr_ref, *, lt):
    b = pl.program_id(0)
    gw = SSM_INNER // SSM_GROUPS
    c8 = _rows_block([c_ref[i, pl.ds(b, 1), :] for i in range(lt)], SUBLANES).astype(BF16)
    b128 = _rows_block([b_ref[i, pl.ds(b, 1), :] for i in range(lt)], LANES).astype(BF16)
    xaug = _rows_block([xw_ref[i, pl.ds(b, 1), :] for i in range(lt)] + [dec_ref[pl.ds(b, 1), :]], LANES)
    for g in range(SSM_GROUPS):
        hg = h0_ref[0, g * gw:(g + 1) * gw, :]
        yraw = _dot_nt(c8[:, g * SSM_STATE:(g + 1) * SSM_STATE], hg.astype(BF16))
        for i in range(lt):
            yr_ref[i, pl.ds(b, 1), g * gw:(g + 1) * gw] = yraw[i:i + 1, :]
        tr = xaug[:, g * gw:(g + 1) * gw].T
        s = _dot(tr.astype(BF16), b128[:, g * SSM_STATE:(g + 1) * SSM_STATE])
        hn_ref[0, g * gw:(g + 1) * gw, :] = hg * tr[:, lt:lt + 1] + s


def _ssd_s_post_kernel(*refs, lt):
    z_refs = refs[:lt]
    yd_ref, ec_ref, yr_ref, ng_ref, o_ref = refs[lt:]
    nb = z_refs[0].shape[0]
    for i in range(lt):
        y = yd_ref[i] + ec_ref[i] * yr_ref[i]
        gated = y * _silu(z_refs[i][...])
        out = gated * lax.rsqrt(jnp.mean(gated * gated, axis=-1, keepdims=True) + LN_EPS) * ng_ref[...]
        o_ref[i * nb:(i + 1) * nb, :] = out.astype(o_ref.dtype)


def ssd_sample(proj, dtp, tp, nb, lt, conv_state, h0, conv_w, conv_b, dt_bias, a_log, d_skip, norm_g):
    row0 = tp // nb
    cs = jnp.transpose(conv_state, (1, 0, 2))
    one = lambda i: (0, 0)
    one3 = lambda i: (0, 0, 0)

    def rows(t, width, col):
        return pl.BlockSpec((nb, width), lambda i: (row0 + t, col // width))

    in_specs = ([rows(t, SSM_INNER, COL_XS) for t in range(lt)]
                + [rows(t, 2 * SSM_BC, COL_BC) for t in range(lt)]
                + [pl.BlockSpec((nb, LANES), lambda i, t=t: (row0 + t, 0)) for t in range(lt)]
                + [pl.BlockSpec((CONV_K - 1, nb, SSM_INNER), one3),
                   pl.BlockSpec((CONV_K - 1, nb, 2 * SSM_BC), one3),
                   pl.BlockSpec((CONV_K, SSM_INNER), one),
                   pl.BlockSpec((CONV_K, 2 * SSM_BC), one),
                   pl.BlockSpec((1, SSM_INNER), one),
                   pl.BlockSpec((1, 2 * SSM_BC), one),
                   pl.BlockSpec((1, SSM_INNER), one),
                   pl.BlockSpec((1, SSM_INNER), one),
                   pl.BlockSpec((1, SSM_INNER), one),
                   pl.BlockSpec((LANES, SSM_INNER), one),
                   pl.BlockSpec((SSM_BC, SSM_INNER), one)])
    f3 = lambda w: jax.ShapeDtypeStruct((lt, nb, w), F32)
    c_a, b_a, xw_a, dec_a, yd_a, ec_a = pl.pallas_call(
        functools.partial(_ssd_s_pre_kernel, lt=lt),
        grid=(1,),
        in_specs=in_specs,
        out_specs=[pl.BlockSpec((lt, nb, SSM_BC), one3), pl.BlockSpec((lt, nb, SSM_BC), one3),
                   pl.BlockSpec((lt, nb, SSM_INNER), one3), pl.BlockSpec((nb, SSM_INNER), one),
                   pl.BlockSpec((lt, nb, SSM_INNER), one3), pl.BlockSpec((lt, nb, SSM_INNER), one3)],
        out_shape=[f3(SSM_BC), f3(SSM_BC), f3(SSM_INNER), jax.ShapeDtypeStruct((nb, SSM_INNER), F32),
                   f3(SSM_INNER), f3(SSM_INNER)],
        compiler_params=_cparams(1),
        name="ssd_sample_pre",
    )(*([proj] * (2 * lt)), *([dtp] * lt), cs[:, :, :SSM_INNER], cs[:, :, SSM_INNER:],
      conv_w[:, :SSM_INNER], conv_w[:, SSM_INNER:], conv_b[:SSM_INNER].reshape(1, -1),
      conv_b[SSM_INNER:].reshape(1, -1), _expand_heads(dt_bias), _expand_heads(a_log),
      _expand_heads(d_skip), _head_expand_matrix(), _group_expand_matrix())

    h0r = h0.reshape(nb, SSM_INNER, SSM_STATE)
    hn, yr = pl.pallas_call(
        functools.partial(_ssd_s_state_kernel, lt=lt),
        grid=(nb,),
        in_specs=[pl.BlockSpec((lt, nb, SSM_BC), one3), pl.BlockSpec((lt, nb, SSM_BC), one3),
                  pl.BlockSpec((lt, nb, SSM_INNER), one3), pl.BlockSpec((nb, SSM_INNER), one),
                  pl.BlockSpec((1, SSM_INNER, SSM_STATE), lambda i: (i, 0, 0))],
        out_specs=[pl.BlockSpec((1, SSM_INNER, SSM_STATE), lambda i: (i, 0, 0)),
                   pl.BlockSpec((lt, nb, SSM_INNER), one3)],
        out_shape=[jax.ShapeDtypeStruct((nb, SSM_INNER, SSM_STATE), F32), f3(SSM_INNER)],
        compiler_params=_cparams(1),
        name="ssd_sample_state",
    )(c_a, b_a, xw_a, dec_a, h0r)

    ssm = pl.pallas_call(
        functools.partial(_ssd_s_post_kernel, lt=lt),
        grid=(1,),
        in_specs=([rows(t, SSM_INNER, COL_Z) for t in range(lt)]
                  + [pl.BlockSpec((lt, nb, SSM_INNER), one3)] * 3 + [pl.BlockSpec((1, SSM_INNER), one)]),
        out_specs=pl.BlockSpec((lt * nb, SSM_INNER), one),
        out_shape=jax.ShapeDtypeStruct((lt * nb, SSM_INNER), BF16),
        compiler_params=_cparams(1),
        name="ssd_sample_post",
    )(*([proj] * lt), yd_a, ec_a, yr, norm_g.reshape(1, SSM_INNER))
    return ssm, hn


def _xattn_kernel(q_ref, k_ref, v_ref, o_ref):
    q, k, v = q_ref[...], k_ref[0], v_ref[0]
    scale = MEM_HEAD_DIM ** -0.5
    for h in range(MEM_HEADS):
        sl = slice(h * MEM_HEAD_DIM, (h + 1) * MEM_HEAD_DIM)
        s = _dot_nt(q[:, sl].astype(BF16), k[:, sl].astype(BF16)) * scale
        m = jnp.max(s, axis=-1, keepdims=True)
        p = jnp.exp(s - m)
        den = jnp.sum(p, axis=-1, keepdims=True)
        o = _dot(p.astype(BF16), v[:, sl].astype(BF16)) / den
        o_ref[:, sl] = o.astype(o_ref.dtype)


def cross_attention(q, k, v, *, n_seq, seq, tq, name):
    w = q.shape[1]
    m = k.shape[1]
    nq = seq // tq
    return pl.pallas_call(
        _xattn_kernel,
        grid=(n_seq, nq),
        in_specs=[pl.BlockSpec((tq, w), lambda i, n: (i * nq + n, 0)),
                  pl.BlockSpec((1, m, w), lambda i, n: (i, 0, 0)),
                  pl.BlockSpec((1, m, w), lambda i, n: (i, 0, 0))],
        out_specs=pl.BlockSpec((tq, w), lambda i, n: (i * nq + n, 0)),
        out_shape=jax.ShapeDtypeStruct((n_seq * seq, w), F32),
        compiler_params=_cparams(2),
        name=name,
    )(q, k, v)


def _router_kernel(h_ref, w_ref, b_ref, o_ref):
    x = h_ref[...]
    w = w_ref[...]
    x_hi = x.astype(BF16)
    x_lo = (x - x_hi.astype(F32)).astype(BF16)
    w_hi = w.astype(BF16)
    w_lo = (w - w_hi.astype(F32)).astype(BF16)
    logits = _dot(x_hi, w_hi) + _dot(x_hi, w_lo) + _dot(x_lo, w_hi) + b_ref[...]
    lane = lax.broadcasted_iota(jnp.int32, logits.shape, 1)
    lane_f = lane.astype(F32)
    big = float(LANES)
    is_g = lane < N_EGROUPS
    lg = jnp.where(is_g, logits, NEG_BIG)
    mg = jnp.max(lg, axis=-1, keepdims=True)
    zg = jnp.sum(jnp.where(is_g, jnp.exp(lg - mg), 0.0), axis=-1, keepdims=True)
    gi = jnp.min(jnp.where(is_g & (lg == mg), lane_f, big), axis=-1, keepdims=True)
    gw = 1.0 / zg
    lo = N_EGROUPS + gi * EXPERTS_PER_GROUP
    is_e = (lane_f >= lo) & (lane_f < lo + EXPERTS_PER_GROUP)
    le = jnp.where(is_e, logits, NEG_BIG)
    me = jnp.max(le, axis=-1, keepdims=True)
    ee = jnp.where(is_e, jnp.exp(le - me), 0.0)
    pe = ee / jnp.sum(ee, axis=-1, keepdims=True)
    pe = jnp.where(is_e, pe, -1.0)
    p1 = jnp.max(pe, axis=-1, keepdims=True)
    i1 = jnp.min(jnp.where(pe == p1, lane_f, big), axis=-1, keepdims=True)
    pe2 = jnp.where(lane_f == i1, -1.0, pe)
    p2 = jnp.max(pe2, axis=-1, keepdims=True)
    i2 = jnp.min(jnp.where(pe2 == p2, lane_f, big), axis=-1, keepdims=True)
    tot = p1 + p2
    out = jnp.where(lane == 0, i1 - N_EGROUPS,
                    jnp.where(lane == 1, i2 - N_EGROUPS,
                              jnp.where(lane == 2, gw * (p1 / tot),
                                        jnp.where(lane == 3, gw * (p2 / tot), 0.0))))
    o_ref[...] = out


def moe_router(h, w_rg, b_rg, w_re, b_re):
    t, d = h.shape
    tm = _pick(t, (256, 128, 64, 32, 16, 8))
    npad = LANES - N_EGROUPS - N_EXPERTS
    w = jnp.concatenate([w_rg, w_re, jnp.zeros((d, npad), F32)], axis=1)
    b = jnp.concatenate([b_rg, b_re, jnp.zeros((npad,), F32)]).reshape(1, LANES)
    return pl.pallas_call(
        _router_kernel,
        grid=(t // tm,),
        in_specs=[pl.BlockSpec((tm, d), lambda i: (i, 0)),
                  pl.BlockSpec((d, LANES), lambda i: (0, 0)),
                  pl.BlockSpec((1, LANES), lambda i: (0, 0))],
        out_specs=pl.BlockSpec((tm, LANES), lambda i: (i, 0)),
        out_shape=jax.ShapeDtypeStruct((t, LANES), F32),
        compiler_params=_cparams(1),
        name="moe_router",
    )(h, w, b)


def _row_copy(src_hbm, dst, src_row, dst_row, sem):
    return pltpu.make_async_copy(src_hbm.at[pl.ds(src_row, 1)], dst.at[pl.ds(dst_row, 1)], sem)


def _moe_gather_kernel(tok_ref, h_ref, o_ref, sem, *, tm):
    base = pl.program_id(0) * tm

    def start(r, carry):
        _row_copy(h_ref, o_ref, tok_ref[0, 0, r], base + r, sem).start()
        return carry

    def wait(r, carry):
        _row_copy(h_ref, o_ref, tok_ref[0, 0, r], base + r, sem).wait()
        return carry

    lax.fori_loop(0, tm, start, 0)
    lax.fori_loop(0, tm, wait, 0)


def moe_gather(h, row_token, tm):
    r_total = row_token.shape[0]
    d = h.shape[1]
    nblk = r_total // tm
    return pl.pallas_call(
        functools.partial(_moe_gather_kernel, tm=tm),
        grid=(nblk,),
        in_specs=[pl.BlockSpec((1, 1, tm), lambda i: (i, 0, 0), memory_space=pltpu.SMEM),
                  pl.BlockSpec(memory_space=pl.ANY)],
        out_specs=pl.BlockSpec(memory_space=pl.ANY),
        out_shape=jax.ShapeDtypeStruct((r_total, d), h.dtype),
        scratch_shapes=[pltpu.SemaphoreType.DMA(())],
        compiler_params=_cparams(1),
        name="moe_gather",
    )(row_token.reshape(nblk, 1, tm), h)


def _moe_up_kernel(e_ref, f_ref, t_ref, to_ref, fo_ref, v_ref, x_ref, wg_ref, wu_ref, o_ref):
    s = pl.program_id(0)

    @pl.when(v_ref[s] > 0)
    def _():
        x = x_ref[...].astype(BF16)
        a = _dot(x, wg_ref[...].astype(BF16))
        u = _dot(x, wu_ref[...].astype(BF16))
        o_ref[...] = (_silu(a) * u).astype(o_ref.dtype)

    @pl.when(v_ref[s] == 0)
    def _():
        o_ref[...] = jnp.zeros_like(o_ref)


def moe_up(x_sorted, w_gate, w_up, layer, plan, tm, fchunk):
    r_total, d = x_sorted.shape
    ff = w_gate.shape[-1]
    n_steps = plan["step_e"].shape[0]
    grid_spec = pltpu.PrefetchScalarGridSpec(
        num_scalar_prefetch=6,
        grid=(n_steps,),
        in_specs=[
            pl.BlockSpec((tm, d), lambda s, e, f, t, to, fo, v: (t[s], 0)),
            pl.BlockSpec((None, None, d, fchunk), lambda s, e, f, t, to, fo, v: (layer, e[s], 0, f[s])),
            pl.BlockSpec((None, None, d, fchunk), lambda s, e, f, t, to, fo, v: (layer, e[s], 0, f[s])),
        ],
        out_specs=pl.BlockSpec((tm, fchunk), lambda s, e, f, t, to, fo, v: (to[s], fo[s])),
    )
    return pl.pallas_call(
        _moe_up_kernel,
        grid_spec=grid_spec,
        out_shape=jax.ShapeDtypeStruct((r_total, ff), BF16),
        compiler_params=_cparams(1),
        name="moe_up",
    )(plan["step_e"], plan["step_f"], plan["step_t"], plan["step_to"], plan["step_fo"], plan["step_v"],
      x_sorted, w_gate, w_up)


def _moe_down_kernel(e_ref, v_ref, x_ref, w_ref, o_ref):
    s = pl.program_id(0)

    @pl.when(v_ref[s] > 0)
    def _():
        o_ref[...] = _dot(x_ref[...], w_ref[...].astype(BF16))

    @pl.when(v_ref[s] == 0)
    def _():
        o_ref[...] = jnp.zeros_like(o_ref)


def moe_down(hid, w_down, layer, tile_e, tile_v, tm):
    r_total, ff = hid.shape
    d = w_down.shape[-1]
    n_tiles = r_total // tm
    grid_spec = pltpu.PrefetchScalarGridSpec(
        num_scalar_prefetch=2,
        grid=(n_tiles,),
        in_specs=[
            pl.BlockSpec((tm, ff), lambda s, e, v: (v[n_tiles + s], 0)),
            pl.BlockSpec((None, None, ff, d), lambda s, e, v: (layer, e[s], 0, 0)),
        ],
        out_specs=pl.BlockSpec((tm, d), lambda s, e, v: (s, 0)),
    )
    return pl.pallas_call(
        _moe_down_kernel,
        grid_spec=grid_spec,
        out_shape=jax.ShapeDtypeStruct((r_total, d), F32),
        compiler_params=_cparams(1),
        name="moe_down",
    )(tile_e, tile_v, hid, w_down)


def _moe_combine_kernel(pos_ref, y_ref, r_ref, h_ref, g_ref, b_ref, of_ref, ob_ref, ybuf, sem, *, tm, alpha):
    def start(i, carry):
        _row_copy(y_ref, ybuf.at[0], pos_ref[0, 0, 2 * i], i, sem).start()
        _row_copy(y_ref, ybuf.at[1], pos_ref[0, 0, 2 * i + 1], i, sem).start()
        return carry

    def wait(i, carry):
        _row_copy(y_ref, ybuf.at[0], pos_ref[0, 0, 2 * i], i, sem).wait()
        _row_copy(y_ref, ybuf.at[1], pos_ref[0, 0, 2 * i + 1], i, sem).wait()
        return carry

    lax.fori_loop(0, tm, start, 0)
    lax.fori_loop(0, tm, wait, 0)
    route = r_ref[...]
    ff = ybuf[0] * route[:, 2:3] + ybuf[1] * route[:, 3:4]
    h = _ln_rows(alpha * h_ref[...] + ff, g_ref[...], b_ref[...])
    of_ref[...] = h
    ob_ref[...] = h.astype(BF16)


def moe_combine(y_sorted, pos, route, h, g, b, *, alpha):
    t, d = h.shape
    tm = _pick(t, (256, 128, 64, 32, 16, 8))
    nblk = t // tm
    return pl.pallas_call(
        functools.partial(_moe_combine_kernel, tm=tm, alpha=alpha),
        grid=(nblk,),
        in_specs=[pl.BlockSpec((1, 1, 2 * tm), lambda i: (i, 0, 0), memory_space=pltpu.SMEM),
                  pl.BlockSpec(memory_space=pl.ANY),
                  pl.BlockSpec((tm, LANES), lambda i: (i, 0)),
                  pl.BlockSpec((tm, d), lambda i: (i, 0)),
                  pl.BlockSpec((1, d), lambda i: (0, 0)),
                  pl.BlockSpec((1, d), lambda i: (0, 0))],
        out_specs=[pl.BlockSpec((tm, d), lambda i: (i, 0)), pl.BlockSpec((tm, d), lambda i: (i, 0))],
        out_shape=[jax.ShapeDtypeStruct((t, d), F32), jax.ShapeDtypeStruct((t, d), BF16)],
        scratch_shapes=[pltpu.VMEM((2, tm, d), F32), pltpu.SemaphoreType.DMA(())],
        compiler_params=_cparams(1),
        name="moe_combine",
    )(pos.reshape(nblk, 1, 2 * tm), y_sorted, route, h, g.reshape(1, d), b.reshape(1, d))


def moe_plan(route, tm, n_f):
    t = route.shape[0]
    eid = route[:, :2].astype(jnp.int32).reshape(-1)
    onehot = (eid[:, None] == jnp.arange(N_EXPERTS, dtype=jnp.int32)[None, :]).astype(jnp.int32)
    csum = jnp.cumsum(onehot, axis=0)
    rank = jnp.sum((csum - onehot) * onehot, axis=1)
    counts = csum[-1]
    tiles_e = (counts + tm - 1) // tm
    tile_end = jnp.cumsum(tiles_e)
    tile_start = tile_end - tiles_e
    n_used = tile_end[-1]
    n_tiles = (2 * t + N_EXPERTS * (tm - 1)) // tm + 1
    r_total = n_tiles * tm
    dest = tile_start[eid] * tm + rank
    row_token = jnp.zeros((r_total,), jnp.int32).at[dest].set(jnp.arange(2 * t, dtype=jnp.int32) // 2)
    tile_ids = jnp.arange(n_tiles, dtype=jnp.int32)
    tile_clamped = jnp.minimum(tile_ids, n_used - 1)
    tile_e = jnp.searchsorted(tile_end, tile_clamped, side="right").astype(jnp.int32)
    tile_valid = (tile_ids < n_used).astype(jnp.int32)
    tile_v = jnp.concatenate([tile_valid, tile_clamped])
    n_steps = n_f * n_tiles
    sidx = jnp.arange(n_steps, dtype=jnp.int32)
    s_cl = jnp.minimum(sidx, n_f * n_used - 1)
    step_e = jnp.searchsorted(n_f * tile_end, s_cl, side="right").astype(jnp.int32)
    local = s_cl - n_f * tile_start[step_e]
    te = jnp.maximum(tiles_e[step_e], 1)
    step_f = (local // te).astype(jnp.int32)
    step_t = (tile_start[step_e] + local % te).astype(jnp.int32)
    valid = sidx < n_f * n_used
    step_v = valid.astype(jnp.int32)
    spare = sidx - n_f * n_used
    step_to = jnp.where(valid, step_t, n_used + spare // n_f).astype(jnp.int32)
    step_fo = jnp.where(valid, step_f, spare % n_f).astype(jnp.int32)
    return dict(row_token=row_token, pos=dest.astype(jnp.int32), tile_e=tile_e, tile_v=tile_v,
                step_e=step_e, step_f=step_f, step_t=step_t, step_to=step_to, step_fo=step_fo,
                step_v=step_v)


def hierarchical_moe_ln(hf, layer, w_rg, b_rg, w_re, b_re, w_gate, w_up, w_down, ln_g, ln_b, *, alpha):
    route = moe_router(hf, w_rg, b_rg, w_re, b_re)
    n_f = EXPERT_FF // MOE_FCHUNK
    plan = moe_plan(route, MOE_TM, n_f)
    x_sorted = moe_gather(hf, plan["row_token"], MOE_TM)
    hid = moe_up(x_sorted, w_gate, w_up, layer, plan, MOE_TM, MOE_FCHUNK)
    y_sorted = moe_down(hid, w_down, layer, plan["tile_e"], plan["tile_v"], MOE_TM)
    return moe_combine(y_sorted, plan["pos"], route, hf, ln_g, ln_b, alpha=alpha)


def _to_seq_major(x_tm, lt, nb, pad_to):
    w = x_tm.shape[1]
    x = jnp.transpose(x_tm.reshape(lt, nb, w), (1, 0, 2))
    x = jnp.pad(x, ((0, 0), (0, pad_to - lt), (0, 0)))
    return x.reshape(nb * pad_to, w)


def _to_time_major(x_sm, lt, nb, pad_to):
    w = x_sm.shape[1]
    x = x_sm.reshape(nb, pad_to, w)[:, :lt]
    return jnp.transpose(x, (1, 0, 2)).reshape(lt * nb, w)


def kernel(x_prompt, x_sample, mem_prompt, cache_swa_k, cache_swa_v, cache_mem_k, cache_mem_v, state_conv, state_ssm, ln_in_g, ln_in_b, w_in, attn_sinks, gm_ln_g, gm_ln_b, gm_ws, gm_bs, conv_w, conv_b, dt_bias, a_log, d_skip, ssm_norm_g, w_pa, w_pb, w_pc, w_o, ln1_g, ln1_b, w_cq, w_ck, w_cv, w_co, ln2_g, ln2_b, w_rg, b_rg, w_re, b_re, w_gate, w_up, w_down, ln3_g, ln3_b):
    bp, seq, d = x_prompt.shape
    nb, lt, _ = x_sample.shape
    depth = w_in.shape[0]
    mem_len = mem_prompt.shape[1]
    past_len = PAST_LEN
    wb = cache_swa_k.shape[2]
    assert wb == WINDOW and seq % CHUNK == 0 and lt <= SUBLANES
    tp, ts = bp * seq, nb * lt
    alpha = (2 * depth) ** 0.25
    qpad = SUBLANES

    xp = x_prompt.reshape(tp, d)
    xs = jnp.transpose(x_sample, (1, 0, 2)).reshape(ts, d)
    hf, hb = ln_in(xp, xs, ln_in_g, ln_in_b)
    cos_t, sin_t = rope_tables(tp, seq, ts, nb, past_len)
    mem_b = mem_prompt.reshape(bp * mem_len, d).astype(BF16)

    o_k, o_v, o_gu, o_gv = Q_W, Q_W + KV_W, Q_W + 2 * KV_W, Q_W + 2 * KV_W + GM_W
    o_z = o_gv + GM_W
    o_xbc = o_z + SSM_INNER
    o_dt = o_xbc + CONV_DIM
    o_gates = o_dt + SSM_HEADS

    outs = {k: [] for k in ("p_k", "p_v", "p_mk", "p_mv", "p_conv", "p_ssm", "p_gv",
                            "s_k", "s_v", "s_conv", "s_ssm", "s_gv")}
    n_qblk = seq // WINDOW
    for l in range(depth):
        wl = w_in[l]
        w_main = jnp.concatenate(
            [wl[:, :Q_W], wl[:, o_gu:o_gv], wl[:, o_gv:o_z], wl[:, o_z:o_xbc], wl[:, o_gates:],
             wl[:, o_xbc:o_dt], wl[:, o_k:o_v], wl[:, o_v:o_gu]], axis=1).astype(BF16)
        w_dt = jnp.pad(wl[:, o_dt:o_gates], ((0, 0), (0, LANES - SSM_HEADS))).astype(BF16)
        proj = matmul(hb, w_main, name="in_proj")
        dtp = matmul(hb, w_dt, name="dt_proj")
        q_rot, k_rot = rope_qk(proj, cos_t, sin_t)

        kcol, vcol = 0, COL_V // KV_W
        att_p = swa_attention(
            attn_sinks[l], q_rot, k_rot, k_rot, proj, proj,
            n_seq=bp, n_blk=n_qblk, qb=WINDOW, prev_from_block0=False,
            kp_map=lambda i, n: (jnp.maximum(i * n_qblk + n - 1, 0), kcol),
            kc_map=lambda i, n: (i * n_qblk + n, kcol),
            vp_map=lambda i, n: (jnp.maximum(i * n_qblk + n - 1, 0), vcol),
            vc_map=lambda i, n: (i * n_qblk + n, vcol),
            out_dtype=BF16, name="swa_prompt")
        k_s_tm = k_rot[tp:]
        v_s_tm = proj[tp:, COL_V:COL_V + KV_W]
        q_s = _to_seq_major(q_rot[tp:], lt, nb, qpad)
        k_s = _to_seq_major(k_s_tm, lt, nb, qpad)
        v_s = _to_seq_major(v_s_tm, lt, nb, qpad)
        ck = cache_swa_k[l].reshape(nb * wb, KV_W)
        cv = cache_swa_v[l].reshape(nb * wb, KV_W)
        att_s8 = swa_attention(
            attn_sinks[l], q_s, ck, k_s, cv, v_s,
            n_seq=nb, n_blk=1, qb=qpad, prev_from_block0=True,
            kp_map=lambda i, n: (i, 0), kc_map=lambda i, n: (i, 0),
            vp_map=lambda i, n: (i, 0), vc_map=lambda i, n: (i, 0),
            out_dtype=F32, name="swa_sample")
        att_s = _to_time_major(att_s8, lt, nb, qpad).astype(BF16)
        outs["p_k"].append(k_rot[:tp].reshape(bp, seq, N_KV_HEADS, HEAD_DIM)[:, seq - WINDOW:])
        outs["p_v"].append(proj[:tp, COL_V:COL_V + KV_W].reshape(bp, seq, N_KV_HEADS, HEAD_DIM)[:, seq - WINDOW:])
        k_new = jnp.transpose(k_s_tm.reshape(lt, nb, N_KV_HEADS, HEAD_DIM), (1, 0, 2, 3))
        v_new = jnp.transpose(v_s_tm.reshape(lt, nb, N_KV_HEADS, HEAD_DIM), (1, 0, 2, 3))
        outs["s_k"].append(jnp.concatenate([cache_swa_k[l], k_new], axis=1)[:, -wb:])
        outs["s_v"].append(jnp.concatenate([cache_swa_v[l], v_new], axis=1)[:, -wb:])

        gm_p, vg_last = gmlp_prompt(proj, bp, seq, gm_ws[l], gm_bs[l], gm_ln_g[l], gm_ln_b[l])
        gm_s, vg_s = gmlp_sample(proj, tp, nb, lt, gm_ws[l], gm_bs[l], gm_ln_g[l], gm_ln_b[l])
        outs["p_gv"].append(vg_last.reshape(bp, CHUNK, GM_GROUPS, GM_GROUP_DIM))
        outs["s_gv"].append(jnp.transpose(vg_s.reshape(lt, nb, GM_GROUPS, GM_GROUP_DIM), (1, 0, 2, 3)))

        dtt = jnp.transpose(dtp[:tp, :SSM_HEADS])
        y_p, st_p = ssd_prompt(proj, dtp, dtt, bp, seq, conv_w[l], conv_b[l], dt_bias[l], a_log[l],
                               d_skip[l], ssm_norm_g[l])
        ssm_s, st_s = ssd_sample(proj, dtp, tp, nb, lt, state_conv[l], state_ssm[l], conv_w[l], conv_b[l],
                                 dt_bias[l], a_log[l], d_skip[l], ssm_norm_g[l])
        xbc_p = proj[:tp, COL_XS:COL_XS + CONV_DIM].reshape(bp, seq, CONV_DIM)
        outs["p_conv"].append(xbc_p[:, seq - (CONV_K - 1):])
        xbc_s = jnp.transpose(proj[tp:, COL_XS:COL_XS + CONV_DIM].reshape(lt, nb, CONV_DIM), (1, 0, 2))
        outs["s_conv"].append(jnp.concatenate([state_conv[l], xbc_s], axis=1)[:, -(CONV_K - 1):])
        outs["p_ssm"].append(st_p.reshape(bp, SSM_HEADS, SSM_HEAD_DIM, SSM_STATE))
        outs["s_ssm"].append(st_s.reshape(nb, SSM_HEADS, SSM_HEAD_DIM, SSM_STATE))

        merged = gated_merge(att_p, att_s, gm_p, gm_s, y_p, ssm_s, w_pa[l].astype(BF16),
                             w_pb[l].astype(BF16), w_pc[l].astype(BF16), proj)
        h1f, h1b = matmul_ln(merged, w_o[l].astype(BF16), hf, ln1_g[l], ln1_b[l], alpha=alpha, name="out_proj_ln1")

        qc = matmul(h1b, w_cq[l].astype(BF16), name="xattn_q")
        pmk = matmul(mem_b, w_ck[l].astype(BF16), name="mem_k")
        pmv = matmul(mem_b, w_cv[l].astype(BF16), name="mem_v")
        outs["p_mk"].append(pmk.reshape(bp, mem_len, MEM_HEADS, MEM_HEAD_DIM))
        outs["p_mv"].append(pmv.reshape(bp, mem_len, MEM_HEADS, MEM_HEAD_DIM))
        tq = _pick(seq, (512, 256, 128))
        o_p = cross_attention(qc, pmk.reshape(bp, mem_len, MEM_W), pmv.reshape(bp, mem_len, MEM_W),
                              n_seq=bp, seq=seq, tq=tq, name="xattn_prompt")
        qc_s = _to_seq_major(qc[tp:], lt, nb, qpad)
        o_s8 = cross_attention(qc_s, cache_mem_k[l].reshape(nb, mem_len, MEM_W),
                               cache_mem_v[l].reshape(nb, mem_len, MEM_W),
                               n_seq=nb, seq=qpad, tq=qpad, name="xattn_sample")
        o_all = jnp.concatenate([o_p, _to_time_major(o_s8, lt, nb, qpad)], axis=0).astype(BF16)
        h2f, h2b = matmul_ln(o_all, w_co[l].astype(BF16), h1f, ln2_g[l], ln2_b[l], alpha=alpha, name="xattn_out_ln2")

        hf, hb = hierarchical_moe_ln(h2f, l, w_rg[l], b_rg[l], w_re[l], b_re[l], w_gate, w_up, w_down,
                                     ln3_g[l], ln3_b[l], alpha=alpha)

    y_prompt = hf[:tp].reshape(bp, seq, d)
    y_sample = jnp.transpose(hf[tp:].reshape(lt, nb, d), (1, 0, 2))
    st = lambda k: jnp.stack(outs[k])
    return (y_prompt, y_sample, st("p_k"), st("p_v"), st("p_mk"), st("p_mv"), st("p_conv"), st("p_ssm"),
            st("p_gv"), st("s_k"), st("s_v"), st("s_conv"), st("s_ssm"), st("s_gv"))
```

```python
import functools
import math

import numpy as np
import jax
import jax.numpy as jnp
from jax import lax
from jax.experimental import pallas as pl
from jax.experimental.pallas import tpu as pltpu

F32 = jnp.float32
BF16 = jnp.bfloat16

D_MODEL = 2048
N_HEADS = 32
N_KV_HEADS = 4
HEAD_DIM = 64
WINDOW = 128
PAST_LEN = 8192
ROPE_THETA = 10000.0
CHUNK = 128
GM_GROUPS = 16
GM_GROUP_DIM = 128
SSM_HEADS = 32
SSM_HEAD_DIM = 64
SSM_GROUPS = 4
SSM_STATE = 128
CONV_K = 4
MEM_HEADS = 4
MEM_HEAD_DIM = 128
N_EGROUPS = 4
EXPERTS_PER_GROUP = 8
N_EXPERTS = N_EGROUPS * EXPERTS_PER_GROUP
EXPERT_FF = D_MODEL // 2
Q_W = N_HEADS * HEAD_DIM
KV_W = N_KV_HEADS * HEAD_DIM
GM_W = GM_GROUPS * GM_GROUP_DIM
SSM_INNER = SSM_HEADS * SSM_HEAD_DIM
SSM_BC = SSM_GROUPS * SSM_STATE
CONV_DIM = SSM_INNER + 2 * SSM_BC
MEM_W = MEM_HEADS * MEM_HEAD_DIM
LN_EPS = 1e-5
NEG_BIG = -1e30

VMEM_LIMIT_BYTES = 52 * 1024 * 1024
LANES = 128
SUBLANES = 8

COL_Q = 0
COL_GU = 2048
COL_GV = 4096
COL_Z = 6144
COL_GATES = 8192
COL_XS = 14336
COL_BC = 16384
COL_K = 17408
COL_V = 17664
PROJ_W = 17920

MOE_TM = 256
MOE_FCHUNK = 512


def _cparams(n_grid):
    return pltpu.CompilerParams(
        dimension_semantics=("arbitrary",) * n_grid,
        vmem_limit_bytes=VMEM_LIMIT_BYTES,
    )


def _pick(n, prefs):
    for p in prefs:
        if n % p == 0:
            return p
    raise ValueError(f"no tile for {n} in {prefs}")


def _ln_rows(x, g, b):
    mu = jnp.mean(x, axis=-1, keepdims=True)
    xc = x - mu
    var = jnp.mean(xc * xc, axis=-1, keepdims=True)
    return xc * lax.rsqrt(var + LN_EPS) * g + b


def _sigmoid(x):
    return 1.0 / (1.0 + jnp.exp(-x))


def _silu(x):
    return x * _sigmoid(x)


def _softplus(x):
    return jnp.maximum(x, 0.0) + jnp.log1p(jnp.exp(-jnp.abs(x)))


def _gelu(x):
    return jax.nn.gelu(x, approximate=True)


def _split3(x):
    hi = x.astype(BF16)
    r1 = x - hi.astype(F32)
    mid = r1.astype(BF16)
    lo = (r1 - mid.astype(F32)).astype(BF16)
    return hi, mid, lo


def _dot(a, b):
    return jnp.dot(a, b, preferred_element_type=F32)


def _dot_nt(a, b):
    return lax.dot_general(a, b, (((1,), (1,)), ((), ())), preferred_element_type=F32)


def _exact_dot_left(pieces, m):
    acc = _dot(pieces[0], m)
    for p in pieces[1:]:
        acc = acc + _dot(p, m)
    return acc


def _ln_in_kernel(xp_ref, xs_ref, g_ref, b_ref, of_ref, ob_ref, *, n_p):
    i = pl.program_id(0)

    @pl.when(i < n_p)
    def _():
        y = _ln_rows(xp_ref[...], g_ref[...], b_ref[...])
        of_ref[...] = y
        ob_ref[...] = y.astype(BF16)

    @pl.when(i >= n_p)
    def _():
        y = _ln_rows(xs_ref[...], g_ref[...], b_ref[...])
        of_ref[...] = y
        ob_ref[...] = y.astype(BF16)


def ln_in(xp, xs, g, b):
    tp, d = xp.shape
    ts = xs.shape[0]
    tm = _pick(math.gcd(tp, ts), (256, 128, 64, 32, 16, 8))
    n_p, n_s = tp // tm, ts // tm
    t = tp + ts
    return pl.pallas_call(
        functools.partial(_ln_in_kernel, n_p=n_p),
        grid=(n_p + n_s,),
        in_specs=[
            pl.BlockSpec((tm, d), lambda i: (jnp.minimum(i, n_p - 1), 0)),
            pl.BlockSpec((tm, d), lambda i: (jnp.maximum(i - n_p, 0), 0)),
            pl.BlockSpec((1, d), lambda i: (0, 0)),
            pl.BlockSpec((1, d), lambda i: (0, 0)),
        ],
        out_specs=[
            pl.BlockSpec((tm, d), lambda i: (i, 0)),
            pl.BlockSpec((tm, d), lambda i: (i, 0)),
        ],
        out_shape=[jax.ShapeDtypeStruct((t, d), F32), jax.ShapeDtypeStruct((t, d), BF16)],
        compiler_params=_cparams(1),
        name="ln_in",
    )(xp, xs, g.reshape(1, d), b.reshape(1, d))


def _mm_kernel(x_ref, w_ref, o_ref):
    o_ref[...] = _dot(x_ref[...], w_ref[...]).astype(o_ref.dtype)


def matmul(x, w, *, out_dtype=F32, tm_prefs=(1088, 1024, 512, 256, 128, 64, 32, 16, 8),
           tn_prefs=(1280, 1024, 512, 256, 128), name="mm"):
    t, k = x.shape
    n = w.shape[1]
    tm = _pick(t, tm_prefs)
    tn = _pick(n, tn_prefs)
    return pl.pallas_call(
        _mm_kernel,
        grid=(n // tn, t // tm),
        in_specs=[
            pl.BlockSpec((tm, k), lambda j, i: (i, 0)),
            pl.BlockSpec((k, tn), lambda j, i: (0, j)),
        ],
        out_specs=pl.BlockSpec((tm, tn), lambda j, i: (i, j)),
        out_shape=jax.ShapeDtypeStruct((t, n), out_dtype),
        compiler_params=_cparams(2),
        name=name,
    )(x, w)


def _mm_ln_kernel(x_ref, w_ref, r_ref, g_ref, b_ref, of_ref, ob_ref, *, alpha):
    y = _dot(x_ref[...], w_ref[...])
    h = _ln_rows(alpha * r_ref[...] + y, g_ref[...], b_ref[...])
    of_ref[...] = h
    ob_ref[...] = h.astype(BF16)


def matmul_ln(x, w, res, g, b, *, alpha, name="mm_ln"):
    t, k = x.shape
    d = w.shape[1]
    tm = _pick(t, (256, 128, 64, 32, 16, 8))
    return pl.pallas_call(
        functools.partial(_mm_ln_kernel, alpha=alpha),
        grid=(t // tm,),
        in_specs=[
            pl.BlockSpec((tm, k), lambda i: (i, 0)),
            pl.BlockSpec((k, d), lambda i: (0, 0)),
            pl.BlockSpec((tm, d), lambda i: (i, 0)),
            pl.BlockSpec((1, d), lambda i: (0, 0)),
            pl.BlockSpec((1, d), lambda i: (0, 0)),
        ],
        out_specs=[
            pl.BlockSpec((tm, d), lambda i: (i, 0)),
            pl.BlockSpec((tm, d), lambda i: (i, 0)),
        ],
        out_shape=[jax.ShapeDtypeStruct((t, d), F32), jax.ShapeDtypeStruct((t, d), BF16)],
        compiler_params=_cparams(1),
        name=name,
    )(x, w, res, g.reshape(1, d), b.reshape(1, d))


def _merge_kernel(ap_ref, as_ref, bp_ref, bs_ref, cp_ref, cs_ref, wa_ref, wb_ref, wc_ref,
                  ga_ref, gb_ref, gc_ref, o_ref, *, n_p):
    i = pl.program_id(1)
    is_p = i < n_p
    xa = jnp.where(is_p, ap_ref[...], as_ref[...])
    xb = jnp.where(is_p, bp_ref[...], bs_ref[...])
    xc = jnp.where(is_p, cp_ref[...], cs_ref[...])
    acc = _sigmoid(ga_ref[...]) * _dot(xa, wa_ref[...])
    acc = acc + _sigmoid(gb_ref[...]) * _dot(xb, wb_ref[...])
    acc = acc + _sigmoid(gc_ref[...]) * _dot(xc, wc_ref[...])
    o_ref[...] = acc.astype(o_ref.dtype)


def gated_merge(att_p, att_s, gm_p, gm_s, ssm_p, ssm_s, w_pa, w_pb, w_pc, proj):
    tp, k = att_p.shape
    ts = att_s.shape[0]
    d = w_pa.shape[1]
    tm = _pick(math.gcd(tp, ts), (256, 128, 64, 32, 16, 8))
    tn = 512
    n_p, n_s = tp // tm, ts // tm
    gate_blk = COL_GATES // tn
    d_blk = d // tn

    def xp_spec():
        return pl.BlockSpec((tm, k), lambda j, i: (jnp.minimum(i, n_p - 1), 0))

    def xs_spec():
        return pl.BlockSpec((tm, k), lambda j, i: (jnp.maximum(i - n_p, 0), 0))

    def w_spec():
        return pl.BlockSpec((k, tn), lambda j, i: (0, j))

    def g_spec(which):
        return pl.BlockSpec((tm, tn), lambda j, i: (i, gate_blk + which * d_blk + j))

    return pl.pallas_call(
        functools.partial(_merge_kernel, n_p=n_p),
        grid=(d // tn, n_p + n_s),
        in_specs=[xp_spec(), xs_spec(), xp_spec(), xs_spec(), xp_spec(), xs_spec(),
                  w_spec(), w_spec(), w_spec(), g_spec(0), g_spec(1), g_spec(2)],
        out_specs=pl.BlockSpec((tm, tn), lambda j, i: (i, j)),
        out_shape=jax.ShapeDtypeStruct((tp + ts, d), BF16),
        compiler_params=_cparams(2),
        name="gated_merge",
    )(att_p, att_s, gm_p, gm_s, ssm_p, ssm_s, w_pa, w_pb, w_pc, proj, proj, proj)


def _rope_block(x, cos, sin_signed, first_half):
    outs = []
    for c in range(x.shape[1] // LANES):
        xc = x[:, c * LANES:(c + 1) * LANES]
        fwd = pltpu.roll(xc, LANES - HEAD_DIM // 2, axis=1)
        bwd = pltpu.roll(xc, HEAD_DIM // 2, axis=1)
        partner = jnp.where(first_half, fwd, bwd)
        outs.append(xc * cos + partner * sin_signed)
    return outs


def _rope_kernel(q_ref, k_ref, cos_ref, sin_ref, qo_ref, ko_ref):
    cos = cos_ref[...]
    sin_signed = sin_ref[...]
    lane = lax.broadcasted_iota(jnp.int32, cos.shape, 1)
    first_half = (lane % HEAD_DIM) < (HEAD_DIM // 2)
    for c, o in enumerate(_rope_block(q_ref[...], cos, sin_signed, first_half)):
        qo_ref[:, c * LANES:(c + 1) * LANES] = o
    for c, o in enumerate(_rope_block(k_ref[...], cos, sin_signed, first_half)):
        ko_ref[:, c * LANES:(c + 1) * LANES] = o


def rope_qk(proj, cos_t, sin_t):
    t = proj.shape[0]
    tm = _pick(t, (256, 128, 64, 32, 16, 8))
    return pl.pallas_call(
        _rope_kernel,
        grid=(t // tm,),
        in_specs=[
            pl.BlockSpec((tm, Q_W), lambda i: (i, COL_Q // Q_W)),
            pl.BlockSpec((tm, KV_W), lambda i: (i, COL_K // KV_W)),
            pl.BlockSpec((tm, LANES), lambda i: (i, 0)),
            pl.BlockSpec((tm, LANES), lambda i: (i, 0)),
        ],
        out_specs=[
            pl.BlockSpec((tm, Q_W), lambda i: (i, 0)),
            pl.BlockSpec((tm, KV_W), lambda i: (i, 0)),
        ],
        out_shape=[jax.ShapeDtypeStruct((t, Q_W), F32), jax.ShapeDtypeStruct((t, KV_W), F32)],
        compiler_params=_cparams(1),
        name="rope_qk",
    )(proj, proj, cos_t, sin_t)


def rope_tables(tp, seq, ts, nb, past_len):
    half = HEAD_DIM // 2
    inv = ROPE_THETA ** (-jnp.arange(half, dtype=F32) / half)
    pos_p = jnp.arange(tp, dtype=jnp.int32) % seq
    pos_s = past_len + jnp.arange(ts, dtype=jnp.int32) // nb
    pos = jnp.concatenate([pos_p, pos_s]).astype(F32)
    ang = pos[:, None] * inv[None, :]
    cos = jnp.tile(jnp.cos(ang), (1, LANES // half))
    sin = jnp.sin(ang)
    sin_signed = jnp.tile(jnp.concatenate([-sin, sin], axis=1), (1, LANES // HEAD_DIM))
    return cos, sin_signed


def _dup_head(slab, g):
    lane = lax.broadcasted_iota(jnp.int32, slab.shape, 1)
    rolled = pltpu.roll(slab, HEAD_DIM, axis=1)
    if g % 2 == 0:
        return jnp.where(lane < HEAD_DIM, slab, rolled)
    return jnp.where(lane < HEAD_DIM, rolled, slab)


def _swa_bias(qb, prev_ok):
    kw = WINDOW + qb
    ii = lax.broadcasted_iota(jnp.int32, (qb, kw), 0)
    jj = lax.broadcasted_iota(jnp.int32, (qb, kw), 1)
    ok_prev = (jj < WINDOW) & (jj > ii)
    if prev_ok is not True:
        ok_prev = ok_prev & prev_ok
    ok = ok_prev | ((jj >= WINDOW) & ((jj - WINDOW) <= ii))
    return jnp.where(ok, 0.0, NEG_BIG)


def _swa_group(q, kdup, vdup, bias, sink_ref, g, qb, store):
    rep = N_HEADS // N_KV_HEADS
    lane = lax.broadcasted_iota(jnp.int32, (qb, LANES), 1)
    scale = HEAD_DIM ** -0.5
    rows = []
    for r in range(rep):
        h = g * rep + r
        qc = q[:, (h // 2) * LANES:(h // 2 + 1) * LANES]
        keep = (lane < HEAD_DIM) if h % 2 == 0 else (lane >= HEAD_DIM)
        rows.append(jnp.where(keep, qc * scale, 0.0))
    s = _dot_nt(jnp.concatenate(rows, axis=0).astype(BF16), kdup)
    ps, dens = [], []
    for r in range(rep):
        sr = s[r * qb:(r + 1) * qb] + bias
        sink = sink_ref[g * rep + r]
        m = jnp.maximum(jnp.max(sr, axis=-1, keepdims=True), sink)
        p = jnp.exp(sr - m)
        dens.append(jnp.sum(p, axis=-1, keepdims=True) + jnp.exp(sink - m))
        ps.append(p)
    o = _dot(jnp.concatenate(ps, axis=0).astype(BF16), vdup)
    for c in range(rep // 2):
        oa = o[(2 * c) * qb:(2 * c + 1) * qb] / dens[2 * c]
        ob = o[(2 * c + 1) * qb:(2 * c + 2) * qb] / dens[2 * c + 1]
        store((g * rep // 2 + c) * LANES, jnp.where(lane < HEAD_DIM, oa, ob))


def _swa_kernel(sink_ref, q_ref, kp_ref, kc_ref, vp_ref, vc_ref, o_ref, *, qb, prev_from_block0):
    n = pl.program_id(1)
    q = q_ref[...]
    kp, kc, vp, vc = kp_ref[...], kc_ref[...], vp_ref[...], vc_ref[...]
    bias = _swa_bias(qb, True if prev_from_block0 else (n > 0))

    def store(col, val):
        o_ref[:, col:col + LANES] = val.astype(o_ref.dtype)

    for g in range(N_KV_HEADS):
        sl = slice((g // 2) * LANES, (g // 2 + 1) * LANES)
        kdup = _dup_head(jnp.concatenate([kp[:, sl], kc[:, sl]], axis=0), g).astype(BF16)
        vdup = _dup_head(jnp.concatenate([vp[:, sl], vc[:, sl]], axis=0), g).astype(BF16)
        _swa_group(q, kdup, vdup, bias, sink_ref, g, qb, store)


def _swa_cache_kernel(*refs, qb, bseq):
    sink_ref, q_ref, kc_ref, vc_ref, kp_ref, vp_ref, o_ref = refs
    bias = _swa_bias(qb, True)
    for b in range(bseq):
        rows = slice(b * qb, (b + 1) * qb)
        q = q_ref[rows, :]

        def store(col, val, rows=rows):
            o_ref[rows, col:col + LANES] = val.astype(o_ref.dtype)

        for g in range(N_KV_HEADS):
            hs = slice(g * HEAD_DIM, (g + 1) * HEAD_DIM)
            kcat = jnp.concatenate([kp_ref[b, :, g, :], kc_ref[rows, hs]], axis=0)
            vcat = jnp.concatenate([vp_ref[b, :, g, :], vc_ref[rows, hs]], axis=0)
            kdup = jnp.concatenate([kcat, kcat], axis=1).astype(BF16)
            vdup = jnp.concatenate([vcat, vcat], axis=1).astype(BF16)
            _swa_group(q, kdup, vdup, bias, sink_ref, g, qb, store)


def swa_attention_cached(sinks, q, kc, vc, cache_k, cache_v, *, layer, n_seq, qb, bseq, name):
    cache_spec = pl.BlockSpec((None, bseq, WINDOW, N_KV_HEADS, HEAD_DIM), lambda i: (layer, i, 0, 0, 0))
    rows = bseq * qb
    return pl.pallas_call(
        functools.partial(_swa_cache_kernel, qb=qb, bseq=bseq),
        grid=(n_seq // bseq,),
        in_specs=[pl.BlockSpec(memory_space=pltpu.SMEM),
                  pl.BlockSpec((rows, Q_W), lambda i: (i, 0)),
                  pl.BlockSpec((rows, KV_W), lambda i: (i, 0)),
                  pl.BlockSpec((rows, KV_W), lambda i: (i, 0)),
                  cache_spec, cache_spec],
        out_specs=pl.BlockSpec((rows, Q_W), lambda i: (i, 0)),
        out_shape=jax.ShapeDtypeStruct((n_seq * qb, Q_W), F32),
        compiler_params=_cparams(1),
        name=name,
    )(sinks, q, kc, vc, cache_k, cache_v)


def swa_attention(sinks, q, kp, kc, vp, vc, *, n_seq, n_blk, qb, prev_from_block0,
                  kp_map, kc_map, vp_map, vc_map, out_dtype, name):
    return pl.pallas_call(
        functools.partial(_swa_kernel, qb=qb, prev_from_block0=prev_from_block0),
        grid=(n_seq, n_blk),
        in_specs=[
            pl.BlockSpec(memory_space=pltpu.SMEM),
            pl.BlockSpec((qb, Q_W), lambda i, n: (i * n_blk + n, 0)),
            pl.BlockSpec((WINDOW, KV_W), kp_map),
            pl.BlockSpec((qb, KV_W), kc_map),
            pl.BlockSpec((WINDOW, KV_W), vp_map),
            pl.BlockSpec((qb, KV_W), vc_map),
        ],
        out_specs=pl.BlockSpec((qb, Q_W), lambda i, n: (i * n_blk + n, 0)),
        out_shape=jax.ShapeDtypeStruct((n_seq * n_blk * qb, Q_W), out_dtype),
        compiler_params=_cparams(2),
        name=name,
    )(sinks, q, kp, kc, vp, vc)


def _gmlp_p_kernel(gu_ref, gv_ref, ws_ref, bst_ref, lg_ref, lb_ref, gm_ref, vg_ref, *, n_chunks):
    n = pl.program_id(1)
    vg = _ln_rows(_gelu(gv_ref[...]), lg_ref[...], lb_ref[...])
    gu = gu_ref[...]
    ri = lax.broadcasted_iota(jnp.int32, (CHUNK, CHUNK), 0)
    ci = lax.broadcasted_iota(jnp.int32, (CHUNK, CHUNK), 1)
    tril = ri >= ci
    bst = bst_ref[...]
    for g in range(GM_GROUPS):
        sl = slice(g * GM_GROUP_DIM, (g + 1) * GM_GROUP_DIM)
        w = jnp.where(tril, ws_ref[g], 0.0).astype(BF16)
        s = _dot(w, vg[:, sl].astype(BF16)) + bst[:, g:g + 1]
        gm_ref[:, sl] = (_gelu(gu[:, sl]) * s).astype(gm_ref.dtype)

    @pl.when(n == n_chunks - 1)
    def _():
        vg_ref[...] = vg


def gmlp_prompt(proj, n_seq, seq, ws, bs, ln_g, ln_b):
    nc = seq // CHUNK
    return pl.pallas_call(
        functools.partial(_gmlp_p_kernel, n_chunks=nc),
        grid=(n_seq, nc),
        in_specs=[
            pl.BlockSpec((CHUNK, GM_W), lambda i, n: (i * nc + n, COL_GU // GM_W)),
            pl.BlockSpec((CHUNK, GM_W), lambda i, n: (i * nc + n, COL_GV // GM_W)),
            pl.BlockSpec((GM_GROUPS, CHUNK, CHUNK), lambda i, n: (0, 0, 0)),
            pl.BlockSpec((CHUNK, GM_GROUPS), lambda i, n: (0, 0)),
            pl.BlockSpec((1, GM_W), lambda i, n: (0, 0)),
            pl.BlockSpec((1, GM_W), lambda i, n: (0, 0)),
        ],
        out_specs=[
            pl.BlockSpec((CHUNK, GM_W), lambda i, n: (i * nc + n, 0)),
            pl.BlockSpec((CHUNK, GM_W), lambda i, n: (i, 0)),
        ],
        out_shape=[jax.ShapeDtypeStruct((n_seq * seq, GM_W), BF16),
                   jax.ShapeDtypeStruct((n_seq * CHUNK, GM_W), F32)],
        compiler_params=_cparams(2),
        name="gmlp_prompt",
    )(proj, proj, ws, bs.T, ln_g.reshape(1, GM_W), ln_b.reshape(1, GM_W))


def _gmlp_s_kernel(*refs, lt):
    gu_refs = refs[:lt]
    gv_refs = refs[lt:2 * lt]
    wrow_ref, brow_ref, lg_ref, lb_ref, gm_ref, vg_ref = refs[2 * lt:]
    nb = gu_refs[0].shape[0]
    vgs = [_ln_rows(_gelu(gv_refs[t][...]), lg_ref[...], lb_ref[...]) for t in range(lt)]
    for i in range(lt):
        s = brow_ref[i:i + 1, :]
        for j in range(i + 1):
            s = s + wrow_ref[i * lt + j:i * lt + j + 1, :] * vgs[j]
        gm_ref[i * nb:(i + 1) * nb, :] = (_gelu(gu_refs[i][...]) * s).astype(gm_ref.dtype)
        vg_ref[i * nb:(i + 1) * nb, :] = vgs[i]


def gmlp_sample(proj, tp, nb, lt, ws, bs, ln_g, ln_b):
    w_small = ws[:, :lt, :lt]
    wrow = jnp.repeat(jnp.transpose(w_small, (1, 2, 0)).reshape(lt * lt, GM_GROUPS), GM_GROUP_DIM, axis=1)
    brow = jnp.repeat(bs[:, :lt].T, GM_GROUP_DIM, axis=1)
    row0 = tp // nb

    def spec(t, col):
        return pl.BlockSpec((nb, GM_W), lambda i: (row0 + t, col // GM_W))

    in_specs = [spec(t, COL_GU) for t in range(lt)] + [spec(t, COL_GV) for t in range(lt)] + [
        pl.BlockSpec((lt * lt, GM_W), lambda i: (0, 0)),
        pl.BlockSpec((lt, GM_W), lambda i: (0, 0)),
        pl.BlockSpec((1, GM_W), lambda i: (0, 0)),
        pl.BlockSpec((1, GM_W), lambda i: (0, 0)),
    ]
    return pl.pallas_call(
        functools.partial(_gmlp_s_kernel, lt=lt),
        grid=(1,),
        in_specs=in_specs,
        out_specs=[pl.BlockSpec((lt * nb, GM_W), lambda i: (0, 0)),
                   pl.BlockSpec((lt * nb, GM_W), lambda i: (0, 0))],
        out_shape=[jax.ShapeDtypeStruct((lt * nb, GM_W), BF16),
                   jax.ShapeDtypeStruct((lt * nb, GM_W), F32)],
        compiler_params=_cparams(1),
        name="gmlp_sample",
    )(*([proj] * (2 * lt)), wrow, brow, ln_g.reshape(1, GM_W), ln_b.reshape(1, GM_W))


def _conv_silu(cur, prev8, w, bias):
    q = cur.shape[0]
    up = jnp.concatenate([prev8, cur], axis=0)
    acc = bias + up[SUBLANES:SUBLANES + q] * w[CONV_K - 1:CONV_K]
    for j in range(CONV_K - 1):
        off = SUBLANES - (CONV_K - 1) + j
        acc = acc + up[off:off + q] * w[j:j + 1]
    return _silu(acc)


def _ssd_p_kernel(z_ref, xs_ref, bc_ref, xsp_ref, bcp_ref, dt_ref, dtt_ref,
                  cwx_ref, cwbc_ref, cbx_ref, cbbc_ref, dtbe_ref, dtbc_ref, aloge_ref, alogc_ref,
                  dske_ref, ng_ref, e_ref, y_ref, st_ref, s_scr, *, n_chunks):
    c = pl.program_id(1)
    q = CHUNK
    rep = SSM_HEADS // SSM_GROUPS
    gw = rep * SSM_HEAD_DIM

    @pl.when(c == 0)
    def _():
        s_scr[...] = jnp.zeros_like(s_scr)

    has_prev = (c > 0).astype(F32)
    xs = _conv_silu(xs_ref[...], xsp_ref[...] * has_prev, cwx_ref[...], cbx_ref[...])
    bcm = _conv_silu(bc_ref[...], bcp_ref[...] * has_prev, cwbc_ref[...], cbbc_ref[...])

    ri = lax.broadcasted_iota(jnp.int32, (q, q), 0)
    ci = lax.broadcasted_iota(jnp.int32, (q, q), 1)
    tril = ri >= ci
    ones_tril = jnp.where(tril, 1.0, 0.0).astype(BF16)
    ones_triu = jnp.where(ri <= ci, 1.0, 0.0).astype(BF16)

    dt_e = _softplus(_exact_dot_left(_split3(dt_ref[...]), e_ref[...]) + dtbe_ref[...])
    a_e = dt_e * (-jnp.exp(aloge_ref[...]))
    a_hi, a_mid, a_lo = _split3(a_e)
    cum_e = _dot(ones_tril, a_hi) + _dot(ones_tril, a_mid) + _dot(ones_tril, a_lo)
    dt_t = _softplus(dtt_ref[...] + dtbc_ref[...])
    a_t = dt_t * (-jnp.exp(alogc_ref[...]))
    cum_t = _exact_dot_left(_split3(a_t), ones_triu)

    xdt = xs * dt_e
    cum_last = cum_e[q - 1:q, :]
    lane = lax.broadcasted_iota(jnp.int32, (q, LANES), 1)
    z = z_ref[...]
    ys = []
    for g in range(SSM_GROUPS):
        gs = slice(g * gw, (g + 1) * gw)
        bg = bcm[:, g * SSM_STATE:(g + 1) * SSM_STATE]
        cg = bcm[:, SSM_BC + g * SSM_STATE:SSM_BC + (g + 1) * SSM_STATE]
        bg16, cg16 = bg.astype(BF16), cg.astype(BF16)
        cb = _dot_nt(cg16, bg16)
        ydiag = []
        for pr in range(rep // 2):
            ms = []
            for hh in (2 * pr, 2 * pr + 1):
                h = g * rep + hh
                col = cum_e[:, h * SSM_HEAD_DIM:h * SSM_HEAD_DIM + 1]
                row = cum_t[h:h + 1, :]
                seg = jnp.where(tril, col - row, NEG_BIG)
                ms.append(cb * jnp.exp(seg))
            lhs = jnp.concatenate(ms, axis=1).astype(BF16)
            xslab = xdt[:, (g * rep + 2 * pr) * SSM_HEAD_DIM:(g * rep + 2 * pr + 2) * SSM_HEAD_DIM]
            xbd = jnp.concatenate([jnp.where(lane < SSM_HEAD_DIM, xslab, 0.0),
                                   jnp.where(lane >= SSM_HEAD_DIM, xslab, 0.0)], axis=0).astype(BF16)
            ydiag.append(_dot(lhs, xbd))
        ydiag = jnp.concatenate(ydiag, axis=1)
        s_old = s_scr[g]
        yoff = _dot(cg16, s_old.astype(BF16)) * jnp.exp(cum_e[:, gs])
        ys.append(ydiag + yoff)
        xw = xdt[:, gs] * jnp.exp(cum_last[:, gs] - cum_e[:, gs])
        s_scr[g] = s_old * jnp.exp(cum_last[:, gs]) + _dot(bg.T.astype(BF16), xw.astype(BF16))
    y = jnp.concatenate(ys, axis=1) + dske_ref[...] * xs
    gated = y * _silu(z)
    out = gated * lax.rsqrt(jnp.mean(gated * gated, axis=-1, keepdims=True) + LN_EPS) * ng_ref[...]
    y_ref[...] = out.astype(y_ref.dtype)

    @pl.when(c == n_chunks - 1)
    def _():
        for g in range(SSM_GROUPS):
            st_ref[0, g * gw:(g + 1) * gw, :] = s_scr[g].T


def _expand_heads(v):
    return jnp.repeat(v.astype(F32), SSM_HEAD_DIM).reshape(1, SSM_INNER)


def _head_expand_matrix():
    e = np.zeros((LANES, SSM_INNER), np.float32)
    for h in range(SSM_HEADS):
        e[h, h * SSM_HEAD_DIM:(h + 1) * SSM_HEAD_DIM] = 1.0
    return jnp.asarray(e, BF16)


def ssd_prompt(proj, dtp, dtt, n_seq, seq, conv_w, conv_b, dt_bias, a_log, d_skip, norm_g):
    b = n_seq
    nc = seq // CHUNK
    blk8 = CHUNK // SUBLANES
    const2 = lambda i, c: (0, 0)
    col_bc = COL_BC // (2 * SSM_BC)
    col_xs = COL_XS // SSM_INNER
    col_z = COL_Z // SSM_INNER
    prev_map_x = lambda i, c: (jnp.maximum((i * nc + c) * blk8 - 1, 0), col_xs)
    prev_map_bc = lambda i, c: (jnp.maximum((i * nc + c) * blk8 - 1, 0), col_bc)
    args = (
        proj, proj, proj, proj, proj, dtp, dtt,
        conv_w[:, :SSM_INNER], conv_w[:, SSM_INNER:], conv_b[:SSM_INNER].reshape(1, -1),
        conv_b[SSM_INNER:].reshape(1, -1),
        _expand_heads(dt_bias), jnp.broadcast_to(dt_bias.astype(F32)[:, None], (SSM_HEADS, CHUNK)),
        _expand_heads(a_log), jnp.broadcast_to(a_log.astype(F32)[:, None], (SSM_HEADS, CHUNK)),
        _expand_heads(d_skip), norm_g.reshape(1, SSM_INNER), _head_expand_matrix(),
    )
    in_specs = [
        pl.BlockSpec((CHUNK, SSM_INNER), lambda i, c: (i * nc + c, col_z)),
        pl.BlockSpec((CHUNK, SSM_INNER), lambda i, c: (i * nc + c, col_xs)),
        pl.BlockSpec((CHUNK, 2 * SSM_BC), lambda i, c: (i * nc + c, col_bc)),
        pl.BlockSpec((SUBLANES, SSM_INNER), prev_map_x),
        pl.BlockSpec((SUBLANES, 2 * SSM_BC), prev_map_bc),
        pl.BlockSpec((CHUNK, LANES), lambda i, c: (i * nc + c, 0)),
        pl.BlockSpec((SSM_HEADS, CHUNK), lambda i, c: (0, i * nc + c)),
        pl.BlockSpec((CONV_K, SSM_INNER), const2),
        pl.BlockSpec((CONV_K, 2 * SSM_BC), const2),
        pl.BlockSpec((1, SSM_INNER), const2),
        pl.BlockSpec((1, 2 * SSM_BC), const2),
        pl.BlockSpec((1, SSM_INNER), const2),
        pl.BlockSpec((SSM_HEADS, CHUNK), const2),
        pl.BlockSpec((1, SSM_INNER), const2),
        pl.BlockSpec((SSM_HEADS, CHUNK), const2),
        pl.BlockSpec((1, SSM_INNER), const2),
        pl.BlockSpec((1, SSM_INNER), const2),
        pl.BlockSpec((LANES, SSM_INNER), const2),
    ]
    return pl.pallas_call(
        functools.partial(_ssd_p_kernel, n_chunks=nc),
        grid=(b, nc),
        in_specs=in_specs,
        out_specs=[
            pl.BlockSpec((CHUNK, SSM_INNER), lambda i, c: (i * nc + c, 0)),
            pl.BlockSpec((1, SSM_INNER, SSM_STATE), lambda i, c: (i, 0, 0)),
        ],
        out_shape=[jax.ShapeDtypeStruct((b * seq, SSM_INNER), BF16),
                   jax.ShapeDtypeStruct((b, SSM_INNER, SSM_STATE), F32)],
        scratch_shapes=[pltpu.VMEM((SSM_GROUPS, SSM_STATE, SSM_INNER // SSM_GROUPS), F32)],
        compiler_params=_cparams(2),
        name="ssd_prompt",
    )(*args)


def _group_expand_matrix():
    gw = SSM_INNER // SSM_GROUPS
    m = np.zeros((SSM_BC, SSM_INNER), np.float32)
    for g in range(SSM_GROUPS):
        m[g * SSM_STATE:(g + 1) * SSM_STATE, g * gw:(g + 1) * gw] = 1.0
    return jnp.asarray(m, BF16)


def _ssd_s_pre_kernel(*refs, lt):
    xs_refs = refs[:lt]
    bc_refs = refs[lt:2 * lt]
    dt_refs = refs[2 * lt:3 * lt]
    (cx_ref, cbc_ref, cwx_ref, cwbc_ref, cbx_ref, cbbc_ref, dtbe_ref, aloge_ref, dske_ref, e_ref,
     gmat_ref, c_ref, b_ref, xw_ref, dec_ref, yd_ref, ec_ref) = refs[3 * lt:]
    nprev = CONV_K - 1
    ux = [cx_ref[j] for j in range(nprev)] + [r[...] for r in xs_refs]
    ub = [cbc_ref[j] for j in range(nprev)] + [r[...] for r in bc_refs]
    cwx, cwbc = cwx_ref[...], cwbc_ref[...]
    neg_a = -jnp.exp(aloge_ref[...])
    xs, bm, cm, xdt, cum = [], [], [], [], []
    run = None
    for t in range(lt):
        ax = cbx_ref[...]
        ab = cbbc_ref[...]
        for j in range(CONV_K):
            ax = ax + ux[t + j] * cwx[j:j + 1]
            ab = ab + ub[t + j] * cwbc[j:j + 1]
        x_t = _silu(ax)
        bc_t = _silu(ab)
        dt_e = _softplus(_exact_dot_left(_split3(dt_refs[t][...]), e_ref[...]) + dtbe_ref[...])
        a_t = dt_e * neg_a
        run = a_t if run is None else run + a_t
        xs.append(x_t)
        bm.append(bc_t[:, :SSM_BC])
        cm.append(bc_t[:, SSM_BC:])
        xdt.append(x_t * dt_e)
        cum.append(run)
    for i in range(lt):
        yd = dske_ref[...] * xs[i]
        for j in range(i + 1):
            hi, mid, _ = _split3(cm[i] * bm[j])
            cbe = _dot(hi, gmat_ref[...]) + _dot(mid, gmat_ref[...])
            yd = yd + cbe * jnp.exp(cum[i] - cum[j]) * xdt[j]
        yd_ref[i] = yd
        ec_ref[i] = jnp.exp(cum[i])
        c_ref[i] = cm[i]
        b_ref[i] = bm[i]
        xw_ref[i] = xdt[i] * jnp.exp(cum[lt - 1] - cum[i])
    dec_ref[...] = jnp.exp(cum[lt - 1])


def _rows_block(rows, total):
    c = rows[0].shape[1]
    rid = lax.broadcasted_iota(jnp.int32, (SUBLANES, c), 0)
    acc = jnp.zeros((SUBLANES, c), F32)
    for j, r in enumerate(rows):
        acc = jnp.where(rid == j, jnp.broadcast_to(r, (SUBLANES, c)), acc)
    if total == SUBLANES:
        return acc
    return jnp.concatenate([acc, jnp.zeros((total - SUBLANES, c), F32)], axis=0)


def _ssd_s_state_kernel(c_ref, b_ref, xw_ref, dec_ref, h0_ref, hn_ref, yr_ref, *, lt):
    b = pl.program_id(0)
    gw = SSM_INNER // SSM_GROUPS
    c8 = _rows_block([c_ref[i, pl.ds(b, 1), :] for i in range(lt)], SUBLANES).astype(BF16)
    b128 = _rows_block([b_ref[i, pl.ds(b, 1), :] for i in range(lt)], LANES).astype(BF16)
    xaug = _rows_block([xw_ref[i, pl.ds(b, 1), :] for i in range(lt)] + [dec_ref[pl.ds(b, 1), :]], LANES)
    for g in range(SSM_GROUPS):
        hg = h0_ref[0, g * gw:(g + 1) * gw, :]
        yraw = _dot_nt(c8[:, g * SSM_STATE:(g + 1) * SSM_STATE], hg.astype(BF16))
        for i in range(lt):
            yr_ref[i, pl.ds(b, 1), g * gw:(g + 1) * gw] = yraw[i:i + 1, :]
        tr = xaug[:, g * gw:(g + 1) * gw].T
        s = _dot(tr.astype(BF16), b128[:, g * SSM_STATE:(g + 1) * SSM_STATE])
        hn_ref[0, g * gw:(g + 1) * gw, :] = hg * tr[:, lt:lt + 1] + s


def _ssd_s_post_kernel(*refs, lt):
    z_refs = refs[:lt]
    yd_ref, ec_ref, yr_ref, ng_ref, o_ref = refs[lt:]
    nb = z_refs[0].shape[0]
    for i in range(lt):
        y = yd_ref[i] + ec_ref[i] * yr_ref[i]
        gated = y * _silu(z_refs[i][...])
        out = gated * lax.rsqrt(jnp.mean(gated * gated, axis=-1, keepdims=True) + LN_EPS) * ng_ref[...]
        o_ref[i * nb:(i + 1) * nb, :] = out.astype(o_ref.dtype)


def ssd_sample(proj, dtp, tp, nb, lt, conv_state, h0, conv_w, conv_b, dt_bias, a_log, d_skip, norm_g):
    row0 = tp // nb
    cs = jnp.transpose(conv_state, (1, 0, 2))
    one = lambda i: (0, 0)
    one3 = lambda i: (0, 0, 0)

    def rows(t, width, col):
        return pl.BlockSpec((nb, width), lambda i: (row0 + t, col // width))

    in_specs = ([rows(t, SSM_INNER, COL_XS) for t in range(lt)]
                + [rows(t, 2 * SSM_BC, COL_BC) for t in range(lt)]
                + [pl.BlockSpec((nb, LANES), lambda i, t=t: (row0 + t, 0)) for t in range(lt)]
                + [pl.BlockSpec((CONV_K - 1, nb, SSM_INNER), one3),
                   pl.BlockSpec((CONV_K - 1, nb, 2 * SSM_BC), one3),
                   pl.BlockSpec((CONV_K, SSM_INNER), one),
                   pl.BlockSpec((CONV_K, 2 * SSM_BC), one),
                   pl.BlockSpec((1, SSM_INNER), one),
                   pl.BlockSpec((1, 2 * SSM_BC), one),
                   pl.BlockSpec((1, SSM_INNER), one),
                   pl.BlockSpec((1, SSM_INNER), one),
                   pl.BlockSpec((1, SSM_INNER), one),
                   pl.BlockSpec((LANES, SSM_INNER), one),
                   pl.BlockSpec((SSM_BC, SSM_INNER), one)])
    f3 = lambda w: jax.ShapeDtypeStruct((lt, nb, w), F32)
    c_a, b_a, xw_a, dec_a, yd_a, ec_a = pl.pallas_call(
        functools.partial(_ssd_s_pre_kernel, lt=lt),
        grid=(1,),
        in_specs=in_specs,
        out_specs=[pl.BlockSpec((lt, nb, SSM_BC), one3), pl.BlockSpec((lt, nb, SSM_BC), one3),
                   pl.BlockSpec((lt, nb, SSM_INNER), one3), pl.BlockSpec((nb, SSM_INNER), one),
                   pl.BlockSpec((lt, nb, SSM_INNER), one3), pl.BlockSpec((lt, nb, SSM_INNER), one3)],
        out_shape=[f3(SSM_BC), f3(SSM_BC), f3(SSM_INNER), jax.ShapeDtypeStruct((nb, SSM_INNER), F32),
                   f3(SSM_INNER), f3(SSM_INNER)],
        compiler_params=_cparams(1),
        name="ssd_sample_pre",
    )(*([proj] * (2 * lt)), *([dtp] * lt), cs[:, :, :SSM_INNER], cs[:, :, SSM_INNER:],
      conv_w[:, :SSM_INNER], conv_w[:, SSM_INNER:], conv_b[:SSM_INNER].reshape(1, -1),
      conv_b[SSM_INNER:].reshape(1, -1), _expand_heads(dt_bias), _expand_heads(a_log),
      _expand_heads(d_skip), _head_expand_matrix(), _group_expand_matrix())

    h0r = h0.reshape(nb, SSM_INNER, SSM_STATE)
    hn, yr = pl.pallas_call(
        functools.partial(_ssd_s_state_kernel, lt=lt),
        grid=(nb,),
        in_specs=[pl.BlockSpec((lt, nb, SSM_BC), one3), pl.BlockSpec((lt, nb, SSM_BC), one3),
                  pl.BlockSpec((lt, nb, SSM_INNER), one3), pl.BlockSpec((nb, SSM_INNER), one),
                  pl.BlockSpec((1, SSM_INNER, SSM_STATE), lambda i: (i, 0, 0))],
        out_specs=[pl.BlockSpec((1, SSM_INNER, SSM_STATE), lambda i: (i, 0, 0)),
                   pl.BlockSpec((lt, nb, SSM_INNER), one3)],
        out_shape=[jax.ShapeDtypeStruct((nb, SSM_INNER, SSM_STATE), F32), f3(SSM_INNER)],
        compiler_params=_cparams(1),
        name="ssd_sample_state",
    )(c_a, b_a, xw_a, dec_a, h0r)

    ssm = pl.pallas_call(
        functools.partial(_ssd_s_post_kernel, lt=lt),
        grid=(1,),
        in_specs=([rows(t, SSM_INNER, COL_Z) for t in range(lt)]
                  + [pl.BlockSpec((lt, nb, SSM_INNER), one3)] * 3 + [pl.BlockSpec((1, SSM_INNER), one)]),
        out_specs=pl.BlockSpec((lt * nb, SSM_INNER), one),
        out_shape=jax.ShapeDtypeStruct((lt * nb, SSM_INNER), BF16),
        compiler_params=_cparams(1),
        name="ssd_sample_post",
    )(*([proj] * lt), yd_a, ec_a, yr, norm_g.reshape(1, SSM_INNER))
    return ssm, hn


def _xattn_kernel(q_ref, k_ref, v_ref, o_ref, *, nh, bseq, tq):
    scale = MEM_HEAD_DIM ** -0.5
    for b in range(bseq):
        rows = slice(b * tq, (b + 1) * tq)
        for h in range(nh):
            sl = slice(h * MEM_HEAD_DIM, (h + 1) * MEM_HEAD_DIM)
            s = _dot_nt(q_ref[rows, sl].astype(BF16), k_ref[b, :, h, :].astype(BF16)) * scale
            m = jnp.max(s, axis=-1, keepdims=True)
            p = jnp.exp(s - m)
            den = jnp.sum(p, axis=-1, keepdims=True)
            o = _dot(p.astype(BF16), v_ref[b, :, h, :].astype(BF16)) / den
            o_ref[rows, sl] = o.astype(o_ref.dtype)


def cross_attention(q, k, v, *, layer, n_seq, seq, tq, bseq, name):
    w = q.shape[1]
    _, _, m, nh, dh = k.shape
    nq = seq // tq
    assert bseq == 1 or nq == 1
    kv_spec = pl.BlockSpec((None, bseq, m, nh, dh), lambda i, n: (layer, i, 0, 0, 0))
    return pl.pallas_call(
        functools.partial(_xattn_kernel, nh=nh, bseq=bseq, tq=tq),
        grid=(n_seq // bseq, nq),
        in_specs=[pl.BlockSpec((bseq * tq, w), lambda i, n: (i * nq + n, 0)), kv_spec, kv_spec],
        out_specs=pl.BlockSpec((bseq * tq, w), lambda i, n: (i * nq + n, 0)),
        out_shape=jax.ShapeDtypeStruct((n_seq * seq, w), F32),
        compiler_params=_cparams(2),
        name=name,
    )(q, k, v)


def _router_kernel(h_ref, w_ref, b_ref, o_ref):
    x = h_ref[...]
    w = w_ref[...]
    x_hi = x.astype(BF16)
    x_lo = (x - x_hi.astype(F32)).astype(BF16)
    w_hi = w.astype(BF16)
    w_lo = (w - w_hi.astype(F32)).astype(BF16)
    logits = _dot(x_hi, w_hi) + _dot(x_hi, w_lo) + _dot(x_lo, w_hi) + b_ref[...]
    lane = lax.broadcasted_iota(jnp.int32, logits.shape, 1)
    lane_f = lane.astype(F32)
    big = float(LANES)
    is_g = lane < N_EGROUPS
    lg = jnp.where(is_g, logits, NEG_BIG)
    mg = jnp.max(lg, axis=-1, keepdims=True)
    zg = jnp.sum(jnp.where(is_g, jnp.exp(lg - mg), 0.0), axis=-1, keepdims=True)
    gi = jnp.min(jnp.where(is_g & (lg == mg), lane_f, big), axis=-1, keepdims=True)
    gw = 1.0 / zg
    lo = N_EGROUPS + gi * EXPERTS_PER_GROUP
    is_e = (lane_f >= lo) & (lane_f < lo + EXPERTS_PER_GROUP)
    le = jnp.where(is_e, logits, NEG_BIG)
    me = jnp.max(le, axis=-1, keepdims=True)
    ee = jnp.where(is_e, jnp.exp(le - me), 0.0)
    pe = ee / jnp.sum(ee, axis=-1, keepdims=True)
    pe = jnp.where(is_e, pe, -1.0)
    p1 = jnp.max(pe, axis=-1, keepdims=True)
    i1 = jnp.min(jnp.where(pe == p1, lane_f, big), axis=-1, keepdims=True)
    pe2 = jnp.where(lane_f == i1, -1.0, pe)
    p2 = jnp.max(pe2, axis=-1, keepdims=True)
    i2 = jnp.min(jnp.where(pe2 == p2, lane_f, big), axis=-1, keepdims=True)
    tot = p1 + p2
    out = jnp.where(lane == 0, i1 - N_EGROUPS,
                    jnp.where(lane == 1, i2 - N_EGROUPS,
                              jnp.where(lane == 2, gw * (p1 / tot),
                                        jnp.where(lane == 3, gw * (p2 / tot), 0.0))))
    o_ref[...] = out


def moe_router(h, w_rg, b_rg, w_re, b_re):
    t, d = h.shape
    tm = _pick(t, (256, 128, 64, 32, 16, 8))
    npad = LANES - N_EGROUPS - N_EXPERTS
    w = jnp.concatenate([w_rg, w_re, jnp.zeros((d, npad), F32)], axis=1)
    b = jnp.concatenate([b_rg, b_re, jnp.zeros((npad,), F32)]).reshape(1, LANES)
    return pl.pallas_call(
        _router_kernel,
        grid=(t // tm,),
        in_specs=[pl.BlockSpec((tm, d), lambda i: (i, 0)),
                  pl.BlockSpec((d, LANES), lambda i: (0, 0)),
                  pl.BlockSpec((1, LANES), lambda i: (0, 0))],
        out_specs=pl.BlockSpec((tm, LANES), lambda i: (i, 0)),
        out_shape=jax.ShapeDtypeStruct((t, LANES), F32),
        compiler_params=_cparams(1),
        name="moe_router",
    )(h, w, b)


def _row_copy(src_hbm, dst, src_row, dst_row, sem):
    return pltpu.make_async_copy(src_hbm.at[pl.ds(src_row, 1)], dst.at[pl.ds(dst_row, 1)], sem)


def _moe_gather_kernel(tok_ref, h_ref, o_ref, buf, sem, *, tm):
    def start(r, carry):
        _row_copy(h_ref, buf, tok_ref[0, 0, r], r, sem).start()
        return carry

    def wait(r, carry):
        _row_copy(h_ref, buf, tok_ref[0, 0, r], r, sem).wait()
        return carry

    lax.fori_loop(0, tm, start, 0)
    lax.fori_loop(0, tm, wait, 0)
    o_ref[...] = buf[...].astype(o_ref.dtype)


def moe_gather(h, row_token, tm):
    r_total = row_token.shape[0]
    d = h.shape[1]
    nblk = r_total // tm
    return pl.pallas_call(
        functools.partial(_moe_gather_kernel, tm=tm),
        grid=(nblk,),
        in_specs=[pl.BlockSpec((1, 1, tm), lambda i: (i, 0, 0), memory_space=pltpu.SMEM),
                  pl.BlockSpec(memory_space=pl.ANY)],
        out_specs=pl.BlockSpec((tm, d), lambda i: (i, 0)),
        out_shape=jax.ShapeDtypeStruct((r_total, d), BF16),
        scratch_shapes=[pltpu.VMEM((tm, d), h.dtype), pltpu.SemaphoreType.DMA(())],
        compiler_params=_cparams(1),
        name="moe_gather",
    )(row_token.reshape(nblk, 1, tm), h)


def _moe_up_kernel(e_ref, f_ref, t_ref, to_ref, fo_ref, v_ref, x_ref, wg_ref, wu_ref, o_ref):
    s = pl.program_id(0)

    @pl.when(v_ref[s] > 0)
    def _():
        x = x_ref[...].astype(BF16)
        a = _dot(x, wg_ref[...].astype(BF16))
        u = _dot(x, wu_ref[...].astype(BF16))
        o_ref[...] = (_silu(a) * u).astype(o_ref.dtype)

    @pl.when(v_ref[s] == 0)
    def _():
        o_ref[...] = jnp.zeros_like(o_ref)


def moe_up(x_sorted, w_gate, w_up, layer, plan, tm, fchunk):
    r_total, d = x_sorted.shape
    ff = w_gate.shape[-1]
    n_steps = plan["step_e"].shape[0]
    grid_spec = pltpu.PrefetchScalarGridSpec(
        num_scalar_prefetch=6,
        grid=(n_steps,),
        in_specs=[
            pl.BlockSpec((tm, d), lambda s, e, f, t, to, fo, v: (t[s], 0)),
            pl.BlockSpec((None, None, d, fchunk), lambda s, e, f, t, to, fo, v: (layer, e[s], 0, f[s])),
            pl.BlockSpec((None, None, d, fchunk), lambda s, e, f, t, to, fo, v: (layer, e[s], 0, f[s])),
        ],
        out_specs=pl.BlockSpec((tm, fchunk), lambda s, e, f, t, to, fo, v: (to[s], fo[s])),
    )
    return pl.pallas_call(
        _moe_up_kernel,
        grid_spec=grid_spec,
        out_shape=jax.ShapeDtypeStruct((r_total, ff), BF16),
        compiler_params=_cparams(1),
        name="moe_up",
    )(plan["step_e"], plan["step_f"], plan["step_t"], plan["step_to"], plan["step_fo"], plan["step_v"],
      x_sorted, w_gate, w_up)


def _moe_down_kernel(e_ref, v_ref, x_ref, w_ref, o_ref):
    s = pl.program_id(0)

    @pl.when(v_ref[s] > 0)
    def _():
        o_ref[...] = _dot(x_ref[...], w_ref[...].astype(BF16))

    @pl.when(v_ref[s] == 0)
    def _():
        o_ref[...] = jnp.zeros_like(o_ref)


def moe_down(hid, w_down, layer, tile_e, tile_v, tm):
    r_total, ff = hid.shape
    d = w_down.shape[-1]
    n_tiles = r_total // tm
    grid_spec = pltpu.PrefetchScalarGridSpec(
        num_scalar_prefetch=2,
        grid=(n_tiles,),
        in_specs=[
            pl.BlockSpec((tm, ff), lambda s, e, v: (v[n_tiles + s], 0)),
            pl.BlockSpec((None, None, ff, d), lambda s, e, v: (layer, e[s], 0, 0)),
        ],
        out_specs=pl.BlockSpec((tm, d), lambda s, e, v: (s, 0)),
    )
    return pl.pallas_call(
        _moe_down_kernel,
        grid_spec=grid_spec,
        out_shape=jax.ShapeDtypeStruct((r_total, d), F32),
        compiler_params=_cparams(1),
        name="moe_down",
    )(tile_e, tile_v, hid, w_down)


def _moe_combine_kernel(pos_ref, y_ref, r_ref, h_ref, g_ref, b_ref, of_ref, ob_ref, ybuf, sem, *, tm, alpha):
    def start(i, carry):
        _row_copy(y_ref, ybuf.at[0], pos_ref[0, 0, 2 * i], i, sem).start()
        _row_copy(y_ref, ybuf.at[1], pos_ref[0, 0, 2 * i + 1], i, sem).start()
        return carry

    def wait(i, carry):
        _row_copy(y_ref, ybuf.at[0], pos_ref[0, 0, 2 * i], i, sem).wait()
        _row_copy(y_ref, ybuf.at[1], pos_ref[0, 0, 2 * i + 1], i, sem).wait()
        return carry

    lax.fori_loop(0, tm, start, 0)
    lax.fori_loop(0, tm, wait, 0)
    route = r_ref[...]
    ff = ybuf[0] * route[:, 2:3] + ybuf[1] * route[:, 3:4]
    h = _ln_rows(alpha * h_ref[...] + ff, g_ref[...], b_ref[...])
    of_ref[...] = h
    ob_ref[...] = h.astype(BF16)


def moe_combine(y_sorted, pos, route, h, g, b, *, alpha):
    t, d = h.shape
    tm = _pick(t, (256, 128, 64, 32, 16, 8))
    nblk = t // tm
    return pl.pallas_call(
        functools.partial(_moe_combine_kernel, tm=tm, alpha=alpha),
        grid=(nblk,),
        in_specs=[pl.BlockSpec((1, 1, 2 * tm), lambda i: (i, 0, 0), memory_space=pltpu.SMEM),
                  pl.BlockSpec(memory_space=pl.ANY),
                  pl.BlockSpec((tm, LANES), lambda i: (i, 0)),
                  pl.BlockSpec((tm, d), lambda i: (i, 0)),
                  pl.BlockSpec((1, d), lambda i: (0, 0)),
                  pl.BlockSpec((1, d), lambda i: (0, 0))],
        out_specs=[pl.BlockSpec((tm, d), lambda i: (i, 0)), pl.BlockSpec((tm, d), lambda i: (i, 0))],
        out_shape=[jax.ShapeDtypeStruct((t, d), F32), jax.ShapeDtypeStruct((t, d), BF16)],
        scratch_shapes=[pltpu.VMEM((2, tm, d), F32), pltpu.SemaphoreType.DMA(())],
        compiler_params=_cparams(1),
        name="moe_combine",
    )(pos.reshape(nblk, 1, 2 * tm), y_sorted, route, h, g.reshape(1, d), b.reshape(1, d))


def moe_plan(route, tm, n_f):
    t = route.shape[0]
    eid = route[:, :2].astype(jnp.int32).reshape(-1)
    onehot = (eid[:, None] == jnp.arange(N_EXPERTS, dtype=jnp.int32)[None, :]).astype(jnp.int32)
    csum = jnp.cumsum(onehot, axis=0)
    rank = jnp.sum((csum - onehot) * onehot, axis=1)
    counts = csum[-1]
    tiles_e = (counts + tm - 1) // tm
    tile_end = jnp.cumsum(tiles_e)
    tile_start = tile_end - tiles_e
    n_used = tile_end[-1]
    n_tiles = (2 * t + N_EXPERTS * (tm - 1)) // tm + 1
    r_total = n_tiles * tm
    dest = tile_start[eid] * tm + rank
    row_token = jnp.zeros((r_total,), jnp.int32).at[dest].set(jnp.arange(2 * t, dtype=jnp.int32) // 2)
    tile_ids = jnp.arange(n_tiles, dtype=jnp.int32)
    tile_clamped = jnp.minimum(tile_ids, n_used - 1)
    tile_e = jnp.sum(tile_end[None, :] <= tile_clamped[:, None], axis=1).astype(jnp.int32)
    tile_valid = (tile_ids < n_used).astype(jnp.int32)
    tile_v = jnp.concatenate([tile_valid, tile_clamped])
    n_steps = n_f * n_tiles
    sidx = jnp.arange(n_steps, dtype=jnp.int32)
    s_cl = jnp.minimum(sidx, n_f * n_used - 1)
    step_e = jnp.sum((n_f * tile_end)[None, :] <= s_cl[:, None], axis=1).astype(jnp.int32)
    local = s_cl - n_f * tile_start[step_e]
    te = jnp.maximum(tiles_e[step_e], 1)
    step_f = (local // te).astype(jnp.int32)
    step_t = (tile_start[step_e] + local % te).astype(jnp.int32)
    valid = sidx < n_f * n_used
    step_v = valid.astype(jnp.int32)
    spare = sidx - n_f * n_used
    step_to = jnp.where(valid, step_t, n_used + spare // n_f).astype(jnp.int32)
    step_fo = jnp.where(valid, step_f, spare % n_f).astype(jnp.int32)
    return dict(row_token=row_token, pos=dest.astype(jnp.int32), tile_e=tile_e, tile_v=tile_v,
                step_e=step_e, step_f=step_f, step_t=step_t, step_to=step_to, step_fo=step_fo,
                step_v=step_v)


def hierarchical_moe_ln(hf, layer, w_rg, b_rg, w_re, b_re, w_gate, w_up, w_down, ln_g, ln_b, *, alpha):
    route = moe_router(hf, w_rg, b_rg, w_re, b_re)
    n_f = EXPERT_FF // MOE_FCHUNK
    plan = moe_plan(route, MOE_TM, n_f)
    x_sorted = moe_gather(hf, plan["row_token"], MOE_TM)
    hid = moe_up(x_sorted, w_gate, w_up, layer, plan, MOE_TM, MOE_FCHUNK)
    y_sorted = moe_down(hid, w_down, layer, plan["tile_e"], plan["tile_v"], MOE_TM)
    return moe_combine(y_sorted, plan["pos"], route, hf, ln_g, ln_b, alpha=alpha)


def _to_seq_major(x_tm, lt, nb, pad_to):
    w = x_tm.shape[1]
    x = jnp.transpose(x_tm.reshape(lt, nb, w), (1, 0, 2))
    x = jnp.pad(x, ((0, 0), (0, pad_to - lt), (0, 0)))
    return x.reshape(nb * pad_to, w)


def _to_time_major(x_sm, lt, nb, pad_to):
    w = x_sm.shape[1]
    x = x_sm.reshape(nb, pad_to, w)[:, :lt]
    return jnp.transpose(x, (1, 0, 2)).reshape(lt * nb, w)


def kernel(x_prompt, x_sample, mem_prompt, cache_swa_k, cache_swa_v, cache_mem_k, cache_mem_v, state_conv, state_ssm, ln_in_g, ln_in_b, w_in, attn_sinks, gm_ln_g, gm_ln_b, gm_ws, gm_bs, conv_w, conv_b, dt_bias, a_log, d_skip, ssm_norm_g, w_pa, w_pb, w_pc, w_o, ln1_g, ln1_b, w_cq, w_ck, w_cv, w_co, ln2_g, ln2_b, w_rg, b_rg, w_re, b_re, w_gate, w_up, w_down, ln3_g, ln3_b):
    bp, seq, d = x_prompt.shape
    nb, lt, _ = x_sample.shape
    depth = w_in.shape[0]
    mem_len = mem_prompt.shape[1]
    past_len = PAST_LEN
    wb = cache_swa_k.shape[2]
    assert wb == WINDOW and seq % CHUNK == 0 and lt <= SUBLANES
    tp, ts = bp * seq, nb * lt
    alpha = (2 * depth) ** 0.25
    qpad = SUBLANES

    xp = x_prompt.reshape(tp, d)
    xs = jnp.transpose(x_sample, (1, 0, 2)).reshape(ts, d)
    hf, hb = ln_in(xp, xs, ln_in_g, ln_in_b)
    cos_t, sin_t = rope_tables(tp, seq, ts, nb, past_len)
    mem_b = mem_prompt.reshape(bp * mem_len, d).astype(BF16)

    o_k, o_v, o_gu, o_gv = Q_W, Q_W + KV_W, Q_W + 2 * KV_W, Q_W + 2 * KV_W + GM_W
    o_z = o_gv + GM_W
    o_xbc = o_z + SSM_INNER
    o_dt = o_xbc + CONV_DIM
    o_gates = o_dt + SSM_HEADS

    outs = {k: [] for k in ("p_k", "p_v", "p_mk", "p_mv", "p_conv", "p_ssm", "p_gv",
                            "s_k", "s_v", "s_conv", "s_ssm", "s_gv")}
    n_qblk = seq // WINDOW
    for l in range(depth):
        wl = w_in[l]
        w_main = jnp.concatenate(
            [wl[:, :Q_W], wl[:, o_gu:o_gv], wl[:, o_gv:o_z], wl[:, o_z:o_xbc], wl[:, o_gates:],
             wl[:, o_xbc:o_dt], wl[:, o_k:o_v], wl[:, o_v:o_gu]], axis=1).astype(BF16)
        w_dt = jnp.pad(wl[:, o_dt:o_gates], ((0, 0), (0, LANES - SSM_HEADS))).astype(BF16)
        proj = matmul(hb, w_main, name="in_proj")
        dtp = matmul(hb, w_dt, name="dt_proj")
        q_rot, k_rot = rope_qk(proj, cos_t, sin_t)

        kcol, vcol = 0, COL_V // KV_W
        att_p = swa_attention(
            attn_sinks[l], q_rot, k_rot, k_rot, proj, proj,
            n_seq=bp, n_blk=n_qblk, qb=WINDOW, prev_from_block0=False,
            kp_map=lambda i, n: (jnp.maximum(i * n_qblk + n - 1, 0), kcol),
            kc_map=lambda i, n: (i * n_qblk + n, kcol),
            vp_map=lambda i, n: (jnp.maximum(i * n_qblk + n - 1, 0), vcol),
            vc_map=lambda i, n: (i * n_qblk + n, vcol),
            out_dtype=BF16, name="swa_prompt")
        k_s_tm = k_rot[tp:]
        v_s_tm = proj[tp:, COL_V:COL_V + KV_W]
        q_s = _to_seq_major(q_rot[tp:], lt, nb, qpad)
        k_s = _to_seq_major(k_s_tm, lt, nb, qpad)
        v_s = _to_seq_major(v_s_tm, lt, nb, qpad)
        att_s8 = swa_attention_cached(attn_sinks[l], q_s, k_s, v_s, cache_swa_k, cache_swa_v, layer=l,
                                      n_seq=nb, qb=qpad, bseq=_pick(nb, (8, 4, 2, 1)), name="swa_sample")
        att_s = _to_time_major(att_s8, lt, nb, qpad).astype(BF16)
        last_w = lambda a, c0: jnp.stack(
            [a[(i + 1) * seq - WINDOW:(i + 1) * seq, c0:c0 + KV_W] for i in range(bp)]
        ).reshape(bp, WINDOW, N_KV_HEADS, HEAD_DIM)
        outs["p_k"].append(last_w(k_rot, 0))
        outs["p_v"].append(last_w(proj, COL_V))
        k_new = jnp.transpose(k_s_tm.reshape(lt, nb, N_KV_HEADS, HEAD_DIM), (1, 0, 2, 3))
        v_new = jnp.transpose(v_s_tm.reshape(lt, nb, N_KV_HEADS, HEAD_DIM), (1, 0, 2, 3))
        outs["s_k"].append(jnp.concatenate([cache_swa_k[l], k_new], axis=1)[:, -wb:])
        outs["s_v"].append(jnp.concatenate([cache_swa_v[l], v_new], axis=1)[:, -wb:])

        gm_p, vg_last = gmlp_prompt(proj, bp, seq, gm_ws[l], gm_bs[l], gm_ln_g[l], gm_ln_b[l])
        gm_s, vg_s = gmlp_sample(proj, tp, nb, lt, gm_ws[l], gm_bs[l], gm_ln_g[l], gm_ln_b[l])
        outs["p_gv"].append(vg_last.reshape(bp, CHUNK, GM_GROUPS, GM_GROUP_DIM))
        outs["s_gv"].append(jnp.transpose(vg_s.reshape(lt, nb, GM_GROUPS, GM_GROUP_DIM), (1, 0, 2, 3)))

        dtt = jnp.transpose(dtp[:tp, :SSM_HEADS])
        y_p, st_p = ssd_prompt(proj, dtp, dtt, bp, seq, conv_w[l], conv_b[l], dt_bias[l], a_log[l],
                               d_skip[l], ssm_norm_g[l])
        ssm_s, st_s = ssd_sample(proj, dtp, tp, nb, lt, state_conv[l], state_ssm[l], conv_w[l], conv_b[l],
                                 dt_bias[l], a_log[l], d_skip[l], ssm_norm_g[l])
        outs["p_conv"].append(jnp.stack(
            [proj[(i + 1) * seq - (CONV_K - 1):(i + 1) * seq, COL_XS:COL_XS + CONV_DIM] for i in range(bp)]))
        xbc_s = jnp.transpose(proj[tp:, COL_XS:COL_XS + CONV_DIM].reshape(lt, nb, CONV_DIM), (1, 0, 2))
        outs["s_conv"].append(jnp.concatenate([state_conv[l], xbc_s], axis=1)[:, -(CONV_K - 1):])
        outs["p_ssm"].append(st_p.reshape(bp, SSM_HEADS, SSM_HEAD_DIM, SSM_STATE))
        outs["s_ssm"].append(st_s.reshape(nb, SSM_HEADS, SSM_HEAD_DIM, SSM_STATE))

        merged = gated_merge(att_p, att_s, gm_p, gm_s, y_p, ssm_s, w_pa[l].astype(BF16),
                             w_pb[l].astype(BF16), w_pc[l].astype(BF16), proj)
        h1f, h1b = matmul_ln(merged, w_o[l].astype(BF16), hf, ln1_g[l], ln1_b[l], alpha=alpha, name="out_proj_ln1")

        qc = matmul(h1b, w_cq[l].astype(BF16), name="xattn_q")
        pmk = matmul(mem_b, w_ck[l].astype(BF16), name="mem_k")
        pmv = matmul(mem_b, w_cv[l].astype(BF16), name="mem_v")
        pmk5 = pmk.reshape(1, bp, mem_len, MEM_HEADS, MEM_HEAD_DIM)
        pmv5 = pmv.reshape(1, bp, mem_len, MEM_HEADS, MEM_HEAD_DIM)
        outs["p_mk"].append(pmk5[0])
        outs["p_mv"].append(pmv5[0])
        tq = _pick(seq, (512, 256, 128))
        o_p = cross_attention(qc, pmk5, pmv5, layer=0, n_seq=bp, seq=seq, tq=tq, bseq=1,
                              name="xattn_prompt")
        qc_s = _to_seq_major(qc[tp:], lt, nb, qpad)
        o_s8 = cross_attention(qc_s, cache_mem_k, cache_mem_v, layer=l, n_seq=nb, seq=qpad, tq=qpad,
                               bseq=_pick(nb, (4, 2, 1)), name="xattn_sample")
        o_all = jnp.concatenate([o_p, _to_time_major(o_s8, lt, nb, qpad)], axis=0).astype(BF16)
        h2f, h2b = matmul_ln(o_all, w_co[l].astype(BF16), h1f, ln2_g[l], ln2_b[l], alpha=alpha, name="xattn_out_ln2")

        hf, hb = hierarchical_moe_ln(h2f, l, w_rg[l], b_rg[l], w_re[l], b_re[l], w_gate, w_up, w_down,
                                     ln3_g[l], ln3_b[l], alpha=alpha)

    y_prompt = hf[:tp].reshape(bp, seq, d)
    y_sample = jnp.transpose(hf[tp:].reshape(lt, nb, d), (1, 0, 2))
    st = lambda k: jnp.stack(outs[k])
    return (y_prompt, y_sample, st("p_k"), st("p_v"), st("p_mk"), st("p_mv"), st("p_conv"), st("p_ssm"),
            st("p_gv"), st("s_k"), st("s_v"), st("s_conv"), st("s_ssm"), st("s_gv"))
```

```python
import functools
import math

import numpy as np
import jax
import jax.numpy as jnp
from jax import lax
from jax.experimental import pallas as pl
from jax.experimental.pallas import tpu as pltpu

F32 = jnp.float32
BF16 = jnp.bfloat16

D_MODEL = 2048
N_HEADS = 32
N_KV_HEADS = 4
HEAD_DIM = 64
WINDOW = 128
PAST_LEN = 8192
ROPE_THETA = 10000.0
CHUNK = 128
GM_GROUPS = 16
GM_GROUP_DIM = 128
SSM_HEADS = 32
SSM_HEAD_DIM = 64
SSM_GROUPS = 4
SSM_STATE = 128
CONV_K = 4
MEM_HEADS = 4
MEM_HEAD_DIM = 128
N_EGROUPS = 4
EXPERTS_PER_GROUP = 8
N_EXPERTS = N_EGROUPS * EXPERTS_PER_GROUP
EXPERT_FF = D_MODEL // 2
Q_W = N_HEADS * HEAD_DIM
KV_W = N_KV_HEADS * HEAD_DIM
GM_W = GM_GROUPS * GM_GROUP_DIM
SSM_INNER = SSM_HEADS * SSM_HEAD_DIM
SSM_BC = SSM_GROUPS * SSM_STATE
CONV_DIM = SSM_INNER + 2 * SSM_BC
MEM_W = MEM_HEADS * MEM_HEAD_DIM
LN_EPS = 1e-5
NEG_BIG = -1e30

VMEM_LIMIT_BYTES = 52 * 1024 * 1024
LANES = 128
SUBLANES = 8

COL_Q = 0
COL_GU = 2048
COL_GV = 4096
COL_Z = 6144
COL_GATES = 8192
COL_XS = 14336
COL_BC = 16384
COL_K = 17408
COL_V = 17664
PROJ_W = 17920

MOE_TM = 256
MOE_FCHUNK = 512
DMA_LOOP_UNROLL = 8


def _cparams(n_grid):
    return pltpu.CompilerParams(
        dimension_semantics=("arbitrary",) * n_grid,
        vmem_limit_bytes=VMEM_LIMIT_BYTES,
    )


def _pick(n, prefs):
    for p in prefs:
        if n % p == 0:
            return p
    raise ValueError(f"no tile for {n} in {prefs}")


def _ln_rows(x, g, b):
    mu = jnp.mean(x, axis=-1, keepdims=True)
    xc = x - mu
    var = jnp.mean(xc * xc, axis=-1, keepdims=True)
    return xc * lax.rsqrt(var + LN_EPS) * g + b


def _sigmoid(x):
    return 1.0 / (1.0 + jnp.exp(-x))


def _silu(x):
    return x * _sigmoid(x)


def _softplus(x):
    return jnp.maximum(x, 0.0) + jnp.log1p(jnp.exp(-jnp.abs(x)))


def _gelu(x):
    return jax.nn.gelu(x, approximate=True)


def _split3(x):
    hi = x.astype(BF16)
    r1 = x - hi.astype(F32)
    mid = r1.astype(BF16)
    lo = (r1 - mid.astype(F32)).astype(BF16)
    return hi, mid, lo


def _dot(a, b):
    return jnp.dot(a, b, preferred_element_type=F32)


def _dot_nt(a, b):
    return lax.dot_general(a, b, (((1,), (1,)), ((), ())), preferred_element_type=F32)


def _exact_dot_left(pieces, m):
    acc = _dot(pieces[0], m)
    for p in pieces[1:]:
        acc = acc + _dot(p, m)
    return acc


def _ln_in_kernel(xp_ref, xs_ref, g_ref, b_ref, of_ref, ob_ref, *, n_p):
    i = pl.program_id(0)

    @pl.when(i < n_p)
    def _():
        y = _ln_rows(xp_ref[...], g_ref[...], b_ref[...])
        of_ref[...] = y
        ob_ref[...] = y.astype(BF16)

    @pl.when(i >= n_p)
    def _():
        y = _ln_rows(xs_ref[...], g_ref[...], b_ref[...])
        of_ref[...] = y
        ob_ref[...] = y.astype(BF16)


def ln_in(xp, xs, g, b):
    tp, d = xp.shape
    ts = xs.shape[0]
    tm = _pick(math.gcd(tp, ts), (256, 128, 64, 32, 16, 8))
    n_p, n_s = tp // tm, ts // tm
    t = tp + ts
    return pl.pallas_call(
        functools.partial(_ln_in_kernel, n_p=n_p),
        grid=(n_p + n_s,),
        in_specs=[
            pl.BlockSpec((tm, d), lambda i: (jnp.minimum(i, n_p - 1), 0)),
            pl.BlockSpec((tm, d), lambda i: (jnp.maximum(i - n_p, 0), 0)),
            pl.BlockSpec((1, d), lambda i: (0, 0)),
            pl.BlockSpec((1, d), lambda i: (0, 0)),
        ],
        out_specs=[
            pl.BlockSpec((tm, d), lambda i: (i, 0)),
            pl.BlockSpec((tm, d), lambda i: (i, 0)),
        ],
        out_shape=[jax.ShapeDtypeStruct((t, d), F32), jax.ShapeDtypeStruct((t, d), BF16)],
        compiler_params=_cparams(1),
        name="ln_in",
    )(xp, xs, g.reshape(1, d), b.reshape(1, d))


def _mm_kernel(x_ref, w_ref, o_ref):
    o_ref[...] = _dot(x_ref[...], w_ref[...]).astype(o_ref.dtype)


def matmul(x, w, *, out_dtype=F32, tm_prefs=(1088, 1024, 512, 256, 128, 64, 32, 16, 8),
           tn_prefs=(1280, 1024, 512, 256, 128), name="mm"):
    t, k = x.shape
    n = w.shape[1]
    tm = _pick(t, tm_prefs)
    tn = _pick(n, tn_prefs)
    return pl.pallas_call(
        _mm_kernel,
        grid=(n // tn, t // tm),
        in_specs=[
            pl.BlockSpec((tm, k), lambda j, i: (i, 0)),
            pl.BlockSpec((k, tn), lambda j, i: (0, j)),
        ],
        out_specs=pl.BlockSpec((tm, tn), lambda j, i: (i, j)),
        out_shape=jax.ShapeDtypeStruct((t, n), out_dtype),
        compiler_params=_cparams(2),
        name=name,
    )(x, w)


def _mm_ln_kernel(x_ref, w_ref, r_ref, g_ref, b_ref, of_ref, ob_ref, *, alpha):
    y = _dot(x_ref[...], w_ref[...])
    h = _ln_rows(alpha * r_ref[...] + y, g_ref[...], b_ref[...])
    of_ref[...] = h
    ob_ref[...] = h.astype(BF16)


def matmul_ln(x, w, res, g, b, *, alpha, name="mm_ln"):
    t, k = x.shape
    d = w.shape[1]
    tm = _pick(t, (256, 128, 64, 32, 16, 8))
    return pl.pallas_call(
        functools.partial(_mm_ln_kernel, alpha=alpha),
        grid=(t // tm,),
        in_specs=[
            pl.BlockSpec((tm, k), lambda i: (i, 0)),
            pl.BlockSpec((k, d), lambda i: (0, 0)),
            pl.BlockSpec((tm, d), lambda i: (i, 0)),
            pl.BlockSpec((1, d), lambda i: (0, 0)),
            pl.BlockSpec((1, d), lambda i: (0, 0)),
        ],
        out_specs=[
            pl.BlockSpec((tm, d), lambda i: (i, 0)),
            pl.BlockSpec((tm, d), lambda i: (i, 0)),
        ],
        out_shape=[jax.ShapeDtypeStruct((t, d), F32), jax.ShapeDtypeStruct((t, d), BF16)],
        compiler_params=_cparams(1),
        name=name,
    )(x, w, res, g.reshape(1, d), b.reshape(1, d))


def _merge_kernel(ap_ref, as_ref, bp_ref, bs_ref, cp_ref, cs_ref, wa_ref, wb_ref, wc_ref,
                  ga_ref, gb_ref, gc_ref, o_ref, *, n_p):
    i = pl.program_id(1)
    is_p = i < n_p
    xa = jnp.where(is_p, ap_ref[...], as_ref[...])
    xb = jnp.where(is_p, bp_ref[...], bs_ref[...])
    xc = jnp.where(is_p, cp_ref[...], cs_ref[...])
    acc = _sigmoid(ga_ref[...]) * _dot(xa, wa_ref[...])
    acc = acc + _sigmoid(gb_ref[...]) * _dot(xb, wb_ref[...])
    acc = acc + _sigmoid(gc_ref[...]) * _dot(xc, wc_ref[...])
    o_ref[...] = acc.astype(o_ref.dtype)


def gated_merge(att_p, att_s, gm_p, gm_s, ssm_p, ssm_s, w_pa, w_pb, w_pc, proj):
    tp, k = att_p.shape
    ts = att_s.shape[0]
    d = w_pa.shape[1]
    tm = _pick(math.gcd(tp, ts), (256, 128, 64, 32, 16, 8))
    tn = 1024
    n_p, n_s = tp // tm, ts // tm
    gate_blk = COL_GATES // tn
    d_blk = d // tn

    def xp_spec():
        return pl.BlockSpec((tm, k), lambda j, i: (jnp.minimum(i, n_p - 1), 0))

    def xs_spec():
        return pl.BlockSpec((tm, k), lambda j, i: (jnp.maximum(i - n_p, 0), 0))

    def w_spec():
        return pl.BlockSpec((k, tn), lambda j, i: (0, j))

    def g_spec(which):
        return pl.BlockSpec((tm, tn), lambda j, i: (i, gate_blk + which * d_blk + j))

    return pl.pallas_call(
        functools.partial(_merge_kernel, n_p=n_p),
        grid=(d // tn, n_p + n_s),
        in_specs=[xp_spec(), xs_spec(), xp_spec(), xs_spec(), xp_spec(), xs_spec(),
                  w_spec(), w_spec(), w_spec(), g_spec(0), g_spec(1), g_spec(2)],
        out_specs=pl.BlockSpec((tm, tn), lambda j, i: (i, j)),
        out_shape=jax.ShapeDtypeStruct((tp + ts, d), BF16),
        compiler_params=_cparams(2),
        name="gated_merge",
    )(att_p, att_s, gm_p, gm_s, ssm_p, ssm_s, w_pa, w_pb, w_pc, proj, proj, proj)


def _rope_block(x, cos, sin_signed, first_half):
    outs = []
    for c in range(x.shape[1] // LANES):
        xc = x[:, c * LANES:(c + 1) * LANES]
        fwd = pltpu.roll(xc, LANES - HEAD_DIM // 2, axis=1)
        bwd = pltpu.roll(xc, HEAD_DIM // 2, axis=1)
        partner = jnp.where(first_half, fwd, bwd)
        outs.append(xc * cos + partner * sin_signed)
    return outs


def _rope_kernel(q_ref, k_ref, cos_ref, sin_ref, qo_ref, ko_ref):
    cos = cos_ref[...]
    sin_signed = sin_ref[...]
    lane = lax.broadcasted_iota(jnp.int32, cos.shape, 1)
    first_half = (lane % HEAD_DIM) < (HEAD_DIM // 2)
    for c, o in enumerate(_rope_block(q_ref[...], cos, sin_signed, first_half)):
        qo_ref[:, c * LANES:(c + 1) * LANES] = o
    for c, o in enumerate(_rope_block(k_ref[...], cos, sin_signed, first_half)):
        ko_ref[:, c * LANES:(c + 1) * LANES] = o


def rope_qk(proj, cos_t, sin_t):
    t = proj.shape[0]
    tm = _pick(t, (256, 128, 64, 32, 16, 8))
    return pl.pallas_call(
        _rope_kernel,
        grid=(t // tm,),
        in_specs=[
            pl.BlockSpec((tm, Q_W), lambda i: (i, COL_Q // Q_W)),
            pl.BlockSpec((tm, KV_W), lambda i: (i, COL_K // KV_W)),
            pl.BlockSpec((tm, LANES), lambda i: (i, 0)),
            pl.BlockSpec((tm, LANES), lambda i: (i, 0)),
        ],
        out_specs=[
            pl.BlockSpec((tm, Q_W), lambda i: (i, 0)),
            pl.BlockSpec((tm, KV_W), lambda i: (i, 0)),
        ],
        out_shape=[jax.ShapeDtypeStruct((t, Q_W), F32), jax.ShapeDtypeStruct((t, KV_W), F32)],
        compiler_params=_cparams(1),
        name="rope_qk",
    )(proj, proj, cos_t, sin_t)


def rope_tables(tp, seq, ts, nb, past_len):
    half = HEAD_DIM // 2
    inv = ROPE_THETA ** (-jnp.arange(half, dtype=F32) / half)
    pos_p = jnp.arange(tp, dtype=jnp.int32) % seq
    pos_s = past_len + jnp.arange(ts, dtype=jnp.int32) // nb
    pos = jnp.concatenate([pos_p, pos_s]).astype(F32)
    ang = pos[:, None] * inv[None, :]
    cos = jnp.tile(jnp.cos(ang), (1, LANES // half))
    sin = jnp.sin(ang)
    sin_signed = jnp.tile(jnp.concatenate([-sin, sin], axis=1), (1, LANES // HEAD_DIM))
    return cos, sin_signed


def _dup_head(slab, g):
    lane = lax.broadcasted_iota(jnp.int32, slab.shape, 1)
    rolled = pltpu.roll(slab, HEAD_DIM, axis=1)
    if g % 2 == 0:
        return jnp.where(lane < HEAD_DIM, slab, rolled)
    return jnp.where(lane < HEAD_DIM, rolled, slab)


def _swa_bias(qb, prev_ok):
    kw = WINDOW + qb
    ii = lax.broadcasted_iota(jnp.int32, (qb, kw), 0)
    jj = lax.broadcasted_iota(jnp.int32, (qb, kw), 1)
    ok_prev = (jj < WINDOW) & (jj > ii)
    if prev_ok is not True:
        ok_prev = ok_prev & prev_ok
    ok = ok_prev | ((jj >= WINDOW) & ((jj - WINDOW) <= ii))
    return jnp.where(ok, 0.0, NEG_BIG)


def _swa_group(q, kdup, vdup, bias, sink_ref, g, qb, store):
    rep = N_HEADS // N_KV_HEADS
    lane = lax.broadcasted_iota(jnp.int32, (qb, LANES), 1)
    scale = HEAD_DIM ** -0.5
    rows = []
    for r in range(rep):
        h = g * rep + r
        qc = q[:, (h // 2) * LANES:(h // 2 + 1) * LANES]
        keep = (lane < HEAD_DIM) if h % 2 == 0 else (lane >= HEAD_DIM)
        rows.append(jnp.where(keep, qc * scale, 0.0))
    s = _dot_nt(jnp.concatenate(rows, axis=0).astype(BF16), kdup)
    ps, dens = [], []
    for r in range(rep):
        sr = s[r * qb:(r + 1) * qb] + bias
        sink = sink_ref[g * rep + r]
        m = jnp.maximum(jnp.max(sr, axis=-1, keepdims=True), sink)
        p = jnp.exp(sr - m)
        dens.append(jnp.sum(p, axis=-1, keepdims=True) + jnp.exp(sink - m))
        ps.append(p)
    o = _dot(jnp.concatenate(ps, axis=0).astype(BF16), vdup)
    for c in range(rep // 2):
        oa = o[(2 * c) * qb:(2 * c + 1) * qb] / dens[2 * c]
        ob = o[(2 * c + 1) * qb:(2 * c + 2) * qb] / dens[2 * c + 1]
        store((g * rep // 2 + c) * LANES, jnp.where(lane < HEAD_DIM, oa, ob))


def _swa_kernel(sink_ref, q_ref, kp_ref, kc_ref, vp_ref, vc_ref, o_ref, *, qb, prev_from_block0):
    n = pl.program_id(1)
    q = q_ref[...]
    kp, kc, vp, vc = kp_ref[...], kc_ref[...], vp_ref[...], vc_ref[...]
    bias = _swa_bias(qb, True if prev_from_block0 else (n > 0))

    def store(col, val):
        o_ref[:, col:col + LANES] = val.astype(o_ref.dtype)

    for g in range(N_KV_HEADS):
        sl = slice((g // 2) * LANES, (g // 2 + 1) * LANES)
        kdup = _dup_head(jnp.concatenate([kp[:, sl], kc[:, sl]], axis=0), g).astype(BF16)
        vdup = _dup_head(jnp.concatenate([vp[:, sl], vc[:, sl]], axis=0), g).astype(BF16)
        _swa_group(q, kdup, vdup, bias, sink_ref, g, qb, store)


def _swa_cache_kernel(*refs, qb, bseq):
    sink_ref, q_ref, kc_ref, vc_ref, kp_ref, vp_ref, o_ref = refs
    bias = _swa_bias(qb, True)
    for b in range(bseq):
        rows = slice(b * qb, (b + 1) * qb)
        q = q_ref[rows, :]

        def store(col, val, rows=rows):
            o_ref[rows, col:col + LANES] = val.astype(o_ref.dtype)

        for g in range(N_KV_HEADS):
            hs = slice(g * HEAD_DIM, (g + 1) * HEAD_DIM)
            kcat = jnp.concatenate([kp_ref[b, :, g, :], kc_ref[rows, hs]], axis=0)
            vcat = jnp.concatenate([vp_ref[b, :, g, :], vc_ref[rows, hs]], axis=0)
            kdup = jnp.concatenate([kcat, kcat], axis=1).astype(BF16)
            vdup = jnp.concatenate([vcat, vcat], axis=1).astype(BF16)
            _swa_group(q, kdup, vdup, bias, sink_ref, g, qb, store)


def swa_attention_cached(sinks, q, kc, vc, cache_k, cache_v, *, layer, n_seq, qb, bseq, name):
    cache_spec = pl.BlockSpec((None, bseq, WINDOW, N_KV_HEADS, HEAD_DIM), lambda i: (layer, i, 0, 0, 0))
    rows = bseq * qb
    return pl.pallas_call(
        functools.partial(_swa_cache_kernel, qb=qb, bseq=bseq),
        grid=(n_seq // bseq,),
        in_specs=[pl.BlockSpec(memory_space=pltpu.SMEM),
                  pl.BlockSpec((rows, Q_W), lambda i: (i, 0)),
                  pl.BlockSpec((rows, KV_W), lambda i: (i, 0)),
                  pl.BlockSpec((rows, KV_W), lambda i: (i, 0)),
                  cache_spec, cache_spec],
        out_specs=pl.BlockSpec((rows, Q_W), lambda i: (i, 0)),
        out_shape=jax.ShapeDtypeStruct((n_seq * qb, Q_W), F32),
        compiler_params=_cparams(1),
        name=name,
    )(sinks, q, kc, vc, cache_k, cache_v)


def swa_attention(sinks, q, kp, kc, vp, vc, *, n_seq, n_blk, qb, prev_from_block0,
                  kp_map, kc_map, vp_map, vc_map, out_dtype, name):
    return pl.pallas_call(
        functools.partial(_swa_kernel, qb=qb, prev_from_block0=prev_from_block0),
        grid=(n_seq, n_blk),
        in_specs=[
            pl.BlockSpec(memory_space=pltpu.SMEM),
            pl.BlockSpec((qb, Q_W), lambda i, n: (i * n_blk + n, 0)),
            pl.BlockSpec((WINDOW, KV_W), kp_map),
            pl.BlockSpec((qb, KV_W), kc_map),
            pl.BlockSpec((WINDOW, KV_W), vp_map),
            pl.BlockSpec((qb, KV_W), vc_map),
        ],
        out_specs=pl.BlockSpec((qb, Q_W), lambda i, n: (i * n_blk + n, 0)),
        out_shape=jax.ShapeDtypeStruct((n_seq * n_blk * qb, Q_W), out_dtype),
        compiler_params=_cparams(2),
        name=name,
    )(sinks, q, kp, kc, vp, vc)


def _gmlp_p_kernel(gu_ref, gv_ref, ws_ref, bst_ref, lg_ref, lb_ref, gm_ref, vg_ref, *, n_chunks):
    n = pl.program_id(1)
    vg = _ln_rows(_gelu(gv_ref[...]), lg_ref[...], lb_ref[...])
    gu = gu_ref[...]
    ri = lax.broadcasted_iota(jnp.int32, (CHUNK, CHUNK), 0)
    ci = lax.broadcasted_iota(jnp.int32, (CHUNK, CHUNK), 1)
    tril = ri >= ci
    bst = bst_ref[...]
    for g in range(GM_GROUPS):
        sl = slice(g * GM_GROUP_DIM, (g + 1) * GM_GROUP_DIM)
        w = jnp.where(tril, ws_ref[g], 0.0).astype(BF16)
        s = _dot(w, vg[:, sl].astype(BF16)) + bst[:, g:g + 1]
        gm_ref[:, sl] = (_gelu(gu[:, sl]) * s).astype(gm_ref.dtype)

    @pl.when(n == n_chunks - 1)
    def _():
        vg_ref[...] = vg


def gmlp_prompt(proj, n_seq, seq, ws, bs, ln_g, ln_b):
    nc = seq // CHUNK
    return pl.pallas_call(
        functools.partial(_gmlp_p_kernel, n_chunks=nc),
        grid=(n_seq, nc),
        in_specs=[
            pl.BlockSpec((CHUNK, GM_W), lambda i, n: (i * nc + n, COL_GU // GM_W)),
            pl.BlockSpec((CHUNK, GM_W), lambda i, n: (i * nc + n, COL_GV // GM_W)),
            pl.BlockSpec((GM_GROUPS, CHUNK, CHUNK), lambda i, n: (0, 0, 0)),
            pl.BlockSpec((CHUNK, GM_GROUPS), lambda i, n: (0, 0)),
            pl.BlockSpec((1, GM_W), lambda i, n: (0, 0)),
            pl.BlockSpec((1, GM_W), lambda i, n: (0, 0)),
        ],
        out_specs=[
            pl.BlockSpec((CHUNK, GM_W), lambda i, n: (i * nc + n, 0)),
            pl.BlockSpec((CHUNK, GM_W), lambda i, n: (i, 0)),
        ],
        out_shape=[jax.ShapeDtypeStruct((n_seq * seq, GM_W), BF16),
                   jax.ShapeDtypeStruct((n_seq * CHUNK, GM_W), F32)],
        compiler_params=_cparams(2),
        name="gmlp_prompt",
    )(proj, proj, ws, bs.T, ln_g.reshape(1, GM_W), ln_b.reshape(1, GM_W))


def _gmlp_s_kernel(*refs, lt):
    gu_refs = refs[:lt]
    gv_refs = refs[lt:2 * lt]
    wrow_ref, brow_ref, lg_ref, lb_ref, gm_ref, vg_ref = refs[2 * lt:]
    nb = gu_refs[0].shape[0]
    vgs = [_ln_rows(_gelu(gv_refs[t][...]), lg_ref[...], lb_ref[...]) for t in range(lt)]
    for i in range(lt):
        s = brow_ref[i:i + 1, :]
        for j in range(i + 1):
            s = s + wrow_ref[i * lt + j:i * lt + j + 1, :] * vgs[j]
        gm_ref[i * nb:(i + 1) * nb, :] = (_gelu(gu_refs[i][...]) * s).astype(gm_ref.dtype)
        vg_ref[i * nb:(i + 1) * nb, :] = vgs[i]


def gmlp_sample(proj, tp, nb, lt, ws, bs, ln_g, ln_b):
    w_small = ws[:, :lt, :lt]
    wrow = jnp.repeat(jnp.transpose(w_small, (1, 2, 0)).reshape(lt * lt, GM_GROUPS), GM_GROUP_DIM, axis=1)
    brow = jnp.repeat(bs[:, :lt].T, GM_GROUP_DIM, axis=1)
    row0 = tp // nb

    def spec(t, col):
        return pl.BlockSpec((nb, GM_W), lambda i: (row0 + t, col // GM_W))

    in_specs = [spec(t, COL_GU) for t in range(lt)] + [spec(t, COL_GV) for t in range(lt)] + [
        pl.BlockSpec((lt * lt, GM_W), lambda i: (0, 0)),
        pl.BlockSpec((lt, GM_W), lambda i: (0, 0)),
        pl.BlockSpec((1, GM_W), lambda i: (0, 0)),
        pl.BlockSpec((1, GM_W), lambda i: (0, 0)),
    ]
    return pl.pallas_call(
        functools.partial(_gmlp_s_kernel, lt=lt),
        grid=(1,),
        in_specs=in_specs,
        out_specs=[pl.BlockSpec((lt * nb, GM_W), lambda i: (0, 0)),
                   pl.BlockSpec((lt * nb, GM_W), lambda i: (0, 0))],
        out_shape=[jax.ShapeDtypeStruct((lt * nb, GM_W), BF16),
                   jax.ShapeDtypeStruct((lt * nb, GM_W), F32)],
        compiler_params=_cparams(1),
        name="gmlp_sample",
    )(*([proj] * (2 * lt)), wrow, brow, ln_g.reshape(1, GM_W), ln_b.reshape(1, GM_W))


def _conv_silu(cur, prev8, w, bias):
    q = cur.shape[0]
    up = jnp.concatenate([prev8, cur], axis=0)
    acc = bias + up[SUBLANES:SUBLANES + q] * w[CONV_K - 1:CONV_K]
    for j in range(CONV_K - 1):
        off = SUBLANES - (CONV_K - 1) + j
        acc = acc + up[off:off + q] * w[j:j + 1]
    return _silu(acc)


def _ssd_p_kernel(z_ref, xs_ref, bc_ref, xsp_ref, bcp_ref, dt_ref, dtt_ref,
                  cwx_ref, cwbc_ref, cbx_ref, cbbc_ref, dtbe_ref, dtbc_ref, aloge_ref, alogc_ref,
                  dske_ref, ng_ref, e_ref, y_ref, st_ref, s_scr, *, n_chunks):
    c = pl.program_id(1)
    q = CHUNK
    rep = SSM_HEADS // SSM_GROUPS
    gw = rep * SSM_HEAD_DIM

    @pl.when(c == 0)
    def _():
        s_scr[...] = jnp.zeros_like(s_scr)

    has_prev = (c > 0).astype(F32)
    xs = _conv_silu(xs_ref[...], xsp_ref[...] * has_prev, cwx_ref[...], cbx_ref[...])
    bcm = _conv_silu(bc_ref[...], bcp_ref[...] * has_prev, cwbc_ref[...], cbbc_ref[...])

    ri = lax.broadcasted_iota(jnp.int32, (q, q), 0)
    ci = lax.broadcasted_iota(jnp.int32, (q, q), 1)
    tril = ri >= ci
    ones_tril = jnp.where(tril, 1.0, 0.0).astype(BF16)
    ones_triu = jnp.where(ri <= ci, 1.0, 0.0).astype(BF16)

    dt_e = _softplus(_exact_dot_left(_split3(dt_ref[...]), e_ref[...]) + dtbe_ref[...])
    a_e = dt_e * (-jnp.exp(aloge_ref[...]))
    a_hi, a_mid, a_lo = _split3(a_e)
    cum_e = _dot(ones_tril, a_hi) + _dot(ones_tril, a_mid) + _dot(ones_tril, a_lo)
    dt_t = _softplus(dtt_ref[...] + dtbc_ref[...])
    a_t = dt_t * (-jnp.exp(alogc_ref[...]))
    cum_t = _exact_dot_left(_split3(a_t), ones_triu)

    xdt = xs * dt_e
    cum_last = cum_e[q - 1:q, :]
    lane = lax.broadcasted_iota(jnp.int32, (q, LANES), 1)
    z = z_ref[...]
    ys = []
    for g in range(SSM_GROUPS):
        gs = slice(g * gw, (g + 1) * gw)
        bg = bcm[:, g * SSM_STATE:(g + 1) * SSM_STATE]
        cg = bcm[:, SSM_BC + g * SSM_STATE:SSM_BC + (g + 1) * SSM_STATE]
        bg16, cg16 = bg.astype(BF16), cg.astype(BF16)
        cb = _dot_nt(cg16, bg16)
        ydiag = []
        for pr in range(rep // 2):
            ms = []
            for hh in (2 * pr, 2 * pr + 1):
                h = g * rep + hh
                col = cum_e[:, h * SSM_HEAD_DIM:h * SSM_HEAD_DIM + 1]
                row = cum_t[h:h + 1, :]
                seg = jnp.where(tril, col - row, NEG_BIG)
                ms.append(cb * jnp.exp(seg))
            lhs = jnp.concatenate(ms, axis=1).astype(BF16)
            xslab = xdt[:, (g * rep + 2 * pr) * SSM_HEAD_DIM:(g * rep + 2 * pr + 2) * SSM_HEAD_DIM]
            xbd = jnp.concatenate([jnp.where(lane < SSM_HEAD_DIM, xslab, 0.0),
                                   jnp.where(lane >= SSM_HEAD_DIM, xslab, 0.0)], axis=0).astype(BF16)
            ydiag.append(_dot(lhs, xbd))
        ydiag = jnp.concatenate(ydiag, axis=1)
        s_old = s_scr[g]
        yoff = _dot(cg16, s_old.astype(BF16)) * jnp.exp(cum_e[:, gs])
        ys.append(ydiag + yoff)
        xw = xdt[:, gs] * jnp.exp(cum_last[:, gs] - cum_e[:, gs])
        s_scr[g] = s_old * jnp.exp(cum_last[:, gs]) + _dot(bg.T.astype(BF16), xw.astype(BF16))
    y = jnp.concatenate(ys, axis=1) + dske_ref[...] * xs
    gated = y * _silu(z)
    out = gated * lax.rsqrt(jnp.mean(gated * gated, axis=-1, keepdims=True) + LN_EPS) * ng_ref[...]
    y_ref[...] = out.astype(y_ref.dtype)

    @pl.when(c == n_chunks - 1)
    def _():
        for g in range(SSM_GROUPS):
            st_ref[0, g * gw:(g + 1) * gw, :] = s_scr[g].T


def _expand_heads(v):
    return jnp.repeat(v.astype(F32), SSM_HEAD_DIM).reshape(1, SSM_INNER)


def _head_expand_matrix():
    e = np.zeros((LANES, SSM_INNER), np.float32)
    for h in range(SSM_HEADS):
        e[h, h * SSM_HEAD_DIM:(h + 1) * SSM_HEAD_DIM] = 1.0
    return jnp.asarray(e, BF16)


def ssd_prompt(proj, dtp, dtt, n_seq, seq, conv_w, conv_b, dt_bias, a_log, d_skip, norm_g):
    b = n_seq
    nc = seq // CHUNK
    blk8 = CHUNK // SUBLANES
    const2 = lambda i, c: (0, 0)
    col_bc = COL_BC // (2 * SSM_BC)
    col_xs = COL_XS // SSM_INNER
    col_z = COL_Z // SSM_INNER
    prev_map_x = lambda i, c: (jnp.maximum((i * nc + c) * blk8 - 1, 0), col_xs)
    prev_map_bc = lambda i, c: (jnp.maximum((i * nc + c) * blk8 - 1, 0), col_bc)
    args = (
        proj, proj, proj, proj, proj, dtp, dtt,
        conv_w[:, :SSM_INNER], conv_w[:, SSM_INNER:], conv_b[:SSM_INNER].reshape(1, -1),
        conv_b[SSM_INNER:].reshape(1, -1),
        _expand_heads(dt_bias), jnp.broadcast_to(dt_bias.astype(F32)[:, None], (SSM_HEADS, CHUNK)),
        _expand_heads(a_log), jnp.broadcast_to(a_log.astype(F32)[:, None], (SSM_HEADS, CHUNK)),
        _expand_heads(d_skip), norm_g.reshape(1, SSM_INNER), _head_expand_matrix(),
    )
    in_specs = [
        pl.BlockSpec((CHUNK, SSM_INNER), lambda i, c: (i * nc + c, col_z)),
        pl.BlockSpec((CHUNK, SSM_INNER), lambda i, c: (i * nc + c, col_xs)),
        pl.BlockSpec((CHUNK, 2 * SSM_BC), lambda i, c: (i * nc + c, col_bc)),
        pl.BlockSpec((SUBLANES, SSM_INNER), prev_map_x),
        pl.BlockSpec((SUBLANES, 2 * SSM_BC), prev_map_bc),
        pl.BlockSpec((CHUNK, LANES), lambda i, c: (i * nc + c, 0)),
        pl.BlockSpec((SSM_HEADS, CHUNK), lambda i, c: (0, i * nc + c)),
        pl.BlockSpec((CONV_K, SSM_INNER), const2),
        pl.BlockSpec((CONV_K, 2 * SSM_BC), const2),
        pl.BlockSpec((1, SSM_INNER), const2),
        pl.BlockSpec((1, 2 * SSM_BC), const2),
        pl.BlockSpec((1, SSM_INNER), const2),
        pl.BlockSpec((SSM_HEADS, CHUNK), const2),
        pl.BlockSpec((1, SSM_INNER), const2),
        pl.BlockSpec((SSM_HEADS, CHUNK), const2),
        pl.BlockSpec((1, SSM_INNER), const2),
        pl.BlockSpec((1, SSM_INNER), const2),
        pl.BlockSpec((LANES, SSM_INNER), const2),
    ]
    return pl.pallas_call(
        functools.partial(_ssd_p_kernel, n_chunks=nc),
        grid=(b, nc),
        in_specs=in_specs,
        out_specs=[
            pl.BlockSpec((CHUNK, SSM_INNER), lambda i, c: (i * nc + c, 0)),
            pl.BlockSpec((1, SSM_INNER, SSM_STATE), lambda i, c: (i, 0, 0)),
        ],
        out_shape=[jax.ShapeDtypeStruct((b * seq, SSM_INNER), BF16),
                   jax.ShapeDtypeStruct((b, SSM_INNER, SSM_STATE), F32)],
        scratch_shapes=[pltpu.VMEM((SSM_GROUPS, SSM_STATE, SSM_INNER // SSM_GROUPS), F32)],
        compiler_params=_cparams(2),
        name="ssd_prompt",
    )(*args)


def _group_expand_matrix():
    gw = SSM_INNER // SSM_GROUPS
    m = np.zeros((SSM_BC, SSM_INNER), np.float32)
    for g in range(SSM_GROUPS):
        m[g * SSM_STATE:(g + 1) * SSM_STATE, g * gw:(g + 1) * gw] = 1.0
    return jnp.asarray(m, BF16)


def _ssd_s_pre_kernel(*refs, lt):
    xs_refs = refs[:lt]
    bc_refs = refs[lt:2 * lt]
    dt_refs = refs[2 * lt:3 * lt]
    (cx_ref, cbc_ref, cwx_ref, cwbc_ref, cbx_ref, cbbc_ref, dtbe_ref, aloge_ref, dske_ref, e_ref,
     gmat_ref, c_ref, b_ref, xw_ref, dec_ref, yd_ref, ec_ref) = refs[3 * lt:]
    nprev = CONV_K - 1
    ux = [cx_ref[j] for j in range(nprev)] + [r[...] for r in xs_refs]
    ub = [cbc_ref[j] for j in range(nprev)] + [r[...] for r in bc_refs]
    cwx, cwbc = cwx_ref[...], cwbc_ref[...]
    neg_a = -jnp.exp(aloge_ref[...])
    xs, bm, cm, xdt, cum = [], [], [], [], []
    run = None
    for t in range(lt):
        ax = cbx_ref[...]
        ab = cbbc_ref[...]
        for j in range(CONV_K):
            ax = ax + ux[t + j] * cwx[j:j + 1]
            ab = ab + ub[t + j] * cwbc[j:j + 1]
        x_t = _silu(ax)
        bc_t = _silu(ab)
        dt_e = _softplus(_exact_dot_left(_split3(dt_refs[t][...]), e_ref[...]) + dtbe_ref[...])
        a_t = dt_e * neg_a
        run = a_t if run is None else run + a_t
        xs.append(x_t)
        bm.append(bc_t[:, :SSM_BC])
        cm.append(bc_t[:, SSM_BC:])
        xdt.append(x_t * dt_e)
        cum.append(run)
    for i in range(lt):
        yd = dske_ref[...] * xs[i]
        for j in range(i + 1):
            hi, mid, _ = _split3(cm[i] * bm[j])
            cbe = _dot(hi, gmat_ref[...]) + _dot(mid, gmat_ref[...])
            yd = yd + cbe * jnp.exp(cum[i] - cum[j]) * xdt[j]
        yd_ref[i] = yd
        ec_ref[i] = jnp.exp(cum[i])
        c_ref[i] = cm[i]
        b_ref[i] = bm[i]
        xw_ref[i] = xdt[i] * jnp.exp(cum[lt - 1] - cum[i])
    dec_ref[...] = jnp.exp(cum[lt - 1])


def _rows_block(rows, total):
    c = rows[0].shape[1]
    rid = lax.broadcasted_iota(jnp.int32, (SUBLANES, c), 0)
    acc = jnp.zeros((SUBLANES, c), F32)
    for j, r in enumerate(rows):
        acc = jnp.where(rid == j, jnp.broadcast_to(r, (SUBLANES, c)), acc)
    if total == SUBLANES:
        return acc
    return jnp.concatenate([acc, jnp.zeros((total - SUBLANES, c), F32)], axis=0)


def _ssd_s_state_kernel_inplace(c_ref, b_ref, xw_ref, dec_ref, h0_ref, prev_ref, hn_ref, yr_ref, *, lt):
    del prev_ref
    _ssd_s_state_kernel(c_ref, b_ref, xw_ref, dec_ref, h0_ref, hn_ref, yr_ref, lt=lt)


def _ssd_s_state_kernel(c_ref, b_ref, xw_ref, dec_ref, h0_ref, hn_ref, yr_ref, *, lt, slot=0,
                        fill_slots=None):
    b = pl.program_id(0)
    gw = SSM_INNER // SSM_GROUPS
    c8 = _rows_block([c_ref[i, pl.ds(b, 1), :] for i in range(lt)], SUBLANES).astype(BF16)
    b128 = _rows_block([b_ref[i, pl.ds(b, 1), :] for i in range(lt)], LANES).astype(BF16)
    xaug = _rows_block([xw_ref[i, pl.ds(b, 1), :] for i in range(lt)] + [dec_ref[pl.ds(b, 1), :]], LANES)
    for g in range(SSM_GROUPS):
        hg = h0_ref[0, g * gw:(g + 1) * gw, :]
        yraw = _dot_nt(c8[:, g * SSM_STATE:(g + 1) * SSM_STATE], hg.astype(BF16))
        for i in range(lt):
            yr_ref[i, pl.ds(b, 1), g * gw:(g + 1) * gw] = yraw[i:i + 1, :]
        tr = xaug[:, g * gw:(g + 1) * gw].T
        s = _dot(tr.astype(BF16), b128[:, g * SSM_STATE:(g + 1) * SSM_STATE])
        new = hg * tr[:, lt:lt + 1] + s
        if fill_slots is None:
            hn_ref[0, g * gw:(g + 1) * gw, :] = new
        else:
            hn_ref[slot, 0, g * gw:(g + 1) * gw, :] = new
    if fill_slots is not None:
        for other in fill_slots:
            hn_ref[other] = jnp.zeros(hn_ref.shape[1:], hn_ref.dtype)


def _ssd_s_post_kernel(*refs, lt):
    z_refs = refs[:lt]
    yd_ref, ec_ref, yr_ref, ng_ref, o_ref = refs[lt:]
    nb = z_refs[0].shape[0]
    for i in range(lt):
        y = yd_ref[i] + ec_ref[i] * yr_ref[i]
        gated = y * _silu(z_refs[i][...])
        out = gated * lax.rsqrt(jnp.mean(gated * gated, axis=-1, keepdims=True) + LN_EPS) * ng_ref[...]
        o_ref[i * nb:(i + 1) * nb, :] = out.astype(o_ref.dtype)


def ssd_sample(proj, dtp, tp, nb, lt, conv_state, state_all, new_states, layer, conv_w, conv_b, dt_bias,
               a_log, d_skip, norm_g):
    row0 = tp // nb
    cs = jnp.transpose(conv_state, (1, 0, 2))
    one = lambda i: (0, 0)
    one3 = lambda i: (0, 0, 0)

    def rows(t, width, col):
        return pl.BlockSpec((nb, width), lambda i: (row0 + t, col // width))

    in_specs = ([rows(t, SSM_INNER, COL_XS) for t in range(lt)]
                + [rows(t, 2 * SSM_BC, COL_BC) for t in range(lt)]
                + [pl.BlockSpec((nb, LANES), lambda i, t=t: (row0 + t, 0)) for t in range(lt)]
                + [pl.BlockSpec((CONV_K - 1, nb, SSM_INNER), one3),
                   pl.BlockSpec((CONV_K - 1, nb, 2 * SSM_BC), one3),
                   pl.BlockSpec((CONV_K, SSM_INNER), one),
                   pl.BlockSpec((CONV_K, 2 * SSM_BC), one),
                   pl.BlockSpec((1, SSM_INNER), one),
                   pl.BlockSpec((1, 2 * SSM_BC), one),
                   pl.BlockSpec((1, SSM_INNER), one),
                   pl.BlockSpec((1, SSM_INNER), one),
                   pl.BlockSpec((1, SSM_INNER), one),
                   pl.BlockSpec((LANES, SSM_INNER), one),
                   pl.BlockSpec((SSM_BC, SSM_INNER), one)])
    f3 = lambda w: jax.ShapeDtypeStruct((lt, nb, w), F32)
    c_a, b_a, xw_a, dec_a, yd_a, ec_a = pl.pallas_call(
        functools.partial(_ssd_s_pre_kernel, lt=lt),
        grid=(1,),
        in_specs=in_specs,
        out_specs=[pl.BlockSpec((lt, nb, SSM_BC), one3), pl.BlockSpec((lt, nb, SSM_BC), one3),
                   pl.BlockSpec((lt, nb, SSM_INNER), one3), pl.BlockSpec((nb, SSM_INNER), one),
                   pl.BlockSpec((lt, nb, SSM_INNER), one3), pl.BlockSpec((lt, nb, SSM_INNER), one3)],
        out_shape=[f3(SSM_BC), f3(SSM_BC), f3(SSM_INNER), jax.ShapeDtypeStruct((nb, SSM_INNER), F32),
                   f3(SSM_INNER), f3(SSM_INNER)],
        compiler_params=_cparams(1),
        name="ssd_sample_pre",
    )(*([proj] * (2 * lt)), *([dtp] * lt), cs[:, :, :SSM_INNER], cs[:, :, SSM_INNER:],
      conv_w[:, :SSM_INNER], conv_w[:, SSM_INNER:], conv_b[:SSM_INNER].reshape(1, -1),
      conv_b[SSM_INNER:].reshape(1, -1), _expand_heads(dt_bias), _expand_heads(a_log),
      _expand_heads(d_skip), _head_expand_matrix(), _group_expand_matrix())

    depth = state_all.shape[0]
    h0r = state_all.reshape(depth, nb, SSM_INNER, SSM_STATE)
    state_specs = [pl.BlockSpec((lt, nb, SSM_BC), one3), pl.BlockSpec((lt, nb, SSM_BC), one3),
                   pl.BlockSpec((lt, nb, SSM_INNER), one3), pl.BlockSpec((nb, SSM_INNER), one),
                   pl.BlockSpec((None, 1, SSM_INNER, SSM_STATE), lambda i: (layer, i, 0, 0))]
    hn_shape = jax.ShapeDtypeStruct((depth, nb, SSM_INNER, SSM_STATE), F32)
    if new_states is None:
        fill = tuple(s for s in range(depth) if s != layer)
        hn, yr = pl.pallas_call(
            functools.partial(_ssd_s_state_kernel, lt=lt, slot=layer, fill_slots=fill),
            grid=(nb,),
            in_specs=state_specs,
            out_specs=[pl.BlockSpec((depth, 1, SSM_INNER, SSM_STATE), lambda i: (0, i, 0, 0)),
                       pl.BlockSpec((lt, nb, SSM_INNER), one3)],
            out_shape=[hn_shape, f3(SSM_INNER)],
            compiler_params=_cparams(1),
            name="ssd_sample_state",
        )(c_a, b_a, xw_a, dec_a, h0r)
    else:
        hn, yr = pl.pallas_call(
            functools.partial(_ssd_s_state_kernel_inplace, lt=lt),
            grid=(nb,),
            in_specs=state_specs + [pl.BlockSpec(memory_space=pl.ANY)],
            out_specs=[pl.BlockSpec((None, 1, SSM_INNER, SSM_STATE), lambda i: (layer, i, 0, 0)),
                       pl.BlockSpec((lt, nb, SSM_INNER), one3)],
            out_shape=[hn_shape, f3(SSM_INNER)],
            input_output_aliases={5: 0},
            compiler_params=_cparams(1),
            name="ssd_sample_state",
        )(c_a, b_a, xw_a, dec_a, h0r, new_states)

    ssm = pl.pallas_call(
        functools.partial(_ssd_s_post_kernel, lt=lt),
        grid=(1,),
        in_specs=([rows(t, SSM_INNER, COL_Z) for t in range(lt)]
                  + [pl.BlockSpec((lt, nb, SSM_INNER), one3)] * 3 + [pl.BlockSpec((1, SSM_INNER), one)]),
        out_specs=pl.BlockSpec((lt * nb, SSM_INNER), one),
        out_shape=jax.ShapeDtypeStruct((lt * nb, SSM_INNER), BF16),
        compiler_params=_cparams(1),
        name="ssd_sample_post",
    )(*([proj] * lt), yd_a, ec_a, yr, norm_g.reshape(1, SSM_INNER))
    return ssm, hn


def _xattn_kernel(q_ref, k_ref, v_ref, o_ref, *, nh, bseq, tq):
    scale = MEM_HEAD_DIM ** -0.5
    for b in range(bseq):
        rows = slice(b * tq, (b + 1) * tq)
        for h in range(nh):
            sl = slice(h * MEM_HEAD_DIM, (h + 1) * MEM_HEAD_DIM)
            s = _dot_nt(q_ref[rows, sl].astype(BF16), k_ref[b, :, h, :].astype(BF16)) * scale
            m = jnp.max(s, axis=-1, keepdims=True)
            p = jnp.exp(s - m)
            den = jnp.sum(p, axis=-1, keepdims=True)
            o = _dot(p.astype(BF16), v_ref[b, :, h, :].astype(BF16)) / den
            o_ref[rows, sl] = o.astype(o_ref.dtype)


def cross_attention(q, k, v, *, layer, n_seq, seq, tq, bseq, name):
    w = q.shape[1]
    _, _, m, nh, dh = k.shape
    nq = seq // tq
    assert bseq == 1 or nq == 1
    kv_spec = pl.BlockSpec((None, bseq, m, nh, dh), lambda i, n: (layer, i, 0, 0, 0))
    return pl.pallas_call(
        functools.partial(_xattn_kernel, nh=nh, bseq=bseq, tq=tq),
        grid=(n_seq // bseq, nq),
        in_specs=[pl.BlockSpec((bseq * tq, w), lambda i, n: (i * nq + n, 0)), kv_spec, kv_spec],
        out_specs=pl.BlockSpec((bseq * tq, w), lambda i, n: (i * nq + n, 0)),
        out_shape=jax.ShapeDtypeStruct((n_seq * seq, w), F32),
        compiler_params=_cparams(2),
        name=name,
    )(q, k, v)


def _router_kernel(h_ref, w_ref, b_ref, o_ref):
    x = h_ref[...]
    w = w_ref[...]
    x_hi = x.astype(BF16)
    x_lo = (x - x_hi.astype(F32)).astype(BF16)
    w_hi = w.astype(BF16)
    w_lo = (w - w_hi.astype(F32)).astype(BF16)
    logits = _dot(x_hi, w_hi) + _dot(x_hi, w_lo) + _dot(x_lo, w_hi) + b_ref[...]
    lane = lax.broadcasted_iota(jnp.int32, logits.shape, 1)
    lane_f = lane.astype(F32)
    big = float(LANES)
    is_g = lane < N_EGROUPS
    lg = jnp.where(is_g, logits, NEG_BIG)
    mg = jnp.max(lg, axis=-1, keepdims=True)
    zg = jnp.sum(jnp.where(is_g, jnp.exp(lg - mg), 0.0), axis=-1, keepdims=True)
    gi = jnp.min(jnp.where(is_g & (lg == mg), lane_f, big), axis=-1, keepdims=True)
    gw = 1.0 / zg
    lo = N_EGROUPS + gi * EXPERTS_PER_GROUP
    is_e = (lane_f >= lo) & (lane_f < lo + EXPERTS_PER_GROUP)
    le = jnp.where(is_e, logits, NEG_BIG)
    me = jnp.max(le, axis=-1, keepdims=True)
    ee = jnp.where(is_e, jnp.exp(le - me), 0.0)
    pe = ee / jnp.sum(ee, axis=-1, keepdims=True)
    pe = jnp.where(is_e, pe, -1.0)
    p1 = jnp.max(pe, axis=-1, keepdims=True)
    i1 = jnp.min(jnp.where(pe == p1, lane_f, big), axis=-1, keepdims=True)
    pe2 = jnp.where(lane_f == i1, -1.0, pe)
    p2 = jnp.max(pe2, axis=-1, keepdims=True)
    i2 = jnp.min(jnp.where(pe2 == p2, lane_f, big), axis=-1, keepdims=True)
    tot = p1 + p2
    out = jnp.where(lane == 0, i1 - N_EGROUPS,
                    jnp.where(lane == 1, i2 - N_EGROUPS,
                              jnp.where(lane == 2, gw * (p1 / tot),
                                        jnp.where(lane == 3, gw * (p2 / tot), 0.0))))
    o_ref[...] = out


def moe_router(h, w_rg, b_rg, w_re, b_re):
    t, d = h.shape
    tm = _pick(t, (256, 128, 64, 32, 16, 8))
    npad = LANES - N_EGROUPS - N_EXPERTS
    w = jnp.concatenate([w_rg, w_re, jnp.zeros((d, npad), F32)], axis=1)
    b = jnp.concatenate([b_rg, b_re, jnp.zeros((npad,), F32)]).reshape(1, LANES)
    return pl.pallas_call(
        _router_kernel,
        grid=(t // tm,),
        in_specs=[pl.BlockSpec((tm, d), lambda i: (i, 0)),
                  pl.BlockSpec((d, LANES), lambda i: (0, 0)),
                  pl.BlockSpec((1, LANES), lambda i: (0, 0))],
        out_specs=pl.BlockSpec((tm, LANES), lambda i: (i, 0)),
        out_shape=jax.ShapeDtypeStruct((t, LANES), F32),
        compiler_params=_cparams(1),
        name="moe_router",
    )(h, w, b)


def _row_copy(src_hbm, dst, src_row, dst_row, sem):
    return pltpu.make_async_copy(src_hbm.at[pl.ds(src_row, 1)], dst.at[pl.ds(dst_row, 1)], sem)


def _moe_gather_kernel(tok_ref, h_ref, o_ref, buf, sem, *, tm):
    def start(r, carry):
        _row_copy(h_ref, buf, tok_ref[0, 0, r], r, sem).start()
        return carry

    def wait(r, carry):
        _row_copy(h_ref, buf, tok_ref[0, 0, r], r, sem).wait()
        return carry

    lax.fori_loop(0, tm, start, 0, unroll=DMA_LOOP_UNROLL)
    lax.fori_loop(0, tm, wait, 0, unroll=DMA_LOOP_UNROLL)
    o_ref[...] = buf[...].astype(o_ref.dtype)


def moe_gather(h, row_token, tm):
    r_total = row_token.shape[0]
    d = h.shape[1]
    nblk = r_total // tm
    return pl.pallas_call(
        functools.partial(_moe_gather_kernel, tm=tm),
        grid=(nblk,),
        in_specs=[pl.BlockSpec((1, 1, tm), lambda i: (i, 0, 0), memory_space=pltpu.SMEM),
                  pl.BlockSpec(memory_space=pl.ANY)],
        out_specs=pl.BlockSpec((tm, d), lambda i: (i, 0)),
        out_shape=jax.ShapeDtypeStruct((r_total, d), BF16),
        scratch_shapes=[pltpu.VMEM((tm, d), h.dtype), pltpu.SemaphoreType.DMA(())],
        compiler_params=_cparams(1),
        name="moe_gather",
    )(row_token.reshape(nblk, 1, tm), h)


def _moe_up_kernel(e_ref, f_ref, t_ref, to_ref, fo_ref, v_ref, x_ref, wg_ref, wu_ref, o_ref, wg16, wu16):
    s = pl.program_id(0)

    @pl.when(v_ref[s] == 2)
    def _():
        wg16[...] = wg_ref[...].astype(BF16)
        wu16[...] = wu_ref[...].astype(BF16)

    @pl.when(v_ref[s] > 0)
    def _():
        x = x_ref[...]
        a = _dot(x, wg16[...])
        u = _dot(x, wu16[...])
        o_ref[...] = (_silu(a) * u).astype(o_ref.dtype)

    @pl.when(v_ref[s] == 0)
    def _():
        o_ref[...] = jnp.zeros_like(o_ref)


def moe_up(x_sorted, w_gate, w_up, layer, plan, tm, fchunk):
    r_total, d = x_sorted.shape
    ff = w_gate.shape[-1]
    n_steps = plan["step_e"].shape[0]
    grid_spec = pltpu.PrefetchScalarGridSpec(
        num_scalar_prefetch=6,
        grid=(n_steps,),
        in_specs=[
            pl.BlockSpec((tm, d), lambda s, e, f, t, to, fo, v: (t[s], 0)),
            pl.BlockSpec((None, None, d, fchunk), lambda s, e, f, t, to, fo, v: (layer, e[s], 0, f[s])),
            pl.BlockSpec((None, None, d, fchunk), lambda s, e, f, t, to, fo, v: (layer, e[s], 0, f[s])),
        ],
        out_specs=pl.BlockSpec((tm, fchunk), lambda s, e, f, t, to, fo, v: (to[s], fo[s])),
        scratch_shapes=[pltpu.VMEM((d, fchunk), BF16), pltpu.VMEM((d, fchunk), BF16)],
    )
    return pl.pallas_call(
        _moe_up_kernel,
        grid_spec=grid_spec,
        out_shape=jax.ShapeDtypeStruct((r_total, ff), BF16),
        compiler_params=_cparams(1),
        name="moe_up",
    )(plan["step_e"], plan["step_f"], plan["step_t"], plan["step_to"], plan["step_fo"], plan["step_v"],
      x_sorted, w_gate, w_up)


def _moe_down_kernel(e_ref, v_ref, x_ref, w_ref, o_ref, w16):
    s = pl.program_id(0)

    @pl.when(v_ref[s] == 2)
    def _():
        w16[...] = w_ref[...].astype(BF16)

    @pl.when(v_ref[s] > 0)
    def _():
        o_ref[...] = _dot(x_ref[...], w16[...])

    @pl.when(v_ref[s] == 0)
    def _():
        o_ref[...] = jnp.zeros_like(o_ref)


def moe_down(hid, w_down, layer, tile_e, tile_v, tm):
    r_total, ff = hid.shape
    d = w_down.shape[-1]
    n_tiles = r_total // tm
    grid_spec = pltpu.PrefetchScalarGridSpec(
        num_scalar_prefetch=2,
        grid=(n_tiles,),
        in_specs=[
            pl.BlockSpec((tm, ff), lambda s, e, v: (v[n_tiles + s], 0)),
            pl.BlockSpec((None, None, ff, d), lambda s, e, v: (layer, e[s], 0, 0)),
        ],
        out_specs=pl.BlockSpec((tm, d), lambda s, e, v: (s, 0)),
        scratch_shapes=[pltpu.VMEM((ff, d), BF16)],
    )
    return pl.pallas_call(
        _moe_down_kernel,
        grid_spec=grid_spec,
        out_shape=jax.ShapeDtypeStruct((r_total, d), F32),
        compiler_params=_cparams(1),
        name="moe_down",
    )(tile_e, tile_v, hid, w_down)


def _moe_combine_kernel(pos_ref, y_ref, r_ref, h_ref, g_ref, b_ref, o1_ref, o2_ref, ybuf, sem, *,
                        tm, alpha, n_first):
    def start(i, carry):
        _row_copy(y_ref, ybuf.at[0], pos_ref[0, 0, 2 * i], i, sem).start()
        _row_copy(y_ref, ybuf.at[1], pos_ref[0, 0, 2 * i + 1], i, sem).start()
        return carry

    def wait(i, carry):
        _row_copy(y_ref, ybuf.at[0], pos_ref[0, 0, 2 * i], i, sem).wait()
        _row_copy(y_ref, ybuf.at[1], pos_ref[0, 0, 2 * i + 1], i, sem).wait()
        return carry

    lax.fori_loop(0, tm, start, 0, unroll=DMA_LOOP_UNROLL)
    lax.fori_loop(0, tm, wait, 0, unroll=DMA_LOOP_UNROLL)
    route = r_ref[...]
    ff = ybuf[0] * route[:, 2:3] + ybuf[1] * route[:, 3:4]
    h = _ln_rows(alpha * h_ref[...] + ff, g_ref[...], b_ref[...])
    if n_first is None:
        o1_ref[...] = h
        o2_ref[...] = h.astype(BF16)
    else:
        i = pl.program_id(0)

        @pl.when(i < n_first)
        def _():
            o1_ref[...] = h

        @pl.when(i >= n_first)
        def _():
            o2_ref[...] = h


def moe_combine(y_sorted, pos, route, h, g, b, *, alpha, split_rows=None):
    t, d = h.shape
    tm = _pick(t if split_rows is None else math.gcd(split_rows, t - split_rows), (256, 128, 64, 32, 16, 8))
    nblk = t // tm
    if split_rows is None:
        n_first = None
        out_specs = [pl.BlockSpec((tm, d), lambda i: (i, 0)), pl.BlockSpec((tm, d), lambda i: (i, 0))]
        out_shape = [jax.ShapeDtypeStruct((t, d), F32), jax.ShapeDtypeStruct((t, d), BF16)]
    else:
        n_first = split_rows // tm
        out_specs = [pl.BlockSpec((tm, d), lambda i: (jnp.minimum(i, n_first - 1), 0)),
                     pl.BlockSpec((tm, d), lambda i: (jnp.maximum(i - n_first, 0), 0))]
        out_shape = [jax.ShapeDtypeStruct((split_rows, d), F32),
                     jax.ShapeDtypeStruct((t - split_rows, d), F32)]
    return pl.pallas_call(
        functools.partial(_moe_combine_kernel, tm=tm, alpha=alpha, n_first=n_first),
        grid=(nblk,),
        in_specs=[pl.BlockSpec((1, 1, 2 * tm), lambda i: (i, 0, 0), memory_space=pltpu.SMEM),
                  pl.BlockSpec(memory_space=pl.ANY),
                  pl.BlockSpec((tm, LANES), lambda i: (i, 0)),
                  pl.BlockSpec((tm, d), lambda i: (i, 0)),
                  pl.BlockSpec((1, d), lambda i: (0, 0)),
                  pl.BlockSpec((1, d), lambda i: (0, 0))],
        out_specs=out_specs,
        out_shape=out_shape,
        scratch_shapes=[pltpu.VMEM((2, tm, d), F32), pltpu.SemaphoreType.DMA(())],
        compiler_params=_cparams(1),
        name="moe_combine",
    )(pos.reshape(nblk, 1, 2 * tm), y_sorted, route, h, g.reshape(1, d), b.reshape(1, d))


def moe_plan(route, tm, n_f):
    t = route.shape[0]
    eid = route[:, :2].astype(jnp.int32).reshape(-1)
    onehot = (eid[:, None] == jnp.arange(N_EXPERTS, dtype=jnp.int32)[None, :]).astype(jnp.int32)
    csum = jnp.cumsum(onehot, axis=0)
    rank = jnp.sum((csum - onehot) * onehot, axis=1)
    counts = csum[-1]
    tiles_e = (counts + tm - 1) // tm
    tile_end = jnp.cumsum(tiles_e)
    tile_start = tile_end - tiles_e
    n_used = tile_end[-1]
    n_tiles = (2 * t + N_EXPERTS * (tm - 1)) // tm + 1
    r_total = n_tiles * tm
    dest = tile_start[eid] * tm + rank
    row_token = (jnp.arange(r_total, dtype=jnp.int32) % t).at[dest].set(
        jnp.arange(2 * t, dtype=jnp.int32) // 2)
    tile_ids = jnp.arange(n_tiles, dtype=jnp.int32)
    tile_clamped = jnp.minimum(tile_ids, n_used - 1)
    tile_e = jnp.sum(tile_end[None, :] <= tile_clamped[:, None], axis=1).astype(jnp.int32)
    tile_first = tile_ids == tile_start[tile_e]
    tile_valid = jnp.where(tile_ids < n_used, 1 + tile_first.astype(jnp.int32), 0)
    tile_v = jnp.concatenate([tile_valid, tile_clamped])
    n_steps = n_f * n_tiles
    sidx = jnp.arange(n_steps, dtype=jnp.int32)
    s_cl = jnp.minimum(sidx, n_f * n_used - 1)
    step_e = jnp.sum((n_f * tile_end)[None, :] <= s_cl[:, None], axis=1).astype(jnp.int32)
    local = s_cl - n_f * tile_start[step_e]
    te = jnp.maximum(tiles_e[step_e], 1)
    step_f = (local // te).astype(jnp.int32)
    step_t = (tile_start[step_e] + local % te).astype(jnp.int32)
    valid = sidx < n_f * n_used
    step_v = jnp.where(valid, 1 + (local % te == 0).astype(jnp.int32), 0)
    spare = sidx - n_f * n_used
    step_to = jnp.where(valid, step_t, n_used + spare // n_f).astype(jnp.int32)
    step_fo = jnp.where(valid, step_f, spare % n_f).astype(jnp.int32)
    return dict(row_token=row_token, pos=dest.astype(jnp.int32), tile_e=tile_e, tile_v=tile_v,
                step_e=step_e, step_f=step_f, step_t=step_t, step_to=step_to, step_fo=step_fo,
                step_v=step_v)


def hierarchical_moe_ln(hf, layer, w_rg, b_rg, w_re, b_re, w_gate, w_up, w_down, ln_g, ln_b, *, alpha,
                        split_rows=None):
    route = moe_router(hf, w_rg, b_rg, w_re, b_re)
    n_f = EXPERT_FF // MOE_FCHUNK
    plan = moe_plan(route, MOE_TM, n_f)
    x_sorted = moe_gather(hf, plan["row_token"], MOE_TM)
    hid = moe_up(x_sorted, w_gate, w_up, layer, plan, MOE_TM, MOE_FCHUNK)
    y_sorted = moe_down(hid, w_down, layer, plan["tile_e"], plan["tile_v"], MOE_TM)
    return moe_combine(y_sorted, plan["pos"], route, hf, ln_g, ln_b, alpha=alpha, split_rows=split_rows)


def _to_seq_major(x_tm, lt, nb, pad_to):
    w = x_tm.shape[1]
    x = jnp.transpose(x_tm.reshape(lt, nb, w), (1, 0, 2))
    x = jnp.pad(x, ((0, 0), (0, pad_to - lt), (0, 0)))
    return x.reshape(nb * pad_to, w)


def _to_time_major(x_sm, lt, nb, pad_to):
    w = x_sm.shape[1]
    x = x_sm.reshape(nb, pad_to, w)[:, :lt]
    return jnp.transpose(x, (1, 0, 2)).reshape(lt * nb, w)


def kernel(x_prompt, x_sample, mem_prompt, cache_swa_k, cache_swa_v, cache_mem_k, cache_mem_v, state_conv, state_ssm, ln_in_g, ln_in_b, w_in, attn_sinks, gm_ln_g, gm_ln_b, gm_ws, gm_bs, conv_w, conv_b, dt_bias, a_log, d_skip, ssm_norm_g, w_pa, w_pb, w_pc, w_o, ln1_g, ln1_b, w_cq, w_ck, w_cv, w_co, ln2_g, ln2_b, w_rg, b_rg, w_re, b_re, w_gate, w_up, w_down, ln3_g, ln3_b):
    bp, seq, d = x_prompt.shape
    nb, lt, _ = x_sample.shape
    depth = w_in.shape[0]
    mem_len = mem_prompt.shape[1]
    past_len = PAST_LEN
    wb = cache_swa_k.shape[2]
    assert wb == WINDOW and seq % CHUNK == 0 and lt <= SUBLANES
    tp, ts = bp * seq, nb * lt
    alpha = (2 * depth) ** 0.25
    qpad = SUBLANES

    xp = x_prompt.reshape(tp, d)
    xs = jnp.transpose(x_sample, (1, 0, 2)).reshape(ts, d)
    hf, hb = ln_in(xp, xs, ln_in_g, ln_in_b)
    cos_t, sin_t = rope_tables(tp, seq, ts, nb, past_len)
    mem_b = mem_prompt.reshape(bp * mem_len, d).astype(BF16)

    o_k, o_v, o_gu, o_gv = Q_W, Q_W + KV_W, Q_W + 2 * KV_W, Q_W + 2 * KV_W + GM_W
    o_z = o_gv + GM_W
    o_xbc = o_z + SSM_INNER
    o_dt = o_xbc + CONV_DIM
    o_gates = o_dt + SSM_HEADS

    outs = {k: [] for k in ("p_k", "p_v", "p_mk", "p_mv", "p_conv", "p_ssm", "p_gv",
                            "s_k", "s_v", "s_conv", "s_ssm", "s_gv")}
    n_qblk = seq // WINDOW
    s_states = None
    for l in range(depth):
        wl = w_in[l]
        w_main = jnp.concatenate(
            [wl[:, :Q_W], wl[:, o_gu:o_gv], wl[:, o_gv:o_z], wl[:, o_z:o_xbc], wl[:, o_gates:],
             wl[:, o_xbc:o_dt], wl[:, o_k:o_v], wl[:, o_v:o_gu]], axis=1).astype(BF16)
        w_dt = jnp.pad(wl[:, o_dt:o_gates], ((0, 0), (0, LANES - SSM_HEADS))).astype(BF16)
        proj = matmul(hb, w_main, name="in_proj")
        dtp = matmul(hb, w_dt, name="dt_proj")
        q_rot, k_rot = rope_qk(proj, cos_t, sin_t)

        kcol, vcol = 0, COL_V // KV_W
        att_p = swa_attention(
            attn_sinks[l], q_rot, k_rot, k_rot, proj, proj,
            n_seq=bp, n_blk=n_qblk, qb=WINDOW, prev_from_block0=False,
            kp_map=lambda i, n: (jnp.maximum(i * n_qblk + n - 1, 0), kcol),
            kc_map=lambda i, n: (i * n_qblk + n, kcol),
            vp_map=lambda i, n: (jnp.maximum(i * n_qblk + n - 1, 0), vcol),
            vc_map=lambda i, n: (i * n_qblk + n, vcol),
            out_dtype=BF16, name="swa_prompt")
        k_s_tm = k_rot[tp:]
        v_s_tm = proj[tp:, COL_V:COL_V + KV_W]
        q_s = _to_seq_major(q_rot[tp:], lt, nb, qpad)
        k_s = _to_seq_major(k_s_tm, lt, nb, qpad)
        v_s = _to_seq_major(v_s_tm, lt, nb, qpad)
        att_s8 = swa_attention_cached(attn_sinks[l], q_s, k_s, v_s, cache_swa_k, cache_swa_v, layer=l,
                                      n_seq=nb, qb=qpad, bseq=_pick(nb, (8, 4, 2, 1)), name="swa_sample")
        att_s = _to_time_major(att_s8, lt, nb, qpad).astype(BF16)
        last_w = lambda a, c0: jnp.stack(
            [a[(i + 1) * seq - WINDOW:(i + 1) * seq, c0:c0 + KV_W] for i in range(bp)]
        ).reshape(bp, WINDOW, N_KV_HEADS, HEAD_DIM)
        outs["p_k"].append(last_w(k_rot, 0))
        outs["p_v"].append(last_w(proj, COL_V))
        k_new = jnp.transpose(k_s_tm.reshape(lt, nb, N_KV_HEADS, HEAD_DIM), (1, 0, 2, 3))
        v_new = jnp.transpose(v_s_tm.reshape(lt, nb, N_KV_HEADS, HEAD_DIM), (1, 0, 2, 3))
        outs["s_k"].append(jnp.concatenate([cache_swa_k[l], k_new], axis=1)[:, -wb:])
        outs["s_v"].append(jnp.concatenate([cache_swa_v[l], v_new], axis=1)[:, -wb:])

        gm_p, vg_last = gmlp_prompt(proj, bp, seq, gm_ws[l], gm_bs[l], gm_ln_g[l], gm_ln_b[l])
        gm_s, vg_s = gmlp_sample(proj, tp, nb, lt, gm_ws[l], gm_bs[l], gm_ln_g[l], gm_ln_b[l])
        outs["p_gv"].append(vg_last.reshape(bp, CHUNK, GM_GROUPS, GM_GROUP_DIM))
        outs["s_gv"].append(jnp.transpose(vg_s.reshape(lt, nb, GM_GROUPS, GM_GROUP_DIM), (1, 0, 2, 3)))

        dtt = jnp.transpose(dtp[:tp, :SSM_HEADS])
        y_p, st_p = ssd_prompt(proj, dtp, dtt, bp, seq, conv_w[l], conv_b[l], dt_bias[l], a_log[l],
                               d_skip[l], ssm_norm_g[l])
        ssm_s, s_states = ssd_sample(proj, dtp, tp, nb, lt, state_conv[l], state_ssm, s_states, l, conv_w[l],
                                     conv_b[l], dt_bias[l], a_log[l], d_skip[l], ssm_norm_g[l])
        outs["p_conv"].append(jnp.stack(
            [proj[(i + 1) * seq - (CONV_K - 1):(i + 1) * seq, COL_XS:COL_XS + CONV_DIM] for i in range(bp)]))
        xbc_s = jnp.transpose(proj[tp:, COL_XS:COL_XS + CONV_DIM].reshape(lt, nb, CONV_DIM), (1, 0, 2))
        outs["s_conv"].append(jnp.concatenate([state_conv[l], xbc_s], axis=1)[:, -(CONV_K - 1):])
        outs["p_ssm"].append(st_p.reshape(bp, SSM_HEADS, SSM_HEAD_DIM, SSM_STATE))

        merged = gated_merge(att_p, att_s, gm_p, gm_s, y_p, ssm_s, w_pa[l].astype(BF16),
                             w_pb[l].astype(BF16), w_pc[l].astype(BF16), proj)
        h1f, h1b = matmul_ln(merged, w_o[l].astype(BF16), hf, ln1_g[l], ln1_b[l], alpha=alpha, name="out_proj_ln1")

        qc = matmul(h1b, w_cq[l].astype(BF16), name="xattn_q")
        pmk = matmul(mem_b, w_ck[l].astype(BF16), name="mem_k")
        pmv = matmul(mem_b, w_cv[l].astype(BF16), name="mem_v")
        pmk5 = pmk.reshape(1, bp, mem_len, MEM_HEADS, MEM_HEAD_DIM)
        pmv5 = pmv.reshape(1, bp, mem_len, MEM_HEADS, MEM_HEAD_DIM)
        outs["p_mk"].append(pmk5[0])
        outs["p_mv"].append(pmv5[0])
        tq = _pick(seq, (512, 256, 128))
        o_p = cross_attention(qc, pmk5, pmv5, layer=0, n_seq=bp, seq=seq, tq=tq, bseq=1,
                              name="xattn_prompt")
        qc_s = _to_seq_major(qc[tp:], lt, nb, qpad)
        o_s8 = cross_attention(qc_s, cache_mem_k, cache_mem_v, layer=l, n_seq=nb, seq=qpad, tq=qpad,
                               bseq=_pick(nb, (4, 2, 1)), name="xattn_sample")
        o_all = jnp.concatenate([o_p, _to_time_major(o_s8, lt, nb, qpad)], axis=0).astype(BF16)
        h2f, h2b = matmul_ln(o_all, w_co[l].astype(BF16), h1f, ln2_g[l], ln2_b[l], alpha=alpha, name="xattn_out_ln2")

        hf, hb = hierarchical_moe_ln(h2f, l, w_rg[l], b_rg[l], w_re[l], b_re[l], w_gate, w_up, w_down,
                                     ln3_g[l], ln3_b[l], alpha=alpha,
                                     split_rows=tp if l == depth - 1 else None)

    y_prompt = hf.reshape(bp, seq, d)
    y_sample = jnp.transpose(hb.reshape(lt, nb, d), (1, 0, 2))
    st = lambda k: jnp.stack(outs[k])
    s_ssm = s_states.reshape(depth, nb, SSM_HEADS, SSM_HEAD_DIM, SSM_STATE)
    return (y_prompt, y_sample, st("p_k"), st("p_v"), st("p_mk"), st("p_mv"), st("p_conv"), st("p_ssm"),
            st("p_gv"), st("s_k"), st("s_v"), st("s_conv"), s_ssm, st("s_gv"))
```

```python
import functools
import math

import numpy as np
import jax
import jax.numpy as jnp
from jax import lax
from jax.experimental import pallas as pl
from jax.experimental.pallas import tpu as pltpu

F32 = jnp.float32
BF16 = jnp.bfloat16

D_MODEL = 2048
N_HEADS = 32
N_KV_HEADS = 4
HEAD_DIM = 64
WINDOW = 128
PAST_LEN = 8192
ROPE_THETA = 10000.0
CHUNK = 128
GM_GROUPS = 16
GM_GROUP_DIM = 128
SSM_HEADS = 32
SSM_HEAD_DIM = 64
SSM_GROUPS = 4
SSM_STATE = 128
CONV_K = 4
MEM_HEADS = 4
MEM_HEAD_DIM = 128
N_EGROUPS = 4
EXPERTS_PER_GROUP = 8
N_EXPERTS = N_EGROUPS * EXPERTS_PER_GROUP
EXPERT_FF = D_MODEL // 2
Q_W = N_HEADS * HEAD_DIM
KV_W = N_KV_HEADS * HEAD_DIM
GM_W = GM_GROUPS * GM_GROUP_DIM
SSM_INNER = SSM_HEADS * SSM_HEAD_DIM
SSM_BC = SSM_GROUPS * SSM_STATE
CONV_DIM = SSM_INNER + 2 * SSM_BC
MEM_W = MEM_HEADS * MEM_HEAD_DIM
LN_EPS = 1e-5
NEG_BIG = -1e30

VMEM_LIMIT_BYTES = 52 * 1024 * 1024
LANES = 128
SUBLANES = 8

IN_TN = 512
COL_Q = 0
COL_GU = 2048
COL_GV = 4096
COL_Z = 6144
COL_XS = 8192
COL_BC = 10240
COL_DT = 11264
COL_GATES = COL_DT + SSM_HEADS
COL_K = 17920
COL_V = 18176
PROJ_W = 18432

MOE_TM = 256
DMA_LOOP_UNROLL = 8


def _cparams(n_grid):
    return pltpu.CompilerParams(
        dimension_semantics=("arbitrary",) * n_grid,
        vmem_limit_bytes=VMEM_LIMIT_BYTES,
    )


def _pick(n, prefs):
    for p in prefs:
        if n % p == 0:
            return p
    raise ValueError(f"no tile for {n} in {prefs}")


def _ln_rows(x, g, b):
    mu = jnp.mean(x, axis=-1, keepdims=True)
    xc = x - mu
    var = jnp.mean(xc * xc, axis=-1, keepdims=True)
    return xc * lax.rsqrt(var + LN_EPS) * g + b


def _sigmoid(x):
    return 1.0 / (1.0 + jnp.exp(-x))


def _silu(x):
    return x * _sigmoid(x)


def _softplus(x):
    return jnp.maximum(x, 0.0) + jnp.log1p(jnp.exp(-jnp.abs(x)))


def _gelu(x):
    return jax.nn.gelu(x, approximate=True)


def _split3(x):
    hi = x.astype(BF16)
    r1 = x - hi.astype(F32)
    mid = r1.astype(BF16)
    lo = (r1 - mid.astype(F32)).astype(BF16)
    return hi, mid, lo


def _dot(a, b):
    return jnp.dot(a, b, preferred_element_type=F32)


def _dot_nt(a, b):
    return lax.dot_general(a, b, (((1,), (1,)), ((), ())), preferred_element_type=F32)


def _exact_dot_left(pieces, m):
    acc = _dot(pieces[0], m)
    for p in pieces[1:]:
        acc = acc + _dot(p, m)
    return acc


def _ln_in_kernel(xp_ref, xs_ref, g_ref, b_ref, of_ref, ob_ref, *, n_p):
    i = pl.program_id(0)

    @pl.when(i < n_p)
    def _():
        y = _ln_rows(xp_ref[...], g_ref[...], b_ref[...])
        of_ref[...] = y
        ob_ref[...] = y.astype(BF16)

    @pl.when(i >= n_p)
    def _():
        y = _ln_rows(xs_ref[...], g_ref[...], b_ref[...])
        of_ref[...] = y
        ob_ref[...] = y.astype(BF16)


def ln_in(xp, xs, g, b):
    tp, d = xp.shape
    ts = xs.shape[0]
    tm = _pick(math.gcd(tp, ts), (256, 128, 64, 32, 16, 8))
    n_p, n_s = tp // tm, ts // tm
    t = tp + ts
    return pl.pallas_call(
        functools.partial(_ln_in_kernel, n_p=n_p),
        grid=(n_p + n_s,),
        in_specs=[
            pl.BlockSpec((tm, d), lambda i: (jnp.minimum(i, n_p - 1), 0)),
            pl.BlockSpec((tm, d), lambda i: (jnp.maximum(i - n_p, 0), 0)),
            pl.BlockSpec((1, d), lambda i: (0, 0)),
            pl.BlockSpec((1, d), lambda i: (0, 0)),
        ],
        out_specs=[
            pl.BlockSpec((tm, d), lambda i: (i, 0)),
            pl.BlockSpec((tm, d), lambda i: (i, 0)),
        ],
        out_shape=[jax.ShapeDtypeStruct((t, d), F32), jax.ShapeDtypeStruct((t, d), BF16)],
        compiler_params=_cparams(1),
        name="ln_in",
    )(xp, xs, g.reshape(1, d), b.reshape(1, d))


def _mm_kernel(x_ref, w_ref, o_ref):
    o_ref[...] = _dot(x_ref[...], w_ref[...]).astype(o_ref.dtype)


def matmul(x, w, *, out_dtype=F32, tm_prefs=(1088, 1024, 512, 256, 128, 64, 32, 16, 8),
           tn_prefs=(1280, 1024, 512, 256, 128), name="mm"):
    t, k = x.shape
    n = w.shape[1]
    tm = _pick(t, tm_prefs)
    tn = _pick(n, tn_prefs)
    return pl.pallas_call(
        _mm_kernel,
        grid=(n // tn, t // tm),
        in_specs=[
            pl.BlockSpec((tm, k), lambda j, i: (i, 0)),
            pl.BlockSpec((k, tn), lambda j, i: (0, j)),
        ],
        out_specs=pl.BlockSpec((tm, tn), lambda j, i: (i, j)),
        out_shape=jax.ShapeDtypeStruct((t, n), out_dtype),
        compiler_params=_cparams(2),
        name=name,
    )(x, w)


def in_projection(x, w_pad, layer):
    t, k = x.shape
    n_blk = PROJ_W // IN_TN
    kv_blk = Q_W // IN_TN
    assert 2 * KV_W == IN_TN and COL_K == (n_blk - 1) * IN_TN
    tm = _pick(t, (2176, 1088, 544, 32, 16, 8))

    def out_map(i, j):
        return i, jnp.where(j < kv_blk, j, jnp.where(j == kv_blk, n_blk - 1, j - 1))

    return pl.pallas_call(
        _mm_kernel,
        grid=(t // tm, n_blk),
        in_specs=[
            pl.BlockSpec((tm, k), lambda i, j: (i, 0)),
            pl.BlockSpec((None, k, IN_TN), lambda i, j: (layer, 0, j)),
        ],
        out_specs=pl.BlockSpec((tm, IN_TN), out_map),
        out_shape=jax.ShapeDtypeStruct((t, PROJ_W), F32),
        compiler_params=_cparams(2),
        name="in_proj",
    )(x, w_pad)


def _mm_ln_kernel(x_ref, w_ref, r_ref, g_ref, b_ref, of_ref, ob_ref, *, alpha):
    y = _dot(x_ref[...], w_ref[...])
    h = _ln_rows(alpha * r_ref[...] + y, g_ref[...], b_ref[...])
    of_ref[...] = h
    ob_ref[...] = h.astype(BF16)


def matmul_ln(x, w, res, g, b, *, alpha, name="mm_ln"):
    t, k = x.shape
    d = w.shape[1]
    tm = _pick(t, (256, 128, 64, 32, 16, 8))
    return pl.pallas_call(
        functools.partial(_mm_ln_kernel, alpha=alpha),
        grid=(t // tm,),
        in_specs=[
            pl.BlockSpec((tm, k), lambda i: (i, 0)),
            pl.BlockSpec((k, d), lambda i: (0, 0)),
            pl.BlockSpec((tm, d), lambda i: (i, 0)),
            pl.BlockSpec((1, d), lambda i: (0, 0)),
            pl.BlockSpec((1, d), lambda i: (0, 0)),
        ],
        out_specs=[
            pl.BlockSpec((tm, d), lambda i: (i, 0)),
            pl.BlockSpec((tm, d), lambda i: (i, 0)),
        ],
        out_shape=[jax.ShapeDtypeStruct((t, d), F32), jax.ShapeDtypeStruct((t, d), BF16)],
        compiler_params=_cparams(1),
        name=name,
    )(x, w, res, g.reshape(1, d), b.reshape(1, d))


def _merge_kernel(ap_ref, as_ref, bp_ref, bs_ref, cp_ref, cs_ref, wa_ref, wb_ref, wc_ref,
                  ga_ref, ga2_ref, gb_ref, gb2_ref, gc_ref, gc2_ref, o_ref, *, n_p, shift):
    i = pl.program_id(1)
    is_p = i < n_p
    tn = o_ref.shape[1]

    def gate(main_ref, tail_ref):
        win = jnp.concatenate([main_ref[...], tail_ref[...]], axis=1)
        return _sigmoid(win[:, shift:shift + tn])

    xa = jnp.where(is_p, ap_ref[...], as_ref[...])
    xb = jnp.where(is_p, bp_ref[...], bs_ref[...])
    xc = jnp.where(is_p, cp_ref[...], cs_ref[...])
    acc = gate(ga_ref, ga2_ref) * _dot(xa, wa_ref[...])
    acc = acc + gate(gb_ref, gb2_ref) * _dot(xb, wb_ref[...])
    acc = acc + gate(gc_ref, gc2_ref) * _dot(xc, wc_ref[...])
    o_ref[...] = acc.astype(o_ref.dtype)


def gated_merge(att_p, att_s, gm_p, gm_s, ssm_p, ssm_s, w_pa, w_pb, w_pc, proj):
    tp, k = att_p.shape
    ts = att_s.shape[0]
    d = w_pa.shape[1]
    tm = _pick(math.gcd(tp, ts), (256, 128, 64, 32, 16, 8))
    tn = 1024
    n_p, n_s = tp // tm, ts // tm
    shift = COL_GATES - COL_DT
    assert COL_DT % tn == 0 and d % tn == 0 and shift < LANES

    def xp_spec():
        return pl.BlockSpec((tm, k), lambda j, i: (jnp.minimum(i, n_p - 1), 0))

    def xs_spec():
        return pl.BlockSpec((tm, k), lambda j, i: (jnp.maximum(i - n_p, 0), 0))

    def w_spec():
        return pl.BlockSpec((k, tn), lambda j, i: (0, j))

    def g_specs(which):
        base = (COL_DT + which * d) // tn
        tail = (COL_DT + which * d) // LANES
        return [pl.BlockSpec((tm, tn), lambda j, i: (i, base + j)),
                pl.BlockSpec((tm, LANES), lambda j, i: (i, tail + (j + 1) * (tn // LANES)))]

    return pl.pallas_call(
        functools.partial(_merge_kernel, n_p=n_p, shift=shift),
        grid=(d // tn, n_p + n_s),
        in_specs=[xp_spec(), xs_spec(), xp_spec(), xs_spec(), xp_spec(), xs_spec(),
                  w_spec(), w_spec(), w_spec()] + g_specs(0) + g_specs(1) + g_specs(2),
        out_specs=pl.BlockSpec((tm, tn), lambda j, i: (i, j)),
        out_shape=jax.ShapeDtypeStruct((tp + ts, d), BF16),
        compiler_params=_cparams(2),
        name="gated_merge",
    )(att_p, att_s, gm_p, gm_s, ssm_p, ssm_s, w_pa, w_pb, w_pc, *([proj] * 6))


def _rope_block(x, cos, sin_signed, first_half):
    outs = []
    for c in range(x.shape[1] // LANES):
        xc = x[:, c * LANES:(c + 1) * LANES]
        fwd = pltpu.roll(xc, LANES - HEAD_DIM // 2, axis=1)
        bwd = pltpu.roll(xc, HEAD_DIM // 2, axis=1)
        partner = jnp.where(first_half, fwd, bwd)
        outs.append(xc * cos + partner * sin_signed)
    return outs


def _rope_kernel(q_ref, k_ref, cos_ref, sin_ref, qo_ref, ko_ref):
    cos = cos_ref[...]
    sin_signed = sin_ref[...]
    lane = lax.broadcasted_iota(jnp.int32, cos.shape, 1)
    first_half = (lane % HEAD_DIM) < (HEAD_DIM // 2)
    for c, o in enumerate(_rope_block(q_ref[...], cos, sin_signed, first_half)):
        qo_ref[:, c * LANES:(c + 1) * LANES] = o
    for c, o in enumerate(_rope_block(k_ref[...], cos, sin_signed, first_half)):
        ko_ref[:, c * LANES:(c + 1) * LANES] = o


def rope_qk(proj, cos_t, sin_t):
    t = proj.shape[0]
    tm = _pick(t, (256, 128, 64, 32, 16, 8))
    return pl.pallas_call(
        _rope_kernel,
        grid=(t // tm,),
        in_specs=[
            pl.BlockSpec((tm, Q_W), lambda i: (i, COL_Q // Q_W)),
            pl.BlockSpec((tm, KV_W), lambda i: (i, COL_K // KV_W)),
            pl.BlockSpec((tm, LANES), lambda i: (i, 0)),
            pl.BlockSpec((tm, LANES), lambda i: (i, 0)),
        ],
        out_specs=[
            pl.BlockSpec((tm, Q_W), lambda i: (i, 0)),
            pl.BlockSpec((tm, KV_W), lambda i: (i, 0)),
        ],
        out_shape=[jax.ShapeDtypeStruct((t, Q_W), F32), jax.ShapeDtypeStruct((t, KV_W), F32)],
        compiler_params=_cparams(1),
        name="rope_qk",
    )(proj, proj, cos_t, sin_t)


def rope_tables(tp, seq, ts, nb, past_len):
    half = HEAD_DIM // 2
    inv = ROPE_THETA ** (-jnp.arange(half, dtype=F32) / half)
    pos_p = jnp.arange(tp, dtype=jnp.int32) % seq
    pos_s = past_len + jnp.arange(ts, dtype=jnp.int32) // nb
    pos = jnp.concatenate([pos_p, pos_s]).astype(F32)
    ang = pos[:, None] * inv[None, :]
    cos = jnp.tile(jnp.cos(ang), (1, LANES // half))
    sin = jnp.sin(ang)
    sin_signed = jnp.tile(jnp.concatenate([-sin, sin], axis=1), (1, LANES // HEAD_DIM))
    return cos, sin_signed


def _dup_head(slab, g):
    lane = lax.broadcasted_iota(jnp.int32, slab.shape, 1)
    rolled = pltpu.roll(slab, HEAD_DIM, axis=1)
    if g % 2 == 0:
        return jnp.where(lane < HEAD_DIM, slab, rolled)
    return jnp.where(lane < HEAD_DIM, rolled, slab)


def _swa_bias(qb, prev_ok):
    kw = WINDOW + qb
    ii = lax.broadcasted_iota(jnp.int32, (qb, kw), 0)
    jj = lax.broadcasted_iota(jnp.int32, (qb, kw), 1)
    ok_prev = (jj < WINDOW) & (jj > ii)
    if prev_ok is not True:
        ok_prev = ok_prev & prev_ok
    ok = ok_prev | ((jj >= WINDOW) & ((jj - WINDOW) <= ii))
    return jnp.where(ok, 0.0, NEG_BIG)


def _swa_group(q, kdup, vdup, bias, sink_ref, g, qb, store):
    rep = N_HEADS // N_KV_HEADS
    lane = lax.broadcasted_iota(jnp.int32, (qb, LANES), 1)
    scale = HEAD_DIM ** -0.5
    rows = []
    for r in range(rep):
        h = g * rep + r
        qc = q[:, (h // 2) * LANES:(h // 2 + 1) * LANES]
        keep = (lane < HEAD_DIM) if h % 2 == 0 else (lane >= HEAD_DIM)
        rows.append(jnp.where(keep, qc * scale, 0.0))
    s = _dot_nt(jnp.concatenate(rows, axis=0).astype(BF16), kdup)
    ps, dens = [], []
    for r in range(rep):
        sr = s[r * qb:(r + 1) * qb] + bias
        sink = sink_ref[g * rep + r]
        m = jnp.maximum(jnp.max(sr, axis=-1, keepdims=True), sink)
        p = jnp.exp(sr - m)
        dens.append(jnp.sum(p, axis=-1, keepdims=True) + jnp.exp(sink - m))
        ps.append(p)
    o = _dot(jnp.concatenate(ps, axis=0).astype(BF16), vdup)
    for c in range(rep // 2):
        oa = o[(2 * c) * qb:(2 * c + 1) * qb] / dens[2 * c]
        ob = o[(2 * c + 1) * qb:(2 * c + 2) * qb] / dens[2 * c + 1]
        store((g * rep // 2 + c) * LANES, jnp.where(lane < HEAD_DIM, oa, ob))


def _swa_kernel(sink_ref, q_ref, kp_ref, kc_ref, vp_ref, vc_ref, o_ref, *, qb, prev_from_block0):
    n = pl.program_id(1)
    q = q_ref[...]
    kp, kc, vp, vc = kp_ref[...], kc_ref[...], vp_ref[...], vc_ref[...]
    bias = _swa_bias(qb, True if prev_from_block0 else (n > 0))

    def store(col, val):
        o_ref[:, col:col + LANES] = val.astype(o_ref.dtype)

    for g in range(N_KV_HEADS):
        sl = slice((g // 2) * LANES, (g // 2 + 1) * LANES)
        kdup = _dup_head(jnp.concatenate([kp[:, sl], kc[:, sl]], axis=0), g).astype(BF16)
        vdup = _dup_head(jnp.concatenate([vp[:, sl], vc[:, sl]], axis=0), g).astype(BF16)
        _swa_group(q, kdup, vdup, bias, sink_ref, g, qb, store)


def _swa_cache_kernel(*refs, qb, bseq):
    sink_ref, q_ref, kc_ref, vc_ref, kp_ref, vp_ref, o_ref = refs
    bias = _swa_bias(qb, True)
    for b in range(bseq):
        rows = slice(b * qb, (b + 1) * qb)
        q = q_ref[rows, :]

        def store(col, val, rows=rows):
            o_ref[rows, col:col + LANES] = val.astype(o_ref.dtype)

        for g in range(N_KV_HEADS):
            hs = slice(g * HEAD_DIM, (g + 1) * HEAD_DIM)
            kcat = jnp.concatenate([kp_ref[b, :, g, :], kc_ref[rows, hs]], axis=0)
            vcat = jnp.concatenate([vp_ref[b, :, g, :], vc_ref[rows, hs]], axis=0)
            kdup = jnp.concatenate([kcat, kcat], axis=1).astype(BF16)
            vdup = jnp.concatenate([vcat, vcat], axis=1).astype(BF16)
            _swa_group(q, kdup, vdup, bias, sink_ref, g, qb, store)


def swa_attention_cached(sinks, q, kc, vc, cache_k, cache_v, *, layer, n_seq, qb, bseq, name):
    cache_spec = pl.BlockSpec((None, bseq, WINDOW, N_KV_HEADS, HEAD_DIM), lambda i: (layer, i, 0, 0, 0))
    rows = bseq * qb
    return pl.pallas_call(
        functools.partial(_swa_cache_kernel, qb=qb, bseq=bseq),
        grid=(n_seq // bseq,),
        in_specs=[pl.BlockSpec(memory_space=pltpu.SMEM),
                  pl.BlockSpec((rows, Q_W), lambda i: (i, 0)),
                  pl.BlockSpec((rows, KV_W), lambda i: (i, 0)),
                  pl.BlockSpec((rows, KV_W), lambda i: (i, 0)),
                  cache_spec, cache_spec],
        out_specs=pl.BlockSpec((rows, Q_W), lambda i: (i, 0)),
        out_shape=jax.ShapeDtypeStruct((n_seq * qb, Q_W), F32),
        compiler_params=_cparams(1),
        name=name,
    )(sinks, q, kc, vc, cache_k, cache_v)


def swa_attention(sinks, q, kp, kc, vp, vc, *, n_seq, n_blk, qb, prev_from_block0,
                  kp_map, kc_map, vp_map, vc_map, out_dtype, name):
    return pl.pallas_call(
        functools.partial(_swa_kernel, qb=qb, prev_from_block0=prev_from_block0),
        grid=(n_seq, n_blk),
        in_specs=[
            pl.BlockSpec(memory_space=pltpu.SMEM),
            pl.BlockSpec((qb, Q_W), lambda i, n: (i * n_blk + n, 0)),
            pl.BlockSpec((WINDOW, KV_W), kp_map),
            pl.BlockSpec((qb, KV_W), kc_map),
            pl.BlockSpec((WINDOW, KV_W), vp_map),
            pl.BlockSpec((qb, KV_W), vc_map),
        ],
        out_specs=pl.BlockSpec((qb, Q_W), lambda i, n: (i * n_blk + n, 0)),
        out_shape=jax.ShapeDtypeStruct((n_seq * n_blk * qb, Q_W), out_dtype),
        compiler_params=_cparams(2),
        name=name,
    )(sinks, q, kp, kc, vp, vc)


def _gmlp_p_kernel(gu_ref, gv_ref, ws_ref, bst_ref, lg_ref, lb_ref, gm_ref, vg_ref, *, n_chunks):
    n = pl.program_id(1)
    vg = _ln_rows(_gelu(gv_ref[...]), lg_ref[...], lb_ref[...])
    gu = gu_ref[...]
    ri = lax.broadcasted_iota(jnp.int32, (CHUNK, CHUNK), 0)
    ci = lax.broadcasted_iota(jnp.int32, (CHUNK, CHUNK), 1)
    tril = ri >= ci
    bst = bst_ref[...]
    for g in range(GM_GROUPS):
        sl = slice(g * GM_GROUP_DIM, (g + 1) * GM_GROUP_DIM)
        w = jnp.where(tril, ws_ref[g], 0.0).astype(BF16)
        s = _dot(w, vg[:, sl].astype(BF16)) + bst[:, g:g + 1]
        gm_ref[:, sl] = (_gelu(gu[:, sl]) * s).astype(gm_ref.dtype)

    @pl.when(n == n_chunks - 1)
    def _():
        vg_ref[...] = vg


def gmlp_prompt(proj, n_seq, seq, ws, bs, ln_g, ln_b):
    nc = seq // CHUNK
    return pl.pallas_call(
        functools.partial(_gmlp_p_kernel, n_chunks=nc),
        grid=(n_seq, nc),
        in_specs=[
            pl.BlockSpec((CHUNK, GM_W), lambda i, n: (i * nc + n, COL_GU // GM_W)),
            pl.BlockSpec((CHUNK, GM_W), lambda i, n: (i * nc + n, COL_GV // GM_W)),
            pl.BlockSpec((GM_GROUPS, CHUNK, CHUNK), lambda i, n: (0, 0, 0)),
            pl.BlockSpec((CHUNK, GM_GROUPS), lambda i, n: (0, 0)),
            pl.BlockSpec((1, GM_W), lambda i, n: (0, 0)),
            pl.BlockSpec((1, GM_W), lambda i, n: (0, 0)),
        ],
        out_specs=[
            pl.BlockSpec((CHUNK, GM_W), lambda i, n: (i * nc + n, 0)),
            pl.BlockSpec((CHUNK, GM_W), lambda i, n: (i, 0)),
        ],
        out_shape=[jax.ShapeDtypeStruct((n_seq * seq, GM_W), BF16),
                   jax.ShapeDtypeStruct((n_seq * CHUNK, GM_W), F32)],
        compiler_params=_cparams(2),
        name="gmlp_prompt",
    )(proj, proj, ws, bs.T, ln_g.reshape(1, GM_W), ln_b.reshape(1, GM_W))


def _gmlp_s_kernel(*refs, lt):
    gu_refs = refs[:lt]
    gv_refs = refs[lt:2 * lt]
    wrow_ref, brow_ref, lg_ref, lb_ref, gm_ref, vg_ref = refs[2 * lt:]
    nb = gu_refs[0].shape[0]
    vgs = [_ln_rows(_gelu(gv_refs[t][...]), lg_ref[...], lb_ref[...]) for t in range(lt)]
    for i in range(lt):
        s = brow_ref[i:i + 1, :]
        for j in range(i + 1):
            s = s + wrow_ref[i * lt + j:i * lt + j + 1, :] * vgs[j]
        gm_ref[i * nb:(i + 1) * nb, :] = (_gelu(gu_refs[i][...]) * s).astype(gm_ref.dtype)
        vg_ref[i * nb:(i + 1) * nb, :] = vgs[i]


def gmlp_sample(proj, tp, nb, lt, ws, bs, ln_g, ln_b):
    w_small = ws[:, :lt, :lt]
    wrow = jnp.repeat(jnp.transpose(w_small, (1, 2, 0)).reshape(lt * lt, GM_GROUPS), GM_GROUP_DIM, axis=1)
    brow = jnp.repeat(bs[:, :lt].T, GM_GROUP_DIM, axis=1)
    row0 = tp // nb

    def spec(t, col):
        return pl.BlockSpec((nb, GM_W), lambda i: (row0 + t, col // GM_W))

    in_specs = [spec(t, COL_GU) for t in range(lt)] + [spec(t, COL_GV) for t in range(lt)] + [
        pl.BlockSpec((lt * lt, GM_W), lambda i: (0, 0)),
        pl.BlockSpec((lt, GM_W), lambda i: (0, 0)),
        pl.BlockSpec((1, GM_W), lambda i: (0, 0)),
        pl.BlockSpec((1, GM_W), lambda i: (0, 0)),
    ]
    return pl.pallas_call(
        functools.partial(_gmlp_s_kernel, lt=lt),
        grid=(1,),
        in_specs=in_specs,
        out_specs=[pl.BlockSpec((lt * nb, GM_W), lambda i: (0, 0)),
                   pl.BlockSpec((lt * nb, GM_W), lambda i: (0, 0))],
        out_shape=[jax.ShapeDtypeStruct((lt * nb, GM_W), BF16),
                   jax.ShapeDtypeStruct((lt * nb, GM_W), F32)],
        compiler_params=_cparams(1),
        name="gmlp_sample",
    )(*([proj] * (2 * lt)), wrow, brow, ln_g.reshape(1, GM_W), ln_b.reshape(1, GM_W))


def _conv_silu(cur, prev8, w, bias):
    q = cur.shape[0]
    up = jnp.concatenate([prev8, cur], axis=0)
    acc = bias + up[SUBLANES:SUBLANES + q] * w[CONV_K - 1:CONV_K]
    for j in range(CONV_K - 1):
        off = SUBLANES - (CONV_K - 1) + j
        acc = acc + up[off:off + q] * w[j:j + 1]
    return _silu(acc)


def _ssd_p_kernel(z_ref, xs_ref, bc_ref, xsp_ref, bcp_ref, dt_ref, dtt_ref,
                  cwx_ref, cwbc_ref, cbx_ref, cbbc_ref, dtbe_ref, dtbc_ref, aloge_ref, alogc_ref,
                  dske_ref, ng_ref, e_ref, y_ref, st_ref, s_scr, *, n_chunks):
    c = pl.program_id(1)
    q = CHUNK
    rep = SSM_HEADS // SSM_GROUPS
    gw = rep * SSM_HEAD_DIM

    @pl.when(c == 0)
    def _():
        s_scr[...] = jnp.zeros_like(s_scr)

    has_prev = (c > 0).astype(F32)
    xs = _conv_silu(xs_ref[...], xsp_ref[...] * has_prev, cwx_ref[...], cbx_ref[...])
    bcm = _conv_silu(bc_ref[...], bcp_ref[...] * has_prev, cwbc_ref[...], cbbc_ref[...])

    ri = lax.broadcasted_iota(jnp.int32, (q, q), 0)
    ci = lax.broadcasted_iota(jnp.int32, (q, q), 1)
    tril = ri >= ci
    ones_tril = jnp.where(tril, 1.0, 0.0).astype(BF16)
    ones_triu = jnp.where(ri <= ci, 1.0, 0.0).astype(BF16)

    dt_e = _softplus(_exact_dot_left(_split3(dt_ref[...]), e_ref[...]) + dtbe_ref[...])
    a_e = dt_e * (-jnp.exp(aloge_ref[...]))
    a_hi, a_mid, a_lo = _split3(a_e)
    cum_e = _dot(ones_tril, a_hi) + _dot(ones_tril, a_mid) + _dot(ones_tril, a_lo)
    dt_t = _softplus(dtt_ref[...] + dtbc_ref[...])
    a_t = dt_t * (-jnp.exp(alogc_ref[...]))
    cum_t = _exact_dot_left(_split3(a_t), ones_triu)

    xdt = xs * dt_e
    cum_last = cum_e[q - 1:q, :]
    lane = lax.broadcasted_iota(jnp.int32, (q, LANES), 1)
    z = z_ref[...]
    ys = []
    for g in range(SSM_GROUPS):
        gs = slice(g * gw, (g + 1) * gw)
        bg = bcm[:, g * SSM_STATE:(g + 1) * SSM_STATE]
        cg = bcm[:, SSM_BC + g * SSM_STATE:SSM_BC + (g + 1) * SSM_STATE]
        bg16, cg16 = bg.astype(BF16), cg.astype(BF16)
        cb = _dot_nt(cg16, bg16)
        ydiag = []
        for pr in range(rep // 2):
            ms = []
            for hh in (2 * pr, 2 * pr + 1):
                h = g * rep + hh
                col = cum_e[:, h * SSM_HEAD_DIM:h * SSM_HEAD_DIM + 1]
                row = cum_t[h:h + 1, :]
                seg = jnp.where(tril, col - row, NEG_BIG)
                ms.append(cb * jnp.exp(seg))
            lhs = jnp.concatenate(ms, axis=1).astype(BF16)
            xslab = xdt[:, (g * rep + 2 * pr) * SSM_HEAD_DIM:(g * rep + 2 * pr + 2) * SSM_HEAD_DIM]
            xbd = jnp.concatenate([jnp.where(lane < SSM_HEAD_DIM, xslab, 0.0),
                                   jnp.where(lane >= SSM_HEAD_DIM, xslab, 0.0)], axis=0).astype(BF16)
            ydiag.append(_dot(lhs, xbd))
        ydiag = jnp.concatenate(ydiag, axis=1)
        s_old = s_scr[g]
        yoff = _dot(cg16, s_old.astype(BF16)) * jnp.exp(cum_e[:, gs])
        ys.append(ydiag + yoff)
        xw = xdt[:, gs] * jnp.exp(cum_last[:, gs] - cum_e[:, gs])
        s_scr[g] = s_old * jnp.exp(cum_last[:, gs]) + _dot(bg.T.astype(BF16), xw.astype(BF16))
    y = jnp.concatenate(ys, axis=1) + dske_ref[...] * xs
    gated = y * _silu(z)
    out = gated * lax.rsqrt(jnp.mean(gated * gated, axis=-1, keepdims=True) + LN_EPS) * ng_ref[...]
    y_ref[...] = out.astype(y_ref.dtype)

    @pl.when(c == n_chunks - 1)
    def _():
        for g in range(SSM_GROUPS):
            st_ref[0, g * gw:(g + 1) * gw, :] = s_scr[g].T


def _expand_heads(v):
    return jnp.repeat(v.astype(F32), SSM_HEAD_DIM).reshape(1, SSM_INNER)


def _head_expand_matrix():
    e = np.zeros((LANES, SSM_INNER), np.float32)
    for h in range(SSM_HEADS):
        e[h, h * SSM_HEAD_DIM:(h + 1) * SSM_HEAD_DIM] = 1.0
    return jnp.asarray(e, BF16)


def ssd_prompt(proj, dtt, n_seq, seq, conv_w, conv_b, dt_bias, a_log, d_skip, norm_g):
    b = n_seq
    nc = seq // CHUNK
    blk8 = CHUNK // SUBLANES
    const2 = lambda i, c: (0, 0)
    col_bc = COL_BC // (2 * SSM_BC)
    col_xs = COL_XS // SSM_INNER
    col_z = COL_Z // SSM_INNER
    prev_map_x = lambda i, c: (jnp.maximum((i * nc + c) * blk8 - 1, 0), col_xs)
    prev_map_bc = lambda i, c: (jnp.maximum((i * nc + c) * blk8 - 1, 0), col_bc)
    args = (
        proj, proj, proj, proj, proj, proj, dtt,
        conv_w[:, :SSM_INNER], conv_w[:, SSM_INNER:], conv_b[:SSM_INNER].reshape(1, -1),
        conv_b[SSM_INNER:].reshape(1, -1),
        _expand_heads(dt_bias), jnp.broadcast_to(dt_bias.astype(F32)[:, None], (SSM_HEADS, CHUNK)),
        _expand_heads(a_log), jnp.broadcast_to(a_log.astype(F32)[:, None], (SSM_HEADS, CHUNK)),
        _expand_heads(d_skip), norm_g.reshape(1, SSM_INNER), _head_expand_matrix(),
    )
    in_specs = [
        pl.BlockSpec((CHUNK, SSM_INNER), lambda i, c: (i * nc + c, col_z)),
        pl.BlockSpec((CHUNK, SSM_INNER), lambda i, c: (i * nc + c, col_xs)),
        pl.BlockSpec((CHUNK, 2 * SSM_BC), lambda i, c: (i * nc + c, col_bc)),
        pl.BlockSpec((SUBLANES, SSM_INNER), prev_map_x),
        pl.BlockSpec((SUBLANES, 2 * SSM_BC), prev_map_bc),
        pl.BlockSpec((CHUNK, LANES), lambda i, c: (i * nc + c, COL_DT // LANES)),
        pl.BlockSpec((SSM_HEADS, CHUNK), lambda i, c: (0, i * nc + c)),
        pl.BlockSpec((CONV_K, SSM_INNER), const2),
        pl.BlockSpec((CONV_K, 2 * SSM_BC), const2),
        pl.BlockSpec((1, SSM_INNER), const2),
        pl.BlockSpec((1, 2 * SSM_BC), const2),
        pl.BlockSpec((1, SSM_INNER), const2),
        pl.BlockSpec((SSM_HEADS, CHUNK), const2),
        pl.BlockSpec((1, SSM_INNER), const2),
        pl.BlockSpec((SSM_HEADS, CHUNK), const2),
        pl.BlockSpec((1, SSM_INNER), const2),
        pl.BlockSpec((1, SSM_INNER), const2),
        pl.BlockSpec((LANES, SSM_INNER), const2),
    ]
    return pl.pallas_call(
        functools.partial(_ssd_p_kernel, n_chunks=nc),
        grid=(b, nc),
        in_specs=in_specs,
        out_specs=[
            pl.BlockSpec((CHUNK, SSM_INNER), lambda i, c: (i * nc + c, 0)),
            pl.BlockSpec((1, SSM_INNER, SSM_STATE), lambda i, c: (i, 0, 0)),
        ],
        out_shape=[jax.ShapeDtypeStruct((b * seq, SSM_INNER), BF16),
                   jax.ShapeDtypeStruct((b, SSM_INNER, SSM_STATE), F32)],
        scratch_shapes=[pltpu.VMEM((SSM_GROUPS, SSM_STATE, SSM_INNER // SSM_GROUPS), F32)],
        compiler_params=_cparams(2),
        name="ssd_prompt",
    )(*args)


def _group_expand_matrix():
    gw = SSM_INNER // SSM_GROUPS
    m = np.zeros((SSM_BC, SSM_INNER), np.float32)
    for g in range(SSM_GROUPS):
        m[g * SSM_STATE:(g + 1) * SSM_STATE, g * gw:(g + 1) * gw] = 1.0
    return jnp.asarray(m, BF16)


def _ssd_s_pre_kernel(*refs, lt):
    xs_refs = refs[:lt]
    bc_refs = refs[lt:2 * lt]
    dt_refs = refs[2 * lt:3 * lt]
    (cx_ref, cbc_ref, cwx_ref, cwbc_ref, cbx_ref, cbbc_ref, dtbe_ref, aloge_ref, dske_ref, e_ref,
     gmat_ref, c_ref, b_ref, xw_ref, dec_ref, yd_ref, ec_ref) = refs[3 * lt:]
    nprev = CONV_K - 1
    ux = [cx_ref[j] for j in range(nprev)] + [r[...] for r in xs_refs]
    ub = [cbc_ref[j] for j in range(nprev)] + [r[...] for r in bc_refs]
    cwx, cwbc = cwx_ref[...], cwbc_ref[...]
    neg_a = -jnp.exp(aloge_ref[...])
    xs, bm, cm, xdt, cum = [], [], [], [], []
    run = None
    for t in range(lt):
        ax = cbx_ref[...]
        ab = cbbc_ref[...]
        for j in range(CONV_K):
            ax = ax + ux[t + j] * cwx[j:j + 1]
            ab = ab + ub[t + j] * cwbc[j:j + 1]
        x_t = _silu(ax)
        bc_t = _silu(ab)
        dt_e = _softplus(_exact_dot_left(_split3(dt_refs[t][...]), e_ref[...]) + dtbe_ref[...])
        a_t = dt_e * neg_a
        run = a_t if run is None else run + a_t
        xs.append(x_t)
        bm.append(bc_t[:, :SSM_BC])
        cm.append(bc_t[:, SSM_BC:])
        xdt.append(x_t * dt_e)
        cum.append(run)
    for i in range(lt):
        yd = dske_ref[...] * xs[i]
        for j in range(i + 1):
            hi, mid, _ = _split3(cm[i] * bm[j])
            cbe = _dot(hi, gmat_ref[...]) + _dot(mid, gmat_ref[...])
            yd = yd + cbe * jnp.exp(cum[i] - cum[j]) * xdt[j]
        yd_ref[i] = yd
        ec_ref[i] = jnp.exp(cum[i])
        c_ref[i] = cm[i]
        b_ref[i] = bm[i]
        xw_ref[i] = xdt[i] * jnp.exp(cum[lt - 1] - cum[i])
    dec_ref[...] = jnp.exp(cum[lt - 1])


def _rows_block(rows, total):
    c = rows[0].shape[1]
    rid = lax.broadcasted_iota(jnp.int32, (SUBLANES, c), 0)
    acc = jnp.zeros((SUBLANES, c), F32)
    for j, r in enumerate(rows):
        acc = jnp.where(rid == j, jnp.broadcast_to(r, (SUBLANES, c)), acc)
    if total == SUBLANES:
        return acc
    return jnp.concatenate([acc, jnp.zeros((total - SUBLANES, c), F32)], axis=0)


def _ssd_s_state_kernel_inplace(c_ref, b_ref, xw_ref, dec_ref, h0_ref, prev_ref, hn_ref, yr_ref, *, lt):
    del prev_ref
    _ssd_s_state_kernel(c_ref, b_ref, xw_ref, dec_ref, h0_ref, hn_ref, yr_ref, lt=lt)


def _ssd_s_state_kernel(c_ref, b_ref, xw_ref, dec_ref, h0_ref, hn_ref, yr_ref, *, lt, slot=0,
                        fill_slots=None):
    b = pl.program_id(0)
    gw = SSM_INNER // SSM_GROUPS
    c8 = _rows_block([c_ref[i, pl.ds(b, 1), :] for i in range(lt)], SUBLANES).astype(BF16)
    b128 = _rows_block([b_ref[i, pl.ds(b, 1), :] for i in range(lt)], LANES).astype(BF16)
    xaug = _rows_block([xw_ref[i, pl.ds(b, 1), :] for i in range(lt)] + [dec_ref[pl.ds(b, 1), :]], LANES)
    for g in range(SSM_GROUPS):
        hg = h0_ref[0, g * gw:(g + 1) * gw, :]
        yraw = _dot_nt(c8[:, g * SSM_STATE:(g + 1) * SSM_STATE], hg.astype(BF16))
        for i in range(lt):
            yr_ref[i, pl.ds(b, 1), g * gw:(g + 1) * gw] = yraw[i:i + 1, :]
        tr = xaug[:, g * gw:(g + 1) * gw].T
        s = _dot(tr.astype(BF16), b128[:, g * SSM_STATE:(g + 1) * SSM_STATE])
        new = hg * tr[:, lt:lt + 1] + s
        if fill_slots is None:
            hn_ref[0, g * gw:(g + 1) * gw, :] = new
        else:
            hn_ref[slot, 0, g * gw:(g + 1) * gw, :] = new
    if fill_slots is not None:
        for other in fill_slots:
            hn_ref[other] = jnp.zeros(hn_ref.shape[1:], hn_ref.dtype)


def _ssd_s_post_kernel(*refs, lt):
    z_refs = refs[:lt]
    yd_ref, ec_ref, yr_ref, ng_ref, o_ref = refs[lt:]
    nb = z_refs[0].shape[0]
    for i in range(lt):
        y = yd_ref[i] + ec_ref[i] * yr_ref[i]
        gated = y * _silu(z_refs[i][...])
        out = gated * lax.rsqrt(jnp.mean(gated * gated, axis=-1, keepdims=True) + LN_EPS) * ng_ref[...]
        o_ref[i * nb:(i + 1) * nb, :] = out.astype(o_ref.dtype)


def ssd_sample(proj, tp, nb, lt, conv_state, state_all, new_states, layer, conv_w, conv_b, dt_bias,
               a_log, d_skip, norm_g):
    row0 = tp // nb
    cs = jnp.transpose(conv_state, (1, 0, 2))
    one = lambda i: (0, 0)
    one3 = lambda i: (0, 0, 0)

    def rows(t, width, col):
        return pl.BlockSpec((nb, width), lambda i: (row0 + t, col // width))

    in_specs = ([rows(t, SSM_INNER, COL_XS) for t in range(lt)]
                + [rows(t, 2 * SSM_BC, COL_BC) for t in range(lt)]
                + [pl.BlockSpec((nb, LANES), lambda i, t=t: (row0 + t, COL_DT // LANES)) for t in range(lt)]
                + [pl.BlockSpec((CONV_K - 1, nb, SSM_INNER), one3),
                   pl.BlockSpec((CONV_K - 1, nb, 2 * SSM_BC), one3),
                   pl.BlockSpec((CONV_K, SSM_INNER), one),
                   pl.BlockSpec((CONV_K, 2 * SSM_BC), one),
                   pl.BlockSpec((1, SSM_INNER), one),
                   pl.BlockSpec((1, 2 * SSM_BC), one),
                   pl.BlockSpec((1, SSM_INNER), one),
                   pl.BlockSpec((1, SSM_INNER), one),
                   pl.BlockSpec((1, SSM_INNER), one),
                   pl.BlockSpec((LANES, SSM_INNER), one),
                   pl.BlockSpec((SSM_BC, SSM_INNER), one)])
    f3 = lambda w: jax.ShapeDtypeStruct((lt, nb, w), F32)
    c_a, b_a, xw_a, dec_a, yd_a, ec_a = pl.pallas_call(
        functools.partial(_ssd_s_pre_kernel, lt=lt),
        grid=(1,),
        in_specs=in_specs,
        out_specs=[pl.BlockSpec((lt, nb, SSM_BC), one3), pl.BlockSpec((lt, nb, SSM_BC), one3),
                   pl.BlockSpec((lt, nb, SSM_INNER), one3), pl.BlockSpec((nb, SSM_INNER), one),
                   pl.BlockSpec((lt, nb, SSM_INNER), one3), pl.BlockSpec((lt, nb, SSM_INNER), one3)],
        out_shape=[f3(SSM_BC), f3(SSM_BC), f3(SSM_INNER), jax.ShapeDtypeStruct((nb, SSM_INNER), F32),
                   f3(SSM_INNER), f3(SSM_INNER)],
        compiler_params=_cparams(1),
        name="ssd_sample_pre",
    )(*([proj] * (3 * lt)), cs[:, :, :SSM_INNER], cs[:, :, SSM_INNER:],
      conv_w[:, :SSM_INNER], conv_w[:, SSM_INNER:], conv_b[:SSM_INNER].reshape(1, -1),
      conv_b[SSM_INNER:].reshape(1, -1), _expand_heads(dt_bias), _expand_heads(a_log),
      _expand_heads(d_skip), _head_expand_matrix(), _group_expand_matrix())

    depth = state_all.shape[0]
    h0r = state_all.reshape(depth, nb, SSM_INNER, SSM_STATE)
    state_specs = [pl.BlockSpec((lt, nb, SSM_BC), one3), pl.BlockSpec((lt, nb, SSM_BC), one3),
                   pl.BlockSpec((lt, nb, SSM_INNER), one3), pl.BlockSpec((nb, SSM_INNER), one),
                   pl.BlockSpec((None, 1, SSM_INNER, SSM_STATE), lambda i: (layer, i, 0, 0))]
    hn_shape = jax.ShapeDtypeStruct((depth, nb, SSM_INNER, SSM_STATE), F32)
    if new_states is None:
        fill = tuple(s for s in range(depth) if s != layer)
        hn, yr = pl.pallas_call(
            functools.partial(_ssd_s_state_kernel, lt=lt, slot=layer, fill_slots=fill),
            grid=(nb,),
            in_specs=state_specs,
            out_specs=[pl.BlockSpec((depth, 1, SSM_INNER, SSM_STATE), lambda i: (0, i, 0, 0)),
                       pl.BlockSpec((lt, nb, SSM_INNER), one3)],
            out_shape=[hn_shape, f3(SSM_INNER)],
            compiler_params=_cparams(1),
            name="ssd_sample_state",
        )(c_a, b_a, xw_a, dec_a, h0r)
    else:
        hn, yr = pl.pallas_call(
            functools.partial(_ssd_s_state_kernel_inplace, lt=lt),
            grid=(nb,),
            in_specs=state_specs + [pl.BlockSpec(memory_space=pl.ANY)],
            out_specs=[pl.BlockSpec((None, 1, SSM_INNER, SSM_STATE), lambda i: (layer, i, 0, 0)),
                       pl.BlockSpec((lt, nb, SSM_INNER), one3)],
            out_shape=[hn_shape, f3(SSM_INNER)],
            input_output_aliases={5: 0},
            compiler_params=_cparams(1),
            name="ssd_sample_state",
        )(c_a, b_a, xw_a, dec_a, h0r, new_states)

    ssm = pl.pallas_call(
        functools.partial(_ssd_s_post_kernel, lt=lt),
        grid=(1,),
        in_specs=([rows(t, SSM_INNER, COL_Z) for t in range(lt)]
                  + [pl.BlockSpec((lt, nb, SSM_INNER), one3)] * 3 + [pl.BlockSpec((1, SSM_INNER), one)]),
        out_specs=pl.BlockSpec((lt * nb, SSM_INNER), one),
        out_shape=jax.ShapeDtypeStruct((lt * nb, SSM_INNER), BF16),
        compiler_params=_cparams(1),
        name="ssd_sample_post",
    )(*([proj] * lt), yd_a, ec_a, yr, norm_g.reshape(1, SSM_INNER))
    return ssm, hn


def _xattn_kernel(q_ref, k_ref, v_ref, o_ref, *, nh, bseq, tq):
    scale = MEM_HEAD_DIM ** -0.5
    for b in range(bseq):
        rows = slice(b * tq, (b + 1) * tq)
        for h in range(nh):
            sl = slice(h * MEM_HEAD_DIM, (h + 1) * MEM_HEAD_DIM)
            s = _dot_nt(q_ref[rows, sl].astype(BF16), k_ref[b, :, h, :].astype(BF16)) * scale
            m = jnp.max(s, axis=-1, keepdims=True)
            p = jnp.exp(s - m)
            den = jnp.sum(p, axis=-1, keepdims=True)
            o = _dot(p.astype(BF16), v_ref[b, :, h, :].astype(BF16)) / den
            o_ref[rows, sl] = o.astype(o_ref.dtype)


def cross_attention(q, k, v, *, layer, n_seq, seq, tq, bseq, name):
    w = q.shape[1]
    _, _, m, nh, dh = k.shape
    nq = seq // tq
    assert bseq == 1 or nq == 1
    kv_spec = pl.BlockSpec((None, bseq, m, nh, dh), lambda i, n: (layer, i, 0, 0, 0))
    return pl.pallas_call(
        functools.partial(_xattn_kernel, nh=nh, bseq=bseq, tq=tq),
        grid=(n_seq // bseq, nq),
        in_specs=[pl.BlockSpec((bseq * tq, w), lambda i, n: (i * nq + n, 0)), kv_spec, kv_spec],
        out_specs=pl.BlockSpec((bseq * tq, w), lambda i, n: (i * nq + n, 0)),
        out_shape=jax.ShapeDtypeStruct((n_seq * seq, w), F32),
        compiler_params=_cparams(2),
        name=name,
    )(q, k, v)


def _router_kernel(h_ref, w_ref, b_ref, o_ref):
    x = h_ref[...]
    w = w_ref[...]
    x_hi = x.astype(BF16)
    x_lo = (x - x_hi.astype(F32)).astype(BF16)
    w_hi = w.astype(BF16)
    w_lo = (w - w_hi.astype(F32)).astype(BF16)
    logits = _dot(x_hi, w_hi) + _dot(x_hi, w_lo) + _dot(x_lo, w_hi) + b_ref[...]
    lane = lax.broadcasted_iota(jnp.int32, logits.shape, 1)
    lane_f = lane.astype(F32)
    big = float(LANES)
    is_g = lane < N_EGROUPS
    lg = jnp.where(is_g, logits, NEG_BIG)
    mg = jnp.max(lg, axis=-1, keepdims=True)
    zg = jnp.sum(jnp.where(is_g, jnp.exp(lg - mg), 0.0), axis=-1, keepdims=True)
    gi = jnp.min(jnp.where(is_g & (lg == mg), lane_f, big), axis=-1, keepdims=True)
    gw = 1.0 / zg
    lo = N_EGROUPS + gi * EXPERTS_PER_GROUP
    is_e = (lane_f >= lo) & (lane_f < lo + EXPERTS_PER_GROUP)
    le = jnp.where(is_e, logits, NEG_BIG)
    me = jnp.max(le, axis=-1, keepdims=True)
    ee = jnp.where(is_e, jnp.exp(le - me), 0.0)
    pe = ee / jnp.sum(ee, axis=-1, keepdims=True)
    pe = jnp.where(is_e, pe, -1.0)
    p1 = jnp.max(pe, axis=-1, keepdims=True)
    i1 = jnp.min(jnp.where(pe == p1, lane_f, big), axis=-1, keepdims=True)
    pe2 = jnp.where(lane_f == i1, -1.0, pe)
    p2 = jnp.max(pe2, axis=-1, keepdims=True)
    i2 = jnp.min(jnp.where(pe2 == p2, lane_f, big), axis=-1, keepdims=True)
    tot = p1 + p2
    out = jnp.where(lane == 0, i1 - N_EGROUPS,
                    jnp.where(lane == 1, i2 - N_EGROUPS,
                              jnp.where(lane == 2, gw * (p1 / tot),
                                        jnp.where(lane == 3, gw * (p2 / tot), 0.0))))
    o_ref[...] = out


def moe_router(h, w_rg, b_rg, w_re, b_re):
    t, d = h.shape
    tm = _pick(t, (256, 128, 64, 32, 16, 8))
    npad = LANES - N_EGROUPS - N_EXPERTS
    w = jnp.concatenate([w_rg, w_re, jnp.zeros((d, npad), F32)], axis=1)
    b = jnp.concatenate([b_rg, b_re, jnp.zeros((npad,), F32)]).reshape(1, LANES)
    return pl.pallas_call(
        _router_kernel,
        grid=(t // tm,),
        in_specs=[pl.BlockSpec((tm, d), lambda i: (i, 0)),
                  pl.BlockSpec((d, LANES), lambda i: (0, 0)),
                  pl.BlockSpec((1, LANES), lambda i: (0, 0))],
        out_specs=pl.BlockSpec((tm, LANES), lambda i: (i, 0)),
        out_shape=jax.ShapeDtypeStruct((t, LANES), F32),
        compiler_params=_cparams(1),
        name="moe_router",
    )(h, w, b)


def _row_copy(src_hbm, dst, src_row, dst_row, sem):
    return pltpu.make_async_copy(src_hbm.at[pl.ds(src_row, 1)], dst.at[pl.ds(dst_row, 1)], sem)


def _moe_gather_kernel(tok_ref, h_ref, o_ref, buf, sem, *, tm):
    def start(r, carry):
        _row_copy(h_ref, buf, tok_ref[0, 0, r], r, sem).start()
        return carry

    def wait(r, carry):
        _row_copy(h_ref, buf, tok_ref[0, 0, r], r, sem).wait()
        return carry

    lax.fori_loop(0, tm, start, 0, unroll=DMA_LOOP_UNROLL)
    lax.fori_loop(0, tm, wait, 0, unroll=DMA_LOOP_UNROLL)
    o_ref[...] = buf[...].astype(o_ref.dtype)


def moe_gather(h, row_token, tm):
    r_total = row_token.shape[0]
    d = h.shape[1]
    nblk = r_total // tm
    return pl.pallas_call(
        functools.partial(_moe_gather_kernel, tm=tm),
        grid=(nblk,),
        in_specs=[pl.BlockSpec((1, 1, tm), lambda i: (i, 0, 0), memory_space=pltpu.SMEM),
                  pl.BlockSpec(memory_space=pl.ANY)],
        out_specs=pl.BlockSpec((tm, d), lambda i: (i, 0)),
        out_shape=jax.ShapeDtypeStruct((r_total, d), BF16),
        scratch_shapes=[pltpu.VMEM((tm, d), h.dtype), pltpu.SemaphoreType.DMA(())],
        compiler_params=_cparams(1),
        name="moe_gather",
    )(row_token.reshape(nblk, 1, tm), h)


def _expert_weight_copies(w_refs, bufs, sems, layer, expert, slot):
    return [pltpu.make_async_copy(w.at[layer, expert], buf.at[slot], sems.at[k, slot])
            for k, (w, buf) in enumerate(zip(w_refs, bufs))]


def _expert_weights_step(s, tv_ref, te_ref, tn_ref, ts_ref, w_refs, bufs, w16s, sems, layer):
    @pl.when(s == 0)
    def _():
        for c in _expert_weight_copies(w_refs, bufs, sems, layer, te_ref[0], 0):
            c.start()

    @pl.when(tv_ref[s] == 2)
    def _():
        slot = ts_ref[s]
        for c in _expert_weight_copies(w_refs, bufs, sems, layer, te_ref[s], slot):
            c.wait()

        @pl.when(tn_ref[s] >= 0)
        def _():
            for c in _expert_weight_copies(w_refs, bufs, sems, layer, tn_ref[s], 1 - slot):
                c.start()

        for buf, w16 in zip(bufs, w16s):
            w16[...] = buf[slot].astype(BF16)


def _moe_up_kernel(tv_ref, tc_ref, te_ref, tn_ref, ts_ref, x_ref, wg_ref, wu_ref, o_ref,
                   gbuf, ubuf, wg16, wu16, sems, *, layer):
    s = pl.program_id(0)
    _expert_weights_step(s, tv_ref, te_ref, tn_ref, ts_ref, (wg_ref, wu_ref), (gbuf, ubuf), (wg16, wu16),
                         sems, layer)

    @pl.when(tv_ref[s] > 0)
    def _():
        x = x_ref[...]
        a = _dot(x, wg16[...])
        u = _dot(x, wu16[...])
        o_ref[...] = (_silu(a) * u).astype(o_ref.dtype)

    @pl.when(tv_ref[s] == 0)
    def _():
        o_ref[...] = jnp.zeros_like(o_ref)


def _moe_tables(plan):
    return plan["tile_v"], plan["tile_c"], plan["tile_e"], plan["tile_next"], plan["tile_slot"]


def moe_up(x_sorted, w_gate, w_up, layer, plan, tm):
    r_total, d = x_sorted.shape
    ff = w_gate.shape[-1]
    n_tiles = r_total // tm
    grid_spec = pltpu.PrefetchScalarGridSpec(
        num_scalar_prefetch=5,
        grid=(n_tiles,),
        in_specs=[
            pl.BlockSpec((tm, d), lambda s, tv, tc, te, tn, ts: (tc[s], 0)),
            pl.BlockSpec(memory_space=pl.ANY),
            pl.BlockSpec(memory_space=pl.ANY),
        ],
        out_specs=pl.BlockSpec((tm, ff), lambda s, tv, tc, te, tn, ts: (s, 0)),
        scratch_shapes=[pltpu.VMEM((2, d, ff), F32), pltpu.VMEM((2, d, ff), F32),
                        pltpu.VMEM((d, ff), BF16), pltpu.VMEM((d, ff), BF16),
                        pltpu.SemaphoreType.DMA((2, 2))],
    )
    return pl.pallas_call(
        functools.partial(_moe_up_kernel, layer=layer),
        grid_spec=grid_spec,
        out_shape=jax.ShapeDtypeStruct((r_total, ff), BF16),
        compiler_params=_cparams(1),
        name="moe_up",
    )(*_moe_tables(plan), x_sorted, w_gate, w_up)


def _moe_down_kernel(tv_ref, tc_ref, te_ref, tn_ref, ts_ref, x_ref, w_ref, o_ref, wbuf, w16, sems, *, layer):
    s = pl.program_id(0)
    _expert_weights_step(s, tv_ref, te_ref, tn_ref, ts_ref, (w_ref,), (wbuf,), (w16,), sems, layer)

    @pl.when(tv_ref[s] > 0)
    def _():
        o_ref[...] = _dot(x_ref[...], w16[...])

    @pl.when(tv_ref[s] == 0)
    def _():
        o_ref[...] = jnp.zeros_like(o_ref)


def moe_down(hid, w_down, layer, plan, tm):
    r_total, ff = hid.shape
    d = w_down.shape[-1]
    n_tiles = r_total // tm
    grid_spec = pltpu.PrefetchScalarGridSpec(
        num_scalar_prefetch=5,
        grid=(n_tiles,),
        in_specs=[
            pl.BlockSpec((tm, ff), lambda s, tv, tc, te, tn, ts: (tc[s], 0)),
            pl.BlockSpec(memory_space=pl.ANY),
        ],
        out_specs=pl.BlockSpec((tm, d), lambda s, tv, tc, te, tn, ts: (s, 0)),
        scratch_shapes=[pltpu.VMEM((2, ff, d), F32), pltpu.VMEM((ff, d), BF16),
                        pltpu.SemaphoreType.DMA((1, 2))],
    )
    return pl.pallas_call(
        functools.partial(_moe_down_kernel, layer=layer),
        grid_spec=grid_spec,
        out_shape=jax.ShapeDtypeStruct((r_total, d), F32),
        compiler_params=_cparams(1),
        name="moe_down",
    )(*_moe_tables(plan), hid, w_down)


def _moe_combine_kernel(pos_ref, y_ref, r_ref, h_ref, g_ref, b_ref, o1_ref, o2_ref, ybuf, sem, *,
                        tm, alpha, n_first):
    def start(i, carry):
        _row_copy(y_ref, ybuf.at[0], pos_ref[0, 0, 2 * i], i, sem).start()
        _row_copy(y_ref, ybuf.at[1], pos_ref[0, 0, 2 * i + 1], i, sem).start()
        return carry

    def wait(i, carry):
        _row_copy(y_ref, ybuf.at[0], pos_ref[0, 0, 2 * i], i, sem).wait()
        _row_copy(y_ref, ybuf.at[1], pos_ref[0, 0, 2 * i + 1], i, sem).wait()
        return carry

    lax.fori_loop(0, tm, start, 0, unroll=DMA_LOOP_UNROLL)
    lax.fori_loop(0, tm, wait, 0, unroll=DMA_LOOP_UNROLL)
    route = r_ref[...]
    ff = ybuf[0] * route[:, 2:3] + ybuf[1] * route[:, 3:4]
    h = _ln_rows(alpha * h_ref[...] + ff, g_ref[...], b_ref[...])
    if n_first is None:
        o1_ref[...] = h
        o2_ref[...] = h.astype(BF16)
    else:
        i = pl.program_id(0)

        @pl.when(i < n_first)
        def _():
            o1_ref[...] = h

        @pl.when(i >= n_first)
        def _():
            o2_ref[...] = h


def moe_combine(y_sorted, pos, route, h, g, b, *, alpha, split_rows=None):
    t, d = h.shape
    tm = _pick(t if split_rows is None else math.gcd(split_rows, t - split_rows), (256, 128, 64, 32, 16, 8))
    nblk = t // tm
    if split_rows is None:
        n_first = None
        out_specs = [pl.BlockSpec((tm, d), lambda i: (i, 0)), pl.BlockSpec((tm, d), lambda i: (i, 0))]
        out_shape = [jax.ShapeDtypeStruct((t, d), F32), jax.ShapeDtypeStruct((t, d), BF16)]
    else:
        n_first = split_rows // tm
        out_specs = [pl.BlockSpec((tm, d), lambda i: (jnp.minimum(i, n_first - 1), 0)),
                     pl.BlockSpec((tm, d), lambda i: (jnp.maximum(i - n_first, 0), 0))]
        out_shape = [jax.ShapeDtypeStruct((split_rows, d), F32),
                     jax.ShapeDtypeStruct((t - split_rows, d), F32)]
    return pl.pallas_call(
        functools.partial(_moe_combine_kernel, tm=tm, alpha=alpha, n_first=n_first),
        grid=(nblk,),
        in_specs=[pl.BlockSpec((1, 1, 2 * tm), lambda i: (i, 0, 0), memory_space=pltpu.SMEM),
                  pl.BlockSpec(memory_space=pl.ANY),
                  pl.BlockSpec((tm, LANES), lambda i: (i, 0)),
                  pl.BlockSpec((tm, d), lambda i: (i, 0)),
                  pl.BlockSpec((1, d), lambda i: (0, 0)),
                  pl.BlockSpec((1, d), lambda i: (0, 0))],
        out_specs=out_specs,
        out_shape=out_shape,
        scratch_shapes=[pltpu.VMEM((2, tm, d), F32), pltpu.SemaphoreType.DMA(())],
        compiler_params=_cparams(1),
        name="moe_combine",
    )(pos.reshape(nblk, 1, 2 * tm), y_sorted, route, h, g.reshape(1, d), b.reshape(1, d))


def moe_plan(route, tm):
    t = route.shape[0]
    eid = route[:, :2].astype(jnp.int32).reshape(-1)
    onehot = (eid[:, None] == jnp.arange(N_EXPERTS, dtype=jnp.int32)[None, :]).astype(jnp.int32)
    csum = jnp.cumsum(onehot, axis=0)
    rank = jnp.sum((csum - onehot) * onehot, axis=1)
    counts = csum[-1]
    tiles_e = (counts + tm - 1) // tm
    tile_end = jnp.cumsum(tiles_e)
    tile_start = tile_end - tiles_e
    n_used = tile_end[-1]
    n_tiles = (2 * t + N_EXPERTS * (tm - 1)) // tm + 1
    r_total = n_tiles * tm
    dest = tile_start[eid] * tm + rank
    row_token = (jnp.arange(r_total, dtype=jnp.int32) % t).at[dest].set(
        jnp.arange(2 * t, dtype=jnp.int32) // 2)
    tile_ids = jnp.arange(n_tiles, dtype=jnp.int32)
    tile_clamped = jnp.minimum(tile_ids, n_used - 1)
    tile_e = jnp.sum(tile_end[None, :] <= tile_clamped[:, None], axis=1).astype(jnp.int32)
    tile_first = tile_ids == tile_start[tile_e]
    tile_v = jnp.where(tile_ids < n_used, 1 + tile_first.astype(jnp.int32), 0).astype(jnp.int32)
    e_ids = jnp.arange(N_EXPERTS, dtype=jnp.int32)
    nonempty = tiles_e > 0
    cand = jnp.where(nonempty[None, :] & (e_ids[None, :] > e_ids[:, None]), e_ids[None, :], N_EXPERTS)
    next_e = jnp.min(cand, axis=1)
    next_e = jnp.where(next_e >= N_EXPERTS, -1, next_e).astype(jnp.int32)
    slot_e = ((jnp.cumsum(nonempty.astype(jnp.int32)) - 1) % 2).astype(jnp.int32)
    return dict(row_token=row_token, pos=dest.astype(jnp.int32), tile_v=tile_v, tile_c=tile_clamped,
                tile_e=tile_e, tile_next=next_e[tile_e], tile_slot=slot_e[tile_e])


def hierarchical_moe_ln(hf, layer, w_rg, b_rg, w_re, b_re, w_gate, w_up, w_down, ln_g, ln_b, *, alpha,
                        split_rows=None):
    route = moe_router(hf, w_rg, b_rg, w_re, b_re)
    plan = moe_plan(route, MOE_TM)
    x_sorted = moe_gather(hf, plan["row_token"], MOE_TM)
    hid = moe_up(x_sorted, w_gate, w_up, layer, plan, MOE_TM)
    y_sorted = moe_down(hid, w_down, layer, plan, MOE_TM)
    return moe_combine(y_sorted, plan["pos"], route, hf, ln_g, ln_b, alpha=alpha, split_rows=split_rows)


def _to_seq_major(x_tm, lt, nb, pad_to):
    w = x_tm.shape[1]
    x = jnp.transpose(x_tm.reshape(lt, nb, w), (1, 0, 2))
    x = jnp.pad(x, ((0, 0), (0, pad_to - lt), (0, 0)))
    return x.reshape(nb * pad_to, w)


def _to_time_major(x_sm, lt, nb, pad_to):
    w = x_sm.shape[1]
    x = x_sm.reshape(nb, pad_to, w)[:, :lt]
    return jnp.transpose(x, (1, 0, 2)).reshape(lt * nb, w)


def kernel(x_prompt, x_sample, mem_prompt, cache_swa_k, cache_swa_v, cache_mem_k, cache_mem_v, state_conv, state_ssm, ln_in_g, ln_in_b, w_in, attn_sinks, gm_ln_g, gm_ln_b, gm_ws, gm_bs, conv_w, conv_b, dt_bias, a_log, d_skip, ssm_norm_g, w_pa, w_pb, w_pc, w_o, ln1_g, ln1_b, w_cq, w_ck, w_cv, w_co, ln2_g, ln2_b, w_rg, b_rg, w_re, b_re, w_gate, w_up, w_down, ln3_g, ln3_b):
    bp, seq, d = x_prompt.shape
    nb, lt, _ = x_sample.shape
    depth = w_in.shape[0]
    mem_len = mem_prompt.shape[1]
    past_len = PAST_LEN
    wb = cache_swa_k.shape[2]
    assert wb == WINDOW and seq % CHUNK == 0 and lt <= SUBLANES
    tp, ts = bp * seq, nb * lt
    alpha = (2 * depth) ** 0.25
    qpad = SUBLANES

    xp = x_prompt.reshape(tp, d)
    xs = jnp.transpose(x_sample, (1, 0, 2)).reshape(ts, d)
    hf, hb = ln_in(xp, xs, ln_in_g, ln_in_b)
    cos_t, sin_t = rope_tables(tp, seq, ts, nb, past_len)
    mem_b = mem_prompt.reshape(bp * mem_len, d).astype(BF16)

    in_w = w_in.shape[2]
    assert in_w == Q_W + 2 * KV_W + 2 * GM_W + SSM_INNER + CONV_DIM + SSM_HEADS + 3 * D_MODEL
    w_pad = jnp.pad(w_in.astype(BF16), ((0, 0), (0, 0), (0, PROJ_W - in_w)))

    outs = {k: [] for k in ("p_k", "p_v", "p_mk", "p_mv", "p_conv", "p_ssm", "p_gv",
                            "s_k", "s_v", "s_conv", "s_ssm", "s_gv")}
    n_qblk = seq // WINDOW
    s_states = None
    for l in range(depth):
        proj = in_projection(hb, w_pad, l)
        q_rot, k_rot = rope_qk(proj, cos_t, sin_t)

        kcol, vcol = 0, COL_V // KV_W
        att_p = swa_attention(
            attn_sinks[l], q_rot, k_rot, k_rot, proj, proj,
            n_seq=bp, n_blk=n_qblk, qb=WINDOW, prev_from_block0=False,
            kp_map=lambda i, n: (jnp.maximum(i * n_qblk + n - 1, 0), kcol),
            kc_map=lambda i, n: (i * n_qblk + n, kcol),
            vp_map=lambda i, n: (jnp.maximum(i * n_qblk + n - 1, 0), vcol),
            vc_map=lambda i, n: (i * n_qblk + n, vcol),
            out_dtype=BF16, name="swa_prompt")
        k_s_tm = k_rot[tp:]
        v_s_tm = proj[tp:, COL_V:COL_V + KV_W]
        q_s = _to_seq_major(q_rot[tp:], lt, nb, qpad)
        k_s = _to_seq_major(k_s_tm, lt, nb, qpad)
        v_s = _to_seq_major(v_s_tm, lt, nb, qpad)
        att_s8 = swa_attention_cached(attn_sinks[l], q_s, k_s, v_s, cache_swa_k, cache_swa_v, layer=l,
                                      n_seq=nb, qb=qpad, bseq=_pick(nb, (8, 4, 2, 1)), name="swa_sample")
        att_s = _to_time_major(att_s8, lt, nb, qpad).astype(BF16)
        last_w = lambda a, c0: jnp.stack(
            [a[(i + 1) * seq - WINDOW:(i + 1) * seq, c0:c0 + KV_W] for i in range(bp)]
        ).reshape(bp, WINDOW, N_KV_HEADS, HEAD_DIM)
        outs["p_k"].append(last_w(k_rot, 0))
        outs["p_v"].append(last_w(proj, COL_V))
        k_new = jnp.transpose(k_s_tm.reshape(lt, nb, N_KV_HEADS, HEAD_DIM), (1, 0, 2, 3))
        v_new = jnp.transpose(v_s_tm.reshape(lt, nb, N_KV_HEADS, HEAD_DIM), (1, 0, 2, 3))
        outs["s_k"].append(jnp.concatenate([cache_swa_k[l], k_new], axis=1)[:, -wb:])
        outs["s_v"].append(jnp.concatenate([cache_swa_v[l], v_new], axis=1)[:, -wb:])

        gm_p, vg_last = gmlp_prompt(proj, bp, seq, gm_ws[l], gm_bs[l], gm_ln_g[l], gm_ln_b[l])
        gm_s, vg_s = gmlp_sample(proj, tp, nb, lt, gm_ws[l], gm_bs[l], gm_ln_g[l], gm_ln_b[l])
        outs["p_gv"].append(vg_last.reshape(bp, CHUNK, GM_GROUPS, GM_GROUP_DIM))
        outs["s_gv"].append(jnp.transpose(vg_s.reshape(lt, nb, GM_GROUPS, GM_GROUP_DIM), (1, 0, 2, 3)))

        dtt = jnp.transpose(proj[:tp, COL_DT:COL_DT + SSM_HEADS])
        y_p, st_p = ssd_prompt(proj, dtt, bp, seq, conv_w[l], conv_b[l], dt_bias[l], a_log[l],
                               d_skip[l], ssm_norm_g[l])
        ssm_s, s_states = ssd_sample(proj, tp, nb, lt, state_conv[l], state_ssm, s_states, l, conv_w[l],
                                     conv_b[l], dt_bias[l], a_log[l], d_skip[l], ssm_norm_g[l])
        outs["p_conv"].append(jnp.stack(
            [proj[(i + 1) * seq - (CONV_K - 1):(i + 1) * seq, COL_XS:COL_XS + CONV_DIM] for i in range(bp)]))
        xbc_s = jnp.transpose(proj[tp:, COL_XS:COL_XS + CONV_DIM].reshape(lt, nb, CONV_DIM), (1, 0, 2))
        outs["s_conv"].append(jnp.concatenate([state_conv[l], xbc_s], axis=1)[:, -(CONV_K - 1):])
        outs["p_ssm"].append(st_p.reshape(bp, SSM_HEADS, SSM_HEAD_DIM, SSM_STATE))

        merged = gated_merge(att_p, att_s, gm_p, gm_s, y_p, ssm_s, w_pa[l].astype(BF16),
                             w_pb[l].astype(BF16), w_pc[l].astype(BF16), proj)
        h1f, h1b = matmul_ln(merged, w_o[l].astype(BF16), hf, ln1_g[l], ln1_b[l], alpha=alpha, name="out_proj_ln1")

        qc = matmul(h1b, w_cq[l].astype(BF16), name="xattn_q")
        pmk = matmul(mem_b, w_ck[l].astype(BF16), name="mem_k")
        pmv = matmul(mem_b, w_cv[l].astype(BF16), name="mem_v")
        pmk5 = pmk.reshape(1, bp, mem_len, MEM_HEADS, MEM_HEAD_DIM)
        pmv5 = pmv.reshape(1, bp, mem_len, MEM_HEADS, MEM_HEAD_DIM)
        outs["p_mk"].append(pmk5[0])
        outs["p_mv"].append(pmv5[0])
        tq = _pick(seq, (512, 256, 128))
        o_p = cross_attention(qc, pmk5, pmv5, layer=0, n_seq=bp, seq=seq, tq=tq, bseq=1,
                              name="xattn_prompt")
        qc_s = _to_seq_major(qc[tp:], lt, nb, qpad)
        o_s8 = cross_attention(qc_s, cache_mem_k, cache_mem_v, layer=l, n_seq=nb, seq=qpad, tq=qpad,
                               bseq=_pick(nb, (4, 2, 1)), name="xattn_sample")
        o_all = jnp.concatenate([o_p, _to_time_major(o_s8, lt, nb, qpad)], axis=0).astype(BF16)
        h2f, h2b = matmul_ln(o_all, w_co[l].astype(BF16), h1f, ln2_g[l], ln2_b[l], alpha=alpha, name="xattn_out_ln2")

        hf, hb = hierarchical_moe_ln(h2f, l, w_rg[l], b_rg[l], w_re[l], b_re[l], w_gate, w_up, w_down,
                                     ln3_g[l], ln3_b[l], alpha=alpha,
                                     split_rows=tp if l == depth - 1 else None)

    y_prompt = hf.reshape(bp, seq, d)
    y_sample = jnp.transpose(hb.reshape(lt, nb, d), (1, 0, 2))
    st = lambda k: jnp.stack(outs[k])
    s_ssm = s_states.reshape(depth, nb, SSM_HEADS, SSM_HEAD_DIM, SSM_STATE)
    return (y_prompt, y_sample, st("p_k"), st("p_v"), st("p_mk"), st("p_mv"), st("p_conv"), st("p_ssm"),
            st("p_gv"), st("s_k"), st("s_v"), st("s_conv"), s_ssm, st("s_gv"))
```

```python
import functools
import math

import numpy as np
import jax
import jax.numpy as jnp
from jax import lax
from jax.experimental import pallas as pl
from jax.experimental.pallas import tpu as pltpu

F32 = jnp.float32
BF16 = jnp.bfloat16

D_MODEL = 2048
N_HEADS = 32
N_KV_HEADS = 4
HEAD_DIM = 64
WINDOW = 128
PAST_LEN = 8192
ROPE_THETA = 10000.0
CHUNK = 128
GM_GROUPS = 16
GM_GROUP_DIM = 128
SSM_HEADS = 32
SSM_HEAD_DIM = 64
SSM_GROUPS = 4
SSM_STATE = 128
CONV_K = 4
MEM_HEADS = 4
MEM_HEAD_DIM = 128
N_EGROUPS = 4
EXPERTS_PER_GROUP = 8
N_EXPERTS = N_EGROUPS * EXPERTS_PER_GROUP
EXPERT_FF = D_MODEL // 2
Q_W = N_HEADS * HEAD_DIM
KV_W = N_KV_HEADS * HEAD_DIM
GM_W = GM_GROUPS * GM_GROUP_DIM
SSM_INNER = SSM_HEADS * SSM_HEAD_DIM
SSM_BC = SSM_GROUPS * SSM_STATE
CONV_DIM = SSM_INNER + 2 * SSM_BC
MEM_W = MEM_HEADS * MEM_HEAD_DIM
LN_EPS = 1e-5
NEG_BIG = -1e30

VMEM_LIMIT_BYTES = 52 * 1024 * 1024
LANES = 128
SUBLANES = 8

IN_TN = 512
COL_Q = 0
COL_GU = 2048
COL_GV = 4096
COL_Z = 6144
COL_XS = 8192
COL_BC = 10240
COL_DT = 11264
COL_GATES = COL_DT + SSM_HEADS
COL_K = 17920
COL_V = 18176
PROJ_W = 18432

MOE_TM = 256
DMA_LOOP_UNROLL = 8


def _cparams(n_grid):
    return pltpu.CompilerParams(
        dimension_semantics=("arbitrary",) * n_grid,
        vmem_limit_bytes=VMEM_LIMIT_BYTES,
    )


def _pick(n, prefs):
    for p in prefs:
        if n % p == 0:
            return p
    raise ValueError(f"no tile for {n} in {prefs}")


def _ln_rows(x, g, b):
    mu = jnp.mean(x, axis=-1, keepdims=True)
    xc = x - mu
    var = jnp.mean(xc * xc, axis=-1, keepdims=True)
    return xc * lax.rsqrt(var + LN_EPS) * g + b


def _sigmoid(x):
    return 1.0 / (1.0 + jnp.exp(-x))


def _silu(x):
    return x * _sigmoid(x)


def _softplus(x):
    return jnp.maximum(x, 0.0) + jnp.log1p(jnp.exp(-jnp.abs(x)))


def _gelu(x):
    return jax.nn.gelu(x, approximate=True)


def _split3(x):
    hi = x.astype(BF16)
    r1 = x - hi.astype(F32)
    mid = r1.astype(BF16)
    lo = (r1 - mid.astype(F32)).astype(BF16)
    return hi, mid, lo


def _dot(a, b):
    return jnp.dot(a, b, preferred_element_type=F32)


def _dot_nt(a, b):
    return lax.dot_general(a, b, (((1,), (1,)), ((), ())), preferred_element_type=F32)


def _exact_dot_left(pieces, m):
    acc = _dot(pieces[0], m)
    for p in pieces[1:]:
        acc = acc + _dot(p, m)
    return acc


def _ln_in_kernel(xp_ref, xs_ref, g_ref, b_ref, of_ref, ob_ref, *, n_p):
    i = pl.program_id(0)

    @pl.when(i < n_p)
    def _():
        y = _ln_rows(xp_ref[...], g_ref[...], b_ref[...])
        of_ref[...] = y
        ob_ref[...] = y.astype(BF16)

    @pl.when(i >= n_p)
    def _():
        y = _ln_rows(xs_ref[...], g_ref[...], b_ref[...])
        of_ref[...] = y
        ob_ref[...] = y.astype(BF16)


def ln_in(xp, xs, g, b):
    tp, d = xp.shape
    ts = xs.shape[0]
    tm = _pick(math.gcd(tp, ts), (256, 128, 64, 32, 16, 8))
    n_p, n_s = tp // tm, ts // tm
    t = tp + ts
    return pl.pallas_call(
        functools.partial(_ln_in_kernel, n_p=n_p),
        grid=(n_p + n_s,),
        in_specs=[
            pl.BlockSpec((tm, d), lambda i: (jnp.minimum(i, n_p - 1), 0)),
            pl.BlockSpec((tm, d), lambda i: (jnp.maximum(i - n_p, 0), 0)),
            pl.BlockSpec((1, d), lambda i: (0, 0)),
            pl.BlockSpec((1, d), lambda i: (0, 0)),
        ],
        out_specs=[
            pl.BlockSpec((tm, d), lambda i: (i, 0)),
            pl.BlockSpec((tm, d), lambda i: (i, 0)),
        ],
        out_shape=[jax.ShapeDtypeStruct((t, d), F32), jax.ShapeDtypeStruct((t, d), BF16)],
        compiler_params=_cparams(1),
        name="ln_in",
    )(xp, xs, g.reshape(1, d), b.reshape(1, d))


def _mm_kernel(x_ref, w_ref, o_ref):
    o_ref[...] = _dot(x_ref[...], w_ref[...]).astype(o_ref.dtype)


def matmul(x, w, *, out_dtype=F32, tm_prefs=(1088, 1024, 512, 256, 128, 64, 32, 16, 8),
           tn_prefs=(1280, 1024, 512, 256, 128), name="mm"):
    t, k = x.shape
    n = w.shape[1]
    tm = _pick(t, tm_prefs)
    tn = _pick(n, tn_prefs)
    return pl.pallas_call(
        _mm_kernel,
        grid=(n // tn, t // tm),
        in_specs=[
            pl.BlockSpec((tm, k), lambda j, i: (i, 0)),
            pl.BlockSpec((k, tn), lambda j, i: (0, j)),
        ],
        out_specs=pl.BlockSpec((tm, tn), lambda j, i: (i, j)),
        out_shape=jax.ShapeDtypeStruct((t, n), out_dtype),
        compiler_params=_cparams(2),
        name=name,
    )(x, w)


def in_projection(x, w_pad, layer):
    t, k = x.shape
    n_blk = PROJ_W // IN_TN
    kv_blk = Q_W // IN_TN
    assert 2 * KV_W == IN_TN and COL_K == (n_blk - 1) * IN_TN
    tm = _pick(t, (2176, 1088, 544, 32, 16, 8))

    def out_map(i, j):
        return i, jnp.where(j < kv_blk, j, jnp.where(j == kv_blk, n_blk - 1, j - 1))

    return pl.pallas_call(
        _mm_kernel,
        grid=(t // tm, n_blk),
        in_specs=[
            pl.BlockSpec((tm, k), lambda i, j: (i, 0)),
            pl.BlockSpec((None, k, IN_TN), lambda i, j: (layer, 0, j)),
        ],
        out_specs=pl.BlockSpec((tm, IN_TN), out_map),
        out_shape=jax.ShapeDtypeStruct((t, PROJ_W), F32),
        compiler_params=_cparams(2),
        name="in_proj",
    )(x, w_pad)


def _mm_ln_kernel(x_ref, w_ref, r_ref, g_ref, b_ref, of_ref, ob_ref, *, alpha):
    y = _dot(x_ref[...], w_ref[...])
    h = _ln_rows(alpha * r_ref[...] + y, g_ref[...], b_ref[...])
    of_ref[...] = h
    ob_ref[...] = h.astype(BF16)


def matmul_ln(x, w, res, g, b, *, alpha, name="mm_ln"):
    t, k = x.shape
    d = w.shape[1]
    tm = _pick(t, (256, 128, 64, 32, 16, 8))
    return pl.pallas_call(
        functools.partial(_mm_ln_kernel, alpha=alpha),
        grid=(t // tm,),
        in_specs=[
            pl.BlockSpec((tm, k), lambda i: (i, 0)),
            pl.BlockSpec((k, d), lambda i: (0, 0)),
            pl.BlockSpec((tm, d), lambda i: (i, 0)),
            pl.BlockSpec((1, d), lambda i: (0, 0)),
            pl.BlockSpec((1, d), lambda i: (0, 0)),
        ],
        out_specs=[
            pl.BlockSpec((tm, d), lambda i: (i, 0)),
            pl.BlockSpec((tm, d), lambda i: (i, 0)),
        ],
        out_shape=[jax.ShapeDtypeStruct((t, d), F32), jax.ShapeDtypeStruct((t, d), BF16)],
        compiler_params=_cparams(1),
        name=name,
    )(x, w, res, g.reshape(1, d), b.reshape(1, d))


def _merge_kernel(ap_ref, as_ref, bp_ref, bs_ref, cp_ref, cs_ref, wa_ref, wb_ref, wc_ref,
                  ga_ref, ga2_ref, gb_ref, gb2_ref, gc_ref, gc2_ref, o_ref, *, n_p, shift):
    i = pl.program_id(1)
    is_p = i < n_p
    tn = o_ref.shape[1]

    def gate(main_ref, tail_ref):
        win = jnp.concatenate([main_ref[...], tail_ref[...]], axis=1)
        return _sigmoid(win[:, shift:shift + tn])

    xa = jnp.where(is_p, ap_ref[...], as_ref[...])
    xb = jnp.where(is_p, bp_ref[...], bs_ref[...])
    xc = jnp.where(is_p, cp_ref[...], cs_ref[...])
    acc = gate(ga_ref, ga2_ref) * _dot(xa, wa_ref[...])
    acc = acc + gate(gb_ref, gb2_ref) * _dot(xb, wb_ref[...])
    acc = acc + gate(gc_ref, gc2_ref) * _dot(xc, wc_ref[...])
    o_ref[...] = acc.astype(o_ref.dtype)


def gated_merge(att_p, att_s, gm_p, gm_s, ssm_p, ssm_s, w_pa, w_pb, w_pc, proj):
    tp, k = att_p.shape
    ts = att_s.shape[0]
    d = w_pa.shape[1]
    tm = _pick(math.gcd(tp, ts), (256, 128, 64, 32, 16, 8))
    tn = 1024
    n_p, n_s = tp // tm, ts // tm
    shift = COL_GATES - COL_DT
    assert COL_DT % tn == 0 and d % tn == 0 and shift < LANES

    def xp_spec():
        return pl.BlockSpec((tm, k), lambda j, i: (jnp.minimum(i, n_p - 1), 0))

    def xs_spec():
        return pl.BlockSpec((tm, k), lambda j, i: (jnp.maximum(i - n_p, 0), 0))

    def w_spec():
        return pl.BlockSpec((k, tn), lambda j, i: (0, j))

    def g_specs(which):
        base = (COL_DT + which * d) // tn
        tail = (COL_DT + which * d) // LANES
        return [pl.BlockSpec((tm, tn), lambda j, i: (i, base + j)),
                pl.BlockSpec((tm, LANES), lambda j, i: (i, tail + (j + 1) * (tn // LANES)))]

    return pl.pallas_call(
        functools.partial(_merge_kernel, n_p=n_p, shift=shift),
        grid=(d // tn, n_p + n_s),
        in_specs=[xp_spec(), xs_spec(), xp_spec(), xs_spec(), xp_spec(), xs_spec(),
                  w_spec(), w_spec(), w_spec()] + g_specs(0) + g_specs(1) + g_specs(2),
        out_specs=pl.BlockSpec((tm, tn), lambda j, i: (i, j)),
        out_shape=jax.ShapeDtypeStruct((tp + ts, d), BF16),
        compiler_params=_cparams(2),
        name="gated_merge",
    )(att_p, att_s, gm_p, gm_s, ssm_p, ssm_s, w_pa, w_pb, w_pc, *([proj] * 6))


def _rope_block(x, cos, sin_signed, first_half):
    outs = []
    for c in range(x.shape[1] // LANES):
        xc = x[:, c * LANES:(c + 1) * LANES]
        fwd = pltpu.roll(xc, LANES - HEAD_DIM // 2, axis=1)
        bwd = pltpu.roll(xc, HEAD_DIM // 2, axis=1)
        partner = jnp.where(first_half, fwd, bwd)
        outs.append(xc * cos + partner * sin_signed)
    return outs


def _rope_kernel(q_ref, k_ref, cos_ref, sin_ref, qo_ref, ko_ref):
    cos = cos_ref[...]
    sin_signed = sin_ref[...]
    lane = lax.broadcasted_iota(jnp.int32, cos.shape, 1)
    first_half = (lane % HEAD_DIM) < (HEAD_DIM // 2)
    for c, o in enumerate(_rope_block(q_ref[...], cos, sin_signed, first_half)):
        qo_ref[:, c * LANES:(c + 1) * LANES] = o
    for c, o in enumerate(_rope_block(k_ref[...], cos, sin_signed, first_half)):
        ko_ref[:, c * LANES:(c + 1) * LANES] = o


def rope_qk(proj, cos_t, sin_t):
    t = proj.shape[0]
    tm = _pick(t, (256, 128, 64, 32, 16, 8))
    return pl.pallas_call(
        _rope_kernel,
        grid=(t // tm,),
        in_specs=[
            pl.BlockSpec((tm, Q_W), lambda i: (i, COL_Q // Q_W)),
            pl.BlockSpec((tm, KV_W), lambda i: (i, COL_K // KV_W)),
            pl.BlockSpec((tm, LANES), lambda i: (i, 0)),
            pl.BlockSpec((tm, LANES), lambda i: (i, 0)),
        ],
        out_specs=[
            pl.BlockSpec((tm, Q_W), lambda i: (i, 0)),
            pl.BlockSpec((tm, KV_W), lambda i: (i, 0)),
        ],
        out_shape=[jax.ShapeDtypeStruct((t, Q_W), F32), jax.ShapeDtypeStruct((t, KV_W), F32)],
        compiler_params=_cparams(1),
        name="rope_qk",
    )(proj, proj, cos_t, sin_t)


def rope_tables(tp, seq, ts, nb, past_len):
    half = HEAD_DIM // 2
    inv = ROPE_THETA ** (-jnp.arange(half, dtype=F32) / half)
    pos_p = jnp.arange(tp, dtype=jnp.int32) % seq
    pos_s = past_len + jnp.arange(ts, dtype=jnp.int32) // nb
    pos = jnp.concatenate([pos_p, pos_s]).astype(F32)
    ang = pos[:, None] * inv[None, :]
    cos = jnp.tile(jnp.cos(ang), (1, LANES // half))
    sin = jnp.sin(ang)
    sin_signed = jnp.tile(jnp.concatenate([-sin, sin], axis=1), (1, LANES // HEAD_DIM))
    return cos, sin_signed


def _dup_head(slab, g):
    lane = lax.broadcasted_iota(jnp.int32, slab.shape, 1)
    rolled = pltpu.roll(slab, HEAD_DIM, axis=1)
    if g % 2 == 0:
        return jnp.where(lane < HEAD_DIM, slab, rolled)
    return jnp.where(lane < HEAD_DIM, rolled, slab)


def _swa_bias(qb, prev_ok):
    kw = WINDOW + qb
    ii = lax.broadcasted_iota(jnp.int32, (qb, kw), 0)
    jj = lax.broadcasted_iota(jnp.int32, (qb, kw), 1)
    ok_prev = (jj < WINDOW) & (jj > ii)
    if prev_ok is not True:
        ok_prev = ok_prev & prev_ok
    ok = ok_prev | ((jj >= WINDOW) & ((jj - WINDOW) <= ii))
    return jnp.where(ok, 0.0, NEG_BIG)


def _swa_group(q, kdup, vdup, bias, sink_ref, g, qb, store):
    rep = N_HEADS // N_KV_HEADS
    lane = lax.broadcasted_iota(jnp.int32, (qb, LANES), 1)
    scale = HEAD_DIM ** -0.5
    rows = []
    for r in range(rep):
        h = g * rep + r
        qc = q[:, (h // 2) * LANES:(h // 2 + 1) * LANES]
        keep = (lane < HEAD_DIM) if h % 2 == 0 else (lane >= HEAD_DIM)
        rows.append(jnp.where(keep, qc * scale, 0.0))
    s = _dot_nt(jnp.concatenate(rows, axis=0).astype(BF16), kdup)
    ps = []
    for r in range(rep):
        sr = s[r * qb:(r + 1) * qb] + bias
        sink = sink_ref[g * rep + r]
        m = jnp.maximum(jnp.max(sr, axis=-1, keepdims=True), sink)
        p = jnp.exp(sr - m)
        den = jnp.sum(p, axis=-1, keepdims=True) + jnp.exp(sink - m)
        ps.append(p / den)
    o = _dot(jnp.concatenate(ps, axis=0).astype(BF16), vdup)
    for c in range(rep // 2):
        oa = o[(2 * c) * qb:(2 * c + 1) * qb]
        ob = o[(2 * c + 1) * qb:(2 * c + 2) * qb]
        store((g * rep // 2 + c) * LANES, jnp.where(lane < HEAD_DIM, oa, ob))


def _swa_kernel(sink_ref, q_ref, kp_ref, kc_ref, vp_ref, vc_ref, o_ref, *, qb, prev_from_block0):
    n = pl.program_id(1)
    q = q_ref[...]
    kp, kc, vp, vc = kp_ref[...], kc_ref[...], vp_ref[...], vc_ref[...]
    bias = _swa_bias(qb, True if prev_from_block0 else (n > 0))

    def store(col, val):
        o_ref[:, col:col + LANES] = val.astype(o_ref.dtype)

    for g in range(N_KV_HEADS):
        sl = slice((g // 2) * LANES, (g // 2 + 1) * LANES)
        kdup = _dup_head(jnp.concatenate([kp[:, sl], kc[:, sl]], axis=0), g).astype(BF16)
        vdup = _dup_head(jnp.concatenate([vp[:, sl], vc[:, sl]], axis=0), g).astype(BF16)
        _swa_group(q, kdup, vdup, bias, sink_ref, g, qb, store)


def _swa_cache_kernel(*refs, qb, bseq, layer):
    sink_ref, q_ref, kc_ref, vc_ref, k_hbm, v_hbm, o_ref, kbuf, vbuf, sem = refs
    slot = _prefetch_heads((k_hbm, v_hbm), (kbuf, vbuf), sem, layer, bseq, N_KV_HEADS)
    bias = _swa_bias(qb, True)
    for b in range(bseq):
        rows = slice(b * qb, (b + 1) * qb)
        q = q_ref[rows, :]

        def store(col, val, rows=rows):
            o_ref[rows, col:col + LANES] = val.astype(o_ref.dtype)

        for g in range(N_KV_HEADS):
            hs = slice(g * HEAD_DIM, (g + 1) * HEAD_DIM)
            kcat = jnp.concatenate([kbuf[slot, b, g], kc_ref[rows, hs]], axis=0)
            vcat = jnp.concatenate([vbuf[slot, b, g], vc_ref[rows, hs]], axis=0)
            kdup = jnp.concatenate([kcat, kcat], axis=1).astype(BF16)
            vdup = jnp.concatenate([vcat, vcat], axis=1).astype(BF16)
            _swa_group(q, kdup, vdup, bias, sink_ref, g, qb, store)


def swa_attention_cached(sinks, q, kc, vc, cache_k, cache_v, *, layer, n_seq, qb, bseq, name):
    rows = bseq * qb
    buf = pltpu.VMEM((2, bseq, N_KV_HEADS, WINDOW, HEAD_DIM), F32)
    return pl.pallas_call(
        functools.partial(_swa_cache_kernel, qb=qb, bseq=bseq, layer=layer),
        grid=(n_seq // bseq,),
        in_specs=[pl.BlockSpec(memory_space=pltpu.SMEM),
                  pl.BlockSpec((rows, Q_W), lambda i: (i, 0)),
                  pl.BlockSpec((rows, KV_W), lambda i: (i, 0)),
                  pl.BlockSpec((rows, KV_W), lambda i: (i, 0)),
                  pl.BlockSpec(memory_space=pl.ANY), pl.BlockSpec(memory_space=pl.ANY)],
        out_specs=pl.BlockSpec((rows, Q_W), lambda i: (i, 0)),
        out_shape=jax.ShapeDtypeStruct((n_seq * qb, Q_W), F32),
        scratch_shapes=[buf, buf, pltpu.SemaphoreType.DMA((2,))],
        compiler_params=_cparams(1),
        name=name,
    )(sinks, q, kc, vc, cache_k, cache_v)


def swa_attention(sinks, q, kp, kc, vp, vc, *, n_seq, n_blk, qb, prev_from_block0,
                  kp_map, kc_map, vp_map, vc_map, out_dtype, name):
    return pl.pallas_call(
        functools.partial(_swa_kernel, qb=qb, prev_from_block0=prev_from_block0),
        grid=(n_seq, n_blk),
        in_specs=[
            pl.BlockSpec(memory_space=pltpu.SMEM),
            pl.BlockSpec((qb, Q_W), lambda i, n: (i * n_blk + n, 0)),
            pl.BlockSpec((WINDOW, KV_W), kp_map),
            pl.BlockSpec((qb, KV_W), kc_map),
            pl.BlockSpec((WINDOW, KV_W), vp_map),
            pl.BlockSpec((qb, KV_W), vc_map),
        ],
        out_specs=pl.BlockSpec((qb, Q_W), lambda i, n: (i * n_blk + n, 0)),
        out_shape=jax.ShapeDtypeStruct((n_seq * n_blk * qb, Q_W), out_dtype),
        compiler_params=_cparams(2),
        name=name,
    )(sinks, q, kp, kc, vp, vc)


def _gmlp_p_kernel(gu_ref, gv_ref, ws_ref, bst_ref, lg_ref, lb_ref, gm_ref, vg_ref, *, n_chunks):
    n = pl.program_id(1)
    vg = _ln_rows(_gelu(gv_ref[...]), lg_ref[...], lb_ref[...])
    gu = gu_ref[...]
    ri = lax.broadcasted_iota(jnp.int32, (CHUNK, CHUNK), 0)
    ci = lax.broadcasted_iota(jnp.int32, (CHUNK, CHUNK), 1)
    tril = ri >= ci
    bst = bst_ref[...]
    for g in range(GM_GROUPS):
        sl = slice(g * GM_GROUP_DIM, (g + 1) * GM_GROUP_DIM)
        w = jnp.where(tril, ws_ref[g], 0.0).astype(BF16)
        s = _dot(w, vg[:, sl].astype(BF16)) + bst[:, g:g + 1]
        gm_ref[:, sl] = (_gelu(gu[:, sl]) * s).astype(gm_ref.dtype)

    @pl.when(n == n_chunks - 1)
    def _():
        vg_ref[...] = vg


def gmlp_prompt(proj, n_seq, seq, ws, bs, ln_g, ln_b):
    nc = seq // CHUNK
    return pl.pallas_call(
        functools.partial(_gmlp_p_kernel, n_chunks=nc),
        grid=(n_seq, nc),
        in_specs=[
            pl.BlockSpec((CHUNK, GM_W), lambda i, n: (i * nc + n, COL_GU // GM_W)),
            pl.BlockSpec((CHUNK, GM_W), lambda i, n: (i * nc + n, COL_GV // GM_W)),
            pl.BlockSpec((GM_GROUPS, CHUNK, CHUNK), lambda i, n: (0, 0, 0)),
            pl.BlockSpec((CHUNK, GM_GROUPS), lambda i, n: (0, 0)),
            pl.BlockSpec((1, GM_W), lambda i, n: (0, 0)),
            pl.BlockSpec((1, GM_W), lambda i, n: (0, 0)),
        ],
        out_specs=[
            pl.BlockSpec((CHUNK, GM_W), lambda i, n: (i * nc + n, 0)),
            pl.BlockSpec((CHUNK, GM_W), lambda i, n: (i, 0)),
        ],
        out_shape=[jax.ShapeDtypeStruct((n_seq * seq, GM_W), BF16),
                   jax.ShapeDtypeStruct((n_seq * CHUNK, GM_W), F32)],
        compiler_params=_cparams(2),
        name="gmlp_prompt",
    )(proj, proj, ws, bs.T, ln_g.reshape(1, GM_W), ln_b.reshape(1, GM_W))


def _gmlp_s_kernel(*refs, lt):
    gu_refs = refs[:lt]
    gv_refs = refs[lt:2 * lt]
    wrow_ref, brow_ref, lg_ref, lb_ref, gm_ref, vg_ref = refs[2 * lt:]
    nb = gu_refs[0].shape[0]
    vgs = [_ln_rows(_gelu(gv_refs[t][...]), lg_ref[...], lb_ref[...]) for t in range(lt)]
    for i in range(lt):
        s = brow_ref[i:i + 1, :]
        for j in range(i + 1):
            s = s + wrow_ref[i * lt + j:i * lt + j + 1, :] * vgs[j]
        gm_ref[i * nb:(i + 1) * nb, :] = (_gelu(gu_refs[i][...]) * s).astype(gm_ref.dtype)
        vg_ref[i * nb:(i + 1) * nb, :] = vgs[i]


def gmlp_sample(proj, tp, nb, lt, ws, bs, ln_g, ln_b):
    w_small = ws[:, :lt, :lt]
    wrow = jnp.repeat(jnp.transpose(w_small, (1, 2, 0)).reshape(lt * lt, GM_GROUPS), GM_GROUP_DIM, axis=1)
    brow = jnp.repeat(bs[:, :lt].T, GM_GROUP_DIM, axis=1)
    row0 = tp // nb

    def spec(t, col):
        return pl.BlockSpec((nb, GM_W), lambda i: (row0 + t, col // GM_W))

    in_specs = [spec(t, COL_GU) for t in range(lt)] + [spec(t, COL_GV) for t in range(lt)] + [
        pl.BlockSpec((lt * lt, GM_W), lambda i: (0, 0)),
        pl.BlockSpec((lt, GM_W), lambda i: (0, 0)),
        pl.BlockSpec((1, GM_W), lambda i: (0, 0)),
        pl.BlockSpec((1, GM_W), lambda i: (0, 0)),
    ]
    return pl.pallas_call(
        functools.partial(_gmlp_s_kernel, lt=lt),
        grid=(1,),
        in_specs=in_specs,
        out_specs=[pl.BlockSpec((lt * nb, GM_W), lambda i: (0, 0)),
                   pl.BlockSpec((lt * nb, GM_W), lambda i: (0, 0))],
        out_shape=[jax.ShapeDtypeStruct((lt * nb, GM_W), BF16),
                   jax.ShapeDtypeStruct((lt * nb, GM_W), F32)],
        compiler_params=_cparams(1),
        name="gmlp_sample",
    )(*([proj] * (2 * lt)), wrow, brow, ln_g.reshape(1, GM_W), ln_b.reshape(1, GM_W))


def _conv_silu(cur, prev8, w, bias):
    q = cur.shape[0]
    up = jnp.concatenate([prev8, cur], axis=0)
    acc = bias + up[SUBLANES:SUBLANES + q] * w[CONV_K - 1:CONV_K]
    for j in range(CONV_K - 1):
        off = SUBLANES - (CONV_K - 1) + j
        acc = acc + up[off:off + q] * w[j:j + 1]
    return _silu(acc)


def _ssd_p_kernel(z_ref, xs_ref, bc_ref, xsp_ref, bcp_ref, dt_ref, dtt_ref,
                  cwx_ref, cwbc_ref, cbx_ref, cbbc_ref, dtbe_ref, dtbc_ref, aloge_ref, alogc_ref,
                  dske_ref, ng_ref, e_ref, y_ref, st_ref, s_scr, *, n_chunks):
    c = pl.program_id(1)
    q = CHUNK
    rep = SSM_HEADS // SSM_GROUPS
    gw = rep * SSM_HEAD_DIM

    @pl.when(c == 0)
    def _():
        s_scr[...] = jnp.zeros_like(s_scr)

    has_prev = (c > 0).astype(F32)
    xs = _conv_silu(xs_ref[...], xsp_ref[...] * has_prev, cwx_ref[...], cbx_ref[...])
    bcm = _conv_silu(bc_ref[...], bcp_ref[...] * has_prev, cwbc_ref[...], cbbc_ref[...])

    ri = lax.broadcasted_iota(jnp.int32, (q, q), 0)
    ci = lax.broadcasted_iota(jnp.int32, (q, q), 1)
    tril = ri >= ci
    ones_tril = jnp.where(tril, 1.0, 0.0).astype(BF16)
    ones_triu = jnp.where(ri <= ci, 1.0, 0.0).astype(BF16)

    dt_e = _softplus(_exact_dot_left(_split3(dt_ref[...]), e_ref[...]) + dtbe_ref[...])
    a_e = dt_e * (-jnp.exp(aloge_ref[...]))
    a_hi, a_mid, a_lo = _split3(a_e)
    cum_e = _dot(ones_tril, a_hi) + _dot(ones_tril, a_mid) + _dot(ones_tril, a_lo)
    dt_t = _softplus(dtt_ref[...] + dtbc_ref[...])
    a_t = dt_t * (-jnp.exp(alogc_ref[...]))
    cum_t = _exact_dot_left(_split3(a_t), ones_triu)

    xdt = xs * dt_e
    cum_last = cum_e[q - 1:q, :]
    lane = lax.broadcasted_iota(jnp.int32, (q, LANES), 1)
    z = z_ref[...]
    ys = []
    for g in range(SSM_GROUPS):
        gs = slice(g * gw, (g + 1) * gw)
        bg = bcm[:, g * SSM_STATE:(g + 1) * SSM_STATE]
        cg = bcm[:, SSM_BC + g * SSM_STATE:SSM_BC + (g + 1) * SSM_STATE]
        bg16, cg16 = bg.astype(BF16), cg.astype(BF16)
        cb = _dot_nt(cg16, bg16)
        ydiag = []
        for pr in range(rep // 2):
            ms = []
            for hh in (2 * pr, 2 * pr + 1):
                h = g * rep + hh
                col = cum_e[:, h * SSM_HEAD_DIM:h * SSM_HEAD_DIM + 1]
                row = cum_t[h:h + 1, :]
                seg = jnp.where(tril, col - row, NEG_BIG)
                ms.append(cb * jnp.exp(seg))
            lhs = jnp.concatenate(ms, axis=1).astype(BF16)
            xslab = xdt[:, (g * rep + 2 * pr) * SSM_HEAD_DIM:(g * rep + 2 * pr + 2) * SSM_HEAD_DIM]
            xbd = jnp.concatenate([jnp.where(lane < SSM_HEAD_DIM, xslab, 0.0),
                                   jnp.where(lane >= SSM_HEAD_DIM, xslab, 0.0)], axis=0).astype(BF16)
            ydiag.append(_dot(lhs, xbd))
        ydiag = jnp.concatenate(ydiag, axis=1)
        s_old = s_scr[g]
        yoff = _dot(cg16, s_old.astype(BF16)) * jnp.exp(cum_e[:, gs])
        ys.append(ydiag + yoff)
        xw = xdt[:, gs] * jnp.exp(cum_last[:, gs] - cum_e[:, gs])
        s_scr[g] = s_old * jnp.exp(cum_last[:, gs]) + _dot(bg.T.astype(BF16), xw.astype(BF16))
    y = jnp.concatenate(ys, axis=1) + dske_ref[...] * xs
    gated = y * _silu(z)
    out = gated * lax.rsqrt(jnp.mean(gated * gated, axis=-1, keepdims=True) + LN_EPS) * ng_ref[...]
    y_ref[...] = out.astype(y_ref.dtype)

    @pl.when(c == n_chunks - 1)
    def _():
        for g in range(SSM_GROUPS):
            st_ref[0, g * gw:(g + 1) * gw, :] = s_scr[g].T


def _expand_heads(v):
    return jnp.repeat(v.astype(F32), SSM_HEAD_DIM).reshape(1, SSM_INNER)


def _head_expand_matrix():
    e = np.zeros((LANES, SSM_INNER), np.float32)
    for h in range(SSM_HEADS):
        e[h, h * SSM_HEAD_DIM:(h + 1) * SSM_HEAD_DIM] = 1.0
    return jnp.asarray(e, BF16)


def ssd_prompt(proj, dtt, n_seq, seq, conv_w, conv_b, dt_bias, a_log, d_skip, norm_g):
    b = n_seq
    nc = seq // CHUNK
    blk8 = CHUNK // SUBLANES
    const2 = lambda i, c: (0, 0)
    col_bc = COL_BC // (2 * SSM_BC)
    col_xs = COL_XS // SSM_INNER
    col_z = COL_Z // SSM_INNER
    prev_map_x = lambda i, c: (jnp.maximum((i * nc + c) * blk8 - 1, 0), col_xs)
    prev_map_bc = lambda i, c: (jnp.maximum((i * nc + c) * blk8 - 1, 0), col_bc)
    args = (
        proj, proj, proj, proj, proj, proj, dtt,
        conv_w[:, :SSM_INNER], conv_w[:, SSM_INNER:], conv_b[:SSM_INNER].reshape(1, -1),
        conv_b[SSM_INNER:].reshape(1, -1),
        _expand_heads(dt_bias), jnp.broadcast_to(dt_bias.astype(F32)[:, None], (SSM_HEADS, CHUNK)),
        _expand_heads(a_log), jnp.broadcast_to(a_log.astype(F32)[:, None], (SSM_HEADS, CHUNK)),
        _expand_heads(d_skip), norm_g.reshape(1, SSM_INNER), _head_expand_matrix(),
    )
    in_specs = [
        pl.BlockSpec((CHUNK, SSM_INNER), lambda i, c: (i * nc + c, col_z)),
        pl.BlockSpec((CHUNK, SSM_INNER), lambda i, c: (i * nc + c, col_xs)),
        pl.BlockSpec((CHUNK, 2 * SSM_BC), lambda i, c: (i * nc + c, col_bc)),
        pl.BlockSpec((SUBLANES, SSM_INNER), prev_map_x),
        pl.BlockSpec((SUBLANES, 2 * SSM_BC), prev_map_bc),
        pl.BlockSpec((CHUNK, LANES), lambda i, c: (i * nc + c, COL_DT // LANES)),
        pl.BlockSpec((SSM_HEADS, CHUNK), lambda i, c: (0, i * nc + c)),
        pl.BlockSpec((CONV_K, SSM_INNER), const2),
        pl.BlockSpec((CONV_K, 2 * SSM_BC), const2),
        pl.BlockSpec((1, SSM_INNER), const2),
        pl.BlockSpec((1, 2 * SSM_BC), const2),
        pl.BlockSpec((1, SSM_INNER), const2),
        pl.BlockSpec((SSM_HEADS, CHUNK), const2),
        pl.BlockSpec((1, SSM_INNER), const2),
        pl.BlockSpec((SSM_HEADS, CHUNK), const2),
        pl.BlockSpec((1, SSM_INNER), const2),
        pl.BlockSpec((1, SSM_INNER), const2),
        pl.BlockSpec((LANES, SSM_INNER), const2),
    ]
    return pl.pallas_call(
        functools.partial(_ssd_p_kernel, n_chunks=nc),
        grid=(b, nc),
        in_specs=in_specs,
        out_specs=[
            pl.BlockSpec((CHUNK, SSM_INNER), lambda i, c: (i * nc + c, 0)),
            pl.BlockSpec((1, SSM_INNER, SSM_STATE), lambda i, c: (i, 0, 0)),
        ],
        out_shape=[jax.ShapeDtypeStruct((b * seq, SSM_INNER), BF16),
                   jax.ShapeDtypeStruct((b, SSM_INNER, SSM_STATE), F32)],
        scratch_shapes=[pltpu.VMEM((SSM_GROUPS, SSM_STATE, SSM_INNER // SSM_GROUPS), F32)],
        compiler_params=_cparams(2),
        name="ssd_prompt",
    )(*args)


def _group_expand_matrix():
    gw = SSM_INNER // SSM_GROUPS
    m = np.zeros((SSM_BC, SSM_INNER), np.float32)
    for g in range(SSM_GROUPS):
        m[g * SSM_STATE:(g + 1) * SSM_STATE, g * gw:(g + 1) * gw] = 1.0
    return jnp.asarray(m, BF16)


def _ssd_s_pre_kernel(*refs, lt):
    xs_refs = refs[:lt]
    bc_refs = refs[lt:2 * lt]
    dt_refs = refs[2 * lt:3 * lt]
    (cx_ref, cbc_ref, cwx_ref, cwbc_ref, cbx_ref, cbbc_ref, dtbe_ref, aloge_ref, dske_ref, e_ref,
     gmat_ref, c_ref, b_ref, xw_ref, dec_ref, yd_ref, ec_ref) = refs[3 * lt:]
    nprev = CONV_K - 1
    ux = [cx_ref[j] for j in range(nprev)] + [r[...] for r in xs_refs]
    ub = [cbc_ref[j] for j in range(nprev)] + [r[...] for r in bc_refs]
    cwx, cwbc = cwx_ref[...], cwbc_ref[...]
    neg_a = -jnp.exp(aloge_ref[...])
    xs, bm, cm, xdt, cum = [], [], [], [], []
    run = None
    for t in range(lt):
        ax = cbx_ref[...]
        ab = cbbc_ref[...]
        for j in range(CONV_K):
            ax = ax + ux[t + j] * cwx[j:j + 1]
            ab = ab + ub[t + j] * cwbc[j:j + 1]
        x_t = _silu(ax)
        bc_t = _silu(ab)
        dt_e = _softplus(_exact_dot_left(_split3(dt_refs[t][...]), e_ref[...]) + dtbe_ref[...])
        a_t = dt_e * neg_a
        run = a_t if run is None else run + a_t
        xs.append(x_t)
        bm.append(bc_t[:, :SSM_BC])
        cm.append(bc_t[:, SSM_BC:])
        xdt.append(x_t * dt_e)
        cum.append(run)
    for i in range(lt):
        yd = dske_ref[...] * xs[i]
        for j in range(i + 1):
            hi, mid, _ = _split3(cm[i] * bm[j])
            cbe = _dot(hi, gmat_ref[...]) + _dot(mid, gmat_ref[...])
            yd = yd + cbe * jnp.exp(cum[i] - cum[j]) * xdt[j]
        yd_ref[i] = yd
        ec_ref[i] = jnp.exp(cum[i])
        c_ref[i] = cm[i]
        b_ref[i] = bm[i]
        xw_ref[i] = xdt[i] * jnp.exp(cum[lt - 1] - cum[i])
    dec_ref[...] = jnp.exp(cum[lt - 1])


def _rows_block(rows, total):
    c = rows[0].shape[1]
    rid = lax.broadcasted_iota(jnp.int32, (SUBLANES, c), 0)
    acc = jnp.zeros((SUBLANES, c), F32)
    for j, r in enumerate(rows):
        acc = jnp.where(rid == j, jnp.broadcast_to(r, (SUBLANES, c)), acc)
    if total == SUBLANES:
        return acc
    return jnp.concatenate([acc, jnp.zeros((total - SUBLANES, c), F32)], axis=0)


def _ssd_s_state_kernel_inplace(c_ref, b_ref, xw_ref, dec_ref, h0_ref, prev_ref, hn_ref, yr_ref, *, lt):
    del prev_ref
    _ssd_s_state_kernel(c_ref, b_ref, xw_ref, dec_ref, h0_ref, hn_ref, yr_ref, lt=lt)


def _ssd_s_state_kernel(c_ref, b_ref, xw_ref, dec_ref, h0_ref, hn_ref, yr_ref, *, lt, slot=0,
                        fill_slots=None):
    b = pl.program_id(0)
    gw = SSM_INNER // SSM_GROUPS
    c8 = _rows_block([c_ref[i, pl.ds(b, 1), :] for i in range(lt)], SUBLANES).astype(BF16)
    b128 = _rows_block([b_ref[i, pl.ds(b, 1), :] for i in range(lt)], LANES).astype(BF16)
    xaug = _rows_block([xw_ref[i, pl.ds(b, 1), :] for i in range(lt)] + [dec_ref[pl.ds(b, 1), :]], LANES)
    for g in range(SSM_GROUPS):
        hg = h0_ref[0, g * gw:(g + 1) * gw, :]
        yraw = _dot_nt(c8[:, g * SSM_STATE:(g + 1) * SSM_STATE], hg.astype(BF16))
        for i in range(lt):
            yr_ref[i, pl.ds(b, 1), g * gw:(g + 1) * gw] = yraw[i:i + 1, :]
        tr = xaug[:, g * gw:(g + 1) * gw].T
        s = _dot(tr.astype(BF16), b128[:, g * SSM_STATE:(g + 1) * SSM_STATE])
        new = hg * tr[:, lt:lt + 1] + s
        if fill_slots is None:
            hn_ref[0, g * gw:(g + 1) * gw, :] = new
        else:
            hn_ref[slot, 0, g * gw:(g + 1) * gw, :] = new
    if fill_slots is not None:
        for other in fill_slots:
            hn_ref[other] = jnp.zeros(hn_ref.shape[1:], hn_ref.dtype)


def _ssd_s_post_kernel(*refs, lt):
    z_refs = refs[:lt]
    yd_ref, ec_ref, yr_ref, ng_ref, o_ref = refs[lt:]
    nb = z_refs[0].shape[0]
    for i in range(lt):
        y = yd_ref[i] + ec_ref[i] * yr_ref[i]
        gated = y * _silu(z_refs[i][...])
        out = gated * lax.rsqrt(jnp.mean(gated * gated, axis=-1, keepdims=True) + LN_EPS) * ng_ref[...]
        o_ref[i * nb:(i + 1) * nb, :] = out.astype(o_ref.dtype)


def ssd_sample(proj, tp, nb, lt, conv_state, state_all, new_states, layer, conv_w, conv_b, dt_bias,
               a_log, d_skip, norm_g):
    row0 = tp // nb
    cs = jnp.transpose(conv_state, (1, 0, 2))
    one = lambda i: (0, 0)
    one3 = lambda i: (0, 0, 0)

    def rows(t, width, col):
        return pl.BlockSpec((nb, width), lambda i: (row0 + t, col // width))

    in_specs = ([rows(t, SSM_INNER, COL_XS) for t in range(lt)]
                + [rows(t, 2 * SSM_BC, COL_BC) for t in range(lt)]
                + [pl.BlockSpec((nb, LANES), lambda i, t=t: (row0 + t, COL_DT // LANES)) for t in range(lt)]
                + [pl.BlockSpec((CONV_K - 1, nb, SSM_INNER), one3),
                   pl.BlockSpec((CONV_K - 1, nb, 2 * SSM_BC), one3),
                   pl.BlockSpec((CONV_K, SSM_INNER), one),
                   pl.BlockSpec((CONV_K, 2 * SSM_BC), one),
                   pl.BlockSpec((1, SSM_INNER), one),
                   pl.BlockSpec((1, 2 * SSM_BC), one),
                   pl.BlockSpec((1, SSM_INNER), one),
                   pl.BlockSpec((1, SSM_INNER), one),
                   pl.BlockSpec((1, SSM_INNER), one),
                   pl.BlockSpec((LANES, SSM_INNER), one),
                   pl.BlockSpec((SSM_BC, SSM_INNER), one)])
    f3 = lambda w: jax.ShapeDtypeStruct((lt, nb, w), F32)
    c_a, b_a, xw_a, dec_a, yd_a, ec_a = pl.pallas_call(
        functools.partial(_ssd_s_pre_kernel, lt=lt),
        grid=(1,),
        in_specs=in_specs,
        out_specs=[pl.BlockSpec((lt, nb, SSM_BC), one3), pl.BlockSpec((lt, nb, SSM_BC), one3),
                   pl.BlockSpec((lt, nb, SSM_INNER), one3), pl.BlockSpec((nb, SSM_INNER), one),
                   pl.BlockSpec((lt, nb, SSM_INNER), one3), pl.BlockSpec((lt, nb, SSM_INNER), one3)],
        out_shape=[f3(SSM_BC), f3(SSM_BC), f3(SSM_INNER), jax.ShapeDtypeStruct((nb, SSM_INNER), F32),
                   f3(SSM_INNER), f3(SSM_INNER)],
        compiler_params=_cparams(1),
        name="ssd_sample_pre",
    )(*([proj] * (3 * lt)), cs[:, :, :SSM_INNER], cs[:, :, SSM_INNER:],
      conv_w[:, :SSM_INNER], conv_w[:, SSM_INNER:], conv_b[:SSM_INNER].reshape(1, -1),
      conv_b[SSM_INNER:].reshape(1, -1), _expand_heads(dt_bias), _expand_heads(a_log),
      _expand_heads(d_skip), _head_expand_matrix(), _group_expand_matrix())

    depth = state_all.shape[0]
    h0r = state_all.reshape(depth, nb, SSM_INNER, SSM_STATE)
    state_specs = [pl.BlockSpec((lt, nb, SSM_BC), one3), pl.BlockSpec((lt, nb, SSM_BC), one3),
                   pl.BlockSpec((lt, nb, SSM_INNER), one3), pl.BlockSpec((nb, SSM_INNER), one),
                   pl.BlockSpec((None, 1, SSM_INNER, SSM_STATE), lambda i: (layer, i, 0, 0))]
    hn_shape = jax.ShapeDtypeStruct((depth, nb, SSM_INNER, SSM_STATE), F32)
    if new_states is None:
        fill = tuple(s for s in range(depth) if s != layer)
        hn, yr = pl.pallas_call(
            functools.partial(_ssd_s_state_kernel, lt=lt, slot=layer, fill_slots=fill),
            grid=(nb,),
            in_specs=state_specs,
            out_specs=[pl.BlockSpec((depth, 1, SSM_INNER, SSM_STATE), lambda i: (0, i, 0, 0)),
                       pl.BlockSpec((lt, nb, SSM_INNER), one3)],
            out_shape=[hn_shape, f3(SSM_INNER)],
            compiler_params=_cparams(1),
            name="ssd_sample_state",
        )(c_a, b_a, xw_a, dec_a, h0r)
    else:
        hn, yr = pl.pallas_call(
            functools.partial(_ssd_s_state_kernel_inplace, lt=lt),
            grid=(nb,),
            in_specs=state_specs + [pl.BlockSpec(memory_space=pl.ANY)],
            out_specs=[pl.BlockSpec((None, 1, SSM_INNER, SSM_STATE), lambda i: (layer, i, 0, 0)),
                       pl.BlockSpec((lt, nb, SSM_INNER), one3)],
            out_shape=[hn_shape, f3(SSM_INNER)],
            input_output_aliases={5: 0},
            compiler_params=_cparams(1),
            name="ssd_sample_state",
        )(c_a, b_a, xw_a, dec_a, h0r, new_states)

    ssm = pl.pallas_call(
        functools.partial(_ssd_s_post_kernel, lt=lt),
        grid=(1,),
        in_specs=([rows(t, SSM_INNER, COL_Z) for t in range(lt)]
                  + [pl.BlockSpec((lt, nb, SSM_INNER), one3)] * 3 + [pl.BlockSpec((1, SSM_INNER), one)]),
        out_specs=pl.BlockSpec((lt * nb, SSM_INNER), one),
        out_shape=jax.ShapeDtypeStruct((lt * nb, SSM_INNER), BF16),
        compiler_params=_cparams(1),
        name="ssd_sample_post",
    )(*([proj] * lt), yd_a, ec_a, yr, norm_g.reshape(1, SSM_INNER))
    return ssm, hn


def _softmax_rows(s):
    p = jnp.exp(s - jnp.max(s, axis=-1, keepdims=True))
    return p / jnp.sum(p, axis=-1, keepdims=True)


def _xattn_kernel(q_ref, k_ref, v_ref, o_ref, *, nh, bseq, tq):
    scale = MEM_HEAD_DIM ** -0.5
    for b in range(bseq):
        rows = slice(b * tq, (b + 1) * tq)
        for h in range(nh):
            sl = slice(h * MEM_HEAD_DIM, (h + 1) * MEM_HEAD_DIM)
            s = _dot_nt(q_ref[rows, sl].astype(BF16), k_ref[b, :, h, :].astype(BF16)) * scale
            o = _dot(_softmax_rows(s).astype(BF16), v_ref[b, :, h, :].astype(BF16))
            o_ref[rows, sl] = o.astype(o_ref.dtype)


def cross_attention(q, k, v, *, layer, n_seq, seq, tq, bseq, name):
    w = q.shape[1]
    _, _, m, nh, dh = k.shape
    nq = seq // tq
    assert bseq == 1 or nq == 1
    kv_spec = pl.BlockSpec((None, bseq, m, nh, dh), lambda i, n: (layer, i, 0, 0, 0))
    return pl.pallas_call(
        functools.partial(_xattn_kernel, nh=nh, bseq=bseq, tq=tq),
        grid=(n_seq // bseq, nq),
        in_specs=[pl.BlockSpec((bseq * tq, w), lambda i, n: (i * nq + n, 0)), kv_spec, kv_spec],
        out_specs=pl.BlockSpec((bseq * tq, w), lambda i, n: (i * nq + n, 0)),
        out_shape=jax.ShapeDtypeStruct((n_seq * seq, w), F32),
        compiler_params=_cparams(2),
        name=name,
    )(q, k, v)


def _head_copies(srcs, bufs, sem, layer, step, slot, bseq, nh):
    cs = []
    for b in range(bseq):
        seq = step * bseq + b
        for h in range(nh):
            for src, buf in zip(srcs, bufs):
                cs.append(pltpu.make_async_copy(src.at[layer, seq, :, h, :], buf.at[slot, b, h], sem.at[slot]))
    return cs


def _prefetch_heads(srcs, bufs, sem, layer, bseq, nh):
    i = pl.program_id(0)
    n = pl.num_programs(0)
    slot = i % 2

    @pl.when(i == 0)
    def _():
        for c in _head_copies(srcs, bufs, sem, layer, 0, 0, bseq, nh):
            c.start()

    @pl.when(i + 1 < n)
    def _():
        for c in _head_copies(srcs, bufs, sem, layer, i + 1, 1 - slot, bseq, nh):
            c.start()

    for c in _head_copies(srcs, bufs, sem, layer, i, slot, bseq, nh):
        c.wait()
    return slot


def _xattn_cache_kernel(q_ref, k_hbm, v_hbm, o_ref, kbuf, vbuf, sem, *, layer, nh, bseq, tq):
    slot = _prefetch_heads((k_hbm, v_hbm), (kbuf, vbuf), sem, layer, bseq, nh)
    scale = MEM_HEAD_DIM ** -0.5
    for b in range(bseq):
        rows = slice(b * tq, (b + 1) * tq)
        for h in range(nh):
            sl = slice(h * MEM_HEAD_DIM, (h + 1) * MEM_HEAD_DIM)
            s = _dot_nt(q_ref[rows, sl].astype(BF16), kbuf[slot, b, h].astype(BF16)) * scale
            o = _dot(_softmax_rows(s).astype(BF16), vbuf[slot, b, h].astype(BF16))
            o_ref[rows, sl] = o.astype(o_ref.dtype)


def cross_attention_cached(q, k, v, *, layer, n_seq, tq, bseq, name):
    w = q.shape[1]
    _, _, m, nh, dh = k.shape
    return pl.pallas_call(
        functools.partial(_xattn_cache_kernel, layer=layer, nh=nh, bseq=bseq, tq=tq),
        grid=(n_seq // bseq,),
        in_specs=[pl.BlockSpec((bseq * tq, w), lambda i: (i, 0)),
                  pl.BlockSpec(memory_space=pl.ANY), pl.BlockSpec(memory_space=pl.ANY)],
        out_specs=pl.BlockSpec((bseq * tq, w), lambda i: (i, 0)),
        out_shape=jax.ShapeDtypeStruct((n_seq * tq, w), F32),
        scratch_shapes=[pltpu.VMEM((2, bseq, nh, m, dh), F32), pltpu.VMEM((2, bseq, nh, m, dh), F32),
                        pltpu.SemaphoreType.DMA((2,))],
        compiler_params=_cparams(1),
        name=name,
    )(q, k, v)


def _router_kernel(h_ref, w_ref, b_ref, o_ref):
    logits = _dot(h_ref[...].astype(BF16), w_ref[...].astype(BF16)) + b_ref[...]
    lane = lax.broadcasted_iota(jnp.int32, logits.shape, 1)
    lane_f = lane.astype(F32)
    big = float(LANES)
    is_g = lane < N_EGROUPS
    lg = jnp.where(is_g, logits, NEG_BIG)
    mg = jnp.max(lg, axis=-1, keepdims=True)
    zg = jnp.sum(jnp.where(is_g, jnp.exp(lg - mg), 0.0), axis=-1, keepdims=True)
    gi = jnp.min(jnp.where(is_g & (lg == mg), lane_f, big), axis=-1, keepdims=True)
    gw = 1.0 / zg
    lo = N_EGROUPS + gi * EXPERTS_PER_GROUP
    is_e = (lane_f >= lo) & (lane_f < lo + EXPERTS_PER_GROUP)
    le = jnp.where(is_e, logits, NEG_BIG)
    me = jnp.max(le, axis=-1, keepdims=True)
    ee = jnp.where(is_e, jnp.exp(le - me), 0.0)
    pe = ee / jnp.sum(ee, axis=-1, keepdims=True)
    pe = jnp.where(is_e, pe, -1.0)
    p1 = jnp.max(pe, axis=-1, keepdims=True)
    i1 = jnp.min(jnp.where(pe == p1, lane_f, big), axis=-1, keepdims=True)
    pe2 = jnp.where(lane_f == i1, -1.0, pe)
    p2 = jnp.max(pe2, axis=-1, keepdims=True)
    i2 = jnp.min(jnp.where(pe2 == p2, lane_f, big), axis=-1, keepdims=True)
    tot = p1 + p2
    out = jnp.where(lane == 0, i1 - N_EGROUPS,
                    jnp.where(lane == 1, i2 - N_EGROUPS,
                              jnp.where(lane == 2, gw * (p1 / tot),
                                        jnp.where(lane == 3, gw * (p2 / tot), 0.0))))
    o_ref[...] = out


def moe_router(h, w_rg, b_rg, w_re, b_re):
    t, d = h.shape
    tm = _pick(t, (256, 128, 64, 32, 16, 8))
    npad = LANES - N_EGROUPS - N_EXPERTS
    w = jnp.concatenate([w_rg, w_re, jnp.zeros((d, npad), F32)], axis=1)
    b = jnp.concatenate([b_rg, b_re, jnp.zeros((npad,), F32)]).reshape(1, LANES)
    return pl.pallas_call(
        _router_kernel,
        grid=(t // tm,),
        in_specs=[pl.BlockSpec((tm, d), lambda i: (i, 0)),
                  pl.BlockSpec((d, LANES), lambda i: (0, 0)),
                  pl.BlockSpec((1, LANES), lambda i: (0, 0))],
        out_specs=pl.BlockSpec((tm, LANES), lambda i: (i, 0)),
        out_shape=jax.ShapeDtypeStruct((t, LANES), F32),
        compiler_params=_cparams(1),
        name="moe_router",
    )(h, w, b)


def _row_copy(src_hbm, dst, src_row, dst_row, sem):
    return pltpu.make_async_copy(src_hbm.at[pl.ds(src_row, 1)], dst.at[pl.ds(dst_row, 1)], sem)


def _moe_gather_kernel(tok_ref, h_ref, o_ref, buf, sem, *, tm):
    def start(i, carry):
        for p in range(2):
            r = 2 * i + p
            _row_copy(h_ref, buf, tok_ref[0, 0, r], r, sem).start(priority=p)
        return carry

    def wait(r, carry):
        _row_copy(h_ref, buf, tok_ref[0, 0, r], r, sem).wait()
        return carry

    lax.fori_loop(0, tm // 2, start, 0, unroll=DMA_LOOP_UNROLL // 2)
    lax.fori_loop(0, tm, wait, 0, unroll=DMA_LOOP_UNROLL)
    o_ref[...] = buf[...].astype(o_ref.dtype)


def moe_gather(h, row_token, tm):
    r_total = row_token.shape[0]
    d = h.shape[1]
    nblk = r_total // tm
    return pl.pallas_call(
        functools.partial(_moe_gather_kernel, tm=tm),
        grid=(nblk,),
        in_specs=[pl.BlockSpec((1, 1, tm), lambda i: (i, 0, 0), memory_space=pltpu.SMEM),
                  pl.BlockSpec(memory_space=pl.ANY)],
        out_specs=pl.BlockSpec((tm, d), lambda i: (i, 0)),
        out_shape=jax.ShapeDtypeStruct((r_total, d), BF16),
        scratch_shapes=[pltpu.VMEM((tm, d), h.dtype), pltpu.SemaphoreType.DMA(())],
        compiler_params=_cparams(1),
        name="moe_gather",
    )(row_token.reshape(nblk, 1, tm), h)


def _expert_weight_copies(w_refs, bufs, sems, layer, expert, slot):
    return [pltpu.make_async_copy(w.at[layer, expert], buf.at[slot], sems.at[k, slot])
            for k, (w, buf) in enumerate(zip(w_refs, bufs))]


def _expert_weights_step(s, tv_ref, te_ref, tn_ref, ts_ref, w_refs, bufs, w16s, sems, layer):
    @pl.when(s == 0)
    def _():
        for c in _expert_weight_copies(w_refs, bufs, sems, layer, te_ref[0], 0):
            c.start()

    @pl.when(tv_ref[s] == 2)
    def _():
        slot = ts_ref[s]
        for c in _expert_weight_copies(w_refs, bufs, sems, layer, te_ref[s], slot):
            c.wait()

        @pl.when(tn_ref[s] >= 0)
        def _():
            for c in _expert_weight_copies(w_refs, bufs, sems, layer, tn_ref[s], 1 - slot):
                c.start()

        for buf, w16 in zip(bufs, w16s):
            w16[...] = buf[slot].astype(BF16)


def _moe_up_kernel(tv_ref, tc_ref, te_ref, tn_ref, ts_ref, x_ref, wg_ref, wu_ref, o_ref,
                   gbuf, ubuf, wg16, wu16, sems, *, layer):
    s = pl.program_id(0)
    _expert_weights_step(s, tv_ref, te_ref, tn_ref, ts_ref, (wg_ref, wu_ref), (gbuf, ubuf), (wg16, wu16),
                         sems, layer)

    @pl.when(tv_ref[s] > 0)
    def _():
        x = x_ref[...]
        a = _dot(x, wg16[...])
        u = _dot(x, wu16[...])
        o_ref[...] = (_silu(a) * u).astype(o_ref.dtype)

    @pl.when(tv_ref[s] == 0)
    def _():
        o_ref[...] = jnp.zeros_like(o_ref)


def _moe_tables(plan):
    return plan["tile_v"], plan["tile_c"], plan["tile_e"], plan["tile_next"], plan["tile_slot"]


def moe_up(x_sorted, w_gate, w_up, layer, plan, tm):
    r_total, d = x_sorted.shape
    ff = w_gate.shape[-1]
    n_tiles = r_total // tm
    grid_spec = pltpu.PrefetchScalarGridSpec(
        num_scalar_prefetch=5,
        grid=(n_tiles,),
        in_specs=[
            pl.BlockSpec((tm, d), lambda s, tv, tc, te, tn, ts: (tc[s], 0)),
            pl.BlockSpec(memory_space=pl.ANY),
            pl.BlockSpec(memory_space=pl.ANY),
        ],
        out_specs=pl.BlockSpec((tm, ff), lambda s, tv, tc, te, tn, ts: (s, 0)),
        scratch_shapes=[pltpu.VMEM((2, d, ff), F32), pltpu.VMEM((2, d, ff), F32),
                        pltpu.VMEM((d, ff), BF16), pltpu.VMEM((d, ff), BF16),
                        pltpu.SemaphoreType.DMA((2, 2))],
    )
    return pl.pallas_call(
        functools.partial(_moe_up_kernel, layer=layer),
        grid_spec=grid_spec,
        out_shape=jax.ShapeDtypeStruct((r_total, ff), BF16),
        compiler_params=_cparams(1),
        name="moe_up",
    )(*_moe_tables(plan), x_sorted, w_gate, w_up)


def _moe_down_kernel(tv_ref, tc_ref, te_ref, tn_ref, ts_ref, x_ref, w_ref, o_ref, wbuf, w16, sems, *, layer):
    s = pl.program_id(0)
    _expert_weights_step(s, tv_ref, te_ref, tn_ref, ts_ref, (w_ref,), (wbuf,), (w16,), sems, layer)

    @pl.when(tv_ref[s] > 0)
    def _():
        o_ref[...] = _dot(x_ref[...], w16[...])

    @pl.when(tv_ref[s] == 0)
    def _():
        o_ref[...] = jnp.zeros_like(o_ref)


def moe_down(hid, w_down, layer, plan, tm):
    r_total, ff = hid.shape
    d = w_down.shape[-1]
    n_tiles = r_total // tm
    grid_spec = pltpu.PrefetchScalarGridSpec(
        num_scalar_prefetch=5,
        grid=(n_tiles,),
        in_specs=[
            pl.BlockSpec((tm, ff), lambda s, tv, tc, te, tn, ts: (tc[s], 0)),
            pl.BlockSpec(memory_space=pl.ANY),
        ],
        out_specs=pl.BlockSpec((tm, d), lambda s, tv, tc, te, tn, ts: (s, 0)),
        scratch_shapes=[pltpu.VMEM((2, ff, d), F32), pltpu.VMEM((ff, d), BF16),
                        pltpu.SemaphoreType.DMA((1, 2))],
    )
    return pl.pallas_call(
        functools.partial(_moe_down_kernel, layer=layer),
        grid_spec=grid_spec,
        out_shape=jax.ShapeDtypeStruct((r_total, d), F32),
        compiler_params=_cparams(1),
        name="moe_down",
    )(*_moe_tables(plan), hid, w_down)


def _moe_combine_kernel(pos_ref, y_ref, r_ref, h_ref, g_ref, b_ref, o1_ref, o2_ref, ybuf, sem, *,
                        tm, alpha, n_first):
    def start(i, carry):
        _row_copy(y_ref, ybuf.at[0], pos_ref[0, 0, 2 * i], i, sem).start(priority=0)
        _row_copy(y_ref, ybuf.at[1], pos_ref[0, 0, 2 * i + 1], i, sem).start(priority=1)
        return carry

    def wait(i, carry):
        _row_copy(y_ref, ybuf.at[0], pos_ref[0, 0, 2 * i], i, sem).wait()
        _row_copy(y_ref, ybuf.at[1], pos_ref[0, 0, 2 * i + 1], i, sem).wait()
        return carry

    lax.fori_loop(0, tm, start, 0, unroll=DMA_LOOP_UNROLL)
    lax.fori_loop(0, tm, wait, 0, unroll=DMA_LOOP_UNROLL)
    route = r_ref[...]
    ff = ybuf[0] * route[:, 2:3] + ybuf[1] * route[:, 3:4]
    h = _ln_rows(alpha * h_ref[...] + ff, g_ref[...], b_ref[...])
    if n_first is None:
        o1_ref[...] = h
        o2_ref[...] = h.astype(BF16)
    else:
        i = pl.program_id(0)

        @pl.when(i < n_first)
        def _():
            o1_ref[...] = h

        @pl.when(i >= n_first)
        def _():
            o2_ref[...] = h


def moe_combine(y_sorted, pos, route, h, g, b, *, alpha, split_rows=None):
    t, d = h.shape
    tm = _pick(t if split_rows is None else math.gcd(split_rows, t - split_rows), (256, 128, 64, 32, 16, 8))
    nblk = t // tm
    if split_rows is None:
        n_first = None
        out_specs = [pl.BlockSpec((tm, d), lambda i: (i, 0)), pl.BlockSpec((tm, d), lambda i: (i, 0))]
        out_shape = [jax.ShapeDtypeStruct((t, d), F32), jax.ShapeDtypeStruct((t, d), BF16)]
    else:
        n_first = split_rows // tm
        out_specs = [pl.BlockSpec((tm, d), lambda i: (jnp.minimum(i, n_first - 1), 0)),
                     pl.BlockSpec((tm, d), lambda i: (jnp.maximum(i - n_first, 0), 0))]
        out_shape = [jax.ShapeDtypeStruct((split_rows, d), F32),
                     jax.ShapeDtypeStruct((t - split_rows, d), F32)]
    return pl.pallas_call(
        functools.partial(_moe_combine_kernel, tm=tm, alpha=alpha, n_first=n_first),
        grid=(nblk,),
        in_specs=[pl.BlockSpec((1, 1, 2 * tm), lambda i: (i, 0, 0), memory_space=pltpu.SMEM),
                  pl.BlockSpec(memory_space=pl.ANY),
                  pl.BlockSpec((tm, LANES), lambda i: (i, 0)),
                  pl.BlockSpec((tm, d), lambda i: (i, 0)),
                  pl.BlockSpec((1, d), lambda i: (0, 0)),
                  pl.BlockSpec((1, d), lambda i: (0, 0))],
        out_specs=out_specs,
        out_shape=out_shape,
        scratch_shapes=[pltpu.VMEM((2, tm, d), F32), pltpu.SemaphoreType.DMA(())],
        compiler_params=_cparams(1),
        name="moe_combine",
    )(pos.reshape(nblk, 1, 2 * tm), y_sorted, route, h, g.reshape(1, d), b.reshape(1, d))


def moe_plan(route, tm):
    t = route.shape[0]
    eid = route[:, :2].astype(jnp.int32).reshape(-1)
    onehot = (eid[:, None] == jnp.arange(N_EXPERTS, dtype=jnp.int32)[None, :]).astype(jnp.int32)
    csum = jnp.cumsum(onehot, axis=0)
    rank = jnp.sum((csum - onehot) * onehot, axis=1)
    counts = csum[-1]
    tiles_e = (counts + tm - 1) // tm
    tile_end = jnp.cumsum(tiles_e)
    tile_start = tile_end - tiles_e
    n_used = tile_end[-1]
    n_tiles = (2 * t + N_EXPERTS * (tm - 1)) // tm + 1
    r_total = n_tiles * tm
    dest = tile_start[eid] * tm + rank
    row_token = (jnp.arange(r_total, dtype=jnp.int32) % t).at[dest].set(
        jnp.arange(2 * t, dtype=jnp.int32) // 2)
    tile_ids = jnp.arange(n_tiles, dtype=jnp.int32)
    tile_clamped = jnp.minimum(tile_ids, n_used - 1)
    tile_e = jnp.sum(tile_end[None, :] <= tile_clamped[:, None], axis=1).astype(jnp.int32)
    tile_first = tile_ids == tile_start[tile_e]
    tile_v = jnp.where(tile_ids < n_used, 1 + tile_first.astype(jnp.int32), 0).astype(jnp.int32)
    e_ids = jnp.arange(N_EXPERTS, dtype=jnp.int32)
    nonempty = tiles_e > 0
    cand = jnp.where(nonempty[None, :] & (e_ids[None, :] > e_ids[:, None]), e_ids[None, :], N_EXPERTS)
    next_e = jnp.min(cand, axis=1)
    next_e = jnp.where(next_e >= N_EXPERTS, -1, next_e).astype(jnp.int32)
    slot_e = ((jnp.cumsum(nonempty.astype(jnp.int32)) - 1) % 2).astype(jnp.int32)
    return dict(row_token=row_token, pos=dest.astype(jnp.int32), tile_v=tile_v, tile_c=tile_clamped,
                tile_e=tile_e, tile_next=next_e[tile_e], tile_slot=slot_e[tile_e])


def hierarchical_moe_ln(hf, layer, w_rg, b_rg, w_re, b_re, w_gate, w_up, w_down, ln_g, ln_b, *, alpha,
                        split_rows=None):
    route = moe_router(hf, w_rg, b_rg, w_re, b_re)
    plan = moe_plan(route, MOE_TM)
    x_sorted = moe_gather(hf, plan["row_token"], MOE_TM)
    hid = moe_up(x_sorted, w_gate, w_up, layer, plan, MOE_TM)
    y_sorted = moe_down(hid, w_down, layer, plan, MOE_TM)
    return moe_combine(y_sorted, plan["pos"], route, hf, ln_g, ln_b, alpha=alpha, split_rows=split_rows)


def _to_seq_major(x_tm, lt, nb, pad_to):
    w = x_tm.shape[1]
    x = jnp.transpose(x_tm.reshape(lt, nb, w), (1, 0, 2))
    x = jnp.pad(x, ((0, 0), (0, pad_to - lt), (0, 0)))
    return x.reshape(nb * pad_to, w)


def _to_time_major(x_sm, lt, nb, pad_to):
    w = x_sm.shape[1]
    x = x_sm.reshape(nb, pad_to, w)[:, :lt]
    return jnp.transpose(x, (1, 0, 2)).reshape(lt * nb, w)


def kernel(x_prompt, x_sample, mem_prompt, cache_swa_k, cache_swa_v, cache_mem_k, cache_mem_v, state_conv, state_ssm, ln_in_g, ln_in_b, w_in, attn_sinks, gm_ln_g, gm_ln_b, gm_ws, gm_bs, conv_w, conv_b, dt_bias, a_log, d_skip, ssm_norm_g, w_pa, w_pb, w_pc, w_o, ln1_g, ln1_b, w_cq, w_ck, w_cv, w_co, ln2_g, ln2_b, w_rg, b_rg, w_re, b_re, w_gate, w_up, w_down, ln3_g, ln3_b):
    bp, seq, d = x_prompt.shape
    nb, lt, _ = x_sample.shape
    depth = w_in.shape[0]
    mem_len = mem_prompt.shape[1]
    past_len = PAST_LEN
    wb = cache_swa_k.shape[2]
    assert wb == WINDOW and seq % CHUNK == 0 and lt <= SUBLANES
    tp, ts = bp * seq, nb * lt
    alpha = (2 * depth) ** 0.25
    qpad = SUBLANES

    xp = x_prompt.reshape(tp, d)
    xs = jnp.transpose(x_sample, (1, 0, 2)).reshape(ts, d)
    hf, hb = ln_in(xp, xs, ln_in_g, ln_in_b)
    cos_t, sin_t = rope_tables(tp, seq, ts, nb, past_len)
    mem_b = mem_prompt.reshape(bp * mem_len, d).astype(BF16)

    in_w = w_in.shape[2]
    assert in_w == Q_W + 2 * KV_W + 2 * GM_W + SSM_INNER + CONV_DIM + SSM_HEADS + 3 * D_MODEL
    w_pad = jnp.concatenate(
        [w_in.astype(BF16), jnp.zeros((depth, d, PROJ_W - in_w), BF16)], axis=2)

    outs = {k: [] for k in ("p_k", "p_v", "p_mk", "p_mv", "p_conv", "p_ssm", "p_gv",
                            "s_k", "s_v", "s_conv", "s_ssm", "s_gv")}
    n_qblk = seq // WINDOW
    s_states = None
    for l in range(depth):
        proj = in_projection(hb, w_pad, l)
        q_rot, k_rot = rope_qk(proj, cos_t, sin_t)

        kcol, vcol = 0, COL_V // KV_W
        att_p = swa_attention(
            attn_sinks[l], q_rot, k_rot, k_rot, proj, proj,
            n_seq=bp, n_blk=n_qblk, qb=WINDOW, prev_from_block0=False,
            kp_map=lambda i, n: (jnp.maximum(i * n_qblk + n - 1, 0), kcol),
            kc_map=lambda i, n: (i * n_qblk + n, kcol),
            vp_map=lambda i, n: (jnp.maximum(i * n_qblk + n - 1, 0), vcol),
            vc_map=lambda i, n: (i * n_qblk + n, vcol),
            out_dtype=BF16, name="swa_prompt")
        k_s_tm = k_rot[tp:]
        v_s_tm = proj[tp:, COL_V:COL_V + KV_W]
        q_s = _to_seq_major(q_rot[tp:], lt, nb, qpad)
        k_s = _to_seq_major(k_s_tm, lt, nb, qpad)
        v_s = _to_seq_major(v_s_tm, lt, nb, qpad)
        att_s8 = swa_attention_cached(attn_sinks[l], q_s, k_s, v_s, cache_swa_k, cache_swa_v, layer=l,
                                      n_seq=nb, qb=qpad, bseq=_pick(nb, (8, 4, 2, 1)), name="swa_sample")
        att_s = _to_time_major(att_s8, lt, nb, qpad).astype(BF16)
        last_w = lambda a, c0: jnp.stack(
            [a[(i + 1) * seq - WINDOW:(i + 1) * seq, c0:c0 + KV_W] for i in range(bp)]
        ).reshape(bp, WINDOW, N_KV_HEADS, HEAD_DIM)
        outs["p_k"].append(last_w(k_rot, 0))
        outs["p_v"].append(last_w(proj, COL_V))
        k_new = jnp.transpose(k_s_tm.reshape(lt, nb, N_KV_HEADS, HEAD_DIM), (1, 0, 2, 3))
        v_new = jnp.transpose(v_s_tm.reshape(lt, nb, N_KV_HEADS, HEAD_DIM), (1, 0, 2, 3))
        outs["s_k"].append(jnp.concatenate([cache_swa_k[l], k_new], axis=1)[:, -wb:])
        outs["s_v"].append(jnp.concatenate([cache_swa_v[l], v_new], axis=1)[:, -wb:])

        gm_p, vg_last = gmlp_prompt(proj, bp, seq, gm_ws[l], gm_bs[l], gm_ln_g[l], gm_ln_b[l])
        gm_s, vg_s = gmlp_sample(proj, tp, nb, lt, gm_ws[l], gm_bs[l], gm_ln_g[l], gm_ln_b[l])
        outs["p_gv"].append(vg_last.reshape(bp, CHUNK, GM_GROUPS, GM_GROUP_DIM))
        outs["s_gv"].append(jnp.transpose(vg_s.reshape(lt, nb, GM_GROUPS, GM_GROUP_DIM), (1, 0, 2, 3)))

        dtt = jnp.transpose(proj[:tp, COL_DT:COL_DT + SSM_HEADS])
        y_p, st_p = ssd_prompt(proj, dtt, bp, seq, conv_w[l], conv_b[l], dt_bias[l], a_log[l],
                               d_skip[l], ssm_norm_g[l])
        ssm_s, s_states = ssd_sample(proj, tp, nb, lt, state_conv[l], state_ssm, s_states, l, conv_w[l],
                                     conv_b[l], dt_bias[l], a_log[l], d_skip[l], ssm_norm_g[l])
        outs["p_conv"].append(jnp.stack(
            [proj[(i + 1) * seq - (CONV_K - 1):(i + 1) * seq, COL_XS:COL_XS + CONV_DIM] for i in range(bp)]))
        xbc_s = jnp.transpose(proj[tp:, COL_XS:COL_XS + CONV_DIM].reshape(lt, nb, CONV_DIM), (1, 0, 2))
        outs["s_conv"].append(jnp.concatenate([state_conv[l], xbc_s], axis=1)[:, -(CONV_K - 1):])
        outs["p_ssm"].append(st_p.reshape(bp, SSM_HEADS, SSM_HEAD_DIM, SSM_STATE))

        merged = gated_merge(att_p, att_s, gm_p, gm_s, y_p, ssm_s, w_pa[l].astype(BF16),
                             w_pb[l].astype(BF16), w_pc[l].astype(BF16), proj)
        h1f, h1b = matmul_ln(merged, w_o[l].astype(BF16), hf, ln1_g[l], ln1_b[l], alpha=alpha, name="out_proj_ln1")

        qc = matmul(h1b, w_cq[l].astype(BF16), name="xattn_q")
        pmk = matmul(mem_b, w_ck[l].astype(BF16), name="mem_k")
        pmv = matmul(mem_b, w_cv[l].astype(BF16), name="mem_v")
        pmk5 = pmk.reshape(1, bp, mem_len, MEM_HEADS, MEM_HEAD_DIM)
        pmv5 = pmv.reshape(1, bp, mem_len, MEM_HEADS, MEM_HEAD_DIM)
        outs["p_mk"].append(pmk5[0])
        outs["p_mv"].append(pmv5[0])
        tq = _pick(seq, (512, 256, 128))
        o_p = cross_attention(qc, pmk5, pmv5, layer=0, n_seq=bp, seq=seq, tq=tq, bseq=1,
                              name="xattn_prompt")
        qc_s = _to_seq_major(qc[tp:], lt, nb, qpad)
        o_s8 = cross_attention_cached(qc_s, cache_mem_k, cache_mem_v, layer=l, n_seq=nb, tq=qpad,
                                      bseq=_pick(nb, (8, 4, 2, 1)), name="xattn_sample")
        o_all = jnp.concatenate([o_p, _to_time_major(o_s8, lt, nb, qpad)], axis=0).astype(BF16)
        h2f, h2b = matmul_ln(o_all, w_co[l].astype(BF16), h1f, ln2_g[l], ln2_b[l], alpha=alpha, name="xattn_out_ln2")

        hf, hb = hierarchical_moe_ln(h2f, l, w_rg[l], b_rg[l], w_re[l], b_re[l], w_gate, w_up, w_down,
                                     ln3_g[l], ln3_b[l], alpha=alpha,
                                     split_rows=tp if l == depth - 1 else None)

    y_prompt = hf.reshape(bp, seq, d)
    y_sample = jnp.transpose(hb.reshape(lt, nb, d), (1, 0, 2))
    st = lambda k: jnp.stack(outs[k])
    s_ssm = s_states.reshape(depth, nb, SSM_HEADS, SSM_HEAD_DIM, SSM_STATE)
    return (y_prompt, y_sample, st("p_k"), st("p_v"), st("p_mk"), st("p_mv"), st("p_conv"), st("p_ssm"),
            st("p_gv"), st("s_k"), st("s_v"), st("s_conv"), s_ssm, st("s_gv"))
```

```python
import functools
import math

import numpy as np
import jax
import jax.numpy as jnp
from jax import lax
from jax.experimental import pallas as pl
from jax.experimental.pallas import tpu as pltpu

F32 = jnp.float32
BF16 = jnp.bfloat16

D_MODEL = 2048
N_HEADS = 32
N_KV_HEADS = 4
HEAD_DIM = 64
WINDOW = 128
PAST_LEN = 8192
ROPE_THETA = 10000.0
CHUNK = 128
GM_GROUPS = 16
GM_GROUP_DIM = 128
SSM_HEADS = 32
SSM_HEAD_DIM = 64
SSM_GROUPS = 4
SSM_STATE = 128
CONV_K = 4
MEM_HEADS = 4
MEM_HEAD_DIM = 128
N_EGROUPS = 4
EXPERTS_PER_GROUP = 8
N_EXPERTS = N_EGROUPS * EXPERTS_PER_GROUP
EXPERT_FF = D_MODEL // 2
Q_W = N_HEADS * HEAD_DIM
KV_W = N_KV_HEADS * HEAD_DIM
GM_W = GM_GROUPS * GM_GROUP_DIM
SSM_INNER = SSM_HEADS * SSM_HEAD_DIM
SSM_BC = SSM_GROUPS * SSM_STATE
CONV_DIM = SSM_INNER + 2 * SSM_BC
MEM_W = MEM_HEADS * MEM_HEAD_DIM
LN_EPS = 1e-5
NEG_BIG = -1e30

VMEM_LIMIT_BYTES = 52 * 1024 * 1024
LANES = 128
SUBLANES = 8

IN_TN = 512
COL_Q = 0
COL_GU = 2048
COL_GV = 4096
COL_Z = 6144
COL_XS = 8192
COL_BC = 10240
COL_DT = 11264
COL_GATES = COL_DT + SSM_HEADS
COL_K = 17920
COL_V = 18176
PROJ_W = 18432

MOE_TM = 256
DMA_LOOP_UNROLL = 8


def _cparams(n_grid):
    return pltpu.CompilerParams(
        dimension_semantics=("arbitrary",) * n_grid,
        vmem_limit_bytes=VMEM_LIMIT_BYTES,
    )


def _pick(n, prefs):
    for p in prefs:
        if n % p == 0:
            return p
    raise ValueError(f"no tile for {n} in {prefs}")


def _ln_rows(x, g, b):
    mu = jnp.mean(x, axis=-1, keepdims=True)
    xc = x - mu
    var = jnp.mean(xc * xc, axis=-1, keepdims=True)
    return xc * lax.rsqrt(var + LN_EPS) * g + b


def _sigmoid(x):
    return 1.0 / (1.0 + jnp.exp(-x))


def _silu(x):
    return x * _sigmoid(x)


def _softplus(x):
    return jnp.maximum(x, 0.0) + jnp.log1p(jnp.exp(-jnp.abs(x)))


def _gelu(x):
    return jax.nn.gelu(x, approximate=True)


def _split3(x):
    hi = x.astype(BF16)
    r1 = x - hi.astype(F32)
    mid = r1.astype(BF16)
    lo = (r1 - mid.astype(F32)).astype(BF16)
    return hi, mid, lo


def _dot(a, b):
    return jnp.dot(a, b, preferred_element_type=F32)


def _dot_nt(a, b):
    return lax.dot_general(a, b, (((1,), (1,)), ((), ())), preferred_element_type=F32)


def _exact_dot_left(pieces, m):
    acc = _dot(pieces[0], m)
    for p in pieces[1:]:
        acc = acc + _dot(p, m)
    return acc


def _ln_in_kernel(xp_ref, xs_ref, g_ref, b_ref, of_ref, ob_ref, *, n_p):
    i = pl.program_id(0)

    @pl.when(i < n_p)
    def _():
        y = _ln_rows(xp_ref[...], g_ref[...], b_ref[...])
        of_ref[...] = y
        ob_ref[...] = y.astype(BF16)

    @pl.when(i >= n_p)
    def _():
        y = _ln_rows(xs_ref[...], g_ref[...], b_ref[...])
        of_ref[...] = y
        ob_ref[...] = y.astype(BF16)


def ln_in(xp, xs, g, b):
    tp, d = xp.shape
    ts = xs.shape[0]
    tm = _pick(math.gcd(tp, ts), (256, 128, 64, 32, 16, 8))
    n_p, n_s = tp // tm, ts // tm
    t = tp + ts
    return pl.pallas_call(
        functools.partial(_ln_in_kernel, n_p=n_p),
        grid=(n_p + n_s,),
        in_specs=[
            pl.BlockSpec((tm, d), lambda i: (jnp.minimum(i, n_p - 1), 0)),
            pl.BlockSpec((tm, d), lambda i: (jnp.maximum(i - n_p, 0), 0)),
            pl.BlockSpec((1, d), lambda i: (0, 0)),
            pl.BlockSpec((1, d), lambda i: (0, 0)),
        ],
        out_specs=[
            pl.BlockSpec((tm, d), lambda i: (i, 0)),
            pl.BlockSpec((tm, d), lambda i: (i, 0)),
        ],
        out_shape=[jax.ShapeDtypeStruct((t, d), F32), jax.ShapeDtypeStruct((t, d), BF16)],
        compiler_params=_cparams(1),
        name="ln_in",
    )(xp, xs, g.reshape(1, d), b.reshape(1, d))


def _mm_kernel(x_ref, w_ref, o_ref):
    o_ref[...] = _dot(x_ref[...], w_ref[...]).astype(o_ref.dtype)


def matmul(x, w, *, out_dtype=F32, tm_prefs=(1088, 1024, 512, 256, 128, 64, 32, 16, 8),
           tn_prefs=(1280, 1024, 512, 256, 128), name="mm"):
    t, k = x.shape
    n = w.shape[1]
    tm = _pick(t, tm_prefs)
    tn = _pick(n, tn_prefs)
    return pl.pallas_call(
        _mm_kernel,
        grid=(n // tn, t // tm),
        in_specs=[
            pl.BlockSpec((tm, k), lambda j, i: (i, 0)),
            pl.BlockSpec((k, tn), lambda j, i: (0, j)),
        ],
        out_specs=pl.BlockSpec((tm, tn), lambda j, i: (i, j)),
        out_shape=jax.ShapeDtypeStruct((t, n), out_dtype),
        compiler_params=_cparams(2),
        name=name,
    )(x, w)


def _in_proj_kernel(x_ref, w_ref, wt_ref, o_ref, *, n_main):
    j = pl.program_id(1)

    @pl.when(j < n_main)
    def _():
        o_ref[...] = _dot(x_ref[...], w_ref[...].astype(BF16))

    @pl.when(j >= n_main)
    def _():
        o_ref[...] = _dot(x_ref[...], wt_ref[...].astype(BF16))


def in_projection(x, w_in, w_tail, layer):
    t, k = x.shape
    n_blk = PROJ_W // IN_TN
    n_main = w_in.shape[2] // IN_TN
    kv_blk = Q_W // IN_TN
    assert 2 * KV_W == IN_TN and COL_K == (n_blk - 1) * IN_TN and n_main == n_blk - 1
    tm = _pick(t, (2176, 1088, 544, 32, 16, 8))

    def out_map(i, j):
        return i, jnp.where(j < kv_blk, j, jnp.where(j == kv_blk, n_blk - 1, j - 1))

    return pl.pallas_call(
        functools.partial(_in_proj_kernel, n_main=n_main),
        grid=(t // tm, n_blk),
        in_specs=[
            pl.BlockSpec((tm, k), lambda i, j: (i, 0)),
            pl.BlockSpec((None, k, IN_TN), lambda i, j: (layer, 0, jnp.minimum(j, n_main - 1))),
            pl.BlockSpec((None, k, IN_TN), lambda i, j: (layer, 0, 0)),
        ],
        out_specs=pl.BlockSpec((tm, IN_TN), out_map),
        out_shape=jax.ShapeDtypeStruct((t, PROJ_W), F32),
        compiler_params=_cparams(2),
        name="in_proj",
    )(x, w_in, w_tail)


def _mm_ln_kernel(x_ref, w_ref, r_ref, g_ref, b_ref, of_ref, ob_ref, *, alpha):
    y = _dot(x_ref[...], w_ref[...])
    h = _ln_rows(alpha * r_ref[...] + y, g_ref[...], b_ref[...])
    of_ref[...] = h
    ob_ref[...] = h.astype(BF16)


def matmul_ln(x, w, res, g, b, *, alpha, name="mm_ln"):
    t, k = x.shape
    d = w.shape[1]
    tm = _pick(t, (256, 128, 64, 32, 16, 8))
    return pl.pallas_call(
        functools.partial(_mm_ln_kernel, alpha=alpha),
        grid=(t // tm,),
        in_specs=[
            pl.BlockSpec((tm, k), lambda i: (i, 0)),
            pl.BlockSpec((k, d), lambda i: (0, 0)),
            pl.BlockSpec((tm, d), lambda i: (i, 0)),
            pl.BlockSpec((1, d), lambda i: (0, 0)),
            pl.BlockSpec((1, d), lambda i: (0, 0)),
        ],
        out_specs=[
            pl.BlockSpec((tm, d), lambda i: (i, 0)),
            pl.BlockSpec((tm, d), lambda i: (i, 0)),
        ],
        out_shape=[jax.ShapeDtypeStruct((t, d), F32), jax.ShapeDtypeStruct((t, d), BF16)],
        compiler_params=_cparams(1),
        name=name,
    )(x, w, res, g.reshape(1, d), b.reshape(1, d))


def _merge_kernel(ap_ref, as_ref, bp_ref, bs_ref, cp_ref, cs_ref, wa_ref, wb_ref, wc_ref,
                  ga_ref, ga2_ref, gb_ref, gb2_ref, gc_ref, gc2_ref, o_ref, *, n_p, shift):
    i = pl.program_id(1)
    is_p = i < n_p
    tn = o_ref.shape[1]

    def gate(main_ref, tail_ref):
        win = jnp.concatenate([main_ref[...], tail_ref[...]], axis=1)
        return _sigmoid(win[:, shift:shift + tn])

    xa = jnp.where(is_p, ap_ref[...], as_ref[...])
    xb = jnp.where(is_p, bp_ref[...], bs_ref[...])
    xc = jnp.where(is_p, cp_ref[...], cs_ref[...])
    acc = gate(ga_ref, ga2_ref) * _dot(xa, wa_ref[...])
    acc = acc + gate(gb_ref, gb2_ref) * _dot(xb, wb_ref[...])
    acc = acc + gate(gc_ref, gc2_ref) * _dot(xc, wc_ref[...])
    o_ref[...] = acc.astype(o_ref.dtype)


def gated_merge(att_p, att_s, gm_p, gm_s, ssm_p, ssm_s, w_pa, w_pb, w_pc, proj):
    tp, k = att_p.shape
    ts = att_s.shape[0]
    d = w_pa.shape[1]
    tm = _pick(math.gcd(tp, ts), (256, 128, 64, 32, 16, 8))
    tn = 1024
    n_p, n_s = tp // tm, ts // tm
    shift = COL_GATES - COL_DT
    assert COL_DT % tn == 0 and d % tn == 0 and shift < LANES

    def xp_spec():
        return pl.BlockSpec((tm, k), lambda j, i: (jnp.minimum(i, n_p - 1), 0))

    def xs_spec():
        return pl.BlockSpec((tm, k), lambda j, i: (jnp.maximum(i - n_p, 0), 0))

    def w_spec():
        return pl.BlockSpec((k, tn), lambda j, i: (0, j))

    def g_specs(which):
        base = (COL_DT + which * d) // tn
        tail = (COL_DT + which * d) // LANES
        return [pl.BlockSpec((tm, tn), lambda j, i: (i, base + j)),
                pl.BlockSpec((tm, LANES), lambda j, i: (i, tail + (j + 1) * (tn // LANES)))]

    return pl.pallas_call(
        functools.partial(_merge_kernel, n_p=n_p, shift=shift),
        grid=(d // tn, n_p + n_s),
        in_specs=[xp_spec(), xs_spec(), xp_spec(), xs_spec(), xp_spec(), xs_spec(),
                  w_spec(), w_spec(), w_spec()] + g_specs(0) + g_specs(1) + g_specs(2),
        out_specs=pl.BlockSpec((tm, tn), lambda j, i: (i, j)),
        out_shape=jax.ShapeDtypeStruct((tp + ts, d), BF16),
        compiler_params=_cparams(2),
        name="gated_merge",
    )(att_p, att_s, gm_p, gm_s, ssm_p, ssm_s, w_pa, w_pb, w_pc, *([proj] * 6))


def _rope_block(x, cos, sin_signed, first_half):
    outs = []
    for c in range(x.shape[1] // LANES):
        xc = x[:, c * LANES:(c + 1) * LANES]
        fwd = pltpu.roll(xc, LANES - HEAD_DIM // 2, axis=1)
        bwd = pltpu.roll(xc, HEAD_DIM // 2, axis=1)
        partner = jnp.where(first_half, fwd, bwd)
        outs.append(xc * cos + partner * sin_signed)
    return outs


def _rope_kernel(q_ref, k_ref, cos_ref, sin_ref, qo_ref, ko_ref):
    cos = cos_ref[...]
    sin_signed = sin_ref[...]
    lane = lax.broadcasted_iota(jnp.int32, cos.shape, 1)
    first_half = (lane % HEAD_DIM) < (HEAD_DIM // 2)
    for c, o in enumerate(_rope_block(q_ref[...], cos, sin_signed, first_half)):
        qo_ref[:, c * LANES:(c + 1) * LANES] = o
    for c, o in enumerate(_rope_block(k_ref[...], cos, sin_signed, first_half)):
        ko_ref[:, c * LANES:(c + 1) * LANES] = o


def rope_qk(proj, cos_t, sin_t):
    t = proj.shape[0]
    tm = _pick(t, (256, 128, 64, 32, 16, 8))
    return pl.pallas_call(
        _rope_kernel,
        grid=(t // tm,),
        in_specs=[
            pl.BlockSpec((tm, Q_W), lambda i: (i, COL_Q // Q_W)),
            pl.BlockSpec((tm, KV_W), lambda i: (i, COL_K // KV_W)),
            pl.BlockSpec((tm, LANES), lambda i: (i, 0)),
            pl.BlockSpec((tm, LANES), lambda i: (i, 0)),
        ],
        out_specs=[
            pl.BlockSpec((tm, Q_W), lambda i: (i, 0)),
            pl.BlockSpec((tm, KV_W), lambda i: (i, 0)),
        ],
        out_shape=[jax.ShapeDtypeStruct((t, Q_W), F32), jax.ShapeDtypeStruct((t, KV_W), F32)],
        compiler_params=_cparams(1),
        name="rope_qk",
    )(proj, proj, cos_t, sin_t)


def rope_tables(tp, seq, ts, nb, past_len):
    half = HEAD_DIM // 2
    inv = ROPE_THETA ** (-jnp.arange(half, dtype=F32) / half)
    pos_p = jnp.arange(tp, dtype=jnp.int32) % seq
    pos_s = past_len + jnp.arange(ts, dtype=jnp.int32) // nb
    pos = jnp.concatenate([pos_p, pos_s]).astype(F32)
    ang = pos[:, None] * inv[None, :]
    cos = jnp.tile(jnp.cos(ang), (1, LANES // half))
    sin = jnp.sin(ang)
    sin_signed = jnp.tile(jnp.concatenate([-sin, sin], axis=1), (1, LANES // HEAD_DIM))
    return cos, sin_signed


def _dup_head(slab, g):
    lane = lax.broadcasted_iota(jnp.int32, slab.shape, 1)
    rolled = pltpu.roll(slab, HEAD_DIM, axis=1)
    if g % 2 == 0:
        return jnp.where(lane < HEAD_DIM, slab, rolled)
    return jnp.where(lane < HEAD_DIM, rolled, slab)


def _swa_bias(qb, prev_ok):
    kw = WINDOW + qb
    ii = lax.broadcasted_iota(jnp.int32, (qb, kw), 0)
    jj = lax.broadcasted_iota(jnp.int32, (qb, kw), 1)
    ok_prev = (jj < WINDOW) & (jj > ii)
    if prev_ok is not True:
        ok_prev = ok_prev & prev_ok
    ok = ok_prev | ((jj >= WINDOW) & ((jj - WINDOW) <= ii))
    return jnp.where(ok, 0.0, NEG_BIG)


def _swa_group(q, kdup, vdup, bias, sink_ref, g, qb, store):
    rep = N_HEADS // N_KV_HEADS
    lane = lax.broadcasted_iota(jnp.int32, (qb, LANES), 1)
    scale = HEAD_DIM ** -0.5
    rows = []
    for r in range(rep):
        h = g * rep + r
        qc = q[:, (h // 2) * LANES:(h // 2 + 1) * LANES]
        keep = (lane < HEAD_DIM) if h % 2 == 0 else (lane >= HEAD_DIM)
        rows.append(jnp.where(keep, qc * scale, 0.0))
    s = _dot_nt(jnp.concatenate(rows, axis=0).astype(BF16), kdup)
    ps = []
    for r in range(rep):
        sr = s[r * qb:(r + 1) * qb] + bias
        sink = sink_ref[g * rep + r]
        m = jnp.maximum(jnp.max(sr, axis=-1, keepdims=True), sink)
        p = jnp.exp(sr - m)
        den = jnp.sum(p, axis=-1, keepdims=True) + jnp.exp(sink - m)
        ps.append(p * (1.0 / den))
    o = _dot(jnp.concatenate(ps, axis=0).astype(BF16), vdup)
    for c in range(rep // 2):
        oa = o[(2 * c) * qb:(2 * c + 1) * qb]
        ob = o[(2 * c + 1) * qb:(2 * c + 2) * qb]
        store((g * rep // 2 + c) * LANES, jnp.where(lane < HEAD_DIM, oa, ob))


def _swa_kernel(sink_ref, q_ref, kp_ref, kc_ref, vp_ref, vc_ref, o_ref, *, qb, prev_from_block0):
    n = pl.program_id(1)
    q = q_ref[...]
    kp, kc, vp, vc = kp_ref[...], kc_ref[...], vp_ref[...], vc_ref[...]
    bias = _swa_bias(qb, True if prev_from_block0 else (n > 0))

    def store(col, val):
        o_ref[:, col:col + LANES] = val.astype(o_ref.dtype)

    for g in range(N_KV_HEADS):
        sl = slice((g // 2) * LANES, (g // 2 + 1) * LANES)
        kdup = _dup_head(jnp.concatenate([kp[:, sl], kc[:, sl]], axis=0), g).astype(BF16)
        vdup = _dup_head(jnp.concatenate([vp[:, sl], vc[:, sl]], axis=0), g).astype(BF16)
        _swa_group(q, kdup, vdup, bias, sink_ref, g, qb, store)


def _swa_cache_kernel(sink_ref, q_ref, kc_ref, vc_ref, kp_ref, vp_ref, o_ref, *, qb, bseq):
    bias = _swa_bias(qb, True)
    for b in range(bseq):
        rows = slice(b * qb, (b + 1) * qb)
        q = q_ref[rows, :]

        def store(col, val, rows=rows):
            o_ref[rows, col:col + LANES] = val.astype(o_ref.dtype)

        for g in range(N_KV_HEADS):
            hs = slice(g * HEAD_DIM, (g + 1) * HEAD_DIM)
            kcat = jnp.concatenate([kp_ref[b, :, g, :], kc_ref[rows, hs]], axis=0)
            vcat = jnp.concatenate([vp_ref[b, :, g, :], vc_ref[rows, hs]], axis=0)
            kdup = jnp.concatenate([kcat, kcat], axis=1).astype(BF16)
            vdup = jnp.concatenate([vcat, vcat], axis=1).astype(BF16)
            _swa_group(q, kdup, vdup, bias, sink_ref, g, qb, store)


def swa_attention_cached(sinks, q, kc, vc, cache_k, cache_v, *, layer, n_seq, qb, bseq, name):
    rows = bseq * qb
    cache_spec = pl.BlockSpec((None, bseq, WINDOW, N_KV_HEADS, HEAD_DIM), lambda i: (layer, i, 0, 0, 0))
    return pl.pallas_call(
        functools.partial(_swa_cache_kernel, qb=qb, bseq=bseq),
        grid=(n_seq // bseq,),
        in_specs=[pl.BlockSpec(memory_space=pltpu.SMEM),
                  pl.BlockSpec((rows, Q_W), lambda i: (i, 0)),
                  pl.BlockSpec((rows, KV_W), lambda i: (i, 0)),
                  pl.BlockSpec((rows, KV_W), lambda i: (i, 0)),
                  cache_spec, cache_spec],
        out_specs=pl.BlockSpec((rows, Q_W), lambda i: (i, 0)),
        out_shape=jax.ShapeDtypeStruct((n_seq * qb, Q_W), F32),
        compiler_params=_cparams(1),
        name=name,
    )(sinks, q, kc, vc, cache_k, cache_v)


def swa_attention(sinks, q, kp, kc, vp, vc, *, n_seq, n_blk, qb, prev_from_block0,
                  kp_map, kc_map, vp_map, vc_map, out_dtype, name):
    return pl.pallas_call(
        functools.partial(_swa_kernel, qb=qb, prev_from_block0=prev_from_block0),
        grid=(n_seq, n_blk),
        in_specs=[
            pl.BlockSpec(memory_space=pltpu.SMEM),
            pl.BlockSpec((qb, Q_W), lambda i, n: (i * n_blk + n, 0)),
            pl.BlockSpec((WINDOW, KV_W), kp_map),
            pl.BlockSpec((qb, KV_W), kc_map),
            pl.BlockSpec((WINDOW, KV_W), vp_map),
            pl.BlockSpec((qb, KV_W), vc_map),
        ],
        out_specs=pl.BlockSpec((qb, Q_W), lambda i, n: (i * n_blk + n, 0)),
        out_shape=jax.ShapeDtypeStruct((n_seq * n_blk * qb, Q_W), out_dtype),
        compiler_params=_cparams(2),
        name=name,
    )(sinks, q, kp, kc, vp, vc)


def _gmlp_p_kernel(gu_ref, gv_ref, ws_ref, bst_ref, lg_ref, lb_ref, gm_ref, vg_ref, *, n_chunks):
    n = pl.program_id(1)
    vg = _ln_rows(_gelu(gv_ref[...]), lg_ref[...], lb_ref[...])
    gu = gu_ref[...]
    ri = lax.broadcasted_iota(jnp.int32, (CHUNK, CHUNK), 0)
    ci = lax.broadcasted_iota(jnp.int32, (CHUNK, CHUNK), 1)
    tril = ri >= ci
    bst = bst_ref[...]
    for g in range(GM_GROUPS):
        sl = slice(g * GM_GROUP_DIM, (g + 1) * GM_GROUP_DIM)
        w = jnp.where(tril, ws_ref[g], 0.0).astype(BF16)
        s = _dot(w, vg[:, sl].astype(BF16)) + bst[:, g:g + 1]
        gm_ref[:, sl] = (_gelu(gu[:, sl]) * s).astype(gm_ref.dtype)

    @pl.when(n == n_chunks - 1)
    def _():
        vg_ref[...] = vg


def gmlp_prompt(proj, n_seq, seq, ws, bs, ln_g, ln_b):
    nc = seq // CHUNK
    return pl.pallas_call(
        functools.partial(_gmlp_p_kernel, n_chunks=nc),
        grid=(n_seq, nc),
        in_specs=[
            pl.BlockSpec((CHUNK, GM_W), lambda i, n: (i * nc + n, COL_GU // GM_W)),
            pl.BlockSpec((CHUNK, GM_W), lambda i, n: (i * nc + n, COL_GV // GM_W)),
            pl.BlockSpec((GM_GROUPS, CHUNK, CHUNK), lambda i, n: (0, 0, 0)),
            pl.BlockSpec((CHUNK, GM_GROUPS), lambda i, n: (0, 0)),
            pl.BlockSpec((1, GM_W), lambda i, n: (0, 0)),
            pl.BlockSpec((1, GM_W), lambda i, n: (0, 0)),
        ],
        out_specs=[
            pl.BlockSpec((CHUNK, GM_W), lambda i, n: (i * nc + n, 0)),
            pl.BlockSpec((CHUNK, GM_W), lambda i, n: (i, 0)),
        ],
        out_shape=[jax.ShapeDtypeStruct((n_seq * seq, GM_W), BF16),
                   jax.ShapeDtypeStruct((n_seq * CHUNK, GM_W), F32)],
        compiler_params=_cparams(2),
        name="gmlp_prompt",
    )(proj, proj, ws, bs.T, ln_g.reshape(1, GM_W), ln_b.reshape(1, GM_W))


def _gmlp_s_kernel(*refs, lt):
    gu_refs = refs[:lt]
    gv_refs = refs[lt:2 * lt]
    wrow_ref, brow_ref, lg_ref, lb_ref, gm_ref, vg_ref = refs[2 * lt:]
    nb = gu_refs[0].shape[0]
    vgs = [_ln_rows(_gelu(gv_refs[t][...]), lg_ref[...], lb_ref[...]) for t in range(lt)]
    for i in range(lt):
        s = brow_ref[i:i + 1, :]
        for j in range(i + 1):
            s = s + wrow_ref[i * lt + j:i * lt + j + 1, :] * vgs[j]
        gm_ref[i * nb:(i + 1) * nb, :] = (_gelu(gu_refs[i][...]) * s).astype(gm_ref.dtype)
        vg_ref[i * nb:(i + 1) * nb, :] = vgs[i]


def gmlp_sample(proj, tp, nb, lt, ws, bs, ln_g, ln_b):
    w_small = ws[:, :lt, :lt]
    wrow = jnp.repeat(jnp.transpose(w_small, (1, 2, 0)).reshape(lt * lt, GM_GROUPS), GM_GROUP_DIM, axis=1)
    brow = jnp.repeat(bs[:, :lt].T, GM_GROUP_DIM, axis=1)
    row0 = tp // nb

    def spec(t, col):
        return pl.BlockSpec((nb, GM_W), lambda i: (row0 + t, col // GM_W))

    in_specs = [spec(t, COL_GU) for t in range(lt)] + [spec(t, COL_GV) for t in range(lt)] + [
        pl.BlockSpec((lt * lt, GM_W), lambda i: (0, 0)),
        pl.BlockSpec((lt, GM_W), lambda i: (0, 0)),
        pl.BlockSpec((1, GM_W), lambda i: (0, 0)),
        pl.BlockSpec((1, GM_W), lambda i: (0, 0)),
    ]
    return pl.pallas_call(
        functools.partial(_gmlp_s_kernel, lt=lt),
        grid=(1,),
        in_specs=in_specs,
        out_specs=[pl.BlockSpec((lt * nb, GM_W), lambda i: (0, 0)),
                   pl.BlockSpec((lt * nb, GM_W), lambda i: (0, 0))],
        out_shape=[jax.ShapeDtypeStruct((lt * nb, GM_W), BF16),
                   jax.ShapeDtypeStruct((lt * nb, GM_W), F32)],
        compiler_params=_cparams(1),
        name="gmlp_sample",
    )(*([proj] * (2 * lt)), wrow, brow, ln_g.reshape(1, GM_W), ln_b.reshape(1, GM_W))


def _conv_silu(cur, prev8, w, bias):
    q = cur.shape[0]
    up = jnp.concatenate([prev8, cur], axis=0)
    acc = bias + up[SUBLANES:SUBLANES + q] * w[CONV_K - 1:CONV_K]
    for j in range(CONV_K - 1):
        off = SUBLANES - (CONV_K - 1) + j
        acc = acc + up[off:off + q] * w[j:j + 1]
    return _silu(acc)


def _ssd_p_kernel(z_ref, xs_ref, bc_ref, xsp_ref, bcp_ref, dt_ref, dtt_ref,
                  cwx_ref, cwbc_ref, cbx_ref, cbbc_ref, dtbe_ref, dtbc_ref, aloge_ref, alogc_ref,
                  dske_ref, ng_ref, e_ref, y_ref, st_ref, s_scr, *, n_chunks):
    c = pl.program_id(1)
    q = CHUNK
    rep = SSM_HEADS // SSM_GROUPS
    gw = rep * SSM_HEAD_DIM

    @pl.when(c == 0)
    def _():
        s_scr[...] = jnp.zeros_like(s_scr)

    has_prev = (c > 0).astype(F32)
    xs = _conv_silu(xs_ref[...], xsp_ref[...] * has_prev, cwx_ref[...], cbx_ref[...])
    bcm = _conv_silu(bc_ref[...], bcp_ref[...] * has_prev, cwbc_ref[...], cbbc_ref[...])

    ri = lax.broadcasted_iota(jnp.int32, (q, q), 0)
    ci = lax.broadcasted_iota(jnp.int32, (q, q), 1)
    tril = ri >= ci
    ones_tril = jnp.where(tril, 1.0, 0.0).astype(BF16)
    ones_triu = jnp.where(ri <= ci, 1.0, 0.0).astype(BF16)

    dt_e = _softplus(_exact_dot_left(_split3(dt_ref[...]), e_ref[...]) + dtbe_ref[...])
    a_e = dt_e * (-jnp.exp(aloge_ref[...]))
    a_hi, a_mid, a_lo = _split3(a_e)
    cum_e = _dot(ones_tril, a_hi) + _dot(ones_tril, a_mid) + _dot(ones_tril, a_lo)
    dt_t = _softplus(dtt_ref[...] + dtbc_ref[...])
    a_t = dt_t * (-jnp.exp(alogc_ref[...]))
    cum_t = _exact_dot_left(_split3(a_t), ones_triu)

    xdt = xs * dt_e
    cum_last = cum_e[q - 1:q, :]
    lane = lax.broadcasted_iota(jnp.int32, (q, LANES), 1)
    z = z_ref[...]
    ys = []
    for g in range(SSM_GROUPS):
        gs = slice(g * gw, (g + 1) * gw)
        bg = bcm[:, g * SSM_STATE:(g + 1) * SSM_STATE]
        cg = bcm[:, SSM_BC + g * SSM_STATE:SSM_BC + (g + 1) * SSM_STATE]
        bg16, cg16 = bg.astype(BF16), cg.astype(BF16)
        cb = _dot_nt(cg16, bg16)
        ydiag = []
        for pr in range(rep // 2):
            ms = []
            for hh in (2 * pr, 2 * pr + 1):
                h = g * rep + hh
                col = cum_e[:, h * SSM_HEAD_DIM:h * SSM_HEAD_DIM + 1]
                row = cum_t[h:h + 1, :]
                seg = jnp.where(tril, col - row, NEG_BIG)
                ms.append(cb * jnp.exp(seg))
            lhs = jnp.concatenate(ms, axis=1).astype(BF16)
            xslab = xdt[:, (g * rep + 2 * pr) * SSM_HEAD_DIM:(g * rep + 2 * pr + 2) * SSM_HEAD_DIM]
            xbd = jnp.concatenate([jnp.where(lane < SSM_HEAD_DIM, xslab, 0.0),
                                   jnp.where(lane >= SSM_HEAD_DIM, xslab, 0.0)], axis=0).astype(BF16)
            ydiag.append(_dot(lhs, xbd))
        ydiag = jnp.concatenate(ydiag, axis=1)
        s_old = s_scr[g]
        yoff = _dot(cg16, s_old.astype(BF16)) * jnp.exp(cum_e[:, gs])
        ys.append(ydiag + yoff)
        xw = xdt[:, gs] * jnp.exp(cum_last[:, gs] - cum_e[:, gs])
        s_scr[g] = s_old * jnp.exp(cum_last[:, gs]) + _dot(bg.T.astype(BF16), xw.astype(BF16))
    y = jnp.concatenate(ys, axis=1) + dske_ref[...] * xs
    gated = y * _silu(z)
    out = gated * lax.rsqrt(jnp.mean(gated * gated, axis=-1, keepdims=True) + LN_EPS) * ng_ref[...]
    y_ref[...] = out.astype(y_ref.dtype)

    @pl.when(c == n_chunks - 1)
    def _():
        for g in range(SSM_GROUPS):
            st_ref[0, g * gw:(g + 1) * gw, :] = s_scr[g].T


def _expand_heads(v):
    return jnp.repeat(v.astype(F32), SSM_HEAD_DIM).reshape(1, SSM_INNER)


def _head_expand_matrix():
    e = np.zeros((LANES, SSM_INNER), np.float32)
    for h in range(SSM_HEADS):
        e[h, h * SSM_HEAD_DIM:(h + 1) * SSM_HEAD_DIM] = 1.0
    return jnp.asarray(e, BF16)


def ssd_prompt(proj, dtt, n_seq, seq, conv_w, conv_b, dt_bias, a_log, d_skip, norm_g):
    b = n_seq
    nc = seq // CHUNK
    blk8 = CHUNK // SUBLANES
    const2 = lambda i, c: (0, 0)
    col_bc = COL_BC // (2 * SSM_BC)
    col_xs = COL_XS // SSM_INNER
    col_z = COL_Z // SSM_INNER
    prev_map_x = lambda i, c: (jnp.maximum((i * nc + c) * blk8 - 1, 0), col_xs)
    prev_map_bc = lambda i, c: (jnp.maximum((i * nc + c) * blk8 - 1, 0), col_bc)
    args = (
        proj, proj, proj, proj, proj, proj, dtt,
        conv_w[:, :SSM_INNER], conv_w[:, SSM_INNER:], conv_b[:SSM_INNER].reshape(1, -1),
        conv_b[SSM_INNER:].reshape(1, -1),
        _expand_heads(dt_bias), jnp.broadcast_to(dt_bias.astype(F32)[:, None], (SSM_HEADS, CHUNK)),
        _expand_heads(a_log), jnp.broadcast_to(a_log.astype(F32)[:, None], (SSM_HEADS, CHUNK)),
        _expand_heads(d_skip), norm_g.reshape(1, SSM_INNER), _head_expand_matrix(),
    )
    in_specs = [
        pl.BlockSpec((CHUNK, SSM_INNER), lambda i, c: (i * nc + c, col_z)),
        pl.BlockSpec((CHUNK, SSM_INNER), lambda i, c: (i * nc + c, col_xs)),
        pl.BlockSpec((CHUNK, 2 * SSM_BC), lambda i, c: (i * nc + c, col_bc)),
        pl.BlockSpec((SUBLANES, SSM_INNER), prev_map_x),
        pl.BlockSpec((SUBLANES, 2 * SSM_BC), prev_map_bc),
        pl.BlockSpec((CHUNK, LANES), lambda i, c: (i * nc + c, COL_DT // LANES)),
        pl.BlockSpec((SSM_HEADS, CHUNK), lambda i, c: (0, i * nc + c)),
        pl.BlockSpec((CONV_K, SSM_INNER), const2),
        pl.BlockSpec((CONV_K, 2 * SSM_BC), const2),
        pl.BlockSpec((1, SSM_INNER), const2),
        pl.BlockSpec((1, 2 * SSM_BC), const2),
        pl.BlockSpec((1, SSM_INNER), const2),
        pl.BlockSpec((SSM_HEADS, CHUNK), const2),
        pl.BlockSpec((1, SSM_INNER), const2),
        pl.BlockSpec((SSM_HEADS, CHUNK), const2),
        pl.BlockSpec((1, SSM_INNER), const2),
        pl.BlockSpec((1, SSM_INNER), const2),
        pl.BlockSpec((LANES, SSM_INNER), const2),
    ]
    return pl.pallas_call(
        functools.partial(_ssd_p_kernel, n_chunks=nc),
        grid=(b, nc),
        in_specs=in_specs,
        out_specs=[
            pl.BlockSpec((CHUNK, SSM_INNER), lambda i, c: (i * nc + c, 0)),
            pl.BlockSpec((1, SSM_INNER, SSM_STATE), lambda i, c: (i, 0, 0)),
        ],
        out_shape=[jax.ShapeDtypeStruct((b * seq, SSM_INNER), BF16),
                   jax.ShapeDtypeStruct((b, SSM_INNER, SSM_STATE), F32)],
        scratch_shapes=[pltpu.VMEM((SSM_GROUPS, SSM_STATE, SSM_INNER // SSM_GROUPS), F32)],
        compiler_params=_cparams(2),
        name="ssd_prompt",
    )(*args)


def _group_expand_matrix():
    gw = SSM_INNER // SSM_GROUPS
    m = np.zeros((SSM_BC, SSM_INNER), np.float32)
    for g in range(SSM_GROUPS):
        m[g * SSM_STATE:(g + 1) * SSM_STATE, g * gw:(g + 1) * gw] = 1.0
    return jnp.asarray(m, BF16)


def _ssd_s_pre_kernel(*refs, lt):
    xs_refs = refs[:lt]
    bc_refs = refs[lt:2 * lt]
    dt_refs = refs[2 * lt:3 * lt]
    (cx_ref, cbc_ref, cwx_ref, cwbc_ref, cbx_ref, cbbc_ref, dtbe_ref, aloge_ref, dske_ref, e_ref,
     gmat_ref, c_ref, b_ref, xw_ref, dec_ref, yd_ref, ec_ref) = refs[3 * lt:]
    nprev = CONV_K - 1
    ux = [cx_ref[j] for j in range(nprev)] + [r[...] for r in xs_refs]
    ub = [cbc_ref[j] for j in range(nprev)] + [r[...] for r in bc_refs]
    cwx, cwbc = cwx_ref[...], cwbc_ref[...]
    neg_a = -jnp.exp(aloge_ref[...])
    xs, bm, cm, xdt, cum = [], [], [], [], []
    run = None
    for t in range(lt):
        ax = cbx_ref[...]
        ab = cbbc_ref[...]
        for j in range(CONV_K):
            ax = ax + ux[t + j] * cwx[j:j + 1]
            ab = ab + ub[t + j] * cwbc[j:j + 1]
        x_t = _silu(ax)
        bc_t = _silu(ab)
        dt_e = _softplus(_exact_dot_left(_split3(dt_refs[t][...]), e_ref[...]) + dtbe_ref[...])
        a_t = dt_e * neg_a
        run = a_t if run is None else run + a_t
        xs.append(x_t)
        bm.append(bc_t[:, :SSM_BC])
        cm.append(bc_t[:, SSM_BC:])
        xdt.append(x_t * dt_e)
        cum.append(run)
    for i in range(lt):
        yd = dske_ref[...] * xs[i]
        for j in range(i + 1):
            hi, mid, _ = _split3(cm[i] * bm[j])
            cbe = _dot(hi, gmat_ref[...]) + _dot(mid, gmat_ref[...])
            yd = yd + cbe * jnp.exp(cum[i] - cum[j]) * xdt[j]
        yd_ref[i] = yd
        ec_ref[i] = jnp.exp(cum[i])
        c_ref[i] = cm[i]
        b_ref[i] = bm[i]
        xw_ref[i] = xdt[i] * jnp.exp(cum[lt - 1] - cum[i])
    dec_ref[...] = jnp.exp(cum[lt - 1])


def _rows_block(rows, total):
    c = rows[0].shape[1]
    rid = lax.broadcasted_iota(jnp.int32, (SUBLANES, c), 0)
    acc = jnp.zeros((SUBLANES, c), F32)
    for j, r in enumerate(rows):
        acc = jnp.where(rid == j, jnp.broadcast_to(r, (SUBLANES, c)), acc)
    if total == SUBLANES:
        return acc
    return jnp.concatenate([acc, jnp.zeros((total - SUBLANES, c), F32)], axis=0)


def _ssd_s_state_kernel_inplace(c_ref, b_ref, xw_ref, dec_ref, h0_ref, prev_ref, hn_ref, yr_ref, *, lt):
    del prev_ref
    _ssd_s_state_kernel(c_ref, b_ref, xw_ref, dec_ref, h0_ref, hn_ref, yr_ref, lt=lt)


def _ssd_s_state_kernel(c_ref, b_ref, xw_ref, dec_ref, h0_ref, hn_ref, yr_ref, *, lt, slot=0,
                        fill_slots=None):
    b = pl.program_id(0)
    gw = SSM_INNER // SSM_GROUPS
    c8 = _rows_block([c_ref[i, pl.ds(b, 1), :] for i in range(lt)], SUBLANES).astype(BF16)
    b128 = _rows_block([b_ref[i, pl.ds(b, 1), :] for i in range(lt)], LANES).astype(BF16)
    xaug = _rows_block([xw_ref[i, pl.ds(b, 1), :] for i in range(lt)] + [dec_ref[pl.ds(b, 1), :]], LANES)
    for g in range(SSM_GROUPS):
        hg = h0_ref[0, g * gw:(g + 1) * gw, :]
        yraw = _dot_nt(c8[:, g * SSM_STATE:(g + 1) * SSM_STATE], hg.astype(BF16))
        for i in range(lt):
            yr_ref[i, pl.ds(b, 1), g * gw:(g + 1) * gw] = yraw[i:i + 1, :]
        tr = xaug[:, g * gw:(g + 1) * gw].T
        s = _dot(tr.astype(BF16), b128[:, g * SSM_STATE:(g + 1) * SSM_STATE])
        new = hg * tr[:, lt:lt + 1] + s
        if fill_slots is None:
            hn_ref[0, g * gw:(g + 1) * gw, :] = new
        else:
            hn_ref[slot, 0, g * gw:(g + 1) * gw, :] = new
    if fill_slots is not None:
        for other in fill_slots:
            hn_ref[other] = jnp.zeros(hn_ref.shape[1:], hn_ref.dtype)


def _ssd_s_post_kernel(*refs, lt):
    z_refs = refs[:lt]
    yd_ref, ec_ref, yr_ref, ng_ref, o_ref = refs[lt:]
    nb = z_refs[0].shape[0]
    for i in range(lt):
        y = yd_ref[i] + ec_ref[i] * yr_ref[i]
        gated = y * _silu(z_refs[i][...])
        out = gated * lax.rsqrt(jnp.mean(gated * gated, axis=-1, keepdims=True) + LN_EPS) * ng_ref[...]
        o_ref[i * nb:(i + 1) * nb, :] = out.astype(o_ref.dtype)


def ssd_sample(proj, tp, nb, lt, conv_state, state_all, new_states, layer, conv_w, conv_b, dt_bias,
               a_log, d_skip, norm_g):
    row0 = tp // nb
    cs = jnp.transpose(conv_state, (1, 0, 2))
    one = lambda i: (0, 0)
    one3 = lambda i: (0, 0, 0)

    def rows(t, width, col):
        return pl.BlockSpec((nb, width), lambda i: (row0 + t, col // width))

    in_specs = ([rows(t, SSM_INNER, COL_XS) for t in range(lt)]
                + [rows(t, 2 * SSM_BC, COL_BC) for t in range(lt)]
                + [pl.BlockSpec((nb, LANES), lambda i, t=t: (row0 + t, COL_DT // LANES)) for t in range(lt)]
                + [pl.BlockSpec((CONV_K - 1, nb, SSM_INNER), one3),
                   pl.BlockSpec((CONV_K - 1, nb, 2 * SSM_BC), one3),
                   pl.BlockSpec((CONV_K, SSM_INNER), one),
                   pl.BlockSpec((CONV_K, 2 * SSM_BC), one),
                   pl.BlockSpec((1, SSM_INNER), one),
                   pl.BlockSpec((1, 2 * SSM_BC), one),
                   pl.BlockSpec((1, SSM_INNER), one),
                   pl.BlockSpec((1, SSM_INNER), one),
                   pl.BlockSpec((1, SSM_INNER), one),
                   pl.BlockSpec((LANES, SSM_INNER), one),
                   pl.BlockSpec((SSM_BC, SSM_INNER), one)])
    f3 = lambda w: jax.ShapeDtypeStruct((lt, nb, w), F32)
    c_a, b_a, xw_a, dec_a, yd_a, ec_a = pl.pallas_call(
        functools.partial(_ssd_s_pre_kernel, lt=lt),
        grid=(1,),
        in_specs=in_specs,
        out_specs=[pl.BlockSpec((lt, nb, SSM_BC), one3), pl.BlockSpec((lt, nb, SSM_BC), one3),
                   pl.BlockSpec((lt, nb, SSM_INNER), one3), pl.BlockSpec((nb, SSM_INNER), one),
                   pl.BlockSpec((lt, nb, SSM_INNER), one3), pl.BlockSpec((lt, nb, SSM_INNER), one3)],
        out_shape=[f3(SSM_BC), f3(SSM_BC), f3(SSM_INNER), jax.ShapeDtypeStruct((nb, SSM_INNER), F32),
                   f3(SSM_INNER), f3(SSM_INNER)],
        compiler_params=_cparams(1),
        name="ssd_sample_pre",
    )(*([proj] * (3 * lt)), cs[:, :, :SSM_INNER], cs[:, :, SSM_INNER:],
      conv_w[:, :SSM_INNER], conv_w[:, SSM_INNER:], conv_b[:SSM_INNER].reshape(1, -1),
      conv_b[SSM_INNER:].reshape(1, -1), _expand_heads(dt_bias), _expand_heads(a_log),
      _expand_heads(d_skip), _head_expand_matrix(), _group_expand_matrix())

    depth = state_all.shape[0]
    h0r = state_all.reshape(depth, nb, SSM_INNER, SSM_STATE)
    state_specs = [pl.BlockSpec((lt, nb, SSM_BC), one3), pl.BlockSpec((lt, nb, SSM_BC), one3),
                   pl.BlockSpec((lt, nb, SSM_INNER), one3), pl.BlockSpec((nb, SSM_INNER), one),
                   pl.BlockSpec((None, 1, SSM_INNER, SSM_STATE), lambda i: (layer, i, 0, 0))]
    hn_shape = jax.ShapeDtypeStruct((depth, nb, SSM_INNER, SSM_STATE), F32)
    if new_states is None:
        fill = tuple(s for s in range(depth) if s != layer)
        hn, yr = pl.pallas_call(
            functools.partial(_ssd_s_state_kernel, lt=lt, slot=layer, fill_slots=fill),
            grid=(nb,),
            in_specs=state_specs,
            out_specs=[pl.BlockSpec((depth, 1, SSM_INNER, SSM_STATE), lambda i: (0, i, 0, 0)),
                       pl.BlockSpec((lt, nb, SSM_INNER), one3)],
            out_shape=[hn_shape, f3(SSM_INNER)],
            compiler_params=_cparams(1),
            name="ssd_sample_state",
        )(c_a, b_a, xw_a, dec_a, h0r)
    else:
        hn, yr = pl.pallas_call(
            functools.partial(_ssd_s_state_kernel_inplace, lt=lt),
            grid=(nb,),
            in_specs=state_specs + [pl.BlockSpec(memory_space=pl.ANY)],
            out_specs=[pl.BlockSpec((None, 1, SSM_INNER, SSM_STATE), lambda i: (layer, i, 0, 0)),
                       pl.BlockSpec((lt, nb, SSM_INNER), one3)],
            out_shape=[hn_shape, f3(SSM_INNER)],
            input_output_aliases={5: 0},
            compiler_params=_cparams(1),
            name="ssd_sample_state",
        )(c_a, b_a, xw_a, dec_a, h0r, new_states)

    ssm = pl.pallas_call(
        functools.partial(_ssd_s_post_kernel, lt=lt),
        grid=(1,),
        in_specs=([rows(t, SSM_INNER, COL_Z) for t in range(lt)]
                  + [pl.BlockSpec((lt, nb, SSM_INNER), one3)] * 3 + [pl.BlockSpec((1, SSM_INNER), one)]),
        out_specs=pl.BlockSpec((lt * nb, SSM_INNER), one),
        out_shape=jax.ShapeDtypeStruct((lt * nb, SSM_INNER), BF16),
        compiler_params=_cparams(1),
        name="ssd_sample_post",
    )(*([proj] * lt), yd_a, ec_a, yr, norm_g.reshape(1, SSM_INNER))
    return ssm, hn


def _softmax_rows(s):
    p = jnp.exp(s - jnp.max(s, axis=-1, keepdims=True))
    return p * (1.0 / jnp.sum(p, axis=-1, keepdims=True))


def _xattn_kernel(q_ref, k_ref, v_ref, o_ref, *, nh, bseq, tq):
    scale = MEM_HEAD_DIM ** -0.5
    for b in range(bseq):
        rows = slice(b * tq, (b + 1) * tq)
        for h in range(nh):
            sl = slice(h * MEM_HEAD_DIM, (h + 1) * MEM_HEAD_DIM)
            s = _dot_nt(q_ref[rows, sl].astype(BF16), k_ref[b, :, h, :].astype(BF16)) * scale
            o = _dot(_softmax_rows(s).astype(BF16), v_ref[b, :, h, :].astype(BF16))
            o_ref[rows, sl] = o.astype(o_ref.dtype)


def cross_attention(q, k, v, *, layer, n_seq, seq, tq, bseq, name):
    w = q.shape[1]
    _, _, m, nh, dh = k.shape
    nq = seq // tq
    assert bseq == 1 or nq == 1
    kv_spec = pl.BlockSpec((None, bseq, m, nh, dh), lambda i, n: (layer, i, 0, 0, 0))
    return pl.pallas_call(
        functools.partial(_xattn_kernel, nh=nh, bseq=bseq, tq=tq),
        grid=(n_seq // bseq, nq),
        in_specs=[pl.BlockSpec((bseq * tq, w), lambda i, n: (i * nq + n, 0)), kv_spec, kv_spec],
        out_specs=pl.BlockSpec((bseq * tq, w), lambda i, n: (i * nq + n, 0)),
        out_shape=jax.ShapeDtypeStruct((n_seq * seq, w), F32),
        compiler_params=_cparams(2),
        name=name,
    )(q, k, v)


def _router_kernel(h_ref, w_ref, b_ref, o_ref):
    logits = _dot(h_ref[...].astype(BF16), w_ref[...].astype(BF16)) + b_ref[...]
    lane = lax.broadcasted_iota(jnp.int32, logits.shape, 1)
    lane_f = lane.astype(F32)
    big = float(LANES)
    is_g = lane < N_EGROUPS
    lg = jnp.where(is_g, logits, NEG_BIG)
    mg = jnp.max(lg, axis=-1, keepdims=True)
    zg = jnp.sum(jnp.where(is_g, jnp.exp(lg - mg), 0.0), axis=-1, keepdims=True)
    gi = jnp.min(jnp.where(is_g & (lg == mg), lane_f, big), axis=-1, keepdims=True)
    gw = 1.0 / zg
    lo = N_EGROUPS + gi * EXPERTS_PER_GROUP
    is_e = (lane_f >= lo) & (lane_f < lo + EXPERTS_PER_GROUP)
    le = jnp.where(is_e, logits, NEG_BIG)
    me = jnp.max(le, axis=-1, keepdims=True)
    ee = jnp.where(is_e, jnp.exp(le - me), 0.0)
    pe = ee / jnp.sum(ee, axis=-1, keepdims=True)
    pe = jnp.where(is_e, pe, -1.0)
    p1 = jnp.max(pe, axis=-1, keepdims=True)
    i1 = jnp.min(jnp.where(pe == p1, lane_f, big), axis=-1, keepdims=True)
    pe2 = jnp.where(lane_f == i1, -1.0, pe)
    p2 = jnp.max(pe2, axis=-1, keepdims=True)
    i2 = jnp.min(jnp.where(pe2 == p2, lane_f, big), axis=-1, keepdims=True)
    tot = p1 + p2
    out = jnp.where(lane == 0, i1 - N_EGROUPS,
                    jnp.where(lane == 1, i2 - N_EGROUPS,
                              jnp.where(lane == 2, gw * (p1 / tot),
                                        jnp.where(lane == 3, gw * (p2 / tot), 0.0))))
    o_ref[...] = out


def moe_router(h, w_rg, b_rg, w_re, b_re):
    t, d = h.shape
    tm = _pick(t, (256, 128, 64, 32, 16, 8))
    npad = LANES - N_EGROUPS - N_EXPERTS
    w = jnp.concatenate([w_rg, w_re, jnp.zeros((d, npad), F32)], axis=1)
    b = jnp.concatenate([b_rg, b_re, jnp.zeros((npad,), F32)]).reshape(1, LANES)
    return pl.pallas_call(
        _router_kernel,
        grid=(t // tm,),
        in_specs=[pl.BlockSpec((tm, d), lambda i: (i, 0)),
                  pl.BlockSpec((d, LANES), lambda i: (0, 0)),
                  pl.BlockSpec((1, LANES), lambda i: (0, 0))],
        out_specs=pl.BlockSpec((tm, LANES), lambda i: (i, 0)),
        out_shape=jax.ShapeDtypeStruct((t, LANES), F32),
        compiler_params=_cparams(1),
        name="moe_router",
    )(h, w, b)


def _row_copy(src_hbm, dst, src_row, dst_row, sem):
    return pltpu.make_async_copy(src_hbm.at[pl.ds(src_row, 1)], dst.at[pl.ds(dst_row, 1)], sem)


def _moe_gather_kernel(tok_ref, h_ref, o_ref, buf, sem, *, tm):
    def start(i, carry):
        for p in range(2):
            r = 2 * i + p
            _row_copy(h_ref, buf, tok_ref[0, 0, r], r, sem).start(priority=p)
        return carry

    def wait(r, carry):
        _row_copy(h_ref, buf, tok_ref[0, 0, r], r, sem).wait()
        return carry

    lax.fori_loop(0, tm // 2, start, 0, unroll=DMA_LOOP_UNROLL // 2)
    lax.fori_loop(0, tm, wait, 0, unroll=DMA_LOOP_UNROLL)
    o_ref[...] = buf[...].astype(o_ref.dtype)


def moe_gather(h, row_token, tm):
    r_total = row_token.shape[0]
    d = h.shape[1]
    nblk = r_total // tm
    return pl.pallas_call(
        functools.partial(_moe_gather_kernel, tm=tm),
        grid=(nblk,),
        in_specs=[pl.BlockSpec((1, 1, tm), lambda i: (i, 0, 0), memory_space=pltpu.SMEM),
                  pl.BlockSpec(memory_space=pl.ANY)],
        out_specs=pl.BlockSpec((tm, d), lambda i: (i, 0)),
        out_shape=jax.ShapeDtypeStruct((r_total, d), BF16),
        scratch_shapes=[pltpu.VMEM((tm, d), h.dtype), pltpu.SemaphoreType.DMA(())],
        compiler_params=_cparams(1),
        name="moe_gather",
    )(row_token.reshape(nblk, 1, tm), h)


def _expert_weight_copies(w_refs, bufs, sems, layer, expert, slot):
    return [pltpu.make_async_copy(w.at[layer, expert], buf.at[slot], sems.at[k, slot])
            for k, (w, buf) in enumerate(zip(w_refs, bufs))]


def _expert_weights_step(s, tv_ref, te_ref, tn_ref, ts_ref, w_refs, bufs, w16s, sems, layer):
    @pl.when(s == 0)
    def _():
        for c in _expert_weight_copies(w_refs, bufs, sems, layer, te_ref[0], 0):
            c.start()

    @pl.when(tv_ref[s] == 2)
    def _():
        slot = ts_ref[s]
        for c in _expert_weight_copies(w_refs, bufs, sems, layer, te_ref[s], slot):
            c.wait()

        @pl.when(tn_ref[s] >= 0)
        def _():
            for c in _expert_weight_copies(w_refs, bufs, sems, layer, tn_ref[s], 1 - slot):
                c.start()

        for buf, w16 in zip(bufs, w16s):
            w16[...] = buf[slot].astype(BF16)


def _moe_up_kernel(tv_ref, tc_ref, te_ref, tn_ref, ts_ref, x_ref, wg_ref, wu_ref, o_ref,
                   gbuf, ubuf, wg16, wu16, sems, *, layer):
    s = pl.program_id(0)
    _expert_weights_step(s, tv_ref, te_ref, tn_ref, ts_ref, (wg_ref, wu_ref), (gbuf, ubuf), (wg16, wu16),
                         sems, layer)

    @pl.when(tv_ref[s] > 0)
    def _():
        x = x_ref[...]
        a = _dot(x, wg16[...])
        u = _dot(x, wu16[...])
        o_ref[...] = (_silu(a) * u).astype(o_ref.dtype)

    @pl.when(tv_ref[s] == 0)
    def _():
        o_ref[...] = jnp.zeros_like(o_ref)


def _moe_tables(plan):
    return plan["tile_v"], plan["tile_c"], plan["tile_e"], plan["tile_next"], plan["tile_slot"]


def moe_up(x_sorted, w_gate, w_up, layer, plan, tm):
    r_total, d = x_sorted.shape
    ff = w_gate.shape[-1]
    n_tiles = r_total // tm
    grid_spec = pltpu.PrefetchScalarGridSpec(
        num_scalar_prefetch=5,
        grid=(n_tiles,),
        in_specs=[
            pl.BlockSpec((tm, d), lambda s, tv, tc, te, tn, ts: (tc[s], 0)),
            pl.BlockSpec(memory_space=pl.ANY),
            pl.BlockSpec(memory_space=pl.ANY),
        ],
        out_specs=pl.BlockSpec((tm, ff), lambda s, tv, tc, te, tn, ts: (s, 0)),
        scratch_shapes=[pltpu.VMEM((2, d, ff), F32), pltpu.VMEM((2, d, ff), F32),
                        pltpu.VMEM((d, ff), BF16), pltpu.VMEM((d, ff), BF16),
                        pltpu.SemaphoreType.DMA((2, 2))],
    )
    return pl.pallas_call(
        functools.partial(_moe_up_kernel, layer=layer),
        grid_spec=grid_spec,
        out_shape=jax.ShapeDtypeStruct((r_total, ff), BF16),
        compiler_params=_cparams(1),
        name="moe_up",
    )(*_moe_tables(plan), x_sorted, w_gate, w_up)


def _moe_down_kernel(tv_ref, tc_ref, te_ref, tn_ref, ts_ref, x_ref, w_ref, o_ref, wbuf, w16, sems, *, layer):
    s = pl.program_id(0)
    _expert_weights_step(s, tv_ref, te_ref, tn_ref, ts_ref, (w_ref,), (wbuf,), (w16,), sems, layer)

    @pl.when(tv_ref[s] > 0)
    def _():
        o_ref[...] = _dot(x_ref[...], w16[...])

    @pl.when(tv_ref[s] == 0)
    def _():
        o_ref[...] = jnp.zeros_like(o_ref)


def moe_down(hid, w_down, layer, plan, tm):
    r_total, ff = hid.shape
    d = w_down.shape[-1]
    n_tiles = r_total // tm
    grid_spec = pltpu.PrefetchScalarGridSpec(
        num_scalar_prefetch=5,
        grid=(n_tiles,),
        in_specs=[
            pl.BlockSpec((tm, ff), lambda s, tv, tc, te, tn, ts: (tc[s], 0)),
            pl.BlockSpec(memory_space=pl.ANY),
        ],
        out_specs=pl.BlockSpec((tm, d), lambda s, tv, tc, te, tn, ts: (s, 0)),
        scratch_shapes=[pltpu.VMEM((2, ff, d), F32), pltpu.VMEM((ff, d), BF16),
                        pltpu.SemaphoreType.DMA((1, 2))],
    )
    return pl.pallas_call(
        functools.partial(_moe_down_kernel, layer=layer),
        grid_spec=grid_spec,
        out_shape=jax.ShapeDtypeStruct((r_total, d), F32),
        compiler_params=_cparams(1),
        name="moe_down",
    )(*_moe_tables(plan), hid, w_down)


def _moe_combine_kernel(pos_ref, y_ref, r_ref, h_ref, g_ref, b_ref, o1_ref, o2_ref, ybuf, sem, *,
                        tm, alpha, n_first):
    def start(i, carry):
        _row_copy(y_ref, ybuf.at[0], pos_ref[0, 0, 2 * i], i, sem).start(priority=0)
        _row_copy(y_ref, ybuf.at[1], pos_ref[0, 0, 2 * i + 1], i, sem).start(priority=1)
        return carry

    def wait(i, carry):
        _row_copy(y_ref, ybuf.at[0], pos_ref[0, 0, 2 * i], i, sem).wait()
        _row_copy(y_ref, ybuf.at[1], pos_ref[0, 0, 2 * i + 1], i, sem).wait()
        return carry

    lax.fori_loop(0, tm, start, 0, unroll=DMA_LOOP_UNROLL)
    lax.fori_loop(0, tm, wait, 0, unroll=DMA_LOOP_UNROLL)
    route = r_ref[...]
    ff = ybuf[0] * route[:, 2:3] + ybuf[1] * route[:, 3:4]
    h = _ln_rows(alpha * h_ref[...] + ff, g_ref[...], b_ref[...])
    if n_first is None:
        o1_ref[...] = h
        o2_ref[...] = h.astype(BF16)
    else:
        i = pl.program_id(0)

        @pl.when(i < n_first)
        def _():
            o1_ref[...] = h

        @pl.when(i >= n_first)
        def _():
            o2_ref[...] = h


def moe_combine(y_sorted, pos, route, h, g, b, *, alpha, split_rows=None):
    t, d = h.shape
    tm = _pick(t if split_rows is None else math.gcd(split_rows, t - split_rows), (256, 128, 64, 32, 16, 8))
    nblk = t // tm
    if split_rows is None:
        n_first = None
        out_specs = [pl.BlockSpec((tm, d), lambda i: (i, 0)), pl.BlockSpec((tm, d), lambda i: (i, 0))]
        out_shape = [jax.ShapeDtypeStruct((t, d), F32), jax.ShapeDtypeStruct((t, d), BF16)]
    else:
        n_first = split_rows // tm
        out_specs = [pl.BlockSpec((tm, d), lambda i: (jnp.minimum(i, n_first - 1), 0)),
                     pl.BlockSpec((tm, d), lambda i: (jnp.maximum(i - n_first, 0), 0))]
        out_shape = [jax.ShapeDtypeStruct((split_rows, d), F32),
                     jax.ShapeDtypeStruct((t - split_rows, d), F32)]
    return pl.pallas_call(
        functools.partial(_moe_combine_kernel, tm=tm, alpha=alpha, n_first=n_first),
        grid=(nblk,),
        in_specs=[pl.BlockSpec((1, 1, 2 * tm), lambda i: (i, 0, 0), memory_space=pltpu.SMEM),
                  pl.BlockSpec(memory_space=pl.ANY),
                  pl.BlockSpec((tm, LANES), lambda i: (i, 0)),
                  pl.BlockSpec((tm, d), lambda i: (i, 0)),
                  pl.BlockSpec((1, d), lambda i: (0, 0)),
                  pl.BlockSpec((1, d), lambda i: (0, 0))],
        out_specs=out_specs,
        out_shape=out_shape,
        scratch_shapes=[pltpu.VMEM((2, tm, d), F32), pltpu.SemaphoreType.DMA(())],
        compiler_params=_cparams(1),
        name="moe_combine",
    )(pos.reshape(nblk, 1, 2 * tm), y_sorted, route, h, g.reshape(1, d), b.reshape(1, d))


def moe_plan(route, tm):
    t = route.shape[0]
    eid = route[:, :2].astype(jnp.int32).reshape(-1)
    onehot = (eid[:, None] == jnp.arange(N_EXPERTS, dtype=jnp.int32)[None, :]).astype(jnp.int32)
    csum = jnp.cumsum(onehot, axis=0)
    rank = jnp.sum((csum - onehot) * onehot, axis=1)
    counts = csum[-1]
    tiles_e = (counts + tm - 1) // tm
    tile_end = jnp.cumsum(tiles_e)
    tile_start = tile_end - tiles_e
    n_used = tile_end[-1]
    n_tiles = (2 * t + N_EXPERTS * (tm - 1)) // tm + 1
    r_total = n_tiles * tm
    dest = tile_start[eid] * tm + rank
    row_token = (jnp.arange(r_total, dtype=jnp.int32) % t).at[dest].set(
        jnp.arange(2 * t, dtype=jnp.int32) // 2)
    tile_ids = jnp.arange(n_tiles, dtype=jnp.int32)
    tile_clamped = jnp.minimum(tile_ids, n_used - 1)
    tile_e = jnp.sum(tile_end[None, :] <= tile_clamped[:, None], axis=1).astype(jnp.int32)
    tile_first = tile_ids == tile_start[tile_e]
    tile_v = jnp.where(tile_ids < n_used, 1 + tile_first.astype(jnp.int32), 0).astype(jnp.int32)
    e_ids = jnp.arange(N_EXPERTS, dtype=jnp.int32)
    nonempty = tiles_e > 0
    cand = jnp.where(nonempty[None, :] & (e_ids[None, :] > e_ids[:, None]), e_ids[None, :], N_EXPERTS)
    next_e = jnp.min(cand, axis=1)
    next_e = jnp.where(next_e >= N_EXPERTS, -1, next_e).astype(jnp.int32)
    slot_e = ((jnp.cumsum(nonempty.astype(jnp.int32)) - 1) % 2).astype(jnp.int32)
    return dict(row_token=row_token, pos=dest.astype(jnp.int32), tile_v=tile_v, tile_c=tile_clamped,
                tile_e=tile_e, tile_next=next_e[tile_e], tile_slot=slot_e[tile_e])


def hierarchical_moe_ln(hf, layer, w_rg, b_rg, w_re, b_re, w_gate, w_up, w_down, ln_g, ln_b, *, alpha,
                        split_rows=None):
    route = moe_router(hf, w_rg, b_rg, w_re, b_re)
    plan = moe_plan(route, MOE_TM)
    x_sorted = moe_gather(hf, plan["row_token"], MOE_TM)
    hid = moe_up(x_sorted, w_gate, w_up, layer, plan, MOE_TM)
    y_sorted = moe_down(hid, w_down, layer, plan, MOE_TM)
    return moe_combine(y_sorted, plan["pos"], route, hf, ln_g, ln_b, alpha=alpha, split_rows=split_rows)


def _to_seq_major(x_tm, lt, nb, pad_to):
    w = x_tm.shape[1]
    x = jnp.transpose(x_tm.reshape(lt, nb, w), (1, 0, 2))
    x = jnp.pad(x, ((0, 0), (0, pad_to - lt), (0, 0)))
    return x.reshape(nb * pad_to, w)


def _to_time_major(x_sm, lt, nb, pad_to):
    w = x_sm.shape[1]
    x = x_sm.reshape(nb, pad_to, w)[:, :lt]
    return jnp.transpose(x, (1, 0, 2)).reshape(lt * nb, w)


def kernel(x_prompt, x_sample, mem_prompt, cache_swa_k, cache_swa_v, cache_mem_k, cache_mem_v, state_conv, state_ssm, ln_in_g, ln_in_b, w_in, attn_sinks, gm_ln_g, gm_ln_b, gm_ws, gm_bs, conv_w, conv_b, dt_bias, a_log, d_skip, ssm_norm_g, w_pa, w_pb, w_pc, w_o, ln1_g, ln1_b, w_cq, w_ck, w_cv, w_co, ln2_g, ln2_b, w_rg, b_rg, w_re, b_re, w_gate, w_up, w_down, ln3_g, ln3_b):
    bp, seq, d = x_prompt.shape
    nb, lt, _ = x_sample.shape
    depth = w_in.shape[0]
    mem_len = mem_prompt.shape[1]
    past_len = PAST_LEN
    wb = cache_swa_k.shape[2]
    assert wb == WINDOW and seq % CHUNK == 0 and lt <= SUBLANES
    tp, ts = bp * seq, nb * lt
    alpha = (2 * depth) ** 0.25
    qpad = SUBLANES

    xp = x_prompt.reshape(tp, d)
    xs = jnp.transpose(x_sample, (1, 0, 2)).reshape(ts, d)
    hf, hb = ln_in(xp, xs, ln_in_g, ln_in_b)
    cos_t, sin_t = rope_tables(tp, seq, ts, nb, past_len)
    mem_b = mem_prompt.reshape(bp * mem_len, d).astype(BF16)

    in_w = w_in.shape[2]
    assert in_w == Q_W + 2 * KV_W + 2 * GM_W + SSM_INNER + CONV_DIM + SSM_HEADS + 3 * D_MODEL
    n_whole = (in_w // IN_TN) * IN_TN
    w_tail = jnp.pad(w_in[:, :, n_whole:], ((0, 0), (0, 0), (0, IN_TN - (in_w - n_whole))))

    outs = {k: [] for k in ("p_k", "p_v", "p_mk", "p_mv", "p_conv", "p_ssm", "p_gv",
                            "s_k", "s_v", "s_conv", "s_ssm", "s_gv")}
    n_qblk = seq // WINDOW
    s_states = None
    for l in range(depth):
        proj = in_projection(hb, w_in, w_tail, l)
        q_rot, k_rot = rope_qk(proj, cos_t, sin_t)

        kcol, vcol = 0, COL_V // KV_W
        att_p = swa_attention(
            attn_sinks[l], q_rot, k_rot, k_rot, proj, proj,
            n_seq=bp, n_blk=n_qblk, qb=WINDOW, prev_from_block0=False,
            kp_map=lambda i, n: (jnp.maximum(i * n_qblk + n - 1, 0), kcol),
            kc_map=lambda i, n: (i * n_qblk + n, kcol),
            vp_map=lambda i, n: (jnp.maximum(i * n_qblk + n - 1, 0), vcol),
            vc_map=lambda i, n: (i * n_qblk + n, vcol),
            out_dtype=BF16, name="swa_prompt")
        k_s_tm = k_rot[tp:]
        v_s_tm = proj[tp:, COL_V:COL_V + KV_W]
        q_s = _to_seq_major(q_rot[tp:], lt, nb, qpad)
        k_s = _to_seq_major(k_s_tm, lt, nb, qpad)
        v_s = _to_seq_major(v_s_tm, lt, nb, qpad)
        att_s8 = swa_attention_cached(attn_sinks[l], q_s, k_s, v_s, cache_swa_k, cache_swa_v, layer=l,
                                      n_seq=nb, qb=qpad, bseq=_pick(nb, (8, 4, 2, 1)), name="swa_sample")
        att_s = _to_time_major(att_s8, lt, nb, qpad).astype(BF16)
        last_w = lambda a, c0: jnp.stack(
            [a[(i + 1) * seq - WINDOW:(i + 1) * seq, c0:c0 + KV_W] for i in range(bp)]
        ).reshape(bp, WINDOW, N_KV_HEADS, HEAD_DIM)
        outs["p_k"].append(last_w(k_rot, 0))
        outs["p_v"].append(last_w(proj, COL_V))
        k_new = jnp.transpose(k_s_tm.reshape(lt, nb, N_KV_HEADS, HEAD_DIM), (1, 0, 2, 3))
        v_new = jnp.transpose(v_s_tm.reshape(lt, nb, N_KV_HEADS, HEAD_DIM), (1, 0, 2, 3))
        outs["s_k"].append(jnp.concatenate([cache_swa_k[l], k_new], axis=1)[:, -wb:])
        outs["s_v"].append(jnp.concatenate([cache_swa_v[l], v_new], axis=1)[:, -wb:])

        gm_p, vg_last = gmlp_prompt(proj, bp, seq, gm_ws[l], gm_bs[l], gm_ln_g[l], gm_ln_b[l])
        gm_s, vg_s = gmlp_sample(proj, tp, nb, lt, gm_ws[l], gm_bs[l], gm_ln_g[l], gm_ln_b[l])
        outs["p_gv"].append(vg_last.reshape(bp, CHUNK, GM_GROUPS, GM_GROUP_DIM))
        outs["s_gv"].append(jnp.transpose(vg_s.reshape(lt, nb, GM_GROUPS, GM_GROUP_DIM), (1, 0, 2, 3)))

        dtt = jnp.transpose(proj[:tp, COL_DT:COL_DT + SSM_HEADS])
        y_p, st_p = ssd_prompt(proj, dtt, bp, seq, conv_w[l], conv_b[l], dt_bias[l], a_log[l],
                               d_skip[l], ssm_norm_g[l])
        ssm_s, s_states = ssd_sample(proj, tp, nb, lt, state_conv[l], state_ssm, s_states, l, conv_w[l],
                                     conv_b[l], dt_bias[l], a_log[l], d_skip[l], ssm_norm_g[l])
        outs["p_conv"].append(jnp.stack(
            [proj[(i + 1) * seq - (CONV_K - 1):(i + 1) * seq, COL_XS:COL_XS + CONV_DIM] for i in range(bp)]))
        xbc_s = jnp.transpose(proj[tp:, COL_XS:COL_XS + CONV_DIM].reshape(lt, nb, CONV_DIM), (1, 0, 2))
        outs["s_conv"].append(jnp.concatenate([state_conv[l], xbc_s], axis=1)[:, -(CONV_K - 1):])
        outs["p_ssm"].append(st_p.reshape(bp, SSM_HEADS, SSM_HEAD_DIM, SSM_STATE))

        merged = gated_merge(att_p, att_s, gm_p, gm_s, y_p, ssm_s, w_pa[l].astype(BF16),
                             w_pb[l].astype(BF16), w_pc[l].astype(BF16), proj)
        h1f, h1b = matmul_ln(merged, w_o[l].astype(BF16), hf, ln1_g[l], ln1_b[l], alpha=alpha, name="out_proj_ln1")

        qc = matmul(h1b, w_cq[l].astype(BF16), name="xattn_q")
        pmk = matmul(mem_b, w_ck[l].astype(BF16), name="mem_k")
        pmv = matmul(mem_b, w_cv[l].astype(BF16), name="mem_v")
        pmk5 = pmk.reshape(1, bp, mem_len, MEM_HEADS, MEM_HEAD_DIM)
        pmv5 = pmv.reshape(1, bp, mem_len, MEM_HEADS, MEM_HEAD_DIM)
        outs["p_mk"].append(pmk5[0])
        outs["p_mv"].append(pmv5[0])
        tq = _pick(seq, (512, 256, 128))
        o_p = cross_attention(qc, pmk5, pmv5, layer=0, n_seq=bp, seq=seq, tq=tq, bseq=1,
                              name="xattn_prompt")
        qc_s = _to_seq_major(qc[tp:], lt, nb, qpad)
        o_s8 = cross_attention(qc_s, cache_mem_k, cache_mem_v, layer=l, n_seq=nb, seq=qpad, tq=qpad,
                               bseq=_pick(nb, (4, 2, 1)), name="xattn_sample")
        o_all = jnp.concatenate([o_p, _to_time_major(o_s8, lt, nb, qpad)], axis=0).astype(BF16)
        h2f, h2b = matmul_ln(o_all, w_co[l].astype(BF16), h1f, ln2_g[l], ln2_b[l], alpha=alpha, name="xattn_out_ln2")

        hf, hb = hierarchical_moe_ln(h2f, l, w_rg[l], b_rg[l], w_re[l], b_re[l], w_gate, w_up, w_down,
                                     ln3_g[l], ln3_b[l], alpha=alpha,
                                     split_rows=tp if l == depth - 1 else None)

    y_prompt = hf.reshape(bp, seq, d)
    y_sample = jnp.transpose(hb.reshape(lt, nb, d), (1, 0, 2))
    st = lambda k: jnp.stack(outs[k])
    s_ssm = s_states.reshape(depth, nb, SSM_HEADS, SSM_HEAD_DIM, SSM_STATE)
    return (y_prompt, y_sample, st("p_k"), st("p_v"), st("p_mk"), st("p_mv"), st("p_conv"), st("p_ssm"),
            st("p_gv"), st("s_k"), st("s_v"), st("s_conv"), s_ssm, st("s_gv"))
```

```python
import functools
import math

import numpy as np
import jax
import jax.numpy as jnp
from jax import lax
from jax.experimental import pallas as pl
from jax.experimental.pallas import tpu as pltpu

F32 = jnp.float32
BF16 = jnp.bfloat16

D_MODEL = 2048
N_HEADS = 32
N_KV_HEADS = 4
HEAD_DIM = 64
WINDOW = 128
PAST_LEN = 8192
ROPE_THETA = 10000.0
CHUNK = 128
GM_GROUPS = 16
GM_GROUP_DIM = 128
SSM_HEADS = 32
SSM_HEAD_DIM = 64
SSM_GROUPS = 4
SSM_STATE = 128
CONV_K = 4
MEM_HEADS = 4
MEM_HEAD_DIM = 128
N_EGROUPS = 4
EXPERTS_PER_GROUP = 8
N_EXPERTS = N_EGROUPS * EXPERTS_PER_GROUP
EXPERT_FF = D_MODEL // 2
Q_W = N_HEADS * HEAD_DIM
KV_W = N_KV_HEADS * HEAD_DIM
GM_W = GM_GROUPS * GM_GROUP_DIM
SSM_INNER = SSM_HEADS * SSM_HEAD_DIM
SSM_BC = SSM_GROUPS * SSM_STATE
CONV_DIM = SSM_INNER + 2 * SSM_BC
MEM_W = MEM_HEADS * MEM_HEAD_DIM
LN_EPS = 1e-5
NEG_BIG = -1e30

VMEM_LIMIT_BYTES = 52 * 1024 * 1024
LANES = 128
SUBLANES = 8

IN_TN = 512
COL_Q = 0
COL_GU = 2048
COL_GV = 4096
COL_Z = 6144
COL_XS = 8192
COL_BC = 10240
COL_DT = 11264
COL_GATES = COL_DT + SSM_HEADS
COL_K = 17920
COL_V = 18176
PROJ_W = 18432

MOE_TM = 256
DMA_LOOP_UNROLL = 8


def _cparams(n_grid):
    return pltpu.CompilerParams(
        dimension_semantics=("arbitrary",) * n_grid,
        vmem_limit_bytes=VMEM_LIMIT_BYTES,
    )


def _pick(n, prefs):
    for p in prefs:
        if n % p == 0:
            return p
    raise ValueError(f"no tile for {n} in {prefs}")


def _ln_rows(x, g, b):
    mu = jnp.mean(x, axis=-1, keepdims=True)
    xc = x - mu
    var = jnp.mean(xc * xc, axis=-1, keepdims=True)
    return xc * lax.rsqrt(var + LN_EPS) * g + b


def _sigmoid(x):
    return 1.0 / (1.0 + jnp.exp(-x))


def _silu(x):
    return x * _sigmoid(x)


def _softplus(x):
    return jnp.maximum(x, 0.0) + jnp.log1p(jnp.exp(-jnp.abs(x)))


def _gelu(x):
    return jax.nn.gelu(x, approximate=True)


def _split3(x):
    hi = x.astype(BF16)
    r1 = x - hi.astype(F32)
    mid = r1.astype(BF16)
    lo = (r1 - mid.astype(F32)).astype(BF16)
    return hi, mid, lo


def _dot(a, b):
    return jnp.dot(a, b, preferred_element_type=F32)


def _dot_nt(a, b):
    return lax.dot_general(a, b, (((1,), (1,)), ((), ())), preferred_element_type=F32)


def _exact_dot_left(pieces, m):
    acc = _dot(pieces[0], m)
    for p in pieces[1:]:
        acc = acc + _dot(p, m)
    return acc


def _ln_in_kernel(xp_ref, xs_ref, g_ref, b_ref, of_ref, ob_ref, *, n_p):
    i = pl.program_id(0)

    @pl.when(i < n_p)
    def _():
        y = _ln_rows(xp_ref[...], g_ref[...], b_ref[...])
        of_ref[...] = y
        ob_ref[...] = y.astype(BF16)

    @pl.when(i >= n_p)
    def _():
        y = _ln_rows(xs_ref[...], g_ref[...], b_ref[...])
        of_ref[...] = y
        ob_ref[...] = y.astype(BF16)


def ln_in(xp, xs, g, b):
    tp, d = xp.shape
    ts = xs.shape[0]
    tm = _pick(math.gcd(tp, ts), (256, 128, 64, 32, 16, 8))
    n_p, n_s = tp // tm, ts // tm
    t = tp + ts
    return pl.pallas_call(
        functools.partial(_ln_in_kernel, n_p=n_p),
        grid=(n_p + n_s,),
        in_specs=[
            pl.BlockSpec((tm, d), lambda i: (jnp.minimum(i, n_p - 1), 0)),
            pl.BlockSpec((tm, d), lambda i: (jnp.maximum(i - n_p, 0), 0)),
            pl.BlockSpec((1, d), lambda i: (0, 0)),
            pl.BlockSpec((1, d), lambda i: (0, 0)),
        ],
        out_specs=[
            pl.BlockSpec((tm, d), lambda i: (i, 0)),
            pl.BlockSpec((tm, d), lambda i: (i, 0)),
        ],
        out_shape=[jax.ShapeDtypeStruct((t, d), F32), jax.ShapeDtypeStruct((t, d), BF16)],
        compiler_params=_cparams(1),
        name="ln_in",
    )(xp, xs, g.reshape(1, d), b.reshape(1, d))


def _mm_kernel(x_ref, w_ref, o_ref):
    o_ref[...] = _dot(x_ref[...], w_ref[...]).astype(o_ref.dtype)


def matmul(x, w, *, out_dtype=F32, tm_prefs=(1088, 1024, 512, 256, 128, 64, 32, 16, 8),
           tn_prefs=(1280, 1024, 512, 256, 128), name="mm"):
    t, k = x.shape
    n = w.shape[1]
    tm = _pick(t, tm_prefs)
    tn = _pick(n, tn_prefs)
    return pl.pallas_call(
        _mm_kernel,
        grid=(n // tn, t // tm),
        in_specs=[
            pl.BlockSpec((tm, k), lambda j, i: (i, 0)),
            pl.BlockSpec((k, tn), lambda j, i: (0, j)),
        ],
        out_specs=pl.BlockSpec((tm, tn), lambda j, i: (i, j)),
        out_shape=jax.ShapeDtypeStruct((t, n), out_dtype),
        compiler_params=_cparams(2),
        name=name,
    )(x, w)


def _in_proj_kernel(x_ref, w_ref, wt_ref, o_ref, *, n_main):
    j = pl.program_id(1)

    @pl.when(j < n_main)
    def _():
        o_ref[...] = _dot_nt(x_ref[...], w_ref[...].astype(BF16))

    @pl.when(j >= n_main)
    def _():
        o_ref[...] = _dot_nt(x_ref[...], wt_ref[...].astype(BF16))


def in_projection(x, w_t, w_tail_t, layer):
    t, k = x.shape
    n_blk = PROJ_W // IN_TN
    n_main = w_t.shape[1] // IN_TN
    kv_blk = Q_W // IN_TN
    assert 2 * KV_W == IN_TN and COL_K == (n_blk - 1) * IN_TN and n_main == n_blk - 1
    tm = _pick(t, (2176, 1088, 544, 32, 16, 8))

    def out_map(i, j):
        return i, jnp.where(j < kv_blk, j, jnp.where(j == kv_blk, n_blk - 1, j - 1))

    return pl.pallas_call(
        functools.partial(_in_proj_kernel, n_main=n_main),
        grid=(t // tm, n_blk),
        in_specs=[
            pl.BlockSpec((tm, k), lambda i, j: (i, 0)),
            pl.BlockSpec((None, IN_TN, k), lambda i, j: (layer, jnp.minimum(j, n_main - 1), 0)),
            pl.BlockSpec((None, IN_TN, k), lambda i, j: (layer, 0, 0)),
        ],
        out_specs=pl.BlockSpec((tm, IN_TN), out_map),
        out_shape=jax.ShapeDtypeStruct((t, PROJ_W), F32),
        compiler_params=_cparams(2),
        name="in_proj",
    )(x, w_t, w_tail_t)


def _mm_ln_kernel(x_ref, w_ref, r_ref, g_ref, b_ref, of_ref, ob_ref, *, alpha):
    y = _dot(x_ref[...], w_ref[...])
    h = _ln_rows(alpha * r_ref[...] + y, g_ref[...], b_ref[...])
    of_ref[...] = h
    ob_ref[...] = h.astype(BF16)


def matmul_ln(x, w, res, g, b, *, alpha, name="mm_ln"):
    t, k = x.shape
    d = w.shape[1]
    tm = _pick(t, (256, 128, 64, 32, 16, 8))
    return pl.pallas_call(
        functools.partial(_mm_ln_kernel, alpha=alpha),
        grid=(t // tm,),
        in_specs=[
            pl.BlockSpec((tm, k), lambda i: (i, 0)),
            pl.BlockSpec((k, d), lambda i: (0, 0)),
            pl.BlockSpec((tm, d), lambda i: (i, 0)),
            pl.BlockSpec((1, d), lambda i: (0, 0)),
            pl.BlockSpec((1, d), lambda i: (0, 0)),
        ],
        out_specs=[
            pl.BlockSpec((tm, d), lambda i: (i, 0)),
            pl.BlockSpec((tm, d), lambda i: (i, 0)),
        ],
        out_shape=[jax.ShapeDtypeStruct((t, d), F32), jax.ShapeDtypeStruct((t, d), BF16)],
        compiler_params=_cparams(1),
        name=name,
    )(x, w, res, g.reshape(1, d), b.reshape(1, d))


def _merge_kernel(ap_ref, as_ref, bp_ref, bs_ref, cp_ref, cs_ref, wa_ref, wb_ref, wc_ref,
                  ga_ref, ga2_ref, gb_ref, gb2_ref, gc_ref, gc2_ref, o_ref, *, n_p, shift):
    i = pl.program_id(1)
    is_p = i < n_p
    tn = o_ref.shape[1]

    def gate(main_ref, tail_ref):
        win = jnp.concatenate([main_ref[...], tail_ref[...]], axis=1)
        return _sigmoid(win[:, shift:shift + tn])

    xa = jnp.where(is_p, ap_ref[...], as_ref[...])
    xb = jnp.where(is_p, bp_ref[...], bs_ref[...])
    xc = jnp.where(is_p, cp_ref[...], cs_ref[...])
    acc = gate(ga_ref, ga2_ref) * _dot(xa, wa_ref[...])
    acc = acc + gate(gb_ref, gb2_ref) * _dot(xb, wb_ref[...])
    acc = acc + gate(gc_ref, gc2_ref) * _dot(xc, wc_ref[...])
    o_ref[...] = acc.astype(o_ref.dtype)


def gated_merge(att_p, att_s, gm_p, gm_s, ssm_p, ssm_s, w_pa, w_pb, w_pc, proj):
    tp, k = att_p.shape
    ts = att_s.shape[0]
    d = w_pa.shape[1]
    tm = _pick(math.gcd(tp, ts), (256, 128, 64, 32, 16, 8))
    tn = 1024
    n_p, n_s = tp // tm, ts // tm
    shift = COL_GATES - COL_DT
    assert COL_DT % tn == 0 and d % tn == 0 and shift < LANES

    def xp_spec():
        return pl.BlockSpec((tm, k), lambda j, i: (jnp.minimum(i, n_p - 1), 0))

    def xs_spec():
        return pl.BlockSpec((tm, k), lambda j, i: (jnp.maximum(i - n_p, 0), 0))

    def w_spec():
        return pl.BlockSpec((k, tn), lambda j, i: (0, j))

    def g_specs(which):
        base = (COL_DT + which * d) // tn
        tail = (COL_DT + which * d) // LANES
        return [pl.BlockSpec((tm, tn), lambda j, i: (i, base + j)),
                pl.BlockSpec((tm, LANES), lambda j, i: (i, tail + (j + 1) * (tn // LANES)))]

    return pl.pallas_call(
        functools.partial(_merge_kernel, n_p=n_p, shift=shift),
        grid=(d // tn, n_p + n_s),
        in_specs=[xp_spec(), xs_spec(), xp_spec(), xs_spec(), xp_spec(), xs_spec(),
                  w_spec(), w_spec(), w_spec()] + g_specs(0) + g_specs(1) + g_specs(2),
        out_specs=pl.BlockSpec((tm, tn), lambda j, i: (i, j)),
        out_shape=jax.ShapeDtypeStruct((tp + ts, d), BF16),
        compiler_params=_cparams(2),
        name="gated_merge",
    )(att_p, att_s, gm_p, gm_s, ssm_p, ssm_s, w_pa, w_pb, w_pc, *([proj] * 6))


def _rope_block(x, cos, sin_signed, first_half):
    outs = []
    for c in range(x.shape[1] // LANES):
        xc = x[:, c * LANES:(c + 1) * LANES]
        fwd = pltpu.roll(xc, LANES - HEAD_DIM // 2, axis=1)
        bwd = pltpu.roll(xc, HEAD_DIM // 2, axis=1)
        partner = jnp.where(first_half, fwd, bwd)
        outs.append(xc * cos + partner * sin_signed)
    return outs


def _rope_kernel(q_ref, k_ref, cos_ref, sin_ref, qo_ref, ko_ref):
    cos = cos_ref[...]
    sin_signed = sin_ref[...]
    lane = lax.broadcasted_iota(jnp.int32, cos.shape, 1)
    first_half = (lane % HEAD_DIM) < (HEAD_DIM // 2)
    for c, o in enumerate(_rope_block(q_ref[...], cos, sin_signed, first_half)):
        qo_ref[:, c * LANES:(c + 1) * LANES] = o
    for c, o in enumerate(_rope_block(k_ref[...], cos, sin_signed, first_half)):
        ko_ref[:, c * LANES:(c + 1) * LANES] = o


def rope_qk(proj, cos_t, sin_t):
    t = proj.shape[0]
    tm = _pick(t, (256, 128, 64, 32, 16, 8))
    return pl.pallas_call(
        _rope_kernel,
        grid=(t // tm,),
        in_specs=[
            pl.BlockSpec((tm, Q_W), lambda i: (i, COL_Q // Q_W)),
            pl.BlockSpec((tm, KV_W), lambda i: (i, COL_K // KV_W)),
            pl.BlockSpec((tm, LANES), lambda i: (i, 0)),
            pl.BlockSpec((tm, LANES), lambda i: (i, 0)),
        ],
        out_specs=[
            pl.BlockSpec((tm, Q_W), lambda i: (i, 0)),
            pl.BlockSpec((tm, KV_W), lambda i: (i, 0)),
        ],
        out_shape=[jax.ShapeDtypeStruct((t, Q_W), F32), jax.ShapeDtypeStruct((t, KV_W), F32)],
        compiler_params=_cparams(1),
        name="rope_qk",
    )(proj, proj, cos_t, sin_t)


def rope_tables(tp, seq, ts, nb, past_len):
    half = HEAD_DIM // 2
    inv = ROPE_THETA ** (-jnp.arange(half, dtype=F32) / half)
    pos_p = jnp.arange(tp, dtype=jnp.int32) % seq
    pos_s = past_len + jnp.arange(ts, dtype=jnp.int32) // nb
    pos = jnp.concatenate([pos_p, pos_s]).astype(F32)
    ang = pos[:, None] * inv[None, :]
    cos = jnp.tile(jnp.cos(ang), (1, LANES // half))
    sin = jnp.sin(ang)
    sin_signed = jnp.tile(jnp.concatenate([-sin, sin], axis=1), (1, LANES // HEAD_DIM))
    return cos, sin_signed


def _dup_head(slab, g):
    lane = lax.broadcasted_iota(jnp.int32, slab.shape, 1)
    rolled = pltpu.roll(slab, HEAD_DIM, axis=1)
    if g % 2 == 0:
        return jnp.where(lane < HEAD_DIM, slab, rolled)
    return jnp.where(lane < HEAD_DIM, rolled, slab)


def _swa_bias(qb, prev_ok):
    kw = WINDOW + qb
    ii = lax.broadcasted_iota(jnp.int32, (qb, kw), 0)
    jj = lax.broadcasted_iota(jnp.int32, (qb, kw), 1)
    ok_prev = (jj < WINDOW) & (jj > ii)
    if prev_ok is not True:
        ok_prev = ok_prev & prev_ok
    ok = ok_prev | ((jj >= WINDOW) & ((jj - WINDOW) <= ii))
    return jnp.where(ok, 0.0, NEG_BIG)


def _swa_group(q, kdup, vdup, bias, sink_ref, g, qb, store):
    rep = N_HEADS // N_KV_HEADS
    lane = lax.broadcasted_iota(jnp.int32, (qb, LANES), 1)
    scale = HEAD_DIM ** -0.5
    rows = []
    for r in range(rep):
        h = g * rep + r
        qc = q[:, (h // 2) * LANES:(h // 2 + 1) * LANES]
        keep = (lane < HEAD_DIM) if h % 2 == 0 else (lane >= HEAD_DIM)
        rows.append(jnp.where(keep, qc * scale, 0.0))
    s = _dot_nt(jnp.concatenate(rows, axis=0).astype(BF16), kdup)
    ps = []
    for r in range(rep):
        sr = s[r * qb:(r + 1) * qb] + bias
        sink = sink_ref[g * rep + r]
        m = jnp.maximum(jnp.max(sr, axis=-1, keepdims=True), sink)
        p = jnp.exp(sr - m)
        den = jnp.sum(p, axis=-1, keepdims=True) + jnp.exp(sink - m)
        ps.append(p * (1.0 / den))
    o = _dot(jnp.concatenate(ps, axis=0).astype(BF16), vdup)
    for c in range(rep // 2):
        oa = o[(2 * c) * qb:(2 * c + 1) * qb]
        ob = o[(2 * c + 1) * qb:(2 * c + 2) * qb]
        store((g * rep // 2 + c) * LANES, jnp.where(lane < HEAD_DIM, oa, ob))


def _swa_kernel(sink_ref, q_ref, kp_ref, kc_ref, vp_ref, vc_ref, o_ref, *, qb, prev_from_block0):
    n = pl.program_id(1)
    q = q_ref[...]
    kp, kc, vp, vc = kp_ref[...], kc_ref[...], vp_ref[...], vc_ref[...]
    bias = _swa_bias(qb, True if prev_from_block0 else (n > 0))

    def store(col, val):
        o_ref[:, col:col + LANES] = val.astype(o_ref.dtype)

    for g in range(N_KV_HEADS):
        sl = slice((g // 2) * LANES, (g // 2 + 1) * LANES)
        kdup = _dup_head(jnp.concatenate([kp[:, sl], kc[:, sl]], axis=0), g).astype(BF16)
        vdup = _dup_head(jnp.concatenate([vp[:, sl], vc[:, sl]], axis=0), g).astype(BF16)
        _swa_group(q, kdup, vdup, bias, sink_ref, g, qb, store)


def _swa_cache_kernel(sink_ref, q_ref, kc_ref, vc_ref, kp_ref, vp_ref, o_ref, *, qb, bseq):
    bias = _swa_bias(qb, True)
    for b in range(bseq):
        rows = slice(b * qb, (b + 1) * qb)
        q = q_ref[rows, :]

        def store(col, val, rows=rows):
            o_ref[rows, col:col + LANES] = val.astype(o_ref.dtype)

        for g in range(N_KV_HEADS):
            hs = slice(g * HEAD_DIM, (g + 1) * HEAD_DIM)
            kcat = jnp.concatenate([kp_ref[b, :, g, :], kc_ref[rows, hs]], axis=0)
            vcat = jnp.concatenate([vp_ref[b, :, g, :], vc_ref[rows, hs]], axis=0)
            kdup = jnp.concatenate([kcat, kcat], axis=1).astype(BF16)
            vdup = jnp.concatenate([vcat, vcat], axis=1).astype(BF16)
            _swa_group(q, kdup, vdup, bias, sink_ref, g, qb, store)


def swa_attention_cached(sinks, q, kc, vc, cache_k, cache_v, *, layer, n_seq, qb, bseq, name):
    rows = bseq * qb
    cache_spec = pl.BlockSpec((None, bseq, WINDOW, N_KV_HEADS, HEAD_DIM), lambda i: (layer, i, 0, 0, 0))
    return pl.pallas_call(
        functools.partial(_swa_cache_kernel, qb=qb, bseq=bseq),
        grid=(n_seq // bseq,),
        in_specs=[pl.BlockSpec(memory_space=pltpu.SMEM),
                  pl.BlockSpec((rows, Q_W), lambda i: (i, 0)),
                  pl.BlockSpec((rows, KV_W), lambda i: (i, 0)),
                  pl.BlockSpec((rows, KV_W), lambda i: (i, 0)),
                  cache_spec, cache_spec],
        out_specs=pl.BlockSpec((rows, Q_W), lambda i: (i, 0)),
        out_shape=jax.ShapeDtypeStruct((n_seq * qb, Q_W), F32),
        compiler_params=_cparams(1),
        name=name,
    )(sinks, q, kc, vc, cache_k, cache_v)


def swa_attention(sinks, q, kp, kc, vp, vc, *, n_seq, n_blk, qb, prev_from_block0,
                  kp_map, kc_map, vp_map, vc_map, out_dtype, name):
    return pl.pallas_call(
        functools.partial(_swa_kernel, qb=qb, prev_from_block0=prev_from_block0),
        grid=(n_seq, n_blk),
        in_specs=[
            pl.BlockSpec(memory_space=pltpu.SMEM),
            pl.BlockSpec((qb, Q_W), lambda i, n: (i * n_blk + n, 0)),
            pl.BlockSpec((WINDOW, KV_W), kp_map),
            pl.BlockSpec((qb, KV_W), kc_map),
            pl.BlockSpec((WINDOW, KV_W), vp_map),
            pl.BlockSpec((qb, KV_W), vc_map),
        ],
        out_specs=pl.BlockSpec((qb, Q_W), lambda i, n: (i * n_blk + n, 0)),
        out_shape=jax.ShapeDtypeStruct((n_seq * n_blk * qb, Q_W), out_dtype),
        compiler_params=_cparams(2),
        name=name,
    )(sinks, q, kp, kc, vp, vc)


def _gmlp_p_kernel(gu_ref, gv_ref, ws_ref, bst_ref, lg_ref, lb_ref, gm_ref, vg_ref, *, n_chunks):
    n = pl.program_id(1)
    vg = _ln_rows(_gelu(gv_ref[...]), lg_ref[...], lb_ref[...])
    gu = gu_ref[...]
    ri = lax.broadcasted_iota(jnp.int32, (CHUNK, CHUNK), 0)
    ci = lax.broadcasted_iota(jnp.int32, (CHUNK, CHUNK), 1)
    tril = ri >= ci
    bst = bst_ref[...]
    for g in range(GM_GROUPS):
        sl = slice(g * GM_GROUP_DIM, (g + 1) * GM_GROUP_DIM)
        w = jnp.where(tril, ws_ref[g], 0.0).astype(BF16)
        s = _dot(w, vg[:, sl].astype(BF16)) + bst[:, g:g + 1]
        gm_ref[:, sl] = (_gelu(gu[:, sl]) * s).astype(gm_ref.dtype)

    @pl.when(n == n_chunks - 1)
    def _():
        vg_ref[...] = vg


def gmlp_prompt(proj, n_seq, seq, ws, bs, ln_g, ln_b):
    nc = seq // CHUNK
    return pl.pallas_call(
        functools.partial(_gmlp_p_kernel, n_chunks=nc),
        grid=(n_seq, nc),
        in_specs=[
            pl.BlockSpec((CHUNK, GM_W), lambda i, n: (i * nc + n, COL_GU // GM_W)),
            pl.BlockSpec((CHUNK, GM_W), lambda i, n: (i * nc + n, COL_GV // GM_W)),
            pl.BlockSpec((GM_GROUPS, CHUNK, CHUNK), lambda i, n: (0, 0, 0)),
            pl.BlockSpec((CHUNK, GM_GROUPS), lambda i, n: (0, 0)),
            pl.BlockSpec((1, GM_W), lambda i, n: (0, 0)),
            pl.BlockSpec((1, GM_W), lambda i, n: (0, 0)),
        ],
        out_specs=[
            pl.BlockSpec((CHUNK, GM_W), lambda i, n: (i * nc + n, 0)),
            pl.BlockSpec((CHUNK, GM_W), lambda i, n: (i, 0)),
        ],
        out_shape=[jax.ShapeDtypeStruct((n_seq * seq, GM_W), BF16),
                   jax.ShapeDtypeStruct((n_seq * CHUNK, GM_W), F32)],
        compiler_params=_cparams(2),
        name="gmlp_prompt",
    )(proj, proj, ws, bs.T, ln_g.reshape(1, GM_W), ln_b.reshape(1, GM_W))


def _gmlp_s_kernel(*refs, lt):
    gu_refs = refs[:lt]
    gv_refs = refs[lt:2 * lt]
    wrow_ref, brow_ref, lg_ref, lb_ref, gm_ref, vg_ref = refs[2 * lt:]
    nb = gu_refs[0].shape[0]
    vgs = [_ln_rows(_gelu(gv_refs[t][...]), lg_ref[...], lb_ref[...]) for t in range(lt)]
    for i in range(lt):
        s = brow_ref[i:i + 1, :]
        for j in range(i + 1):
            s = s + wrow_ref[i * lt + j:i * lt + j + 1, :] * vgs[j]
        gm_ref[i * nb:(i + 1) * nb, :] = (_gelu(gu_refs[i][...]) * s).astype(gm_ref.dtype)
        vg_ref[i * nb:(i + 1) * nb, :] = vgs[i]


def gmlp_sample(proj, tp, nb, lt, ws, bs, ln_g, ln_b):
    w_small = ws[:, :lt, :lt]
    wrow = jnp.repeat(jnp.transpose(w_small, (1, 2, 0)).reshape(lt * lt, GM_GROUPS), GM_GROUP_DIM, axis=1)
    brow = jnp.repeat(bs[:, :lt].T, GM_GROUP_DIM, axis=1)
    row0 = tp // nb

    def spec(t, col):
        return pl.BlockSpec((nb, GM_W), lambda i: (row0 + t, col // GM_W))

    in_specs = [spec(t, COL_GU) for t in range(lt)] + [spec(t, COL_GV) for t in range(lt)] + [
        pl.BlockSpec((lt * lt, GM_W), lambda i: (0, 0)),
        pl.BlockSpec((lt, GM_W), lambda i: (0, 0)),
        pl.BlockSpec((1, GM_W), lambda i: (0, 0)),
        pl.BlockSpec((1, GM_W), lambda i: (0, 0)),
    ]
    return pl.pallas_call(
        functools.partial(_gmlp_s_kernel, lt=lt),
        grid=(1,),
        in_specs=in_specs,
        out_specs=[pl.BlockSpec((lt * nb, GM_W), lambda i: (0, 0)),
                   pl.BlockSpec((lt * nb, GM_W), lambda i: (0, 0))],
        out_shape=[jax.ShapeDtypeStruct((lt * nb, GM_W), BF16),
                   jax.ShapeDtypeStruct((lt * nb, GM_W), F32)],
        compiler_params=_cparams(1),
        name="gmlp_sample",
    )(*([proj] * (2 * lt)), wrow, brow, ln_g.reshape(1, GM_W), ln_b.reshape(1, GM_W))


def _conv_silu(cur, prev8, w, bias):
    q = cur.shape[0]
    up = jnp.concatenate([prev8, cur], axis=0)
    acc = bias + up[SUBLANES:SUBLANES + q] * w[CONV_K - 1:CONV_K]
    for j in range(CONV_K - 1):
        off = SUBLANES - (CONV_K - 1) + j
        acc = acc + up[off:off + q] * w[j:j + 1]
    return _silu(acc)


def _ssd_p_kernel(z_ref, xs_ref, bc_ref, xsp_ref, bcp_ref, dt_ref, dtt_ref,
                  cwx_ref, cwbc_ref, cbx_ref, cbbc_ref, dtbe_ref, dtbc_ref, aloge_ref, alogc_ref,
                  dske_ref, ng_ref, e_ref, y_ref, st_ref, s_scr, *, n_chunks):
    c = pl.program_id(1)
    q = CHUNK
    rep = SSM_HEADS // SSM_GROUPS
    gw = rep * SSM_HEAD_DIM

    @pl.when(c == 0)
    def _():
        s_scr[...] = jnp.zeros_like(s_scr)

    has_prev = (c > 0).astype(F32)
    xs = _conv_silu(xs_ref[...], xsp_ref[...] * has_prev, cwx_ref[...], cbx_ref[...])
    bcm = _conv_silu(bc_ref[...], bcp_ref[...] * has_prev, cwbc_ref[...], cbbc_ref[...])

    ri = lax.broadcasted_iota(jnp.int32, (q, q), 0)
    ci = lax.broadcasted_iota(jnp.int32, (q, q), 1)
    tril = ri >= ci
    ones_tril = jnp.where(tril, 1.0, 0.0).astype(BF16)
    ones_triu = jnp.where(ri <= ci, 1.0, 0.0).astype(BF16)

    dt_e = _softplus(_exact_dot_left(_split3(dt_ref[...]), e_ref[...]) + dtbe_ref[...])
    a_e = dt_e * (-jnp.exp(aloge_ref[...]))
    a_hi, a_mid, a_lo = _split3(a_e)
    cum_e = _dot(ones_tril, a_hi) + _dot(ones_tril, a_mid) + _dot(ones_tril, a_lo)
    dt_t = _softplus(dtt_ref[...] + dtbc_ref[...])
    a_t = dt_t * (-jnp.exp(alogc_ref[...]))
    cum_t = _exact_dot_left(_split3(a_t), ones_triu)

    xdt = xs * dt_e
    cum_last = cum_e[q - 1:q, :]
    lane = lax.broadcasted_iota(jnp.int32, (q, LANES), 1)
    z = z_ref[...]
    ys = []
    for g in range(SSM_GROUPS):
        gs = slice(g * gw, (g + 1) * gw)
        bg = bcm[:, g * SSM_STATE:(g + 1) * SSM_STATE]
        cg = bcm[:, SSM_BC + g * SSM_STATE:SSM_BC + (g + 1) * SSM_STATE]
        bg16, cg16 = bg.astype(BF16), cg.astype(BF16)
        cb = _dot_nt(cg16, bg16)
        ydiag = []
        for pr in range(rep // 2):
            ms = []
            for hh in (2 * pr, 2 * pr + 1):
                h = g * rep + hh
                col = cum_e[:, h * SSM_HEAD_DIM:h * SSM_HEAD_DIM + 1]
                row = cum_t[h:h + 1, :]
                seg = jnp.where(tril, col - row, NEG_BIG)
                ms.append(cb * jnp.exp(seg))
            lhs = jnp.concatenate(ms, axis=1).astype(BF16)
            xslab = xdt[:, (g * rep + 2 * pr) * SSM_HEAD_DIM:(g * rep + 2 * pr + 2) * SSM_HEAD_DIM]
            xbd = jnp.concatenate([jnp.where(lane < SSM_HEAD_DIM, xslab, 0.0),
                                   jnp.where(lane >= SSM_HEAD_DIM, xslab, 0.0)], axis=0).astype(BF16)
            ydiag.append(_dot(lhs, xbd))
        ydiag = jnp.concatenate(ydiag, axis=1)
        s_old = s_scr[g]
        yoff = _dot(cg16, s_old.astype(BF16)) * jnp.exp(cum_e[:, gs])
        ys.append(ydiag + yoff)
        xw = xdt[:, gs] * jnp.exp(cum_last[:, gs] - cum_e[:, gs])
        s_scr[g] = s_old * jnp.exp(cum_last[:, gs]) + _dot(bg.T.astype(BF16), xw.astype(BF16))
    y = jnp.concatenate(ys, axis=1) + dske_ref[...] * xs
    gated = y * _silu(z)
    out = gated * lax.rsqrt(jnp.mean(gated * gated, axis=-1, keepdims=True) + LN_EPS) * ng_ref[...]
    y_ref[...] = out.astype(y_ref.dtype)

    @pl.when(c == n_chunks - 1)
    def _():
        for g in range(SSM_GROUPS):
            st_ref[0, g * gw:(g + 1) * gw, :] = s_scr[g].T


def _expand_heads(v):
    return jnp.repeat(v.astype(F32), SSM_HEAD_DIM).reshape(1, SSM_INNER)


def _head_expand_matrix():
    e = np.zeros((LANES, SSM_INNER), np.float32)
    for h in range(SSM_HEADS):
        e[h, h * SSM_HEAD_DIM:(h + 1) * SSM_HEAD_DIM] = 1.0
    return jnp.asarray(e, BF16)


def ssd_prompt(proj, dtt, n_seq, seq, conv_w, conv_b, dt_bias, a_log, d_skip, norm_g):
    b = n_seq
    nc = seq // CHUNK
    blk8 = CHUNK // SUBLANES
    const2 = lambda i, c: (0, 0)
    col_bc = COL_BC // (2 * SSM_BC)
    col_xs = COL_XS // SSM_INNER
    col_z = COL_Z // SSM_INNER
    prev_map_x = lambda i, c: (jnp.maximum((i * nc + c) * blk8 - 1, 0), col_xs)
    prev_map_bc = lambda i, c: (jnp.maximum((i * nc + c) * blk8 - 1, 0), col_bc)
    args = (
        proj, proj, proj, proj, proj, proj, dtt,
        conv_w[:, :SSM_INNER], conv_w[:, SSM_INNER:], conv_b[:SSM_INNER].reshape(1, -1),
        conv_b[SSM_INNER:].reshape(1, -1),
        _expand_heads(dt_bias), jnp.broadcast_to(dt_bias.astype(F32)[:, None], (SSM_HEADS, CHUNK)),
        _expand_heads(a_log), jnp.broadcast_to(a_log.astype(F32)[:, None], (SSM_HEADS, CHUNK)),
        _expand_heads(d_skip), norm_g.reshape(1, SSM_INNER), _head_expand_matrix(),
    )
    in_specs = [
        pl.BlockSpec((CHUNK, SSM_INNER), lambda i, c: (i * nc + c, col_z)),
        pl.BlockSpec((CHUNK, SSM_INNER), lambda i, c: (i * nc + c, col_xs)),
        pl.BlockSpec((CHUNK, 2 * SSM_BC), lambda i, c: (i * nc + c, col_bc)),
        pl.BlockSpec((SUBLANES, SSM_INNER), prev_map_x),
        pl.BlockSpec((SUBLANES, 2 * SSM_BC), prev_map_bc),
        pl.BlockSpec((CHUNK, LANES), lambda i, c: (i * nc + c, COL_DT // LANES)),
        pl.BlockSpec((SSM_HEADS, CHUNK), lambda i, c: (0, i * nc + c)),
        pl.BlockSpec((CONV_K, SSM_INNER), const2),
        pl.BlockSpec((CONV_K, 2 * SSM_BC), const2),
        pl.BlockSpec((1, SSM_INNER), const2),
        pl.BlockSpec((1, 2 * SSM_BC), const2),
        pl.BlockSpec((1, SSM_INNER), const2),
        pl.BlockSpec((SSM_HEADS, CHUNK), const2),
        pl.BlockSpec((1, SSM_INNER), const2),
        pl.BlockSpec((SSM_HEADS, CHUNK), const2),
        pl.BlockSpec((1, SSM_INNER), const2),
        pl.BlockSpec((1, SSM_INNER), const2),
        pl.BlockSpec((LANES, SSM_INNER), const2),
    ]
    return pl.pallas_call(
        functools.partial(_ssd_p_kernel, n_chunks=nc),
        grid=(b, nc),
        in_specs=in_specs,
        out_specs=[
            pl.BlockSpec((CHUNK, SSM_INNER), lambda i, c: (i * nc + c, 0)),
            pl.BlockSpec((1, SSM_INNER, SSM_STATE), lambda i, c: (i, 0, 0)),
        ],
        out_shape=[jax.ShapeDtypeStruct((b * seq, SSM_INNER), BF16),
                   jax.ShapeDtypeStruct((b, SSM_INNER, SSM_STATE), F32)],
        scratch_shapes=[pltpu.VMEM((SSM_GROUPS, SSM_STATE, SSM_INNER // SSM_GROUPS), F32)],
        compiler_params=_cparams(2),
        name="ssd_prompt",
    )(*args)


def _group_expand_matrix():
    gw = SSM_INNER // SSM_GROUPS
    m = np.zeros((SSM_BC, SSM_INNER), np.float32)
    for g in range(SSM_GROUPS):
        m[g * SSM_STATE:(g + 1) * SSM_STATE, g * gw:(g + 1) * gw] = 1.0
    return jnp.asarray(m, BF16)


def _ssd_s_pre_kernel(*refs, lt):
    xs_refs = refs[:lt]
    bc_refs = refs[lt:2 * lt]
    dt_refs = refs[2 * lt:3 * lt]
    (cx_ref, cbc_ref, cwx_ref, cwbc_ref, cbx_ref, cbbc_ref, dtbe_ref, aloge_ref, dske_ref, e_ref,
     gmat_ref, c_ref, b_ref, xw_ref, dec_ref, yd_ref, ec_ref) = refs[3 * lt:]
    nprev = CONV_K - 1
    ux = [cx_ref[j] for j in range(nprev)] + [r[...] for r in xs_refs]
    ub = [cbc_ref[j] for j in range(nprev)] + [r[...] for r in bc_refs]
    cwx, cwbc = cwx_ref[...], cwbc_ref[...]
    neg_a = -jnp.exp(aloge_ref[...])
    xs, bm, cm, xdt, cum = [], [], [], [], []
    run = None
    for t in range(lt):
        ax = cbx_ref[...]
        ab = cbbc_ref[...]
        for j in range(CONV_K):
            ax = ax + ux[t + j] * cwx[j:j + 1]
            ab = ab + ub[t + j] * cwbc[j:j + 1]
        x_t = _silu(ax)
        bc_t = _silu(ab)
        dt_e = _softplus(_exact_dot_left(_split3(dt_refs[t][...]), e_ref[...]) + dtbe_ref[...])
        a_t = dt_e * neg_a
        run = a_t if run is None else run + a_t
        xs.append(x_t)
        bm.append(bc_t[:, :SSM_BC])
        cm.append(bc_t[:, SSM_BC:])
        xdt.append(x_t * dt_e)
        cum.append(run)
    for i in range(lt):
        yd = dske_ref[...] * xs[i]
        for j in range(i + 1):
            hi, mid, _ = _split3(cm[i] * bm[j])
            cbe = _dot(hi, gmat_ref[...]) + _dot(mid, gmat_ref[...])
            yd = yd + cbe * jnp.exp(cum[i] - cum[j]) * xdt[j]
        yd_ref[i] = yd
        ec_ref[i] = jnp.exp(cum[i])
        c_ref[i] = cm[i]
        b_ref[i] = bm[i]
        xw_ref[i] = xdt[i] * jnp.exp(cum[lt - 1] - cum[i])
    dec_ref[...] = jnp.exp(cum[lt - 1])


def _rows_block(rows, total):
    c = rows[0].shape[1]
    rid = lax.broadcasted_iota(jnp.int32, (SUBLANES, c), 0)
    acc = jnp.zeros((SUBLANES, c), F32)
    for j, r in enumerate(rows):
        acc = jnp.where(rid == j, jnp.broadcast_to(r, (SUBLANES, c)), acc)
    if total == SUBLANES:
        return acc
    return jnp.concatenate([acc, jnp.zeros((total - SUBLANES, c), F32)], axis=0)


def _ssd_s_state_kernel_inplace(c_ref, b_ref, xw_ref, dec_ref, h0_ref, prev_ref, hn_ref, yr_ref, *, lt):
    del prev_ref
    _ssd_s_state_kernel(c_ref, b_ref, xw_ref, dec_ref, h0_ref, hn_ref, yr_ref, lt=lt)


def _ssd_s_state_kernel(c_ref, b_ref, xw_ref, dec_ref, h0_ref, hn_ref, yr_ref, *, lt, slot=0,
                        fill_slots=None):
    b = pl.program_id(0)
    gw = SSM_INNER // SSM_GROUPS
    c8 = _rows_block([c_ref[i, pl.ds(b, 1), :] for i in range(lt)], SUBLANES).astype(BF16)
    b128 = _rows_block([b_ref[i, pl.ds(b, 1), :] for i in range(lt)], LANES).astype(BF16)
    xaug = _rows_block([xw_ref[i, pl.ds(b, 1), :] for i in range(lt)] + [dec_ref[pl.ds(b, 1), :]], LANES)
    for g in range(SSM_GROUPS):
        hg = h0_ref[0, g * gw:(g + 1) * gw, :]
        yraw = _dot_nt(c8[:, g * SSM_STATE:(g + 1) * SSM_STATE], hg.astype(BF16))
        for i in range(lt):
            yr_ref[i, pl.ds(b, 1), g * gw:(g + 1) * gw] = yraw[i:i + 1, :]
        tr = xaug[:, g * gw:(g + 1) * gw].T
        s = _dot(tr.astype(BF16), b128[:, g * SSM_STATE:(g + 1) * SSM_STATE])
        new = hg * tr[:, lt:lt + 1] + s
        if fill_slots is None:
            hn_ref[0, g * gw:(g + 1) * gw, :] = new
        else:
            hn_ref[slot, 0, g * gw:(g + 1) * gw, :] = new
    if fill_slots is not None:
        for other in fill_slots:
            hn_ref[other] = jnp.zeros(hn_ref.shape[1:], hn_ref.dtype)


def _ssd_s_post_kernel(*refs, lt):
    z_refs = refs[:lt]
    yd_ref, ec_ref, yr_ref, ng_ref, o_ref = refs[lt:]
    nb = z_refs[0].shape[0]
    for i in range(lt):
        y = yd_ref[i] + ec_ref[i] * yr_ref[i]
        gated = y * _silu(z_refs[i][...])
        out = gated * lax.rsqrt(jnp.mean(gated * gated, axis=-1, keepdims=True) + LN_EPS) * ng_ref[...]
        o_ref[i * nb:(i + 1) * nb, :] = out.astype(o_ref.dtype)


def ssd_sample(proj, tp, nb, lt, conv_state, state_all, new_states, layer, conv_w, conv_b, dt_bias,
               a_log, d_skip, norm_g):
    row0 = tp // nb
    cs = jnp.transpose(conv_state, (1, 0, 2))
    one = lambda i: (0, 0)
    one3 = lambda i: (0, 0, 0)

    def rows(t, width, col):
        return pl.BlockSpec((nb, width), lambda i: (row0 + t, col // width))

    in_specs = ([rows(t, SSM_INNER, COL_XS) for t in range(lt)]
                + [rows(t, 2 * SSM_BC, COL_BC) for t in range(lt)]
                + [pl.BlockSpec((nb, LANES), lambda i, t=t: (row0 + t, COL_DT // LANES)) for t in range(lt)]
                + [pl.BlockSpec((CONV_K - 1, nb, SSM_INNER), one3),
                   pl.BlockSpec((CONV_K - 1, nb, 2 * SSM_BC), one3),
                   pl.BlockSpec((CONV_K, SSM_INNER), one),
                   pl.BlockSpec((CONV_K, 2 * SSM_BC), one),
                   pl.BlockSpec((1, SSM_INNER), one),
                   pl.BlockSpec((1, 2 * SSM_BC), one),
                   pl.BlockSpec((1, SSM_INNER), one),
                   pl.BlockSpec((1, SSM_INNER), one),
                   pl.BlockSpec((1, SSM_INNER), one),
                   pl.BlockSpec((LANES, SSM_INNER), one),
                   pl.BlockSpec((SSM_BC, SSM_INNER), one)])
    f3 = lambda w: jax.ShapeDtypeStruct((lt, nb, w), F32)
    c_a, b_a, xw_a, dec_a, yd_a, ec_a = pl.pallas_call(
        functools.partial(_ssd_s_pre_kernel, lt=lt),
        grid=(1,),
        in_specs=in_specs,
        out_specs=[pl.BlockSpec((lt, nb, SSM_BC), one3), pl.BlockSpec((lt, nb, SSM_BC), one3),
                   pl.BlockSpec((lt, nb, SSM_INNER), one3), pl.BlockSpec((nb, SSM_INNER), one),
                   pl.BlockSpec((lt, nb, SSM_INNER), one3), pl.BlockSpec((lt, nb, SSM_INNER), one3)],
        out_shape=[f3(SSM_BC), f3(SSM_BC), f3(SSM_INNER), jax.ShapeDtypeStruct((nb, SSM_INNER), F32),
                   f3(SSM_INNER), f3(SSM_INNER)],
        compiler_params=_cparams(1),
        name="ssd_sample_pre",
    )(*([proj] * (3 * lt)), cs[:, :, :SSM_INNER], cs[:, :, SSM_INNER:],
      conv_w[:, :SSM_INNER], conv_w[:, SSM_INNER:], conv_b[:SSM_INNER].reshape(1, -1),
      conv_b[SSM_INNER:].reshape(1, -1), _expand_heads(dt_bias), _expand_heads(a_log),
      _expand_heads(d_skip), _head_expand_matrix(), _group_expand_matrix())

    depth = state_all.shape[0]
    h0r = state_all.reshape(depth, nb, SSM_INNER, SSM_STATE)
    state_specs = [pl.BlockSpec((lt, nb, SSM_BC), one3), pl.BlockSpec((lt, nb, SSM_BC), one3),
                   pl.BlockSpec((lt, nb, SSM_INNER), one3), pl.BlockSpec((nb, SSM_INNER), one),
                   pl.BlockSpec((None, 1, SSM_INNER, SSM_STATE), lambda i: (layer, i, 0, 0))]
    hn_shape = jax.ShapeDtypeStruct((depth, nb, SSM_INNER, SSM_STATE), F32)
    if new_states is None:
        fill = tuple(s for s in range(depth) if s != layer)
        hn, yr = pl.pallas_call(
            functools.partial(_ssd_s_state_kernel, lt=lt, slot=layer, fill_slots=fill),
            grid=(nb,),
            in_specs=state_specs,
            out_specs=[pl.BlockSpec((depth, 1, SSM_INNER, SSM_STATE), lambda i: (0, i, 0, 0)),
                       pl.BlockSpec((lt, nb, SSM_INNER), one3)],
            out_shape=[hn_shape, f3(SSM_INNER)],
            compiler_params=_cparams(1),
            name="ssd_sample_state",
        )(c_a, b_a, xw_a, dec_a, h0r)
    else:
        hn, yr = pl.pallas_call(
            functools.partial(_ssd_s_state_kernel_inplace, lt=lt),
            grid=(nb,),
            in_specs=state_specs + [pl.BlockSpec(memory_space=pl.ANY)],
            out_specs=[pl.BlockSpec((None, 1, SSM_INNER, SSM_STATE), lambda i: (layer, i, 0, 0)),
                       pl.BlockSpec((lt, nb, SSM_INNER), one3)],
            out_shape=[hn_shape, f3(SSM_INNER)],
            input_output_aliases={5: 0},
            compiler_params=_cparams(1),
            name="ssd_sample_state",
        )(c_a, b_a, xw_a, dec_a, h0r, new_states)

    ssm = pl.pallas_call(
        functools.partial(_ssd_s_post_kernel, lt=lt),
        grid=(1,),
        in_specs=([rows(t, SSM_INNER, COL_Z) for t in range(lt)]
                  + [pl.BlockSpec((lt, nb, SSM_INNER), one3)] * 3 + [pl.BlockSpec((1, SSM_INNER), one)]),
        out_specs=pl.BlockSpec((lt * nb, SSM_INNER), one),
        out_shape=jax.ShapeDtypeStruct((lt * nb, SSM_INNER), BF16),
        compiler_params=_cparams(1),
        name="ssd_sample_post",
    )(*([proj] * lt), yd_a, ec_a, yr, norm_g.reshape(1, SSM_INNER))
    return ssm, hn


def _softmax_rows(s):
    p = jnp.exp(s - jnp.max(s, axis=-1, keepdims=True))
    return p * (1.0 / jnp.sum(p, axis=-1, keepdims=True))


def _xattn_kernel(q_ref, k_ref, v_ref, o_ref, *, nh, bseq, tq):
    scale = MEM_HEAD_DIM ** -0.5
    for b in range(bseq):
        rows = slice(b * tq, (b + 1) * tq)
        for h in range(nh):
            sl = slice(h * MEM_HEAD_DIM, (h + 1) * MEM_HEAD_DIM)
            s = _dot_nt(q_ref[rows, sl].astype(BF16), k_ref[b, :, h, :].astype(BF16)) * scale
            o = _dot(_softmax_rows(s).astype(BF16), v_ref[b, :, h, :].astype(BF16))
            o_ref[rows, sl] = o.astype(o_ref.dtype)


def cross_attention(q, k, v, *, layer, n_seq, seq, tq, bseq, name):
    w = q.shape[1]
    _, _, m, nh, dh = k.shape
    nq = seq // tq
    assert bseq == 1 or nq == 1
    kv_spec = pl.BlockSpec((None, bseq, m, nh, dh), lambda i, n: (layer, i, 0, 0, 0))
    return pl.pallas_call(
        functools.partial(_xattn_kernel, nh=nh, bseq=bseq, tq=tq),
        grid=(n_seq // bseq, nq),
        in_specs=[pl.BlockSpec((bseq * tq, w), lambda i, n: (i * nq + n, 0)), kv_spec, kv_spec],
        out_specs=pl.BlockSpec((bseq * tq, w), lambda i, n: (i * nq + n, 0)),
        out_shape=jax.ShapeDtypeStruct((n_seq * seq, w), F32),
        compiler_params=_cparams(2),
        name=name,
    )(q, k, v)


def _router_kernel(h_ref, w_ref, b_ref, o_ref):
    logits = _dot(h_ref[...].astype(BF16), w_ref[...].astype(BF16)) + b_ref[...]
    lane = lax.broadcasted_iota(jnp.int32, logits.shape, 1)
    lane_f = lane.astype(F32)
    big = float(LANES)
    is_g = lane < N_EGROUPS
    lg = jnp.where(is_g, logits, NEG_BIG)
    mg = jnp.max(lg, axis=-1, keepdims=True)
    zg = jnp.sum(jnp.where(is_g, jnp.exp(lg - mg), 0.0), axis=-1, keepdims=True)
    gi = jnp.min(jnp.where(is_g & (lg == mg), lane_f, big), axis=-1, keepdims=True)
    gw = 1.0 / zg
    lo = N_EGROUPS + gi * EXPERTS_PER_GROUP
    is_e = (lane_f >= lo) & (lane_f < lo + EXPERTS_PER_GROUP)
    le = jnp.where(is_e, logits, NEG_BIG)
    me = jnp.max(le, axis=-1, keepdims=True)
    ee = jnp.where(is_e, jnp.exp(le - me), 0.0)
    pe = ee / jnp.sum(ee, axis=-1, keepdims=True)
    pe = jnp.where(is_e, pe, -1.0)
    p1 = jnp.max(pe, axis=-1, keepdims=True)
    i1 = jnp.min(jnp.where(pe == p1, lane_f, big), axis=-1, keepdims=True)
    pe2 = jnp.where(lane_f == i1, -1.0, pe)
    p2 = jnp.max(pe2, axis=-1, keepdims=True)
    i2 = jnp.min(jnp.where(pe2 == p2, lane_f, big), axis=-1, keepdims=True)
    tot = p1 + p2
    out = jnp.where(lane == 0, i1 - N_EGROUPS,
                    jnp.where(lane == 1, i2 - N_EGROUPS,
                              jnp.where(lane == 2, gw * (p1 / tot),
                                        jnp.where(lane == 3, gw * (p2 / tot), 0.0))))
    o_ref[...] = out


def moe_router(h, w_rg, b_rg, w_re, b_re):
    t, d = h.shape
    tm = _pick(t, (256, 128, 64, 32, 16, 8))
    npad = LANES - N_EGROUPS - N_EXPERTS
    w = jnp.concatenate([w_rg, w_re, jnp.zeros((d, npad), F32)], axis=1)
    b = jnp.concatenate([b_rg, b_re, jnp.zeros((npad,), F32)]).reshape(1, LANES)
    return pl.pallas_call(
        _router_kernel,
        grid=(t // tm,),
        in_specs=[pl.BlockSpec((tm, d), lambda i: (i, 0)),
                  pl.BlockSpec((d, LANES), lambda i: (0, 0)),
                  pl.BlockSpec((1, LANES), lambda i: (0, 0))],
        out_specs=pl.BlockSpec((tm, LANES), lambda i: (i, 0)),
        out_shape=jax.ShapeDtypeStruct((t, LANES), F32),
        compiler_params=_cparams(1),
        name="moe_router",
    )(h, w, b)


def _row_copy(src_hbm, dst, src_row, dst_row, sem):
    return pltpu.make_async_copy(src_hbm.at[pl.ds(src_row, 1)], dst.at[pl.ds(dst_row, 1)], sem)


def _moe_gather_kernel(tok_ref, h_ref, o_ref, buf, sem, *, tm):
    def start(i, carry):
        for p in range(2):
            r = 2 * i + p
            _row_copy(h_ref, buf, tok_ref[0, 0, r], r, sem).start(priority=p)
        return carry

    def wait(r, carry):
        _row_copy(h_ref, buf, tok_ref[0, 0, r], r, sem).wait()
        return carry

    lax.fori_loop(0, tm // 2, start, 0, unroll=DMA_LOOP_UNROLL // 2)
    lax.fori_loop(0, tm, wait, 0, unroll=DMA_LOOP_UNROLL)
    o_ref[...] = buf[...].astype(o_ref.dtype)


def moe_gather(h, row_token, tm):
    r_total = row_token.shape[0]
    d = h.shape[1]
    nblk = r_total // tm
    return pl.pallas_call(
        functools.partial(_moe_gather_kernel, tm=tm),
        grid=(nblk,),
        in_specs=[pl.BlockSpec((1, 1, tm), lambda i: (i, 0, 0), memory_space=pltpu.SMEM),
                  pl.BlockSpec(memory_space=pl.ANY)],
        out_specs=pl.BlockSpec((tm, d), lambda i: (i, 0)),
        out_shape=jax.ShapeDtypeStruct((r_total, d), BF16),
        scratch_shapes=[pltpu.VMEM((tm, d), h.dtype), pltpu.SemaphoreType.DMA(())],
        compiler_params=_cparams(1),
        name="moe_gather",
    )(row_token.reshape(nblk, 1, tm), h)


def _expert_weight_copies(w_refs, bufs, sems, layer, expert, slot):
    return [pltpu.make_async_copy(w.at[layer, expert], buf.at[slot], sems.at[k, slot])
            for k, (w, buf) in enumerate(zip(w_refs, bufs))]


def _expert_weights_step(s, tv_ref, te_ref, tn_ref, ts_ref, w_refs, bufs, w16s, sems, layer):
    @pl.when(s == 0)
    def _():
        for c in _expert_weight_copies(w_refs, bufs, sems, layer, te_ref[0], 0):
            c.start()

    @pl.when(tv_ref[s] == 2)
    def _():
        slot = ts_ref[s]
        for c in _expert_weight_copies(w_refs, bufs, sems, layer, te_ref[s], slot):
            c.wait()

        @pl.when(tn_ref[s] >= 0)
        def _():
            for c in _expert_weight_copies(w_refs, bufs, sems, layer, tn_ref[s], 1 - slot):
                c.start()

        for buf, w16 in zip(bufs, w16s):
            w16[...] = buf[slot].astype(BF16)


def _moe_up_kernel(tv_ref, tc_ref, te_ref, tn_ref, ts_ref, x_ref, wg_ref, wu_ref, o_ref,
                   gbuf, ubuf, wg16, wu16, sems, *, layer):
    s = pl.program_id(0)
    _expert_weights_step(s, tv_ref, te_ref, tn_ref, ts_ref, (wg_ref, wu_ref), (gbuf, ubuf), (wg16, wu16),
                         sems, layer)

    @pl.when(tv_ref[s] > 0)
    def _():
        x = x_ref[...]
        a = _dot(x, wg16[...])
        u = _dot(x, wu16[...])
        o_ref[...] = (_silu(a) * u).astype(o_ref.dtype)

    @pl.when(tv_ref[s] == 0)
    def _():
        o_ref[...] = jnp.zeros_like(o_ref)


def _moe_tables(plan):
    return plan["tile_v"], plan["tile_c"], plan["tile_e"], plan["tile_next"], plan["tile_slot"]


def moe_up(x_sorted, w_gate, w_up, layer, plan, tm):
    r_total, d = x_sorted.shape
    ff = w_gate.shape[-1]
    n_tiles = r_total // tm
    grid_spec = pltpu.PrefetchScalarGridSpec(
        num_scalar_prefetch=5,
        grid=(n_tiles,),
        in_specs=[
            pl.BlockSpec((tm, d), lambda s, tv, tc, te, tn, ts: (tc[s], 0)),
            pl.BlockSpec(memory_space=pl.ANY),
            pl.BlockSpec(memory_space=pl.ANY),
        ],
        out_specs=pl.BlockSpec((tm, ff), lambda s, tv, tc, te, tn, ts: (s, 0)),
        scratch_shapes=[pltpu.VMEM((2, d, ff), F32), pltpu.VMEM((2, d, ff), F32),
                        pltpu.VMEM((d, ff), BF16), pltpu.VMEM((d, ff), BF16),
                        pltpu.SemaphoreType.DMA((2, 2))],
    )
    return pl.pallas_call(
        functools.partial(_moe_up_kernel, layer=layer),
        grid_spec=grid_spec,
        out_shape=jax.ShapeDtypeStruct((r_total, ff), BF16),
        compiler_params=_cparams(1),
        name="moe_up",
    )(*_moe_tables(plan), x_sorted, w_gate, w_up)


def _moe_down_kernel(tv_ref, tc_ref, te_ref, tn_ref, ts_ref, x_ref, w_ref, o_ref, wbuf, w16, sems, *, layer):
    s = pl.program_id(0)
    _expert_weights_step(s, tv_ref, te_ref, tn_ref, ts_ref, (w_ref,), (wbuf,), (w16,), sems, layer)

    @pl.when(tv_ref[s] > 0)
    def _():
        o_ref[...] = _dot(x_ref[...], w16[...])

    @pl.when(tv_ref[s] == 0)
    def _():
        o_ref[...] = jnp.zeros_like(o_ref)


def moe_down(hid, w_down, layer, plan, tm):
    r_total, ff = hid.shape
    d = w_down.shape[-1]
    n_tiles = r_total // tm
    grid_spec = pltpu.PrefetchScalarGridSpec(
        num_scalar_prefetch=5,
        grid=(n_tiles,),
        in_specs=[
            pl.BlockSpec((tm, ff), lambda s, tv, tc, te, tn, ts: (tc[s], 0)),
            pl.BlockSpec(memory_space=pl.ANY),
        ],
        out_specs=pl.BlockSpec((tm, d), lambda s, tv, tc, te, tn, ts: (s, 0)),
        scratch_shapes=[pltpu.VMEM((2, ff, d), F32), pltpu.VMEM((ff, d), BF16),
                        pltpu.SemaphoreType.DMA((1, 2))],
    )
    return pl.pallas_call(
        functools.partial(_moe_down_kernel, layer=layer),
        grid_spec=grid_spec,
        out_shape=jax.ShapeDtypeStruct((r_total, d), F32),
        compiler_params=_cparams(1),
        name="moe_down",
    )(*_moe_tables(plan), hid, w_down)


def _moe_combine_kernel(pos_ref, y_ref, r_ref, h_ref, g_ref, b_ref, o1_ref, o2_ref, ybuf, sem, *,
                        tm, alpha, n_first):
    def start(i, carry):
        _row_copy(y_ref, ybuf.at[0], pos_ref[0, 0, 2 * i], i, sem).start(priority=0)
        _row_copy(y_ref, ybuf.at[1], pos_ref[0, 0, 2 * i + 1], i, sem).start(priority=1)
        return carry

    def wait(i, carry):
        _row_copy(y_ref, ybuf.at[0], pos_ref[0, 0, 2 * i], i, sem).wait()
        _row_copy(y_ref, ybuf.at[1], pos_ref[0, 0, 2 * i + 1], i, sem).wait()
        return carry

    lax.fori_loop(0, tm, start, 0, unroll=DMA_LOOP_UNROLL)
    lax.fori_loop(0, tm, wait, 0, unroll=DMA_LOOP_UNROLL)
    route = r_ref[...]
    ff = ybuf[0] * route[:, 2:3] + ybuf[1] * route[:, 3:4]
    h = _ln_rows(alpha * h_ref[...] + ff, g_ref[...], b_ref[...])
    if n_first is None:
        o1_ref[...] = h
        o2_ref[...] = h.astype(BF16)
    else:
        i = pl.program_id(0)

        @pl.when(i < n_first)
        def _():
            o1_ref[...] = h

        @pl.when(i >= n_first)
        def _():
            o2_ref[...] = h


def moe_combine(y_sorted, pos, route, h, g, b, *, alpha, split_rows=None):
    t, d = h.shape
    tm = _pick(t if split_rows is None else math.gcd(split_rows, t - split_rows), (256, 128, 64, 32, 16, 8))
    nblk = t // tm
    if split_rows is None:
        n_first = None
        out_specs = [pl.BlockSpec((tm, d), lambda i: (i, 0)), pl.BlockSpec((tm, d), lambda i: (i, 0))]
        out_shape = [jax.ShapeDtypeStruct((t, d), F32), jax.ShapeDtypeStruct((t, d), BF16)]
    else:
        n_first = split_rows // tm
        out_specs = [pl.BlockSpec((tm, d), lambda i: (jnp.minimum(i, n_first - 1), 0)),
                     pl.BlockSpec((tm, d), lambda i: (jnp.maximum(i - n_first, 0), 0))]
        out_shape = [jax.ShapeDtypeStruct((split_rows, d), F32),
                     jax.ShapeDtypeStruct((t - split_rows, d), F32)]
    return pl.pallas_call(
        functools.partial(_moe_combine_kernel, tm=tm, alpha=alpha, n_first=n_first),
        grid=(nblk,),
        in_specs=[pl.BlockSpec((1, 1, 2 * tm), lambda i: (i, 0, 0), memory_space=pltpu.SMEM),
                  pl.BlockSpec(memory_space=pl.ANY),
                  pl.BlockSpec((tm, LANES), lambda i: (i, 0)),
                  pl.BlockSpec((tm, d), lambda i: (i, 0)),
                  pl.BlockSpec((1, d), lambda i: (0, 0)),
                  pl.BlockSpec((1, d), lambda i: (0, 0))],
        out_specs=out_specs,
        out_shape=out_shape,
        scratch_shapes=[pltpu.VMEM((2, tm, d), F32), pltpu.SemaphoreType.DMA(())],
        compiler_params=_cparams(1),
        name="moe_combine",
    )(pos.reshape(nblk, 1, 2 * tm), y_sorted, route, h, g.reshape(1, d), b.reshape(1, d))


def moe_plan(route, tm):
    t = route.shape[0]
    eid = route[:, :2].astype(jnp.int32).reshape(-1)
    onehot = (eid[:, None] == jnp.arange(N_EXPERTS, dtype=jnp.int32)[None, :]).astype(jnp.int32)
    csum = jnp.cumsum(onehot, axis=0)
    rank = jnp.sum((csum - onehot) * onehot, axis=1)
    counts = csum[-1]
    tiles_e = (counts + tm - 1) // tm
    tile_end = jnp.cumsum(tiles_e)
    tile_start = tile_end - tiles_e
    n_used = tile_end[-1]
    n_tiles = (2 * t + N_EXPERTS * (tm - 1)) // tm + 1
    r_total = n_tiles * tm
    dest = tile_start[eid] * tm + rank
    row_token = (jnp.arange(r_total, dtype=jnp.int32) % t).at[dest].set(
        jnp.arange(2 * t, dtype=jnp.int32) // 2)
    tile_ids = jnp.arange(n_tiles, dtype=jnp.int32)
    tile_clamped = jnp.minimum(tile_ids, n_used - 1)
    tile_e = jnp.sum(tile_end[None, :] <= tile_clamped[:, None], axis=1).astype(jnp.int32)
    tile_first = tile_ids == tile_start[tile_e]
    tile_v = jnp.where(tile_ids < n_used, 1 + tile_first.astype(jnp.int32), 0).astype(jnp.int32)
    e_ids = jnp.arange(N_EXPERTS, dtype=jnp.int32)
    nonempty = tiles_e > 0
    cand = jnp.where(nonempty[None, :] & (e_ids[None, :] > e_ids[:, None]), e_ids[None, :], N_EXPERTS)
    next_e = jnp.min(cand, axis=1)
    next_e = jnp.where(next_e >= N_EXPERTS, -1, next_e).astype(jnp.int32)
    slot_e = ((jnp.cumsum(nonempty.astype(jnp.int32)) - 1) % 2).astype(jnp.int32)
    return dict(row_token=row_token, pos=dest.astype(jnp.int32), tile_v=tile_v, tile_c=tile_clamped,
                tile_e=tile_e, tile_next=next_e[tile_e], tile_slot=slot_e[tile_e])


def hierarchical_moe_ln(hf, layer, w_rg, b_rg, w_re, b_re, w_gate, w_up, w_down, ln_g, ln_b, *, alpha,
                        split_rows=None):
    route = moe_router(hf, w_rg, b_rg, w_re, b_re)
    plan = moe_plan(route, MOE_TM)
    x_sorted = moe_gather(hf, plan["row_token"], MOE_TM)
    hid = moe_up(x_sorted, w_gate, w_up, layer, plan, MOE_TM)
    y_sorted = moe_down(hid, w_down, layer, plan, MOE_TM)
    return moe_combine(y_sorted, plan["pos"], route, hf, ln_g, ln_b, alpha=alpha, split_rows=split_rows)


def _to_seq_major(x_tm, lt, nb, pad_to):
    w = x_tm.shape[1]
    x = jnp.transpose(x_tm.reshape(lt, nb, w), (1, 0, 2))
    x = jnp.pad(x, ((0, 0), (0, pad_to - lt), (0, 0)))
    return x.reshape(nb * pad_to, w)


def _to_time_major(x_sm, lt, nb, pad_to):
    w = x_sm.shape[1]
    x = x_sm.reshape(nb, pad_to, w)[:, :lt]
    return jnp.transpose(x, (1, 0, 2)).reshape(lt * nb, w)


def kernel(x_prompt, x_sample, mem_prompt, cache_swa_k, cache_swa_v, cache_mem_k, cache_mem_v, state_conv, state_ssm, ln_in_g, ln_in_b, w_in, attn_sinks, gm_ln_g, gm_ln_b, gm_ws, gm_bs, conv_w, conv_b, dt_bias, a_log, d_skip, ssm_norm_g, w_pa, w_pb, w_pc, w_o, ln1_g, ln1_b, w_cq, w_ck, w_cv, w_co, ln2_g, ln2_b, w_rg, b_rg, w_re, b_re, w_gate, w_up, w_down, ln3_g, ln3_b):
    bp, seq, d = x_prompt.shape
    nb, lt, _ = x_sample.shape
    depth = w_in.shape[0]
    mem_len = mem_prompt.shape[1]
    past_len = PAST_LEN
    wb = cache_swa_k.shape[2]
    assert wb == WINDOW and seq % CHUNK == 0 and lt <= SUBLANES
    tp, ts = bp * seq, nb * lt
    alpha = (2 * depth) ** 0.25
    qpad = SUBLANES

    xp = x_prompt.reshape(tp, d)
    xs = jnp.transpose(x_sample, (1, 0, 2)).reshape(ts, d)
    hf, hb = ln_in(xp, xs, ln_in_g, ln_in_b)
    cos_t, sin_t = rope_tables(tp, seq, ts, nb, past_len)
    mem_b = mem_prompt.reshape(bp * mem_len, d).astype(BF16)

    in_w = w_in.shape[2]
    assert in_w == Q_W + 2 * KV_W + 2 * GM_W + SSM_INNER + CONV_DIM + SSM_HEADS + 3 * D_MODEL
    n_whole = (in_w // IN_TN) * IN_TN
    w_t = jnp.swapaxes(w_in, 1, 2)
    w_tail = jnp.pad(w_t[:, n_whole:, :], ((0, 0), (0, IN_TN - (in_w - n_whole)), (0, 0)))

    outs = {k: [] for k in ("p_k", "p_v", "p_mk", "p_mv", "p_conv", "p_ssm", "p_gv",
                            "s_k", "s_v", "s_conv", "s_ssm", "s_gv")}
    n_qblk = seq // WINDOW
    s_states = None
    for l in range(depth):
        proj = in_projection(hb, w_t, w_tail, l)
        q_rot, k_rot = rope_qk(proj, cos_t, sin_t)

        kcol, vcol = 0, COL_V // KV_W
        att_p = swa_attention(
            attn_sinks[l], q_rot, k_rot, k_rot, proj, proj,
            n_seq=bp, n_blk=n_qblk, qb=WINDOW, prev_from_block0=False,
            kp_map=lambda i, n: (jnp.maximum(i * n_qblk + n - 1, 0), kcol),
            kc_map=lambda i, n: (i * n_qblk + n, kcol),
            vp_map=lambda i, n: (jnp.maximum(i * n_qblk + n - 1, 0), vcol),
            vc_map=lambda i, n: (i * n_qblk + n, vcol),
            out_dtype=BF16, name="swa_prompt")
        k_s_tm = k_rot[tp:]
        v_s_tm = proj[tp:, COL_V:COL_V + KV_W]
        q_s = _to_seq_major(q_rot[tp:], lt, nb, qpad)
        k_s = _to_seq_major(k_s_tm, lt, nb, qpad)
        v_s = _to_seq_major(v_s_tm, lt, nb, qpad)
        att_s8 = swa_attention_cached(attn_sinks[l], q_s, k_s, v_s, cache_swa_k, cache_swa_v, layer=l,
                                      n_seq=nb, qb=qpad, bseq=_pick(nb, (8, 4, 2, 1)), name="swa_sample")
        att_s = _to_time_major(att_s8, lt, nb, qpad).astype(BF16)
        last_w = lambda a, c0: jnp.stack(
            [a[(i + 1) * seq - WINDOW:(i + 1) * seq, c0:c0 + KV_W] for i in range(bp)]
        ).reshape(bp, WINDOW, N_KV_HEADS, HEAD_DIM)
        outs["p_k"].append(last_w(k_rot, 0))
        outs["p_v"].append(last_w(proj, COL_V))
        k_new = jnp.transpose(k_s_tm.reshape(lt, nb, N_KV_HEADS, HEAD_DIM), (1, 0, 2, 3))
        v_new = jnp.transpose(v_s_tm.reshape(lt, nb, N_KV_HEADS, HEAD_DIM), (1, 0, 2, 3))
        outs["s_k"].append(jnp.concatenate([cache_swa_k[l], k_new], axis=1)[:, -wb:])
        outs["s_v"].append(jnp.concatenate([cache_swa_v[l], v_new], axis=1)[:, -wb:])

        gm_p, vg_last = gmlp_prompt(proj, bp, seq, gm_ws[l], gm_bs[l], gm_ln_g[l], gm_ln_b[l])
        gm_s, vg_s = gmlp_sample(proj, tp, nb, lt, gm_ws[l], gm_bs[l], gm_ln_g[l], gm_ln_b[l])
        outs["p_gv"].append(vg_last.reshape(bp, CHUNK, GM_GROUPS, GM_GROUP_DIM))
        outs["s_gv"].append(jnp.transpose(vg_s.reshape(lt, nb, GM_GROUPS, GM_GROUP_DIM), (1, 0, 2, 3)))

        dtt = jnp.transpose(proj[:tp, COL_DT:COL_DT + SSM_HEADS])
        y_p, st_p = ssd_prompt(proj, dtt, bp, seq, conv_w[l], conv_b[l], dt_bias[l], a_log[l],
                               d_skip[l], ssm_norm_g[l])
        ssm_s, s_states = ssd_sample(proj, tp, nb, lt, state_conv[l], state_ssm, s_states, l, conv_w[l],
                                     conv_b[l], dt_bias[l], a_log[l], d_skip[l], ssm_norm_g[l])
        outs["p_conv"].append(jnp.stack(
            [proj[(i + 1) * seq - (CONV_K - 1):(i + 1) * seq, COL_XS:COL_XS + CONV_DIM] for i in range(bp)]))
        xbc_s = jnp.transpose(proj[tp:, COL_XS:COL_XS + CONV_DIM].reshape(lt, nb, CONV_DIM), (1, 0, 2))
        outs["s_conv"].append(jnp.concatenate([state_conv[l], xbc_s], axis=1)[:, -(CONV_K - 1):])
        outs["p_ssm"].append(st_p.reshape(bp, SSM_HEADS, SSM_HEAD_DIM, SSM_STATE))

        merged = gated_merge(att_p, att_s, gm_p, gm_s, y_p, ssm_s, w_pa[l].astype(BF16),
                             w_pb[l].astype(BF16), w_pc[l].astype(BF16), proj)
        h1f, h1b = matmul_ln(merged, w_o[l].astype(BF16), hf, ln1_g[l], ln1_b[l], alpha=alpha, name="out_proj_ln1")

        qc = matmul(h1b, w_cq[l].astype(BF16), name="xattn_q")
        pmk = matmul(mem_b, w_ck[l].astype(BF16), name="mem_k")
        pmv = matmul(mem_b, w_cv[l].astype(BF16), name="mem_v")
        pmk5 = pmk.reshape(1, bp, mem_len, MEM_HEADS, MEM_HEAD_DIM)
        pmv5 = pmv.reshape(1, bp, mem_len, MEM_HEADS, MEM_HEAD_DIM)
        outs["p_mk"].append(pmk5[0])
        outs["p_mv"].append(pmv5[0])
        tq = _pick(seq, (512, 256, 128))
        o_p = cross_attention(qc, pmk5, pmv5, layer=0, n_seq=bp, seq=seq, tq=tq, bseq=1,
                              name="xattn_prompt")
        qc_s = _to_seq_major(qc[tp:], lt, nb, qpad)
        o_s8 = cross_attention(qc_s, cache_mem_k, cache_mem_v, layer=l, n_seq=nb, seq=qpad, tq=qpad,
                               bseq=_pick(nb, (4, 2, 1)), name="xattn_sample")
        o_all = jnp.concatenate([o_p, _to_time_major(o_s8, lt, nb, qpad)], axis=0).astype(BF16)
        h2f, h2b = matmul_ln(o_all, w_co[l].astype(BF16), h1f, ln2_g[l], ln2_b[l], alpha=alpha, name="xattn_out_ln2")

        hf, hb = hierarchical_moe_ln(h2f, l, w_rg[l], b_rg[l], w_re[l], b_re[l], w_gate, w_up, w_down,
                                     ln3_g[l], ln3_b[l], alpha=alpha,
                                     split_rows=tp if l == depth - 1 else None)

    y_prompt = hf.reshape(bp, seq, d)
    y_sample = jnp.transpose(hb.reshape(lt, nb, d), (1, 0, 2))
    st = lambda k: jnp.stack(outs[k])
    s_ssm = s_states.reshape(depth, nb, SSM_HEADS, SSM_HEAD_DIM, SSM_STATE)
    return (y_prompt, y_sample, st("p_k"), st("p_v"), st("p_mk"), st("p_mv"), st("p_conv"), st("p_ssm"),
            st("p_gv"), st("s_k"), st("s_v"), st("s_conv"), s_ssm, st("s_gv"))
```

```python
import functools
import math

import numpy as np
import jax
import jax.numpy as jnp
from jax import lax
from jax.experimental import pallas as pl
from jax.experimental.pallas import tpu as pltpu

F32 = jnp.float32
BF16 = jnp.bfloat16

D_MODEL = 2048
N_HEADS = 32
N_KV_HEADS = 4
HEAD_DIM = 64
WINDOW = 128
PAST_LEN = 8192
ROPE_THETA = 10000.0
CHUNK = 128
GM_GROUPS = 16
GM_GROUP_DIM = 128
SSM_HEADS = 32
SSM_HEAD_DIM = 64
SSM_GROUPS = 4
SSM_STATE = 128
CONV_K = 4
MEM_HEADS = 4
MEM_HEAD_DIM = 128
N_EGROUPS = 4
EXPERTS_PER_GROUP = 8
N_EXPERTS = N_EGROUPS * EXPERTS_PER_GROUP
EXPERT_FF = D_MODEL // 2
Q_W = N_HEADS * HEAD_DIM
KV_W = N_KV_HEADS * HEAD_DIM
GM_W = GM_GROUPS * GM_GROUP_DIM
SSM_INNER = SSM_HEADS * SSM_HEAD_DIM
SSM_BC = SSM_GROUPS * SSM_STATE
CONV_DIM = SSM_INNER + 2 * SSM_BC
MEM_W = MEM_HEADS * MEM_HEAD_DIM
LN_EPS = 1e-5
NEG_BIG = -1e30

VMEM_LIMIT_BYTES = 52 * 1024 * 1024
LANES = 128
SUBLANES = 8

IN_TN = 512
COL_Q = 0
COL_GU = 2048
COL_GV = 4096
COL_Z = 6144
COL_XS = 8192
COL_BC = 10240
COL_DT = 11264
COL_GATES = COL_DT + SSM_HEADS
COL_K = 17920
COL_V = 18176
PROJ_W = 18432

MOE_TM = 256
DMA_LOOP_UNROLL = 8


def _cparams(n_grid):
    return pltpu.CompilerParams(
        dimension_semantics=("arbitrary",) * n_grid,
        vmem_limit_bytes=VMEM_LIMIT_BYTES,
    )


def _pick(n, prefs):
    for p in prefs:
        if n % p == 0:
            return p
    raise ValueError(f"no tile for {n} in {prefs}")


def _ln_rows(x, g, b):
    mu = jnp.mean(x, axis=-1, keepdims=True)
    xc = x - mu
    var = jnp.mean(xc * xc, axis=-1, keepdims=True)
    return xc * lax.rsqrt(var + LN_EPS) * g + b


def _sigmoid(x):
    return 1.0 / (1.0 + jnp.exp(-x))


def _silu(x):
    return x * _sigmoid(x)


def _softplus(x):
    return jnp.maximum(x, 0.0) + jnp.log1p(jnp.exp(-jnp.abs(x)))


def _gelu(x):
    return jax.nn.gelu(x, approximate=True)


def _split3(x):
    hi = x.astype(BF16)
    r1 = x - hi.astype(F32)
    mid = r1.astype(BF16)
    lo = (r1 - mid.astype(F32)).astype(BF16)
    return hi, mid, lo


def _dot(a, b):
    return jnp.dot(a, b, preferred_element_type=F32)


def _dot_nt(a, b):
    return lax.dot_general(a, b, (((1,), (1,)), ((), ())), preferred_element_type=F32)


def _exact_dot_left(pieces, m):
    acc = _dot(pieces[0], m)
    for p in pieces[1:]:
        acc = acc + _dot(p, m)
    return acc


def _ln_in_kernel(xp_ref, xs_ref, g_ref, b_ref, of_ref, ob_ref, *, n_p):
    i = pl.program_id(0)

    @pl.when(i < n_p)
    def _():
        y = _ln_rows(xp_ref[...], g_ref[...], b_ref[...])
        of_ref[...] = y
        ob_ref[...] = y.astype(BF16)

    @pl.when(i >= n_p)
    def _():
        y = _ln_rows(xs_ref[...], g_ref[...], b_ref[...])
        of_ref[...] = y
        ob_ref[...] = y.astype(BF16)


def ln_in(xp, xs, g, b):
    tp, d = xp.shape
    ts = xs.shape[0]
    tm = _pick(math.gcd(tp, ts), (256, 128, 64, 32, 16, 8))
    n_p, n_s = tp // tm, ts // tm
    t = tp + ts
    return pl.pallas_call(
        functools.partial(_ln_in_kernel, n_p=n_p),
        grid=(n_p + n_s,),
        in_specs=[
            pl.BlockSpec((tm, d), lambda i: (jnp.minimum(i, n_p - 1), 0)),
            pl.BlockSpec((tm, d), lambda i: (jnp.maximum(i - n_p, 0), 0)),
            pl.BlockSpec((1, d), lambda i: (0, 0)),
            pl.BlockSpec((1, d), lambda i: (0, 0)),
        ],
        out_specs=[
            pl.BlockSpec((tm, d), lambda i: (i, 0)),
            pl.BlockSpec((tm, d), lambda i: (i, 0)),
        ],
        out_shape=[jax.ShapeDtypeStruct((t, d), F32), jax.ShapeDtypeStruct((t, d), BF16)],
        compiler_params=_cparams(1),
        name="ln_in",
    )(xp, xs, g.reshape(1, d), b.reshape(1, d))


def _mm_kernel(x_ref, w_ref, o_ref):
    o_ref[...] = _dot(x_ref[...], w_ref[...]).astype(o_ref.dtype)


def matmul(x, w, *, out_dtype=F32, tm_prefs=(1088, 1024, 512, 256, 128, 64, 32, 16, 8),
           tn_prefs=(1280, 1024, 512, 256, 128), name="mm"):
    t, k = x.shape
    n = w.shape[1]
    tm = _pick(t, tm_prefs)
    tn = _pick(n, tn_prefs)
    return pl.pallas_call(
        _mm_kernel,
        grid=(n // tn, t // tm),
        in_specs=[
            pl.BlockSpec((tm, k), lambda j, i: (i, 0)),
            pl.BlockSpec((k, tn), lambda j, i: (0, j)),
        ],
        out_specs=pl.BlockSpec((tm, tn), lambda j, i: (i, j)),
        out_shape=jax.ShapeDtypeStruct((t, n), out_dtype),
        compiler_params=_cparams(2),
        name=name,
    )(x, w)


def _in_proj_kernel(x_ref, w_ref, wt_ref, o_ref, *, n_main):
    j = pl.program_id(1)

    @pl.when(j < n_main)
    def _():
        o_ref[...] = _dot_nt(x_ref[...], w_ref[...].astype(BF16))

    @pl.when(j >= n_main)
    def _():
        o_ref[...] = _dot_nt(x_ref[...], wt_ref[...].astype(BF16))


def in_projection(x, w_t, w_tail_t, layer):
    t, k = x.shape
    n_blk = PROJ_W // IN_TN
    n_main = w_t.shape[1] // IN_TN
    kv_blk = Q_W // IN_TN
    assert 2 * KV_W == IN_TN and COL_K == (n_blk - 1) * IN_TN and n_main == n_blk - 1
    tm = _pick(t, (2176, 1088, 544, 32, 16, 8))

    def out_map(i, j):
        return i, jnp.where(j < kv_blk, j, jnp.where(j == kv_blk, n_blk - 1, j - 1))

    return pl.pallas_call(
        functools.partial(_in_proj_kernel, n_main=n_main),
        grid=(t // tm, n_blk),
        in_specs=[
            pl.BlockSpec((tm, k), lambda i, j: (i, 0)),
            pl.BlockSpec((None, IN_TN, k), lambda i, j: (layer, jnp.minimum(j, n_main - 1), 0)),
            pl.BlockSpec((None, IN_TN, k), lambda i, j: (layer, 0, 0)),
        ],
        out_specs=pl.BlockSpec((tm, IN_TN), out_map),
        out_shape=jax.ShapeDtypeStruct((t, PROJ_W), F32),
        compiler_params=_cparams(2),
        name="in_proj",
    )(x, w_t, w_tail_t)


def _mm_ln_kernel(x_ref, w_ref, r_ref, g_ref, b_ref, of_ref, ob_ref, *, alpha):
    y = _dot(x_ref[...], w_ref[...])
    h = _ln_rows(alpha * r_ref[...] + y, g_ref[...], b_ref[...])
    of_ref[...] = h
    ob_ref[...] = h.astype(BF16)


def matmul_ln(x, w, res, g, b, *, alpha, name="mm_ln"):
    t, k = x.shape
    d = w.shape[1]
    tm = _pick(t, (256, 128, 64, 32, 16, 8))
    return pl.pallas_call(
        functools.partial(_mm_ln_kernel, alpha=alpha),
        grid=(t // tm,),
        in_specs=[
            pl.BlockSpec((tm, k), lambda i: (i, 0)),
            pl.BlockSpec((k, d), lambda i: (0, 0)),
            pl.BlockSpec((tm, d), lambda i: (i, 0)),
            pl.BlockSpec((1, d), lambda i: (0, 0)),
            pl.BlockSpec((1, d), lambda i: (0, 0)),
        ],
        out_specs=[
            pl.BlockSpec((tm, d), lambda i: (i, 0)),
            pl.BlockSpec((tm, d), lambda i: (i, 0)),
        ],
        out_shape=[jax.ShapeDtypeStruct((t, d), F32), jax.ShapeDtypeStruct((t, d), BF16)],
        compiler_params=_cparams(1),
        name=name,
    )(x, w, res, g.reshape(1, d), b.reshape(1, d))


def _merge_kernel(ap_ref, as_ref, bp_ref, bs_ref, cp_ref, cs_ref, wa_ref, wb_ref, wc_ref,
                  ga_ref, ga2_ref, gb_ref, gb2_ref, gc_ref, gc2_ref, o_ref, *, n_p, shift):
    i = pl.program_id(1)
    is_p = i < n_p
    tn = o_ref.shape[1]

    def gate(main_ref, tail_ref):
        win = jnp.concatenate([main_ref[...], tail_ref[...]], axis=1)
        return _sigmoid(win[:, shift:shift + tn])

    xa = jnp.where(is_p, ap_ref[...], as_ref[...])
    xb = jnp.where(is_p, bp_ref[...], bs_ref[...])
    xc = jnp.where(is_p, cp_ref[...], cs_ref[...])
    acc = gate(ga_ref, ga2_ref) * _dot(xa, wa_ref[...])
    acc = acc + gate(gb_ref, gb2_ref) * _dot(xb, wb_ref[...])
    acc = acc + gate(gc_ref, gc2_ref) * _dot(xc, wc_ref[...])
    o_ref[...] = acc.astype(o_ref.dtype)


def gated_merge(att_p, att_s, gm_p, gm_s, ssm_p, ssm_s, w_pa, w_pb, w_pc, proj):
    tp, k = att_p.shape
    ts = att_s.shape[0]
    d = w_pa.shape[1]
    tm = _pick(math.gcd(tp, ts), (256, 128, 64, 32, 16, 8))
    tn = 1024
    n_p, n_s = tp // tm, ts // tm
    shift = COL_GATES - COL_DT
    assert COL_DT % tn == 0 and d % tn == 0 and shift < LANES

    def xp_spec():
        return pl.BlockSpec((tm, k), lambda j, i: (jnp.minimum(i, n_p - 1), 0))

    def xs_spec():
        return pl.BlockSpec((tm, k), lambda j, i: (jnp.maximum(i - n_p, 0), 0))

    def w_spec():
        return pl.BlockSpec((k, tn), lambda j, i: (0, j))

    def g_specs(which):
        base = (COL_DT + which * d) // tn
        tail = (COL_DT + which * d) // LANES
        return [pl.BlockSpec((tm, tn), lambda j, i: (i, base + j)),
                pl.BlockSpec((tm, LANES), lambda j, i: (i, tail + (j + 1) * (tn // LANES)))]

    return pl.pallas_call(
        functools.partial(_merge_kernel, n_p=n_p, shift=shift),
        grid=(d // tn, n_p + n_s),
        in_specs=[xp_spec(), xs_spec(), xp_spec(), xs_spec(), xp_spec(), xs_spec(),
                  w_spec(), w_spec(), w_spec()] + g_specs(0) + g_specs(1) + g_specs(2),
        out_specs=pl.BlockSpec((tm, tn), lambda j, i: (i, j)),
        out_shape=jax.ShapeDtypeStruct((tp + ts, d), BF16),
        compiler_params=_cparams(2),
        name="gated_merge",
    )(att_p, att_s, gm_p, gm_s, ssm_p, ssm_s, w_pa, w_pb, w_pc, *([proj] * 6))


def _rope_block(x, cos, sin_signed, first_half):
    outs = []
    for c in range(x.shape[1] // LANES):
        xc = x[:, c * LANES:(c + 1) * LANES]
        fwd = pltpu.roll(xc, LANES - HEAD_DIM // 2, axis=1)
        bwd = pltpu.roll(xc, HEAD_DIM // 2, axis=1)
        partner = jnp.where(first_half, fwd, bwd)
        outs.append(xc * cos + partner * sin_signed)
    return outs


def _rope_kernel(q_ref, k_ref, cos_ref, sin_ref, qo_ref, ko_ref):
    cos = cos_ref[...]
    sin_signed = sin_ref[...]
    lane = lax.broadcasted_iota(jnp.int32, cos.shape, 1)
    first_half = (lane % HEAD_DIM) < (HEAD_DIM // 2)
    for c, o in enumerate(_rope_block(q_ref[...], cos, sin_signed, first_half)):
        qo_ref[:, c * LANES:(c + 1) * LANES] = o
    for c, o in enumerate(_rope_block(k_ref[...], cos, sin_signed, first_half)):
        ko_ref[:, c * LANES:(c + 1) * LANES] = o


def rope_qk(proj, cos_t, sin_t):
    t = proj.shape[0]
    tm = _pick(t, (256, 128, 64, 32, 16, 8))
    return pl.pallas_call(
        _rope_kernel,
        grid=(t // tm,),
        in_specs=[
            pl.BlockSpec((tm, Q_W), lambda i: (i, COL_Q // Q_W)),
            pl.BlockSpec((tm, KV_W), lambda i: (i, COL_K // KV_W)),
            pl.BlockSpec((tm, LANES), lambda i: (i, 0)),
            pl.BlockSpec((tm, LANES), lambda i: (i, 0)),
        ],
        out_specs=[
            pl.BlockSpec((tm, Q_W), lambda i: (i, 0)),
            pl.BlockSpec((tm, KV_W), lambda i: (i, 0)),
        ],
        out_shape=[jax.ShapeDtypeStruct((t, Q_W), F32), jax.ShapeDtypeStruct((t, KV_W), F32)],
        compiler_params=_cparams(1),
        name="rope_qk",
    )(proj, proj, cos_t, sin_t)


def rope_tables(tp, seq, ts, nb, past_len):
    half = HEAD_DIM // 2
    inv = ROPE_THETA ** (-jnp.arange(half, dtype=F32) / half)
    pos_p = jnp.arange(tp, dtype=jnp.int32) % seq
    pos_s = past_len + jnp.arange(ts, dtype=jnp.int32) // nb
    pos = jnp.concatenate([pos_p, pos_s]).astype(F32)
    ang = pos[:, None] * inv[None, :]
    cos = jnp.tile(jnp.cos(ang), (1, LANES // half))
    sin = jnp.sin(ang)
    sin_signed = jnp.tile(jnp.concatenate([-sin, sin], axis=1), (1, LANES // HEAD_DIM))
    return cos, sin_signed


def _dup_head(slab, g):
    lane = lax.broadcasted_iota(jnp.int32, slab.shape, 1)
    rolled = pltpu.roll(slab, HEAD_DIM, axis=1)
    if g % 2 == 0:
        return jnp.where(lane < HEAD_DIM, slab, rolled)
    return jnp.where(lane < HEAD_DIM, rolled, slab)


def _swa_bias(qb, prev_ok):
    kw = WINDOW + qb
    ii = lax.broadcasted_iota(jnp.int32, (qb, kw), 0)
    jj = lax.broadcasted_iota(jnp.int32, (qb, kw), 1)
    ok_prev = (jj < WINDOW) & (jj > ii)
    if prev_ok is not True:
        ok_prev = ok_prev & prev_ok
    ok = ok_prev | ((jj >= WINDOW) & ((jj - WINDOW) <= ii))
    return jnp.where(ok, 0.0, NEG_BIG)


def _swa_group(q, kdup, vdup, bias, sink_ref, g, qb, store):
    rep = N_HEADS // N_KV_HEADS
    lane = lax.broadcasted_iota(jnp.int32, (qb, LANES), 1)
    scale = HEAD_DIM ** -0.5
    rows = []
    for r in range(rep):
        h = g * rep + r
        qc = q[:, (h // 2) * LANES:(h // 2 + 1) * LANES]
        keep = (lane < HEAD_DIM) if h % 2 == 0 else (lane >= HEAD_DIM)
        rows.append(jnp.where(keep, qc * scale, 0.0))
    s = _dot_nt(jnp.concatenate(rows, axis=0).astype(BF16), kdup)
    ps = []
    for r in range(rep):
        sr = s[r * qb:(r + 1) * qb] + bias
        sink = sink_ref[g * rep + r]
        m = jnp.maximum(jnp.max(sr, axis=-1, keepdims=True), sink)
        p = jnp.exp(sr - m)
        den = jnp.sum(p, axis=-1, keepdims=True) + jnp.exp(sink - m)
        ps.append(p * (1.0 / den))
    o = _dot(jnp.concatenate(ps, axis=0).astype(BF16), vdup)
    for c in range(rep // 2):
        oa = o[(2 * c) * qb:(2 * c + 1) * qb]
        ob = o[(2 * c + 1) * qb:(2 * c + 2) * qb]
        store((g * rep // 2 + c) * LANES, jnp.where(lane < HEAD_DIM, oa, ob))


def _swa_kernel(sink_ref, q_ref, kp_ref, kc_ref, vp_ref, vc_ref, o_ref, *, qb, prev_from_block0):
    n = pl.program_id(1)
    q = q_ref[...]
    kp, kc, vp, vc = kp_ref[...], kc_ref[...], vp_ref[...], vc_ref[...]
    bias = _swa_bias(qb, True if prev_from_block0 else (n > 0))

    def store(col, val):
        o_ref[:, col:col + LANES] = val.astype(o_ref.dtype)

    for g in range(N_KV_HEADS):
        sl = slice((g // 2) * LANES, (g // 2 + 1) * LANES)
        kdup = _dup_head(jnp.concatenate([kp[:, sl], kc[:, sl]], axis=0), g).astype(BF16)
        vdup = _dup_head(jnp.concatenate([vp[:, sl], vc[:, sl]], axis=0), g).astype(BF16)
        _swa_group(q, kdup, vdup, bias, sink_ref, g, qb, store)


def _swa_cache_kernel(sink_ref, q_ref, kc_ref, vc_ref, kp_ref, vp_ref, o_ref, *, qb, bseq):
    rep = N_HEADS // N_KV_HEADS
    bias = jnp.concatenate([_swa_bias(qb, True)] * rep, axis=0)
    q = q_ref[...].reshape(bseq, qb, Q_W)
    kc = kc_ref[...].reshape(bseq, qb, KV_W)
    vc = vc_ref[...].reshape(bseq, qb, KV_W)
    lane = lax.broadcasted_iota(jnp.int32, (bseq, qb, LANES), 2)
    scale = HEAD_DIM ** -0.5
    bdot_nt = lambda a, b: lax.dot_general(a, b, (((2,), (2,)), ((0,), (0,))), preferred_element_type=F32)
    bdot = lambda a, b: lax.dot_general(a, b, (((2,), (1,)), ((0,), (0,))), preferred_element_type=F32)
    for g in range(N_KV_HEADS):
        hs = slice(g * HEAD_DIM, (g + 1) * HEAD_DIM)
        kcat = jnp.concatenate([kp_ref[:, :, g, :], kc[:, :, hs]], axis=1)
        vcat = jnp.concatenate([vp_ref[:, :, g, :], vc[:, :, hs]], axis=1)
        kdup = jnp.concatenate([kcat, kcat], axis=2).astype(BF16)
        vdup = jnp.concatenate([vcat, vcat], axis=2).astype(BF16)
        rows, sinks = [], []
        for r in range(rep):
            h = g * rep + r
            qc = q[:, :, (h // 2) * LANES:(h // 2 + 1) * LANES]
            keep = (lane < HEAD_DIM) if h % 2 == 0 else (lane >= HEAD_DIM)
            rows.append(jnp.where(keep, qc * scale, 0.0))
            sinks.append(jnp.full((qb, 1), sink_ref[h], F32))
        sink = jnp.concatenate(sinks, axis=0)
        s = bdot_nt(jnp.concatenate(rows, axis=1).astype(BF16), kdup) + bias
        m = jnp.maximum(jnp.max(s, axis=-1, keepdims=True), sink)
        p = jnp.exp(s - m)
        den = jnp.sum(p, axis=-1, keepdims=True) + jnp.exp(sink - m)
        o = bdot((p * (1.0 / den)).astype(BF16), vdup)
        for c in range(rep // 2):
            oa = o[:, (2 * c) * qb:(2 * c + 1) * qb]
            ob = o[:, (2 * c + 1) * qb:(2 * c + 2) * qb]
            col = (g * rep // 2 + c) * LANES
            o_ref[:, col:col + LANES] = jnp.where(lane < HEAD_DIM, oa, ob).reshape(bseq * qb, LANES).astype(
                o_ref.dtype)


def swa_attention_cached(sinks, q, kc, vc, cache_k, cache_v, *, layer, n_seq, qb, bseq, name):
    rows = bseq * qb
    cache_spec = pl.BlockSpec((None, bseq, WINDOW, N_KV_HEADS, HEAD_DIM), lambda i: (layer, i, 0, 0, 0))
    return pl.pallas_call(
        functools.partial(_swa_cache_kernel, qb=qb, bseq=bseq),
        grid=(n_seq // bseq,),
        in_specs=[pl.BlockSpec(memory_space=pltpu.SMEM),
                  pl.BlockSpec((rows, Q_W), lambda i: (i, 0)),
                  pl.BlockSpec((rows, KV_W), lambda i: (i, 0)),
                  pl.BlockSpec((rows, KV_W), lambda i: (i, 0)),
                  cache_spec, cache_spec],
        out_specs=pl.BlockSpec((rows, Q_W), lambda i: (i, 0)),
        out_shape=jax.ShapeDtypeStruct((n_seq * qb, Q_W), F32),
        compiler_params=_cparams(1),
        name=name,
    )(sinks, q, kc, vc, cache_k, cache_v)


def swa_attention(sinks, q, kp, kc, vp, vc, *, n_seq, n_blk, qb, prev_from_block0,
                  kp_map, kc_map, vp_map, vc_map, out_dtype, name):
    return pl.pallas_call(
        functools.partial(_swa_kernel, qb=qb, prev_from_block0=prev_from_block0),
        grid=(n_seq, n_blk),
        in_specs=[
            pl.BlockSpec(memory_space=pltpu.SMEM),
            pl.BlockSpec((qb, Q_W), lambda i, n: (i * n_blk + n, 0)),
            pl.BlockSpec((WINDOW, KV_W), kp_map),
            pl.BlockSpec((qb, KV_W), kc_map),
            pl.BlockSpec((WINDOW, KV_W), vp_map),
            pl.BlockSpec((qb, KV_W), vc_map),
        ],
        out_specs=pl.BlockSpec((qb, Q_W), lambda i, n: (i * n_blk + n, 0)),
        out_shape=jax.ShapeDtypeStruct((n_seq * n_blk * qb, Q_W), out_dtype),
        compiler_params=_cparams(2),
        name=name,
    )(sinks, q, kp, kc, vp, vc)


def _gmlp_p_kernel(gu_ref, gv_ref, ws_ref, bst_ref, lg_ref, lb_ref, gm_ref, vg_ref, *, n_chunks):
    n = pl.program_id(1)
    vg = _ln_rows(_gelu(gv_ref[...]), lg_ref[...], lb_ref[...])
    gu = gu_ref[...]
    ri = lax.broadcasted_iota(jnp.int32, (CHUNK, CHUNK), 0)
    ci = lax.broadcasted_iota(jnp.int32, (CHUNK, CHUNK), 1)
    tril = ri >= ci
    bst = bst_ref[...]
    for g in range(GM_GROUPS):
        sl = slice(g * GM_GROUP_DIM, (g + 1) * GM_GROUP_DIM)
        w = jnp.where(tril, ws_ref[g], 0.0).astype(BF16)
        s = _dot(w, vg[:, sl].astype(BF16)) + bst[:, g:g + 1]
        gm_ref[:, sl] = (_gelu(gu[:, sl]) * s).astype(gm_ref.dtype)

    @pl.when(n == n_chunks - 1)
    def _():
        vg_ref[...] = vg


def gmlp_prompt(proj, n_seq, seq, ws, bs, ln_g, ln_b):
    nc = seq // CHUNK
    return pl.pallas_call(
        functools.partial(_gmlp_p_kernel, n_chunks=nc),
        grid=(n_seq, nc),
        in_specs=[
            pl.BlockSpec((CHUNK, GM_W), lambda i, n: (i * nc + n, COL_GU // GM_W)),
            pl.BlockSpec((CHUNK, GM_W), lambda i, n: (i * nc + n, COL_GV // GM_W)),
            pl.BlockSpec((GM_GROUPS, CHUNK, CHUNK), lambda i, n: (0, 0, 0)),
            pl.BlockSpec((CHUNK, GM_GROUPS), lambda i, n: (0, 0)),
            pl.BlockSpec((1, GM_W), lambda i, n: (0, 0)),
            pl.BlockSpec((1, GM_W), lambda i, n: (0, 0)),
        ],
        out_specs=[
            pl.BlockSpec((CHUNK, GM_W), lambda i, n: (i * nc + n, 0)),
            pl.BlockSpec((CHUNK, GM_W), lambda i, n: (i, 0)),
        ],
        out_shape=[jax.ShapeDtypeStruct((n_seq * seq, GM_W), BF16),
                   jax.ShapeDtypeStruct((n_seq * CHUNK, GM_W), F32)],
        compiler_params=_cparams(2),
        name="gmlp_prompt",
    )(proj, proj, ws, bs.T, ln_g.reshape(1, GM_W), ln_b.reshape(1, GM_W))


def _gmlp_s_kernel(*refs, lt):
    gu_refs = refs[:lt]
    gv_refs = refs[lt:2 * lt]
    wrow_ref, brow_ref, lg_ref, lb_ref, gm_ref, vg_ref = refs[2 * lt:]
    nb = gu_refs[0].shape[0]
    vgs = [_ln_rows(_gelu(gv_refs[t][...]), lg_ref[...], lb_ref[...]) for t in range(lt)]
    for i in range(lt):
        s = brow_ref[i:i + 1, :]
        for j in range(i + 1):
            s = s + wrow_ref[i * lt + j:i * lt + j + 1, :] * vgs[j]
        gm_ref[i * nb:(i + 1) * nb, :] = (_gelu(gu_refs[i][...]) * s).astype(gm_ref.dtype)
        vg_ref[i * nb:(i + 1) * nb, :] = vgs[i]


def gmlp_sample(proj, tp, nb, lt, ws, bs, ln_g, ln_b):
    w_small = ws[:, :lt, :lt]
    wrow = jnp.repeat(jnp.transpose(w_small, (1, 2, 0)).reshape(lt * lt, GM_GROUPS), GM_GROUP_DIM, axis=1)
    brow = jnp.repeat(bs[:, :lt].T, GM_GROUP_DIM, axis=1)
    row0 = tp // nb

    def spec(t, col):
        return pl.BlockSpec((nb, GM_W), lambda i: (row0 + t, col // GM_W))

    in_specs = [spec(t, COL_GU) for t in range(lt)] + [spec(t, COL_GV) for t in range(lt)] + [
        pl.BlockSpec((lt * lt, GM_W), lambda i: (0, 0)),
        pl.BlockSpec((lt, GM_W), lambda i: (0, 0)),
        pl.BlockSpec((1, GM_W), lambda i: (0, 0)),
        pl.BlockSpec((1, GM_W), lambda i: (0, 0)),
    ]
    return pl.pallas_call(
        functools.partial(_gmlp_s_kernel, lt=lt),
        grid=(1,),
        in_specs=in_specs,
        out_specs=[pl.BlockSpec((lt * nb, GM_W), lambda i: (0, 0)),
                   pl.BlockSpec((lt * nb, GM_W), lambda i: (0, 0))],
        out_shape=[jax.ShapeDtypeStruct((lt * nb, GM_W), BF16),
                   jax.ShapeDtypeStruct((lt * nb, GM_W), F32)],
        compiler_params=_cparams(1),
        name="gmlp_sample",
    )(*([proj] * (2 * lt)), wrow, brow, ln_g.reshape(1, GM_W), ln_b.reshape(1, GM_W))


def _conv_silu(cur, prev8, w, bias):
    q = cur.shape[0]
    up = jnp.concatenate([prev8, cur], axis=0)
    acc = bias + up[SUBLANES:SUBLANES + q] * w[CONV_K - 1:CONV_K]
    for j in range(CONV_K - 1):
        off = SUBLANES - (CONV_K - 1) + j
        acc = acc + up[off:off + q] * w[j:j + 1]
    return _silu(acc)


def _ssd_p_kernel(z_ref, xs_ref, bc_ref, xsp_ref, bcp_ref, dt_ref, dtt_ref,
                  cwx_ref, cwbc_ref, cbx_ref, cbbc_ref, dtbe_ref, dtbc_ref, aloge_ref, alogc_ref,
                  dske_ref, ng_ref, e_ref, y_ref, st_ref, s_scr, *, n_chunks):
    c = pl.program_id(1)
    q = CHUNK
    rep = SSM_HEADS // SSM_GROUPS
    gw = rep * SSM_HEAD_DIM

    @pl.when(c == 0)
    def _():
        s_scr[...] = jnp.zeros_like(s_scr)

    has_prev = (c > 0).astype(F32)
    xs = _conv_silu(xs_ref[...], xsp_ref[...] * has_prev, cwx_ref[...], cbx_ref[...])
    bcm = _conv_silu(bc_ref[...], bcp_ref[...] * has_prev, cwbc_ref[...], cbbc_ref[...])

    ri = lax.broadcasted_iota(jnp.int32, (q, q), 0)
    ci = lax.broadcasted_iota(jnp.int32, (q, q), 1)
    tril = ri >= ci
    ones_tril = jnp.where(tril, 1.0, 0.0).astype(BF16)
    ones_triu = jnp.where(ri <= ci, 1.0, 0.0).astype(BF16)

    dt_e = _softplus(_exact_dot_left(_split3(dt_ref[...]), e_ref[...]) + dtbe_ref[...])
    a_e = dt_e * (-jnp.exp(aloge_ref[...]))
    a_hi, a_mid, a_lo = _split3(a_e)
    cum_e = _dot(ones_tril, a_hi) + _dot(ones_tril, a_mid) + _dot(ones_tril, a_lo)
    dt_t = _softplus(dtt_ref[...] + dtbc_ref[...])
    a_t = dt_t * (-jnp.exp(alogc_ref[...]))
    cum_t = _exact_dot_left(_split3(a_t), ones_triu)

    xdt = xs * dt_e
    cum_last = cum_e[q - 1:q, :]
    lane = lax.broadcasted_iota(jnp.int32, (q, LANES), 1)
    z = z_ref[...]
    ys = []
    for g in range(SSM_GROUPS):
        gs = slice(g * gw, (g + 1) * gw)
        bg = bcm[:, g * SSM_STATE:(g + 1) * SSM_STATE]
        cg = bcm[:, SSM_BC + g * SSM_STATE:SSM_BC + (g + 1) * SSM_STATE]
        bg16, cg16 = bg.astype(BF16), cg.astype(BF16)
        cb = _dot_nt(cg16, bg16)
        ydiag = []
        for pr in range(rep // 2):
            ms = []
            for hh in (2 * pr, 2 * pr + 1):
                h = g * rep + hh
                col = cum_e[:, h * SSM_HEAD_DIM:h * SSM_HEAD_DIM + 1]
                row = cum_t[h:h + 1, :]
                seg = jnp.where(tril, col - row, NEG_BIG)
                ms.append(cb * jnp.exp(seg))
            lhs = jnp.concatenate(ms, axis=1).astype(BF16)
            xslab = xdt[:, (g * rep + 2 * pr) * SSM_HEAD_DIM:(g * rep + 2 * pr + 2) * SSM_HEAD_DIM]
            xbd = jnp.concatenate([jnp.where(lane < SSM_HEAD_DIM, xslab, 0.0),
                                   jnp.where(lane >= SSM_HEAD_DIM, xslab, 0.0)], axis=0).astype(BF16)
            ydiag.append(_dot(lhs, xbd))
        ydiag = jnp.concatenate(ydiag, axis=1)
        s_old = s_scr[g]
        yoff = _dot(cg16, s_old.astype(BF16)) * jnp.exp(cum_e[:, gs])
        ys.append(ydiag + yoff)
        xw = xdt[:, gs] * jnp.exp(cum_last[:, gs] - cum_e[:, gs])
        s_scr[g] = s_old * jnp.exp(cum_last[:, gs]) + _dot(bg.T.astype(BF16), xw.astype(BF16))
    y = jnp.concatenate(ys, axis=1) + dske_ref[...] * xs
    gated = y * _silu(z)
    out = gated * lax.rsqrt(jnp.mean(gated * gated, axis=-1, keepdims=True) + LN_EPS) * ng_ref[...]
    y_ref[...] = out.astype(y_ref.dtype)

    @pl.when(c == n_chunks - 1)
    def _():
        for g in range(SSM_GROUPS):
            st_ref[0, g * gw:(g + 1) * gw, :] = s_scr[g].T


def _expand_heads(v):
    return jnp.repeat(v.astype(F32), SSM_HEAD_DIM).reshape(1, SSM_INNER)


def _head_expand_matrix():
    e = np.zeros((LANES, SSM_INNER), np.float32)
    for h in range(SSM_HEADS):
        e[h, h * SSM_HEAD_DIM:(h + 1) * SSM_HEAD_DIM] = 1.0
    return jnp.asarray(e, BF16)


def ssd_prompt(proj, dtt, n_seq, seq, conv_w, conv_b, dt_bias, a_log, d_skip, norm_g):
    b = n_seq
    nc = seq // CHUNK
    blk8 = CHUNK // SUBLANES
    const2 = lambda i, c: (0, 0)
    col_bc = COL_BC // (2 * SSM_BC)
    col_xs = COL_XS // SSM_INNER
    col_z = COL_Z // SSM_INNER
    prev_map_x = lambda i, c: (jnp.maximum((i * nc + c) * blk8 - 1, 0), col_xs)
    prev_map_bc = lambda i, c: (jnp.maximum((i * nc + c) * blk8 - 1, 0), col_bc)
    args = (
        proj, proj, proj, proj, proj, proj, dtt,
        conv_w[:, :SSM_INNER], conv_w[:, SSM_INNER:], conv_b[:SSM_INNER].reshape(1, -1),
        conv_b[SSM_INNER:].reshape(1, -1),
        _expand_heads(dt_bias), jnp.broadcast_to(dt_bias.astype(F32)[:, None], (SSM_HEADS, CHUNK)),
        _expand_heads(a_log), jnp.broadcast_to(a_log.astype(F32)[:, None], (SSM_HEADS, CHUNK)),
        _expand_heads(d_skip), norm_g.reshape(1, SSM_INNER), _head_expand_matrix(),
    )
    in_specs = [
        pl.BlockSpec((CHUNK, SSM_INNER), lambda i, c: (i * nc + c, col_z)),
        pl.BlockSpec((CHUNK, SSM_INNER), lambda i, c: (i * nc + c, col_xs)),
        pl.BlockSpec((CHUNK, 2 * SSM_BC), lambda i, c: (i * nc + c, col_bc)),
        pl.BlockSpec((SUBLANES, SSM_INNER), prev_map_x),
        pl.BlockSpec((SUBLANES, 2 * SSM_BC), prev_map_bc),
        pl.BlockSpec((CHUNK, LANES), lambda i, c: (i * nc + c, COL_DT // LANES)),
        pl.BlockSpec((SSM_HEADS, CHUNK), lambda i, c: (0, i * nc + c)),
        pl.BlockSpec((CONV_K, SSM_INNER), const2),
        pl.BlockSpec((CONV_K, 2 * SSM_BC), const2),
        pl.BlockSpec((1, SSM_INNER), const2),
        pl.BlockSpec((1, 2 * SSM_BC), const2),
        pl.BlockSpec((1, SSM_INNER), const2),
        pl.BlockSpec((SSM_HEADS, CHUNK), const2),
        pl.BlockSpec((1, SSM_INNER), const2),
        pl.BlockSpec((SSM_HEADS, CHUNK), const2),
        pl.BlockSpec((1, SSM_INNER), const2),
        pl.BlockSpec((1, SSM_INNER), const2),
        pl.BlockSpec((LANES, SSM_INNER), const2),
    ]
    return pl.pallas_call(
        functools.partial(_ssd_p_kernel, n_chunks=nc),
        grid=(b, nc),
        in_specs=in_specs,
        out_specs=[
            pl.BlockSpec((CHUNK, SSM_INNER), lambda i, c: (i * nc + c, 0)),
            pl.BlockSpec((1, SSM_INNER, SSM_STATE), lambda i, c: (i, 0, 0)),
        ],
        out_shape=[jax.ShapeDtypeStruct((b * seq, SSM_INNER), BF16),
                   jax.ShapeDtypeStruct((b, SSM_INNER, SSM_STATE), F32)],
        scratch_shapes=[pltpu.VMEM((SSM_GROUPS, SSM_STATE, SSM_INNER // SSM_GROUPS), F32)],
        compiler_params=_cparams(2),
        name="ssd_prompt",
    )(*args)


def _group_expand_matrix():
    gw = SSM_INNER // SSM_GROUPS
    m = np.zeros((SSM_BC, SSM_INNER), np.float32)
    for g in range(SSM_GROUPS):
        m[g * SSM_STATE:(g + 1) * SSM_STATE, g * gw:(g + 1) * gw] = 1.0
    return jnp.asarray(m, BF16)


def _ssd_s_pre_kernel(*refs, lt):
    xs_refs = refs[:lt]
    bc_refs = refs[lt:2 * lt]
    dt_refs = refs[2 * lt:3 * lt]
    (cx_ref, cbc_ref, cwx_ref, cwbc_ref, cbx_ref, cbbc_ref, dtbe_ref, aloge_ref, dske_ref, e_ref,
     gmat_ref, c_ref, b_ref, xw_ref, dec_ref, yd_ref, ec_ref) = refs[3 * lt:]
    nprev = CONV_K - 1
    ux = [cx_ref[j] for j in range(nprev)] + [r[...] for r in xs_refs]
    ub = [cbc_ref[j] for j in range(nprev)] + [r[...] for r in bc_refs]
    cwx, cwbc = cwx_ref[...], cwbc_ref[...]
    neg_a = -jnp.exp(aloge_ref[...])
    xs, bm, cm, xdt, cum = [], [], [], [], []
    run = None
    for t in range(lt):
        ax = cbx_ref[...]
        ab = cbbc_ref[...]
        for j in range(CONV_K):
            ax = ax + ux[t + j] * cwx[j:j + 1]
            ab = ab + ub[t + j] * cwbc[j:j + 1]
        x_t = _silu(ax)
        bc_t = _silu(ab)
        dt_e = _softplus(_exact_dot_left(_split3(dt_refs[t][...]), e_ref[...]) + dtbe_ref[...])
        a_t = dt_e * neg_a
        run = a_t if run is None else run + a_t
        xs.append(x_t)
        bm.append(bc_t[:, :SSM_BC])
        cm.append(bc_t[:, SSM_BC:])
        xdt.append(x_t * dt_e)
        cum.append(run)
    for i in range(lt):
        yd = dske_ref[...] * xs[i]
        for j in range(i + 1):
            hi, mid, _ = _split3(cm[i] * bm[j])
            cbe = _dot(hi, gmat_ref[...]) + _dot(mid, gmat_ref[...])
            yd = yd + cbe * jnp.exp(cum[i] - cum[j]) * xdt[j]
        yd_ref[i] = yd
        ec_ref[i] = jnp.exp(cum[i])
        c_ref[i] = cm[i]
        b_ref[i] = bm[i]
        xw_ref[i] = xdt[i] * jnp.exp(cum[lt - 1] - cum[i])
    dec_ref[...] = jnp.exp(cum[lt - 1])


def _rows_block(rows, total):
    c = rows[0].shape[1]
    rid = lax.broadcasted_iota(jnp.int32, (SUBLANES, c), 0)
    acc = jnp.zeros((SUBLANES, c), F32)
    for j, r in enumerate(rows):
        acc = jnp.where(rid == j, jnp.broadcast_to(r, (SUBLANES, c)), acc)
    if total == SUBLANES:
        return acc
    return jnp.concatenate([acc, jnp.zeros((total - SUBLANES, c), F32)], axis=0)


def _ssd_s_state_kernel_inplace(c_ref, b_ref, xw_ref, dec_ref, h0_ref, prev_ref, hn_ref, yr_ref, *, lt):
    del prev_ref
    _ssd_s_state_kernel(c_ref, b_ref, xw_ref, dec_ref, h0_ref, hn_ref, yr_ref, lt=lt)


def _ssd_s_state_kernel(c_ref, b_ref, xw_ref, dec_ref, h0_ref, hn_ref, yr_ref, *, lt, slot=0,
                        fill_slots=None):
    b = pl.program_id(0)
    gw = SSM_INNER // SSM_GROUPS
    c8 = _rows_block([c_ref[i, pl.ds(b, 1), :] for i in range(lt)], SUBLANES).astype(BF16)
    b128 = _rows_block([b_ref[i, pl.ds(b, 1), :] for i in range(lt)], LANES).astype(BF16)
    xaug = _rows_block([xw_ref[i, pl.ds(b, 1), :] for i in range(lt)] + [dec_ref[pl.ds(b, 1), :]], LANES)
    for g in range(SSM_GROUPS):
        hg = h0_ref[0, g * gw:(g + 1) * gw, :]
        yraw = _dot_nt(c8[:, g * SSM_STATE:(g + 1) * SSM_STATE], hg.astype(BF16))
        for i in range(lt):
            yr_ref[i, pl.ds(b, 1), g * gw:(g + 1) * gw] = yraw[i:i + 1, :]
        tr = xaug[:, g * gw:(g + 1) * gw].T
        s = _dot(tr.astype(BF16), b128[:, g * SSM_STATE:(g + 1) * SSM_STATE])
        new = hg * tr[:, lt:lt + 1] + s
        if fill_slots is None:
            hn_ref[0, g * gw:(g + 1) * gw, :] = new
        else:
            hn_ref[slot, 0, g * gw:(g + 1) * gw, :] = new
    if fill_slots is not None:
        for other in fill_slots:
            hn_ref[other] = jnp.zeros(hn_ref.shape[1:], hn_ref.dtype)


def _ssd_s_post_kernel(*refs, lt):
    z_refs = refs[:lt]
    yd_ref, ec_ref, yr_ref, ng_ref, o_ref = refs[lt:]
    nb = z_refs[0].shape[0]
    for i in range(lt):
        y = yd_ref[i] + ec_ref[i] * yr_ref[i]
        gated = y * _silu(z_refs[i][...])
        out = gated * lax.rsqrt(jnp.mean(gated * gated, axis=-1, keepdims=True) + LN_EPS) * ng_ref[...]
        o_ref[i * nb:(i + 1) * nb, :] = out.astype(o_ref.dtype)


def ssd_sample(proj, tp, nb, lt, conv_state, state_all, new_states, layer, conv_w, conv_b, dt_bias,
               a_log, d_skip, norm_g):
    row0 = tp // nb
    cs = jnp.transpose(conv_state, (1, 0, 2))
    one = lambda i: (0, 0)
    one3 = lambda i: (0, 0, 0)

    def rows(t, width, col):
        return pl.BlockSpec((nb, width), lambda i: (row0 + t, col // width))

    in_specs = ([rows(t, SSM_INNER, COL_XS) for t in range(lt)]
                + [rows(t, 2 * SSM_BC, COL_BC) for t in range(lt)]
                + [pl.BlockSpec((nb, LANES), lambda i, t=t: (row0 + t, COL_DT // LANES)) for t in range(lt)]
                + [pl.BlockSpec((CONV_K - 1, nb, SSM_INNER), one3),
                   pl.BlockSpec((CONV_K - 1, nb, 2 * SSM_BC), one3),
                   pl.BlockSpec((CONV_K, SSM_INNER), one),
                   pl.BlockSpec((CONV_K, 2 * SSM_BC), one),
                   pl.BlockSpec((1, SSM_INNER), one),
                   pl.BlockSpec((1, 2 * SSM_BC), one),
                   pl.BlockSpec((1, SSM_INNER), one),
                   pl.BlockSpec((1, SSM_INNER), one),
                   pl.BlockSpec((1, SSM_INNER), one),
                   pl.BlockSpec((LANES, SSM_INNER), one),
                   pl.BlockSpec((SSM_BC, SSM_INNER), one)])
    f3 = lambda w: jax.ShapeDtypeStruct((lt, nb, w), F32)
    c_a, b_a, xw_a, dec_a, yd_a, ec_a = pl.pallas_call(
        functools.partial(_ssd_s_pre_kernel, lt=lt),
        grid=(1,),
        in_specs=in_specs,
        out_specs=[pl.BlockSpec((lt, nb, SSM_BC), one3), pl.BlockSpec((lt, nb, SSM_BC), one3),
                   pl.BlockSpec((lt, nb, SSM_INNER), one3), pl.BlockSpec((nb, SSM_INNER), one),
                   pl.BlockSpec((lt, nb, SSM_INNER), one3), pl.BlockSpec((lt, nb, SSM_INNER), one3)],
        out_shape=[f3(SSM_BC), f3(SSM_BC), f3(SSM_INNER), jax.ShapeDtypeStruct((nb, SSM_INNER), F32),
                   f3(SSM_INNER), f3(SSM_INNER)],
        compiler_params=_cparams(1),
        name="ssd_sample_pre",
    )(*([proj] * (3 * lt)), cs[:, :, :SSM_INNER], cs[:, :, SSM_INNER:],
      conv_w[:, :SSM_INNER], conv_w[:, SSM_INNER:], conv_b[:SSM_INNER].reshape(1, -1),
      conv_b[SSM_INNER:].reshape(1, -1), _expand_heads(dt_bias), _expand_heads(a_log),
      _expand_heads(d_skip), _head_expand_matrix(), _group_expand_matrix())

    depth = state_all.shape[0]
    h0r = state_all.reshape(depth, nb, SSM_INNER, SSM_STATE)
    state_specs = [pl.BlockSpec((lt, nb, SSM_BC), one3), pl.BlockSpec((lt, nb, SSM_BC), one3),
                   pl.BlockSpec((lt, nb, SSM_INNER), one3), pl.BlockSpec((nb, SSM_INNER), one),
                   pl.BlockSpec((None, 1, SSM_INNER, SSM_STATE), lambda i: (layer, i, 0, 0))]
    hn_shape = jax.ShapeDtypeStruct((depth, nb, SSM_INNER, SSM_STATE), F32)
    if new_states is None:
        fill = tuple(s for s in range(depth) if s != layer)
        hn, yr = pl.pallas_call(
            functools.partial(_ssd_s_state_kernel, lt=lt, slot=layer, fill_slots=fill),
            grid=(nb,),
            in_specs=state_specs,
            out_specs=[pl.BlockSpec((depth, 1, SSM_INNER, SSM_STATE), lambda i: (0, i, 0, 0)),
                       pl.BlockSpec((lt, nb, SSM_INNER), one3)],
            out_shape=[hn_shape, f3(SSM_INNER)],
            compiler_params=_cparams(1),
            name="ssd_sample_state",
        )(c_a, b_a, xw_a, dec_a, h0r)
    else:
        hn, yr = pl.pallas_call(
            functools.partial(_ssd_s_state_kernel_inplace, lt=lt),
            grid=(nb,),
            in_specs=state_specs + [pl.BlockSpec(memory_space=pl.ANY)],
            out_specs=[pl.BlockSpec((None, 1, SSM_INNER, SSM_STATE), lambda i: (layer, i, 0, 0)),
                       pl.BlockSpec((lt, nb, SSM_INNER), one3)],
            out_shape=[hn_shape, f3(SSM_INNER)],
            input_output_aliases={5: 0},
            compiler_params=_cparams(1),
            name="ssd_sample_state",
        )(c_a, b_a, xw_a, dec_a, h0r, new_states)

    ssm = pl.pallas_call(
        functools.partial(_ssd_s_post_kernel, lt=lt),
        grid=(1,),
        in_specs=([rows(t, SSM_INNER, COL_Z) for t in range(lt)]
                  + [pl.BlockSpec((lt, nb, SSM_INNER), one3)] * 3 + [pl.BlockSpec((1, SSM_INNER), one)]),
        out_specs=pl.BlockSpec((lt * nb, SSM_INNER), one),
        out_shape=jax.ShapeDtypeStruct((lt * nb, SSM_INNER), BF16),
        compiler_params=_cparams(1),
        name="ssd_sample_post",
    )(*([proj] * lt), yd_a, ec_a, yr, norm_g.reshape(1, SSM_INNER))
    return ssm, hn


def _softmax_rows(s):
    p = jnp.exp(s - jnp.max(s, axis=-1, keepdims=True))
    return p * (1.0 / jnp.sum(p, axis=-1, keepdims=True))


def _xattn_kernel(q_ref, k_ref, v_ref, o_ref, *, nh, bseq, tq):
    scale = MEM_HEAD_DIM ** -0.5
    q = q_ref[...].reshape(bseq, tq, q_ref.shape[1])
    for h in range(nh):
        sl = slice(h * MEM_HEAD_DIM, (h + 1) * MEM_HEAD_DIM)
        s = lax.dot_general(q[:, :, sl].astype(BF16), k_ref[:, :, h, :].astype(BF16),
                            (((2,), (2,)), ((0,), (0,))), preferred_element_type=F32) * scale
        o = lax.dot_general(_softmax_rows(s).astype(BF16), v_ref[:, :, h, :].astype(BF16),
                            (((2,), (1,)), ((0,), (0,))), preferred_element_type=F32)
        o_ref[:, sl] = o.reshape(bseq * tq, MEM_HEAD_DIM).astype(o_ref.dtype)


def cross_attention(q, k, v, *, layer, n_seq, seq, tq, bseq, name):
    w = q.shape[1]
    _, _, m, nh, dh = k.shape
    nq = seq // tq
    assert bseq == 1 or nq == 1
    kv_spec = pl.BlockSpec((None, bseq, m, nh, dh), lambda i, n: (layer, i, 0, 0, 0))
    return pl.pallas_call(
        functools.partial(_xattn_kernel, nh=nh, bseq=bseq, tq=tq),
        grid=(n_seq // bseq, nq),
        in_specs=[pl.BlockSpec((bseq * tq, w), lambda i, n: (i * nq + n, 0)), kv_spec, kv_spec],
        out_specs=pl.BlockSpec((bseq * tq, w), lambda i, n: (i * nq + n, 0)),
        out_shape=jax.ShapeDtypeStruct((n_seq * seq, w), F32),
        compiler_params=_cparams(2),
        name=name,
    )(q, k, v)


def _router_kernel(h_ref, w_ref, b_ref, o_ref):
    logits = _dot(h_ref[...].astype(BF16), w_ref[...].astype(BF16)) + b_ref[...]
    lane = lax.broadcasted_iota(jnp.int32, logits.shape, 1)
    lane_f = lane.astype(F32)
    big = float(LANES)
    is_g = lane < N_EGROUPS
    lg = jnp.where(is_g, logits, NEG_BIG)
    mg = jnp.max(lg, axis=-1, keepdims=True)
    zg = jnp.sum(jnp.where(is_g, jnp.exp(lg - mg), 0.0), axis=-1, keepdims=True)
    gi = jnp.min(jnp.where(is_g & (lg == mg), lane_f, big), axis=-1, keepdims=True)
    gw = 1.0 / zg
    lo = N_EGROUPS + gi * EXPERTS_PER_GROUP
    is_e = (lane_f >= lo) & (lane_f < lo + EXPERTS_PER_GROUP)
    le = jnp.where(is_e, logits, NEG_BIG)
    me = jnp.max(le, axis=-1, keepdims=True)
    ee = jnp.where(is_e, jnp.exp(le - me), 0.0)
    pe = ee / jnp.sum(ee, axis=-1, keepdims=True)
    pe = jnp.where(is_e, pe, -1.0)
    p1 = jnp.max(pe, axis=-1, keepdims=True)
    i1 = jnp.min(jnp.where(pe == p1, lane_f, big), axis=-1, keepdims=True)
    pe2 = jnp.where(lane_f == i1, -1.0, pe)
    p2 = jnp.max(pe2, axis=-1, keepdims=True)
    i2 = jnp.min(jnp.where(pe2 == p2, lane_f, big), axis=-1, keepdims=True)
    tot = p1 + p2
    out = jnp.where(lane == 0, i1 - N_EGROUPS,
                    jnp.where(lane == 1, i2 - N_EGROUPS,
                              jnp.where(lane == 2, gw * (p1 / tot),
                                        jnp.where(lane == 3, gw * (p2 / tot), 0.0))))
    o_ref[...] = out


def moe_router(h, w_rg, b_rg, w_re, b_re):
    t, d = h.shape
    tm = _pick(t, (256, 128, 64, 32, 16, 8))
    npad = LANES - N_EGROUPS - N_EXPERTS
    w = jnp.concatenate([w_rg, w_re, jnp.zeros((d, npad), F32)], axis=1)
    b = jnp.concatenate([b_rg, b_re, jnp.zeros((npad,), F32)]).reshape(1, LANES)
    return pl.pallas_call(
        _router_kernel,
        grid=(t // tm,),
        in_specs=[pl.BlockSpec((tm, d), lambda i: (i, 0)),
                  pl.BlockSpec((d, LANES), lambda i: (0, 0)),
                  pl.BlockSpec((1, LANES), lambda i: (0, 0))],
        out_specs=pl.BlockSpec((tm, LANES), lambda i: (i, 0)),
        out_shape=jax.ShapeDtypeStruct((t, LANES), F32),
        compiler_params=_cparams(1),
        name="moe_router",
    )(h, w, b)


def _row_copy(src_hbm, dst, src_row, dst_row, sem):
    return pltpu.make_async_copy(src_hbm.at[pl.ds(src_row, 1)], dst.at[pl.ds(dst_row, 1)], sem)


def _moe_gather_kernel(tok_ref, h_ref, o_ref, buf, sem, *, tm):
    def start(i, carry):
        for p in range(2):
            r = 2 * i + p
            _row_copy(h_ref, buf, tok_ref[0, 0, r], r, sem).start(priority=p)
        return carry

    def wait(r, carry):
        _row_copy(h_ref, buf, tok_ref[0, 0, r], r, sem).wait()
        return carry

    lax.fori_loop(0, tm // 2, start, 0, unroll=DMA_LOOP_UNROLL // 2)
    lax.fori_loop(0, tm, wait, 0, unroll=DMA_LOOP_UNROLL)
    o_ref[...] = buf[...].astype(o_ref.dtype)


def moe_gather(h, row_token, tm):
    r_total = row_token.shape[0]
    d = h.shape[1]
    nblk = r_total // tm
    return pl.pallas_call(
        functools.partial(_moe_gather_kernel, tm=tm),
        grid=(nblk,),
        in_specs=[pl.BlockSpec((1, 1, tm), lambda i: (i, 0, 0), memory_space=pltpu.SMEM),
                  pl.BlockSpec(memory_space=pl.ANY)],
        out_specs=pl.BlockSpec((tm, d), lambda i: (i, 0)),
        out_shape=jax.ShapeDtypeStruct((r_total, d), BF16),
        scratch_shapes=[pltpu.VMEM((tm, d), h.dtype), pltpu.SemaphoreType.DMA(())],
        compiler_params=_cparams(1),
        name="moe_gather",
    )(row_token.reshape(nblk, 1, tm), h)


def _expert_weight_copies(w_refs, bufs, sems, layer, expert, slot):
    return [pltpu.make_async_copy(w.at[layer, expert], buf.at[slot], sems.at[k, slot])
            for k, (w, buf) in enumerate(zip(w_refs, bufs))]


def _expert_weights_step(s, tv_ref, te_ref, tn_ref, ts_ref, w_refs, bufs, w16s, sems, layer):
    @pl.when(s == 0)
    def _():
        for c in _expert_weight_copies(w_refs, bufs, sems, layer, te_ref[0], 0):
            c.start()

    @pl.when(tv_ref[s] == 2)
    def _():
        slot = ts_ref[s]
        for c in _expert_weight_copies(w_refs, bufs, sems, layer, te_ref[s], slot):
            c.wait()

        @pl.when(tn_ref[s] >= 0)
        def _():
            for c in _expert_weight_copies(w_refs, bufs, sems, layer, tn_ref[s], 1 - slot):
                c.start()

        for buf, w16 in zip(bufs, w16s):
            w16[...] = buf[slot].astype(BF16)


def _moe_up_kernel(tv_ref, tc_ref, te_ref, tn_ref, ts_ref, x_ref, wg_ref, wu_ref, o_ref,
                   gbuf, ubuf, wg16, wu16, sems, *, layer):
    s = pl.program_id(0)
    _expert_weights_step(s, tv_ref, te_ref, tn_ref, ts_ref, (wg_ref, wu_ref), (gbuf, ubuf), (wg16, wu16),
                         sems, layer)

    @pl.when(tv_ref[s] > 0)
    def _():
        x = x_ref[...]
        a = _dot(x, wg16[...])
        u = _dot(x, wu16[...])
        o_ref[...] = (_silu(a) * u).astype(o_ref.dtype)

    @pl.when(tv_ref[s] == 0)
    def _():
        o_ref[...] = jnp.zeros_like(o_ref)


def _moe_tables(plan):
    return plan["tile_v"], plan["tile_c"], plan["tile_e"], plan["tile_next"], plan["tile_slot"]


def moe_up(x_sorted, w_gate, w_up, layer, plan, tm):
    r_total, d = x_sorted.shape
    ff = w_gate.shape[-1]
    n_tiles = r_total // tm
    grid_spec = pltpu.PrefetchScalarGridSpec(
        num_scalar_prefetch=5,
        grid=(n_tiles,),
        in_specs=[
            pl.BlockSpec((tm, d), lambda s, tv, tc, te, tn, ts: (tc[s], 0)),
            pl.BlockSpec(memory_space=pl.ANY),
            pl.BlockSpec(memory_space=pl.ANY),
        ],
        out_specs=pl.BlockSpec((tm, ff), lambda s, tv, tc, te, tn, ts: (s, 0)),
        scratch_shapes=[pltpu.VMEM((2, d, ff), F32), pltpu.VMEM((2, d, ff), F32),
                        pltpu.VMEM((d, ff), BF16), pltpu.VMEM((d, ff), BF16),
                        pltpu.SemaphoreType.DMA((2, 2))],
    )
    return pl.pallas_call(
        functools.partial(_moe_up_kernel, layer=layer),
        grid_spec=grid_spec,
        out_shape=jax.ShapeDtypeStruct((r_total, ff), BF16),
        compiler_params=_cparams(1),
        name="moe_up",
    )(*_moe_tables(plan), x_sorted, w_gate, w_up)


def _moe_down_kernel(tv_ref, tc_ref, te_ref, tn_ref, ts_ref, x_ref, w_ref, o_ref, wbuf, w16, sems, *, layer):
    s = pl.program_id(0)
    _expert_weights_step(s, tv_ref, te_ref, tn_ref, ts_ref, (w_ref,), (wbuf,), (w16,), sems, layer)

    @pl.when(tv_ref[s] > 0)
    def _():
        o_ref[...] = _dot(x_ref[...], w16[...])

    @pl.when(tv_ref[s] == 0)
    def _():
        o_ref[...] = jnp.zeros_like(o_ref)


def moe_down(hid, w_down, layer, plan, tm):
    r_total, ff = hid.shape
    d = w_down.shape[-1]
    n_tiles = r_total // tm
    grid_spec = pltpu.PrefetchScalarGridSpec(
        num_scalar_prefetch=5,
        grid=(n_tiles,),
        in_specs=[
            pl.BlockSpec((tm, ff), lambda s, tv, tc, te, tn, ts: (tc[s], 0)),
            pl.BlockSpec(memory_space=pl.ANY),
        ],
        out_specs=pl.BlockSpec((tm, d), lambda s, tv, tc, te, tn, ts: (s, 0)),
        scratch_shapes=[pltpu.VMEM((2, ff, d), F32), pltpu.VMEM((ff, d), BF16),
                        pltpu.SemaphoreType.DMA((1, 2))],
    )
    return pl.pallas_call(
        functools.partial(_moe_down_kernel, layer=layer),
        grid_spec=grid_spec,
        out_shape=jax.ShapeDtypeStruct((r_total, d), F32),
        compiler_params=_cparams(1),
        name="moe_down",
    )(*_moe_tables(plan), hid, w_down)


def _moe_combine_kernel(pos_ref, y_ref, r_ref, h_ref, g_ref, b_ref, o1_ref, o2_ref, ybuf, sem, *,
                        tm, alpha, n_first):
    def start(i, carry):
        _row_copy(y_ref, ybuf.at[0], pos_ref[0, 0, 2 * i], i, sem).start(priority=0)
        _row_copy(y_ref, ybuf.at[1], pos_ref[0, 0, 2 * i + 1], i, sem).start(priority=1)
        return carry

    def wait(i, carry):
        _row_copy(y_ref, ybuf.at[0], pos_ref[0, 0, 2 * i], i, sem).wait()
        _row_copy(y_ref, ybuf.at[1], pos_ref[0, 0, 2 * i + 1], i, sem).wait()
        return carry

    lax.fori_loop(0, tm, start, 0, unroll=DMA_LOOP_UNROLL)
    lax.fori_loop(0, tm, wait, 0, unroll=DMA_LOOP_UNROLL)
    route = r_ref[...]
    ff = ybuf[0] * route[:, 2:3] + ybuf[1] * route[:, 3:4]
    h = _ln_rows(alpha * h_ref[...] + ff, g_ref[...], b_ref[...])
    if n_first is None:
        o1_ref[...] = h
        o2_ref[...] = h.astype(BF16)
    else:
        i = pl.program_id(0)

        @pl.when(i < n_first)
        def _():
            o1_ref[...] = h

        @pl.when(i >= n_first)
        def _():
            o2_ref[...] = h


def moe_combine(y_sorted, pos, route, h, g, b, *, alpha, split_rows=None):
    t, d = h.shape
    tm = _pick(t if split_rows is None else math.gcd(split_rows, t - split_rows), (256, 128, 64, 32, 16, 8))
    nblk = t // tm
    if split_rows is None:
        n_first = None
        out_specs = [pl.BlockSpec((tm, d), lambda i: (i, 0)), pl.BlockSpec((tm, d), lambda i: (i, 0))]
        out_shape = [jax.ShapeDtypeStruct((t, d), F32), jax.ShapeDtypeStruct((t, d), BF16)]
    else:
        n_first = split_rows // tm
        out_specs = [pl.BlockSpec((tm, d), lambda i: (jnp.minimum(i, n_first - 1), 0)),
                     pl.BlockSpec((tm, d), lambda i: (jnp.maximum(i - n_first, 0), 0))]
        out_shape = [jax.ShapeDtypeStruct((split_rows, d), F32),
                     jax.ShapeDtypeStruct((t - split_rows, d), F32)]
    return pl.pallas_call(
        functools.partial(_moe_combine_kernel, tm=tm, alpha=alpha, n_first=n_first),
        grid=(nblk,),
        in_specs=[pl.BlockSpec((1, 1, 2 * tm), lambda i: (i, 0, 0), memory_space=pltpu.SMEM),
                  pl.BlockSpec(memory_space=pl.ANY),
                  pl.BlockSpec((tm, LANES), lambda i: (i, 0)),
                  pl.BlockSpec((tm, d), lambda i: (i, 0)),
                  pl.BlockSpec((1, d), lambda i: (0, 0)),
                  pl.BlockSpec((1, d), lambda i: (0, 0))],
        out_specs=out_specs,
        out_shape=out_shape,
        scratch_shapes=[pltpu.VMEM((2, tm, d), F32), pltpu.SemaphoreType.DMA(())],
        compiler_params=_cparams(1),
        name="moe_combine",
    )(pos.reshape(nblk, 1, 2 * tm), y_sorted, route, h, g.reshape(1, d), b.reshape(1, d))


def moe_plan(route, tm):
    t = route.shape[0]
    eid = route[:, :2].astype(jnp.int32).reshape(-1)
    onehot = (eid[:, None] == jnp.arange(N_EXPERTS, dtype=jnp.int32)[None, :]).astype(jnp.int32)
    csum = jnp.cumsum(onehot, axis=0)
    rank = jnp.sum((csum - onehot) * onehot, axis=1)
    counts = csum[-1]
    tiles_e = (counts + tm - 1) // tm
    tile_end = jnp.cumsum(tiles_e)
    tile_start = tile_end - tiles_e
    n_used = tile_end[-1]
    n_tiles = (2 * t + N_EXPERTS * (tm - 1)) // tm + 1
    r_total = n_tiles * tm
    dest = tile_start[eid] * tm + rank
    row_token = (jnp.arange(r_total, dtype=jnp.int32) % t).at[dest].set(
        jnp.arange(2 * t, dtype=jnp.int32) // 2)
    tile_ids = jnp.arange(n_tiles, dtype=jnp.int32)
    tile_clamped = jnp.minimum(tile_ids, n_used - 1)
    tile_e = jnp.sum(tile_end[None, :] <= tile_clamped[:, None], axis=1).astype(jnp.int32)
    tile_first = tile_ids == tile_start[tile_e]
    tile_v = jnp.where(tile_ids < n_used, 1 + tile_first.astype(jnp.int32), 0).astype(jnp.int32)
    e_ids = jnp.arange(N_EXPERTS, dtype=jnp.int32)
    nonempty = tiles_e > 0
    cand = jnp.where(nonempty[None, :] & (e_ids[None, :] > e_ids[:, None]), e_ids[None, :], N_EXPERTS)
    next_e = jnp.min(cand, axis=1)
    next_e = jnp.where(next_e >= N_EXPERTS, -1, next_e).astype(jnp.int32)
    slot_e = ((jnp.cumsum(nonempty.astype(jnp.int32)) - 1) % 2).astype(jnp.int32)
    return dict(row_token=row_token, pos=dest.astype(jnp.int32), tile_v=tile_v, tile_c=tile_clamped,
                tile_e=tile_e, tile_next=next_e[tile_e], tile_slot=slot_e[tile_e])


def hierarchical_moe_ln(hf, layer, w_rg, b_rg, w_re, b_re, w_gate, w_up, w_down, ln_g, ln_b, *, alpha,
                        split_rows=None):
    route = moe_router(hf, w_rg, b_rg, w_re, b_re)
    plan = moe_plan(route, MOE_TM)
    x_sorted = moe_gather(hf, plan["row_token"], MOE_TM)
    hid = moe_up(x_sorted, w_gate, w_up, layer, plan, MOE_TM)
    y_sorted = moe_down(hid, w_down, layer, plan, MOE_TM)
    return moe_combine(y_sorted, plan["pos"], route, hf, ln_g, ln_b, alpha=alpha, split_rows=split_rows)


def _to_seq_major(x_tm, lt, nb, pad_to):
    w = x_tm.shape[1]
    x = jnp.transpose(x_tm.reshape(lt, nb, w), (1, 0, 2))
    x = jnp.pad(x, ((0, 0), (0, pad_to - lt), (0, 0)))
    return x.reshape(nb * pad_to, w)


def _to_time_major(x_sm, lt, nb, pad_to):
    w = x_sm.shape[1]
    x = x_sm.reshape(nb, pad_to, w)[:, :lt]
    return jnp.transpose(x, (1, 0, 2)).reshape(lt * nb, w)


def kernel(x_prompt, x_sample, mem_prompt, cache_swa_k, cache_swa_v, cache_mem_k, cache_mem_v, state_conv, state_ssm, ln_in_g, ln_in_b, w_in, attn_sinks, gm_ln_g, gm_ln_b, gm_ws, gm_bs, conv_w, conv_b, dt_bias, a_log, d_skip, ssm_norm_g, w_pa, w_pb, w_pc, w_o, ln1_g, ln1_b, w_cq, w_ck, w_cv, w_co, ln2_g, ln2_b, w_rg, b_rg, w_re, b_re, w_gate, w_up, w_down, ln3_g, ln3_b):
    bp, seq, d = x_prompt.shape
    nb, lt, _ = x_sample.shape
    depth = w_in.shape[0]
    mem_len = mem_prompt.shape[1]
    past_len = PAST_LEN
    wb = cache_swa_k.shape[2]
    assert wb == WINDOW and seq % CHUNK == 0 and lt <= SUBLANES
    tp, ts = bp * seq, nb * lt
    alpha = (2 * depth) ** 0.25
    qpad = SUBLANES

    xp = x_prompt.reshape(tp, d)
    xs = jnp.transpose(x_sample, (1, 0, 2)).reshape(ts, d)
    hf, hb = ln_in(xp, xs, ln_in_g, ln_in_b)
    cos_t, sin_t = rope_tables(tp, seq, ts, nb, past_len)
    mem_b = mem_prompt.reshape(bp * mem_len, d).astype(BF16)

    in_w = w_in.shape[2]
    assert in_w == Q_W + 2 * KV_W + 2 * GM_W + SSM_INNER + CONV_DIM + SSM_HEADS + 3 * D_MODEL
    n_whole = (in_w // IN_TN) * IN_TN
    w_t = jnp.swapaxes(w_in, 1, 2)
    w_tail = jnp.pad(w_t[:, n_whole:, :], ((0, 0), (0, IN_TN - (in_w - n_whole)), (0, 0)))

    outs = {k: [] for k in ("p_k", "p_v", "p_mk", "p_mv", "p_conv", "p_ssm", "p_gv",
                            "s_k", "s_v", "s_conv", "s_ssm", "s_gv")}
    n_qblk = seq // WINDOW
    s_states = None
    for l in range(depth):
        proj = in_projection(hb, w_t, w_tail, l)
        q_rot, k_rot = rope_qk(proj, cos_t, sin_t)

        kcol, vcol = 0, COL_V // KV_W
        att_p = swa_attention(
            attn_sinks[l], q_rot, k_rot, k_rot, proj, proj,
            n_seq=bp, n_blk=n_qblk, qb=WINDOW, prev_from_block0=False,
            kp_map=lambda i, n: (jnp.maximum(i * n_qblk + n - 1, 0), kcol),
            kc_map=lambda i, n: (i * n_qblk + n, kcol),
            vp_map=lambda i, n: (jnp.maximum(i * n_qblk + n - 1, 0), vcol),
            vc_map=lambda i, n: (i * n_qblk + n, vcol),
            out_dtype=BF16, name="swa_prompt")
        k_s_tm = k_rot[tp:]
        v_s_tm = proj[tp:, COL_V:COL_V + KV_W]
        q_s = _to_seq_major(q_rot[tp:], lt, nb, qpad)
        k_s = _to_seq_major(k_s_tm, lt, nb, qpad)
        v_s = _to_seq_major(v_s_tm, lt, nb, qpad)
        att_s8 = swa_attention_cached(attn_sinks[l], q_s, k_s, v_s, cache_swa_k, cache_swa_v, layer=l,
                                      n_seq=nb, qb=qpad, bseq=_pick(nb, (8, 4, 2, 1)), name="swa_sample")
        att_s = _to_time_major(att_s8, lt, nb, qpad).astype(BF16)
        last_w = lambda a, c0: jnp.stack(
            [a[(i + 1) * seq - WINDOW:(i + 1) * seq, c0:c0 + KV_W] for i in range(bp)]
        ).reshape(bp, WINDOW, N_KV_HEADS, HEAD_DIM)
        outs["p_k"].append(last_w(k_rot, 0))
        outs["p_v"].append(last_w(proj, COL_V))
        k_new = jnp.transpose(k_s_tm.reshape(lt, nb, N_KV_HEADS, HEAD_DIM), (1, 0, 2, 3))
        v_new = jnp.transpose(v_s_tm.reshape(lt, nb, N_KV_HEADS, HEAD_DIM), (1, 0, 2, 3))
        outs["s_k"].append(jnp.concatenate([cache_swa_k[l], k_new], axis=1)[:, -wb:])
        outs["s_v"].append(jnp.concatenate([cache_swa_v[l], v_new], axis=1)[:, -wb:])

        gm_p, vg_last = gmlp_prompt(proj, bp, seq, gm_ws[l], gm_bs[l], gm_ln_g[l], gm_ln_b[l])
        gm_s, vg_s = gmlp_sample(proj, tp, nb, lt, gm_ws[l], gm_bs[l], gm_ln_g[l], gm_ln_b[l])
        outs["p_gv"].append(vg_last.reshape(bp, CHUNK, GM_GROUPS, GM_GROUP_DIM))
        outs["s_gv"].append(jnp.transpose(vg_s.reshape(lt, nb, GM_GROUPS, GM_GROUP_DIM), (1, 0, 2, 3)))

        dtt = jnp.transpose(proj[:tp, COL_DT:COL_DT + SSM_HEADS])
        y_p, st_p = ssd_prompt(proj, dtt, bp, seq, conv_w[l], conv_b[l], dt_bias[l], a_log[l],
                               d_skip[l], ssm_norm_g[l])
        ssm_s, s_states = ssd_sample(proj, tp, nb, lt, state_conv[l], state_ssm, s_states, l, conv_w[l],
                                     conv_b[l], dt_bias[l], a_log[l], d_skip[l], ssm_norm_g[l])
        outs["p_conv"].append(jnp.stack(
            [proj[(i + 1) * seq - (CONV_K - 1):(i + 1) * seq, COL_XS:COL_XS + CONV_DIM] for i in range(bp)]))
        xbc_s = jnp.transpose(proj[tp:, COL_XS:COL_XS + CONV_DIM].reshape(lt, nb, CONV_DIM), (1, 0, 2))
        outs["s_conv"].append(jnp.concatenate([state_conv[l], xbc_s], axis=1)[:, -(CONV_K - 1):])
        outs["p_ssm"].append(st_p.reshape(bp, SSM_HEADS, SSM_HEAD_DIM, SSM_STATE))

        merged = gated_merge(att_p, att_s, gm_p, gm_s, y_p, ssm_s, w_pa[l].astype(BF16),
                             w_pb[l].astype(BF16), w_pc[l].astype(BF16), proj)
        h1f, h1b = matmul_ln(merged, w_o[l].astype(BF16), hf, ln1_g[l], ln1_b[l], alpha=alpha, name="out_proj_ln1")

        qc = matmul(h1b, w_cq[l].astype(BF16), name="xattn_q")
        pmk = matmul(mem_b, w_ck[l].astype(BF16), name="mem_k")
        pmv = matmul(mem_b, w_cv[l].astype(BF16), name="mem_v")
        pmk5 = pmk.reshape(1, bp, mem_len, MEM_HEADS, MEM_HEAD_DIM)
        pmv5 = pmv.reshape(1, bp, mem_len, MEM_HEADS, MEM_HEAD_DIM)
        outs["p_mk"].append(pmk5[0])
        outs["p_mv"].append(pmv5[0])
        tq = _pick(seq, (512, 256, 128))
        o_p = cross_attention(qc, pmk5, pmv5, layer=0, n_seq=bp, seq=seq, tq=tq, bseq=1,
                              name="xattn_prompt")
        qc_s = _to_seq_major(qc[tp:], lt, nb, qpad)
        o_s8 = cross_attention(qc_s, cache_mem_k, cache_mem_v, layer=l, n_seq=nb, seq=qpad, tq=qpad,
                               bseq=_pick(nb, (4, 2, 1)), name="xattn_sample")
        o_all = jnp.concatenate([o_p, _to_time_major(o_s8, lt, nb, qpad)], axis=0).astype(BF16)
        h2f, h2b = matmul_ln(o_all, w_co[l].astype(BF16), h1f, ln2_g[l], ln2_b[l], alpha=alpha, name="xattn_out_ln2")

        hf, hb = hierarchical_moe_ln(h2f, l, w_rg[l], b_rg[l], w_re[l], b_re[l], w_gate, w_up, w_down,
                                     ln3_g[l], ln3_b[l], alpha=alpha,
                                     split_rows=tp if l == depth - 1 else None)

    y_prompt = hf.reshape(bp, seq, d)
    y_sample = jnp.transpose(hb.reshape(lt, nb, d), (1, 0, 2))
    st = lambda k: jnp.stack(outs[k])
    s_ssm = s_states.reshape(depth, nb, SSM_HEADS, SSM_HEAD_DIM, SSM_STATE)
    return (y_prompt, y_sample, st("p_k"), st("p_v"), st("p_mk"), st("p_mv"), st("p_conv"), st("p_ssm"),
            st("p_gv"), st("s_k"), st("s_v"), st("s_conv"), s_ssm, st("s_gv"))
```

```python
import functools
import math

import numpy as np
import jax
import jax.numpy as jnp
from jax import lax
from jax.experimental import pallas as pl
from jax.experimental.pallas import tpu as pltpu

F32 = jnp.float32
BF16 = jnp.bfloat16

D_MODEL = 2048
N_HEADS = 32
N_KV_HEADS = 4
HEAD_DIM = 64
WINDOW = 128
PAST_LEN = 8192
ROPE_THETA = 10000.0
CHUNK = 128
GM_GROUPS = 16
GM_GROUP_DIM = 128
SSM_HEADS = 32
SSM_HEAD_DIM = 64
SSM_GROUPS = 4
SSM_STATE = 128
CONV_K = 4
MEM_HEADS = 4
MEM_HEAD_DIM = 128
N_EGROUPS = 4
EXPERTS_PER_GROUP = 8
N_EXPERTS = N_EGROUPS * EXPERTS_PER_GROUP
EXPERT_FF = D_MODEL // 2
Q_W = N_HEADS * HEAD_DIM
KV_W = N_KV_HEADS * HEAD_DIM
GM_W = GM_GROUPS * GM_GROUP_DIM
SSM_INNER = SSM_HEADS * SSM_HEAD_DIM
SSM_BC = SSM_GROUPS * SSM_STATE
CONV_DIM = SSM_INNER + 2 * SSM_BC
MEM_W = MEM_HEADS * MEM_HEAD_DIM
LN_EPS = 1e-5
NEG_BIG = -1e30

VMEM_LIMIT_BYTES = 52 * 1024 * 1024
LANES = 128
SUBLANES = 8

IN_TN = 512
COL_Q = 0
COL_GU = 2048
COL_GV = 4096
COL_Z = 6144
COL_XS = 8192
COL_BC = 10240
COL_DT = 11264
COL_GATES = COL_DT + SSM_HEADS
COL_K = 17920
COL_V = 18176
PROJ_W = 18432

MOE_TM = 256
DMA_LOOP_UNROLL = 8


def _cparams(n_grid):
    return pltpu.CompilerParams(
        dimension_semantics=("arbitrary",) * n_grid,
        vmem_limit_bytes=VMEM_LIMIT_BYTES,
    )


def _pick(n, prefs):
    for p in prefs:
        if n % p == 0:
            return p
    raise ValueError(f"no tile for {n} in {prefs}")


def _ln_rows(x, g, b):
    mu = jnp.mean(x, axis=-1, keepdims=True)
    xc = x - mu
    var = jnp.mean(xc * xc, axis=-1, keepdims=True)
    return xc * lax.rsqrt(var + LN_EPS) * g + b


def _sigmoid(x):
    return 1.0 / (1.0 + jnp.exp(-x))


def _silu(x):
    return x * _sigmoid(x)


def _softplus(x):
    return jnp.maximum(x, 0.0) + jnp.log1p(jnp.exp(-jnp.abs(x)))


def _gelu(x):
    return jax.nn.gelu(x, approximate=True)


def _split3(x):
    hi = x.astype(BF16)
    r1 = x - hi.astype(F32)
    mid = r1.astype(BF16)
    lo = (r1 - mid.astype(F32)).astype(BF16)
    return hi, mid, lo


def _dot(a, b):
    return jnp.dot(a, b, preferred_element_type=F32)


def _dot_nt(a, b):
    return lax.dot_general(a, b, (((1,), (1,)), ((), ())), preferred_element_type=F32)


def _exact_dot_left(pieces, m):
    acc = _dot(pieces[0], m)
    for p in pieces[1:]:
        acc = acc + _dot(p, m)
    return acc


def _ln_in_kernel(xp_ref, xs_ref, g_ref, b_ref, of_ref, ob_ref, *, n_p):
    i = pl.program_id(0)

    @pl.when(i < n_p)
    def _():
        y = _ln_rows(xp_ref[...], g_ref[...], b_ref[...])
        of_ref[...] = y
        ob_ref[...] = y.astype(BF16)

    @pl.when(i >= n_p)
    def _():
        y = _ln_rows(xs_ref[...], g_ref[...], b_ref[...])
        of_ref[...] = y
        ob_ref[...] = y.astype(BF16)


def ln_in(xp, xs, g, b):
    tp, d = xp.shape
    ts = xs.shape[0]
    tm = _pick(math.gcd(tp, ts), (256, 128, 64, 32, 16, 8))
    n_p, n_s = tp // tm, ts // tm
    t = tp + ts
    return pl.pallas_call(
        functools.partial(_ln_in_kernel, n_p=n_p),
        grid=(n_p + n_s,),
        in_specs=[
            pl.BlockSpec((tm, d), lambda i: (jnp.minimum(i, n_p - 1), 0)),
            pl.BlockSpec((tm, d), lambda i: (jnp.maximum(i - n_p, 0), 0)),
            pl.BlockSpec((1, d), lambda i: (0, 0)),
            pl.BlockSpec((1, d), lambda i: (0, 0)),
        ],
        out_specs=[
            pl.BlockSpec((tm, d), lambda i: (i, 0)),
            pl.BlockSpec((tm, d), lambda i: (i, 0)),
        ],
        out_shape=[jax.ShapeDtypeStruct((t, d), F32), jax.ShapeDtypeStruct((t, d), BF16)],
        compiler_params=_cparams(1),
        name="ln_in",
    )(xp, xs, g.reshape(1, d), b.reshape(1, d))


def _mm_kernel(x_ref, w_ref, o_ref):
    o_ref[...] = _dot(x_ref[...], w_ref[...]).astype(o_ref.dtype)


def matmul(x, w, *, out_dtype=F32, tm_prefs=(1088, 1024, 512, 256, 128, 64, 32, 16, 8),
           tn_prefs=(1280, 1024, 512, 256, 128), name="mm"):
    t, k = x.shape
    n = w.shape[1]
    tm = _pick(t, tm_prefs)
    tn = _pick(n, tn_prefs)
    return pl.pallas_call(
        _mm_kernel,
        grid=(n // tn, t // tm),
        in_specs=[
            pl.BlockSpec((tm, k), lambda j, i: (i, 0)),
            pl.BlockSpec((k, tn), lambda j, i: (0, j)),
        ],
        out_specs=pl.BlockSpec((tm, tn), lambda j, i: (i, j)),
        out_shape=jax.ShapeDtypeStruct((t, n), out_dtype),
        compiler_params=_cparams(2),
        name=name,
    )(x, w)


def _in_proj_kernel(x_ref, w_ref, wt_ref, o_ref, *, n_main):
    j = pl.program_id(1)

    @pl.when(j < n_main)
    def _():
        o_ref[...] = _dot_nt(x_ref[...], w_ref[...].astype(BF16))

    @pl.when(j >= n_main)
    def _():
        o_ref[...] = _dot_nt(x_ref[...], wt_ref[...].astype(BF16))


def in_projection(x, w_t, w_tail_t, layer):
    t, k = x.shape
    n_blk = PROJ_W // IN_TN
    n_main = w_t.shape[1] // IN_TN
    kv_blk = Q_W // IN_TN
    assert 2 * KV_W == IN_TN and COL_K == (n_blk - 1) * IN_TN and n_main == n_blk - 1
    tm = _pick(t, (2176, 1088, 544, 32, 16, 8))

    def out_map(i, j):
        return i, jnp.where(j < kv_blk, j, jnp.where(j == kv_blk, n_blk - 1, j - 1))

    return pl.pallas_call(
        functools.partial(_in_proj_kernel, n_main=n_main),
        grid=(t // tm, n_blk),
        in_specs=[
            pl.BlockSpec((tm, k), lambda i, j: (i, 0)),
            pl.BlockSpec((None, IN_TN, k), lambda i, j: (layer, jnp.minimum(j, n_main - 1), 0)),
            pl.BlockSpec((None, IN_TN, k), lambda i, j: (layer, 0, 0)),
        ],
        out_specs=pl.BlockSpec((tm, IN_TN), out_map),
        out_shape=jax.ShapeDtypeStruct((t, PROJ_W), F32),
        compiler_params=_cparams(2),
        name="in_proj",
    )(x, w_t, w_tail_t)


def _mm_ln_kernel(x_ref, w_ref, r_ref, g_ref, b_ref, of_ref, ob_ref, *, alpha):
    y = _dot(x_ref[...], w_ref[...])
    h = _ln_rows(alpha * r_ref[...] + y, g_ref[...], b_ref[...])
    of_ref[...] = h
    ob_ref[...] = h.astype(BF16)


def matmul_ln(x, w, res, g, b, *, alpha, name="mm_ln"):
    t, k = x.shape
    d = w.shape[1]
    tm = _pick(t, (256, 128, 64, 32, 16, 8))
    return pl.pallas_call(
        functools.partial(_mm_ln_kernel, alpha=alpha),
        grid=(t // tm,),
        in_specs=[
            pl.BlockSpec((tm, k), lambda i: (i, 0)),
            pl.BlockSpec((k, d), lambda i: (0, 0)),
            pl.BlockSpec((tm, d), lambda i: (i, 0)),
            pl.BlockSpec((1, d), lambda i: (0, 0)),
            pl.BlockSpec((1, d), lambda i: (0, 0)),
        ],
        out_specs=[
            pl.BlockSpec((tm, d), lambda i: (i, 0)),
            pl.BlockSpec((tm, d), lambda i: (i, 0)),
        ],
        out_shape=[jax.ShapeDtypeStruct((t, d), F32), jax.ShapeDtypeStruct((t, d), BF16)],
        compiler_params=_cparams(1),
        name=name,
    )(x, w, res, g.reshape(1, d), b.reshape(1, d))


def _merge_kernel(ap_ref, as_ref, bp_ref, bs_ref, cp_ref, cs_ref, wa_ref, wb_ref, wc_ref,
                  ga_ref, ga2_ref, gb_ref, gb2_ref, gc_ref, gc2_ref, o_ref, *, n_p, shift):
    i = pl.program_id(1)
    is_p = i < n_p
    tn = o_ref.shape[1]

    def gate(main_ref, tail_ref):
        win = jnp.concatenate([main_ref[...], tail_ref[...]], axis=1)
        return _sigmoid(win[:, shift:shift + tn])

    xa = jnp.where(is_p, ap_ref[...], as_ref[...])
    xb = jnp.where(is_p, bp_ref[...], bs_ref[...])
    xc = jnp.where(is_p, cp_ref[...], cs_ref[...])
    acc = gate(ga_ref, ga2_ref) * _dot(xa, wa_ref[...])
    acc = acc + gate(gb_ref, gb2_ref) * _dot(xb, wb_ref[...])
    acc = acc + gate(gc_ref, gc2_ref) * _dot(xc, wc_ref[...])
    o_ref[...] = acc.astype(o_ref.dtype)


def gated_merge(att_p, att_s, gm_p, gm_s, ssm_p, ssm_s, w_pa, w_pb, w_pc, proj):
    tp, k = att_p.shape
    ts = att_s.shape[0]
    d = w_pa.shape[1]
    tm = _pick(math.gcd(tp, ts), (256, 128, 64, 32, 16, 8))
    tn = 1024
    n_p, n_s = tp // tm, ts // tm
    shift = COL_GATES - COL_DT
    assert COL_DT % tn == 0 and d % tn == 0 and shift < LANES

    def xp_spec():
        return pl.BlockSpec((tm, k), lambda j, i: (jnp.minimum(i, n_p - 1), 0))

    def xs_spec():
        return pl.BlockSpec((tm, k), lambda j, i: (jnp.maximum(i - n_p, 0), 0))

    def w_spec():
        return pl.BlockSpec((k, tn), lambda j, i: (0, j))

    def g_specs(which):
        base = (COL_DT + which * d) // tn
        tail = (COL_DT + which * d) // LANES
        return [pl.BlockSpec((tm, tn), lambda j, i: (i, base + j)),
                pl.BlockSpec((tm, LANES), lambda j, i: (i, tail + (j + 1) * (tn // LANES)))]

    return pl.pallas_call(
        functools.partial(_merge_kernel, n_p=n_p, shift=shift),
        grid=(d // tn, n_p + n_s),
        in_specs=[xp_spec(), xs_spec(), xp_spec(), xs_spec(), xp_spec(), xs_spec(),
                  w_spec(), w_spec(), w_spec()] + g_specs(0) + g_specs(1) + g_specs(2),
        out_specs=pl.BlockSpec((tm, tn), lambda j, i: (i, j)),
        out_shape=jax.ShapeDtypeStruct((tp + ts, d), BF16),
        compiler_params=_cparams(2),
        name="gated_merge",
    )(att_p, att_s, gm_p, gm_s, ssm_p, ssm_s, w_pa, w_pb, w_pc, *([proj] * 6))


def _rope_block(x, cos, sin_signed, first_half):
    outs = []
    for c in range(x.shape[1] // LANES):
        xc = x[:, c * LANES:(c + 1) * LANES]
        fwd = pltpu.roll(xc, LANES - HEAD_DIM // 2, axis=1)
        bwd = pltpu.roll(xc, HEAD_DIM // 2, axis=1)
        partner = jnp.where(first_half, fwd, bwd)
        outs.append(xc * cos + partner * sin_signed)
    return outs


def _rope_kernel(q_ref, k_ref, cos_ref, sin_ref, qo_ref, ko_ref):
    cos = cos_ref[...]
    sin_signed = sin_ref[...]
    lane = lax.broadcasted_iota(jnp.int32, cos.shape, 1)
    first_half = (lane % HEAD_DIM) < (HEAD_DIM // 2)
    for c, o in enumerate(_rope_block(q_ref[...], cos, sin_signed, first_half)):
        qo_ref[:, c * LANES:(c + 1) * LANES] = o
    for c, o in enumerate(_rope_block(k_ref[...], cos, sin_signed, first_half)):
        ko_ref[:, c * LANES:(c + 1) * LANES] = o


def rope_qk(proj, cos_t, sin_t):
    t = proj.shape[0]
    tm = _pick(t, (256, 128, 64, 32, 16, 8))
    return pl.pallas_call(
        _rope_kernel,
        grid=(t // tm,),
        in_specs=[
            pl.BlockSpec((tm, Q_W), lambda i: (i, COL_Q // Q_W)),
            pl.BlockSpec((tm, KV_W), lambda i: (i, COL_K // KV_W)),
            pl.BlockSpec((tm, LANES), lambda i: (i, 0)),
            pl.BlockSpec((tm, LANES), lambda i: (i, 0)),
        ],
        out_specs=[
            pl.BlockSpec((tm, Q_W), lambda i: (i, 0)),
            pl.BlockSpec((tm, KV_W), lambda i: (i, 0)),
        ],
        out_shape=[jax.ShapeDtypeStruct((t, Q_W), F32), jax.ShapeDtypeStruct((t, KV_W), F32)],
        compiler_params=_cparams(1),
        name="rope_qk",
    )(proj, proj, cos_t, sin_t)


def rope_tables(tp, seq, ts, nb, past_len):
    half = HEAD_DIM // 2
    inv = ROPE_THETA ** (-jnp.arange(half, dtype=F32) / half)
    pos_p = jnp.arange(tp, dtype=jnp.int32) % seq
    pos_s = past_len + jnp.arange(ts, dtype=jnp.int32) // nb
    pos = jnp.concatenate([pos_p, pos_s]).astype(F32)
    ang = pos[:, None] * inv[None, :]
    cos = jnp.tile(jnp.cos(ang), (1, LANES // half))
    sin = jnp.sin(ang)
    sin_signed = jnp.tile(jnp.concatenate([-sin, sin], axis=1), (1, LANES // HEAD_DIM))
    return cos, sin_signed


def _dup_head(slab, g):
    lane = lax.broadcasted_iota(jnp.int32, slab.shape, 1)
    rolled = pltpu.roll(slab, HEAD_DIM, axis=1)
    if g % 2 == 0:
        return jnp.where(lane < HEAD_DIM, slab, rolled)
    return jnp.where(lane < HEAD_DIM, rolled, slab)


def _swa_bias(qb, prev_ok):
    kw = WINDOW + qb
    ii = lax.broadcasted_iota(jnp.int32, (qb, kw), 0)
    jj = lax.broadcasted_iota(jnp.int32, (qb, kw), 1)
    ok_prev = (jj < WINDOW) & (jj > ii)
    if prev_ok is not True:
        ok_prev = ok_prev & prev_ok
    ok = ok_prev | ((jj >= WINDOW) & ((jj - WINDOW) <= ii))
    return jnp.where(ok, 0.0, NEG_BIG)


def _swa_group(q, kdup, vdup, bias, sink_ref, g, qb, store):
    rep = N_HEADS // N_KV_HEADS
    lane = lax.broadcasted_iota(jnp.int32, (qb, LANES), 1)
    scale = HEAD_DIM ** -0.5
    rows = []
    for r in range(rep):
        h = g * rep + r
        qc = q[:, (h // 2) * LANES:(h // 2 + 1) * LANES]
        keep = (lane < HEAD_DIM) if h % 2 == 0 else (lane >= HEAD_DIM)
        rows.append(jnp.where(keep, qc * scale, 0.0))
    s = _dot_nt(jnp.concatenate(rows, axis=0).astype(BF16), kdup)
    ps = []
    for r in range(rep):
        sr = s[r * qb:(r + 1) * qb] + bias
        sink = sink_ref[g * rep + r]
        m = jnp.maximum(jnp.max(sr, axis=-1, keepdims=True), sink)
        p = jnp.exp(sr - m)
        den = jnp.sum(p, axis=-1, keepdims=True) + jnp.exp(sink - m)
        ps.append(p * (1.0 / den))
    o = _dot(jnp.concatenate(ps, axis=0).astype(BF16), vdup)
    for c in range(rep // 2):
        oa = o[(2 * c) * qb:(2 * c + 1) * qb]
        ob = o[(2 * c + 1) * qb:(2 * c + 2) * qb]
        store((g * rep // 2 + c) * LANES, jnp.where(lane < HEAD_DIM, oa, ob))


def _swa_kernel(sink_ref, q_ref, kp_ref, kc_ref, vp_ref, vc_ref, o_ref, *, qb, prev_from_block0):
    n = pl.program_id(1)
    q = q_ref[...]
    kp, kc, vp, vc = kp_ref[...], kc_ref[...], vp_ref[...], vc_ref[...]
    bias = _swa_bias(qb, True if prev_from_block0 else (n > 0))

    def store(col, val):
        o_ref[:, col:col + LANES] = val.astype(o_ref.dtype)

    for g in range(N_KV_HEADS):
        sl = slice((g // 2) * LANES, (g // 2 + 1) * LANES)
        kdup = _dup_head(jnp.concatenate([kp[:, sl], kc[:, sl]], axis=0), g).astype(BF16)
        vdup = _dup_head(jnp.concatenate([vp[:, sl], vc[:, sl]], axis=0), g).astype(BF16)
        _swa_group(q, kdup, vdup, bias, sink_ref, g, qb, store)


def _swa_cache_kernel(sink_ref, q_ref, kc_ref, vc_ref, kp_ref, vp_ref, o_ref, *, qb, bseq):
    rep = N_HEADS // N_KV_HEADS
    bias = jnp.concatenate([_swa_bias(qb, True)] * rep, axis=0)
    q = q_ref[...].reshape(bseq, qb, Q_W)
    kc = kc_ref[...].reshape(bseq, qb, KV_W)
    vc = vc_ref[...].reshape(bseq, qb, KV_W)
    lane = lax.broadcasted_iota(jnp.int32, (bseq, qb, LANES), 2)
    scale = HEAD_DIM ** -0.5
    bdot_nt = lambda a, b: lax.dot_general(a, b, (((2,), (2,)), ((0,), (0,))), preferred_element_type=F32)
    bdot = lambda a, b: lax.dot_general(a, b, (((2,), (1,)), ((0,), (0,))), preferred_element_type=F32)
    for g in range(N_KV_HEADS):
        hs = slice(g * HEAD_DIM, (g + 1) * HEAD_DIM)
        kcat = jnp.concatenate([kp_ref[:, :, g, :], kc[:, :, hs]], axis=1)
        vcat = jnp.concatenate([vp_ref[:, :, g, :], vc[:, :, hs]], axis=1)
        kdup = jnp.concatenate([kcat, kcat], axis=2).astype(BF16)
        vdup = jnp.concatenate([vcat, vcat], axis=2).astype(BF16)
        rows, sinks = [], []
        for r in range(rep):
            h = g * rep + r
            qc = q[:, :, (h // 2) * LANES:(h // 2 + 1) * LANES]
            keep = (lane < HEAD_DIM) if h % 2 == 0 else (lane >= HEAD_DIM)
            rows.append(jnp.where(keep, qc * scale, 0.0))
            sinks.append(jnp.full((qb, 1), sink_ref[h], F32))
        sink = jnp.concatenate(sinks, axis=0)
        s = bdot_nt(jnp.concatenate(rows, axis=1).astype(BF16), kdup) + bias
        m = jnp.maximum(jnp.max(s, axis=-1, keepdims=True), sink)
        p = jnp.exp(s - m)
        den = jnp.sum(p, axis=-1, keepdims=True) + jnp.exp(sink - m)
        o = bdot((p * (1.0 / den)).astype(BF16), vdup)
        for c in range(rep // 2):
            oa = o[:, (2 * c) * qb:(2 * c + 1) * qb]
            ob = o[:, (2 * c + 1) * qb:(2 * c + 2) * qb]
            col = (g * rep // 2 + c) * LANES
            o_ref[:, col:col + LANES] = jnp.where(lane < HEAD_DIM, oa, ob).reshape(bseq * qb, LANES).astype(
                o_ref.dtype)


def swa_attention_cached(sinks, q, kc, vc, cache_k, cache_v, *, layer, n_seq, qb, bseq, name):
    rows = bseq * qb
    cache_spec = pl.BlockSpec((None, bseq, WINDOW, N_KV_HEADS, HEAD_DIM), lambda i: (layer, i, 0, 0, 0))
    return pl.pallas_call(
        functools.partial(_swa_cache_kernel, qb=qb, bseq=bseq),
        grid=(n_seq // bseq,),
        in_specs=[pl.BlockSpec(memory_space=pltpu.SMEM),
                  pl.BlockSpec((rows, Q_W), lambda i: (i, 0)),
                  pl.BlockSpec((rows, KV_W), lambda i: (i, 0)),
                  pl.BlockSpec((rows, KV_W), lambda i: (i, 0)),
                  cache_spec, cache_spec],
        out_specs=pl.BlockSpec((rows, Q_W), lambda i: (i, 0)),
        out_shape=jax.ShapeDtypeStruct((n_seq * qb, Q_W), F32),
        compiler_params=_cparams(1),
        name=name,
    )(sinks, q, kc, vc, cache_k, cache_v)


def swa_attention(sinks, q, kp, kc, vp, vc, *, n_seq, n_blk, qb, prev_from_block0,
                  kp_map, kc_map, vp_map, vc_map, out_dtype, name):
    return pl.pallas_call(
        functools.partial(_swa_kernel, qb=qb, prev_from_block0=prev_from_block0),
        grid=(n_seq, n_blk),
        in_specs=[
            pl.BlockSpec(memory_space=pltpu.SMEM),
            pl.BlockSpec((qb, Q_W), lambda i, n: (i * n_blk + n, 0)),
            pl.BlockSpec((WINDOW, KV_W), kp_map),
            pl.BlockSpec((qb, KV_W), kc_map),
            pl.BlockSpec((WINDOW, KV_W), vp_map),
            pl.BlockSpec((qb, KV_W), vc_map),
        ],
        out_specs=pl.BlockSpec((qb, Q_W), lambda i, n: (i * n_blk + n, 0)),
        out_shape=jax.ShapeDtypeStruct((n_seq * n_blk * qb, Q_W), out_dtype),
        compiler_params=_cparams(2),
        name=name,
    )(sinks, q, kp, kc, vp, vc)


def _gmlp_p_kernel(gu_ref, gv_ref, ws_ref, bst_ref, lg_ref, lb_ref, gm_ref, vg_ref, *, n_chunks):
    n = pl.program_id(1)
    vg = _ln_rows(_gelu(gv_ref[...]), lg_ref[...], lb_ref[...])
    gu = gu_ref[...]
    ri = lax.broadcasted_iota(jnp.int32, (CHUNK, CHUNK), 0)
    ci = lax.broadcasted_iota(jnp.int32, (CHUNK, CHUNK), 1)
    tril = ri >= ci
    bst = bst_ref[...]
    for g in range(GM_GROUPS):
        sl = slice(g * GM_GROUP_DIM, (g + 1) * GM_GROUP_DIM)
        w = jnp.where(tril, ws_ref[g], 0.0).astype(BF16)
        s = _dot(w, vg[:, sl].astype(BF16)) + bst[:, g:g + 1]
        gm_ref[:, sl] = (_gelu(gu[:, sl]) * s).astype(gm_ref.dtype)

    @pl.when(n == n_chunks - 1)
    def _():
        vg_ref[...] = vg


def gmlp_prompt(proj, n_seq, seq, ws, bs, ln_g, ln_b):
    nc = seq // CHUNK
    return pl.pallas_call(
        functools.partial(_gmlp_p_kernel, n_chunks=nc),
        grid=(n_seq, nc),
        in_specs=[
            pl.BlockSpec((CHUNK, GM_W), lambda i, n: (i * nc + n, COL_GU // GM_W)),
            pl.BlockSpec((CHUNK, GM_W), lambda i, n: (i * nc + n, COL_GV // GM_W)),
            pl.BlockSpec((GM_GROUPS, CHUNK, CHUNK), lambda i, n: (0, 0, 0)),
            pl.BlockSpec((CHUNK, GM_GROUPS), lambda i, n: (0, 0)),
            pl.BlockSpec((1, GM_W), lambda i, n: (0, 0)),
            pl.BlockSpec((1, GM_W), lambda i, n: (0, 0)),
        ],
        out_specs=[
            pl.BlockSpec((CHUNK, GM_W), lambda i, n: (i * nc + n, 0)),
            pl.BlockSpec((CHUNK, GM_W), lambda i, n: (i, 0)),
        ],
        out_shape=[jax.ShapeDtypeStruct((n_seq * seq, GM_W), BF16),
                   jax.ShapeDtypeStruct((n_seq * CHUNK, GM_W), F32)],
        compiler_params=_cparams(2),
        name="gmlp_prompt",
    )(proj, proj, ws, bs.T, ln_g.reshape(1, GM_W), ln_b.reshape(1, GM_W))


def _gmlp_s_kernel(*refs, lt):
    gu_refs = refs[:lt]
    gv_refs = refs[lt:2 * lt]
    wrow_ref, brow_ref, lg_ref, lb_ref, gm_ref, vg_ref = refs[2 * lt:]
    nb = gu_refs[0].shape[0]
    vgs = [_ln_rows(_gelu(gv_refs[t][...]), lg_ref[...], lb_ref[...]) for t in range(lt)]
    for i in range(lt):
        s = brow_ref[i:i + 1, :]
        for j in range(i + 1):
            s = s + wrow_ref[i * lt + j:i * lt + j + 1, :] * vgs[j]
        gm_ref[i * nb:(i + 1) * nb, :] = (_gelu(gu_refs[i][...]) * s).astype(gm_ref.dtype)
        vg_ref[i * nb:(i + 1) * nb, :] = vgs[i]


def gmlp_sample(proj, tp, nb, lt, ws, bs, ln_g, ln_b):
    w_small = ws[:, :lt, :lt]
    wrow = jnp.repeat(jnp.transpose(w_small, (1, 2, 0)).reshape(lt * lt, GM_GROUPS), GM_GROUP_DIM, axis=1)
    brow = jnp.repeat(bs[:, :lt].T, GM_GROUP_DIM, axis=1)
    row0 = tp // nb

    def spec(t, col):
        return pl.BlockSpec((nb, GM_W), lambda i: (row0 + t, col // GM_W))

    in_specs = [spec(t, COL_GU) for t in range(lt)] + [spec(t, COL_GV) for t in range(lt)] + [
        pl.BlockSpec((lt * lt, GM_W), lambda i: (0, 0)),
        pl.BlockSpec((lt, GM_W), lambda i: (0, 0)),
        pl.BlockSpec((1, GM_W), lambda i: (0, 0)),
        pl.BlockSpec((1, GM_W), lambda i: (0, 0)),
    ]
    return pl.pallas_call(
        functools.partial(_gmlp_s_kernel, lt=lt),
        grid=(1,),
        in_specs=in_specs,
        out_specs=[pl.BlockSpec((lt * nb, GM_W), lambda i: (0, 0)),
                   pl.BlockSpec((lt * nb, GM_W), lambda i: (0, 0))],
        out_shape=[jax.ShapeDtypeStruct((lt * nb, GM_W), BF16),
                   jax.ShapeDtypeStruct((lt * nb, GM_W), F32)],
        compiler_params=_cparams(1),
        name="gmlp_sample",
    )(*([proj] * (2 * lt)), wrow, brow, ln_g.reshape(1, GM_W), ln_b.reshape(1, GM_W))


def _conv_silu(cur, prev8, w, bias):
    q = cur.shape[0]
    up = jnp.concatenate([prev8, cur], axis=0)
    acc = bias + up[SUBLANES:SUBLANES + q] * w[CONV_K - 1:CONV_K]
    for j in range(CONV_K - 1):
        off = SUBLANES - (CONV_K - 1) + j
        acc = acc + up[off:off + q] * w[j:j + 1]
    return _silu(acc)


def _ssd_p_kernel(z_ref, xs_ref, bc_ref, xsp_ref, bcp_ref, dt_ref, dtt_ref,
                  cwx_ref, cwbc_ref, cbx_ref, cbbc_ref, dtbe_ref, dtbc_ref, aloge_ref, alogc_ref,
                  dske_ref, ng_ref, e_ref, y_ref, st_ref, s_scr, *, n_chunks):
    c = pl.program_id(1)
    q = CHUNK
    rep = SSM_HEADS // SSM_GROUPS
    gw = rep * SSM_HEAD_DIM

    @pl.when(c == 0)
    def _():
        s_scr[...] = jnp.zeros_like(s_scr)

    has_prev = (c > 0).astype(F32)
    xs = _conv_silu(xs_ref[...], xsp_ref[...] * has_prev, cwx_ref[...], cbx_ref[...])
    bcm = _conv_silu(bc_ref[...], bcp_ref[...] * has_prev, cwbc_ref[...], cbbc_ref[...])

    ri = lax.broadcasted_iota(jnp.int32, (q, q), 0)
    ci = lax.broadcasted_iota(jnp.int32, (q, q), 1)
    tril = ri >= ci
    ones_tril = jnp.where(tril, 1.0, 0.0).astype(BF16)
    ones_triu = jnp.where(ri <= ci, 1.0, 0.0).astype(BF16)

    dt_e = _softplus(_exact_dot_left(_split3(dt_ref[...]), e_ref[...]) + dtbe_ref[...])
    a_e = dt_e * (-jnp.exp(aloge_ref[...]))
    a_hi, a_mid, a_lo = _split3(a_e)
    cum_e = _dot(ones_tril, a_hi) + _dot(ones_tril, a_mid) + _dot(ones_tril, a_lo)
    dt_t = _softplus(dtt_ref[...] + dtbc_ref[...])
    a_t = dt_t * (-jnp.exp(alogc_ref[...]))
    cum_t = _exact_dot_left(_split3(a_t), ones_triu)

    xdt = xs * dt_e
    cum_last = cum_e[q - 1:q, :]
    lane = lax.broadcasted_iota(jnp.int32, (q, LANES), 1)
    z = z_ref[...]
    ys = []
    for g in range(SSM_GROUPS):
        gs = slice(g * gw, (g + 1) * gw)
        bg = bcm[:, g * SSM_STATE:(g + 1) * SSM_STATE]
        cg = bcm[:, SSM_BC + g * SSM_STATE:SSM_BC + (g + 1) * SSM_STATE]
        bg16, cg16 = bg.astype(BF16), cg.astype(BF16)
        cb = _dot_nt(cg16, bg16)
        ydiag = []
        for pr in range(rep // 2):
            ms = []
            for hh in (2 * pr, 2 * pr + 1):
                h = g * rep + hh
                col = cum_e[:, h * SSM_HEAD_DIM:h * SSM_HEAD_DIM + 1]
                row = cum_t[h:h + 1, :]
                seg = jnp.where(tril, col - row, NEG_BIG)
                ms.append(cb * jnp.exp(seg))
            lhs = jnp.concatenate(ms, axis=1).astype(BF16)
            xslab = xdt[:, (g * rep + 2 * pr) * SSM_HEAD_DIM:(g * rep + 2 * pr + 2) * SSM_HEAD_DIM]
            xbd = jnp.concatenate([jnp.where(lane < SSM_HEAD_DIM, xslab, 0.0),
                                   jnp.where(lane >= SSM_HEAD_DIM, xslab, 0.0)], axis=0).astype(BF16)
            ydiag.append(_dot(lhs, xbd))
        ydiag = jnp.concatenate(ydiag, axis=1)
        s_old = s_scr[g]
        yoff = _dot(cg16, s_old.astype(BF16)) * jnp.exp(cum_e[:, gs])
        ys.append(ydiag + yoff)
        xw = xdt[:, gs] * jnp.exp(cum_last[:, gs] - cum_e[:, gs])
        s_scr[g] = s_old * jnp.exp(cum_last[:, gs]) + _dot(bg.T.astype(BF16), xw.astype(BF16))
    y = jnp.concatenate(ys, axis=1) + dske_ref[...] * xs
    gated = y * _silu(z)
    out = gated * lax.rsqrt(jnp.mean(gated * gated, axis=-1, keepdims=True) + LN_EPS) * ng_ref[...]
    y_ref[...] = out.astype(y_ref.dtype)

    @pl.when(c == n_chunks - 1)
    def _():
        for g in range(SSM_GROUPS):
            st_ref[0, g * gw:(g + 1) * gw, :] = s_scr[g].T


def _expand_heads(v):
    return jnp.repeat(v.astype(F32), SSM_HEAD_DIM).reshape(1, SSM_INNER)


def _head_expand_matrix():
    e = np.zeros((LANES, SSM_INNER), np.float32)
    for h in range(SSM_HEADS):
        e[h, h * SSM_HEAD_DIM:(h + 1) * SSM_HEAD_DIM] = 1.0
    return jnp.asarray(e, BF16)


def ssd_prompt(proj, dtt, n_seq, seq, conv_w, conv_b, dt_bias, a_log, d_skip, norm_g):
    b = n_seq
    nc = seq // CHUNK
    blk8 = CHUNK // SUBLANES
    const2 = lambda i, c: (0, 0)
    col_bc = COL_BC // (2 * SSM_BC)
    col_xs = COL_XS // SSM_INNER
    col_z = COL_Z // SSM_INNER
    prev_map_x = lambda i, c: (jnp.maximum((i * nc + c) * blk8 - 1, 0), col_xs)
    prev_map_bc = lambda i, c: (jnp.maximum((i * nc + c) * blk8 - 1, 0), col_bc)
    args = (
        proj, proj, proj, proj, proj, proj, dtt,
        conv_w[:, :SSM_INNER], conv_w[:, SSM_INNER:], conv_b[:SSM_INNER].reshape(1, -1),
        conv_b[SSM_INNER:].reshape(1, -1),
        _expand_heads(dt_bias), jnp.broadcast_to(dt_bias.astype(F32)[:, None], (SSM_HEADS, CHUNK)),
        _expand_heads(a_log), jnp.broadcast_to(a_log.astype(F32)[:, None], (SSM_HEADS, CHUNK)),
        _expand_heads(d_skip), norm_g.reshape(1, SSM_INNER), _head_expand_matrix(),
    )
    in_specs = [
        pl.BlockSpec((CHUNK, SSM_INNER), lambda i, c: (i * nc + c, col_z)),
        pl.BlockSpec((CHUNK, SSM_INNER), lambda i, c: (i * nc + c, col_xs)),
        pl.BlockSpec((CHUNK, 2 * SSM_BC), lambda i, c: (i * nc + c, col_bc)),
        pl.BlockSpec((SUBLANES, SSM_INNER), prev_map_x),
        pl.BlockSpec((SUBLANES, 2 * SSM_BC), prev_map_bc),
        pl.BlockSpec((CHUNK, LANES), lambda i, c: (i * nc + c, COL_DT // LANES)),
        pl.BlockSpec((SSM_HEADS, CHUNK), lambda i, c: (0, i * nc + c)),
        pl.BlockSpec((CONV_K, SSM_INNER), const2),
        pl.BlockSpec((CONV_K, 2 * SSM_BC), const2),
        pl.BlockSpec((1, SSM_INNER), const2),
        pl.BlockSpec((1, 2 * SSM_BC), const2),
        pl.BlockSpec((1, SSM_INNER), const2),
        pl.BlockSpec((SSM_HEADS, CHUNK), const2),
        pl.BlockSpec((1, SSM_INNER), const2),
        pl.BlockSpec((SSM_HEADS, CHUNK), const2),
        pl.BlockSpec((1, SSM_INNER), const2),
        pl.BlockSpec((1, SSM_INNER), const2),
        pl.BlockSpec((LANES, SSM_INNER), const2),
    ]
    return pl.pallas_call(
        functools.partial(_ssd_p_kernel, n_chunks=nc),
        grid=(b, nc),
        in_specs=in_specs,
        out_specs=[
            pl.BlockSpec((CHUNK, SSM_INNER), lambda i, c: (i * nc + c, 0)),
            pl.BlockSpec((1, SSM_INNER, SSM_STATE), lambda i, c: (i, 0, 0)),
        ],
        out_shape=[jax.ShapeDtypeStruct((b * seq, SSM_INNER), BF16),
                   jax.ShapeDtypeStruct((b, SSM_INNER, SSM_STATE), F32)],
        scratch_shapes=[pltpu.VMEM((SSM_GROUPS, SSM_STATE, SSM_INNER // SSM_GROUPS), F32)],
        compiler_params=_cparams(2),
        name="ssd_prompt",
    )(*args)


def _group_expand_matrix():
    gw = SSM_INNER // SSM_GROUPS
    m = np.zeros((SSM_BC, SSM_INNER), np.float32)
    for g in range(SSM_GROUPS):
        m[g * SSM_STATE:(g + 1) * SSM_STATE, g * gw:(g + 1) * gw] = 1.0
    return jnp.asarray(m, BF16)


def _ssd_s_pre_kernel(*refs, lt):
    xs_refs = refs[:lt]
    bc_refs = refs[lt:2 * lt]
    dt_refs = refs[2 * lt:3 * lt]
    (cx_ref, cbc_ref, cwx_ref, cwbc_ref, cbx_ref, cbbc_ref, dtbe_ref, aloge_ref, dske_ref, e_ref,
     gmat_ref, c_ref, b_ref, xw_ref, dec_ref, yd_ref, ec_ref) = refs[3 * lt:]
    nprev = CONV_K - 1
    ux = [cx_ref[j] for j in range(nprev)] + [r[...] for r in xs_refs]
    ub = [cbc_ref[j] for j in range(nprev)] + [r[...] for r in bc_refs]
    cwx, cwbc = cwx_ref[...], cwbc_ref[...]
    neg_a = -jnp.exp(aloge_ref[...])
    xs, bm, cm, xdt, cum = [], [], [], [], []
    run = None
    for t in range(lt):
        ax = cbx_ref[...]
        ab = cbbc_ref[...]
        for j in range(CONV_K):
            ax = ax + ux[t + j] * cwx[j:j + 1]
            ab = ab + ub[t + j] * cwbc[j:j + 1]
        x_t = _silu(ax)
        bc_t = _silu(ab)
        dt_e = _softplus(_exact_dot_left(_split3(dt_refs[t][...]), e_ref[...]) + dtbe_ref[...])
        a_t = dt_e * neg_a
        run = a_t if run is None else run + a_t
        xs.append(x_t)
        bm.append(bc_t[:, :SSM_BC])
        cm.append(bc_t[:, SSM_BC:])
        xdt.append(x_t * dt_e)
        cum.append(run)
    for i in range(lt):
        yd = dske_ref[...] * xs[i]
        for j in range(i + 1):
            hi, mid, _ = _split3(cm[i] * bm[j])
            cbe = _dot(hi, gmat_ref[...]) + _dot(mid, gmat_ref[...])
            yd = yd + cbe * jnp.exp(cum[i] - cum[j]) * xdt[j]
        yd_ref[i] = yd
        ec_ref[i] = jnp.exp(cum[i])
        c_ref[i] = cm[i]
        b_ref[i] = bm[i]
        xw_ref[i] = xdt[i] * jnp.exp(cum[lt - 1] - cum[i])
    dec_ref[...] = jnp.exp(cum[lt - 1])


def _rows_block(rows, total):
    c = rows[0].shape[1]
    rid = lax.broadcasted_iota(jnp.int32, (SUBLANES, c), 0)
    acc = jnp.zeros((SUBLANES, c), F32)
    for j, r in enumerate(rows):
        acc = jnp.where(rid == j, jnp.broadcast_to(r, (SUBLANES, c)), acc)
    if total == SUBLANES:
        return acc
    return jnp.concatenate([acc, jnp.zeros((total - SUBLANES, c), F32)], axis=0)


def _ssd_s_state_kernel_inplace(c_ref, b_ref, xw_ref, dec_ref, h0_ref, prev_ref, hn_ref, yr_ref, *, lt, bseq):
    del prev_ref
    _ssd_s_state_kernel(c_ref, b_ref, xw_ref, dec_ref, h0_ref, hn_ref, yr_ref, lt=lt, bseq=bseq)


def _ssd_s_state_kernel(c_ref, b_ref, xw_ref, dec_ref, h0_ref, hn_ref, yr_ref, *, lt, bseq, slot=0,
                        fill_slots=None):
    for bb in range(bseq):
        _ssd_s_state_one(c_ref, b_ref, xw_ref, dec_ref, h0_ref, hn_ref, yr_ref, pl.program_id(0) * bseq + bb,
                         bb, lt, slot, fill_slots)
    if fill_slots is not None:
        for other in fill_slots:
            hn_ref[other] = jnp.zeros(hn_ref.shape[1:], hn_ref.dtype)


def _ssd_s_state_one(c_ref, b_ref, xw_ref, dec_ref, h0_ref, hn_ref, yr_ref, b, bb, lt, slot, fill_slots):
    gw = SSM_INNER // SSM_GROUPS
    c8 = _rows_block([c_ref[i, pl.ds(b, 1), :] for i in range(lt)], SUBLANES).astype(BF16)
    b128 = _rows_block([b_ref[i, pl.ds(b, 1), :] for i in range(lt)], LANES).astype(BF16)
    xaug = _rows_block([xw_ref[i, pl.ds(b, 1), :] for i in range(lt)] + [dec_ref[pl.ds(b, 1), :]], LANES)
    for g in range(SSM_GROUPS):
        hg = h0_ref[bb, g * gw:(g + 1) * gw, :]
        yraw = _dot_nt(c8[:, g * SSM_STATE:(g + 1) * SSM_STATE], hg.astype(BF16))
        for i in range(lt):
            yr_ref[i, pl.ds(b, 1), g * gw:(g + 1) * gw] = yraw[i:i + 1, :]
        tr = xaug[:, g * gw:(g + 1) * gw].T
        s = _dot(tr.astype(BF16), b128[:, g * SSM_STATE:(g + 1) * SSM_STATE])
        new = hg * tr[:, lt:lt + 1] + s
        if fill_slots is None:
            hn_ref[bb, g * gw:(g + 1) * gw, :] = new
        else:
            hn_ref[slot, bb, g * gw:(g + 1) * gw, :] = new


def _ssd_s_post_kernel(*refs, lt):
    z_refs = refs[:lt]
    yd_ref, ec_ref, yr_ref, ng_ref, o_ref = refs[lt:]
    nb = z_refs[0].shape[0]
    for i in range(lt):
        y = yd_ref[i] + ec_ref[i] * yr_ref[i]
        gated = y * _silu(z_refs[i][...])
        out = gated * lax.rsqrt(jnp.mean(gated * gated, axis=-1, keepdims=True) + LN_EPS) * ng_ref[...]
        o_ref[i * nb:(i + 1) * nb, :] = out.astype(o_ref.dtype)


def ssd_sample(proj, tp, nb, lt, conv_state, state_all, new_states, layer, conv_w, conv_b, dt_bias,
               a_log, d_skip, norm_g):
    row0 = tp // nb
    cs = jnp.transpose(conv_state, (1, 0, 2))
    one = lambda i: (0, 0)
    one3 = lambda i: (0, 0, 0)

    def rows(t, width, col):
        return pl.BlockSpec((nb, width), lambda i: (row0 + t, col // width))

    in_specs = ([rows(t, SSM_INNER, COL_XS) for t in range(lt)]
                + [rows(t, 2 * SSM_BC, COL_BC) for t in range(lt)]
                + [pl.BlockSpec((nb, LANES), lambda i, t=t: (row0 + t, COL_DT // LANES)) for t in range(lt)]
                + [pl.BlockSpec((CONV_K - 1, nb, SSM_INNER), one3),
                   pl.BlockSpec((CONV_K - 1, nb, 2 * SSM_BC), one3),
                   pl.BlockSpec((CONV_K, SSM_INNER), one),
                   pl.BlockSpec((CONV_K, 2 * SSM_BC), one),
                   pl.BlockSpec((1, SSM_INNER), one),
                   pl.BlockSpec((1, 2 * SSM_BC), one),
                   pl.BlockSpec((1, SSM_INNER), one),
                   pl.BlockSpec((1, SSM_INNER), one),
                   pl.BlockSpec((1, SSM_INNER), one),
                   pl.BlockSpec((LANES, SSM_INNER), one),
                   pl.BlockSpec((SSM_BC, SSM_INNER), one)])
    f3 = lambda w: jax.ShapeDtypeStruct((lt, nb, w), F32)
    c_a, b_a, xw_a, dec_a, yd_a, ec_a = pl.pallas_call(
        functools.partial(_ssd_s_pre_kernel, lt=lt),
        grid=(1,),
        in_specs=in_specs,
        out_specs=[pl.BlockSpec((lt, nb, SSM_BC), one3), pl.BlockSpec((lt, nb, SSM_BC), one3),
                   pl.BlockSpec((lt, nb, SSM_INNER), one3), pl.BlockSpec((nb, SSM_INNER), one),
                   pl.BlockSpec((lt, nb, SSM_INNER), one3), pl.BlockSpec((lt, nb, SSM_INNER), one3)],
        out_shape=[f3(SSM_BC), f3(SSM_BC), f3(SSM_INNER), jax.ShapeDtypeStruct((nb, SSM_INNER), F32),
                   f3(SSM_INNER), f3(SSM_INNER)],
        compiler_params=_cparams(1),
        name="ssd_sample_pre",
    )(*([proj] * (3 * lt)), cs[:, :, :SSM_INNER], cs[:, :, SSM_INNER:],
      conv_w[:, :SSM_INNER], conv_w[:, SSM_INNER:], conv_b[:SSM_INNER].reshape(1, -1),
      conv_b[SSM_INNER:].reshape(1, -1), _expand_heads(dt_bias), _expand_heads(a_log),
      _expand_heads(d_skip), _head_expand_matrix(), _group_expand_matrix())

    depth = state_all.shape[0]
    bseq = _pick(nb, (2, 1))
    h0r = state_all.reshape(depth, nb, SSM_INNER, SSM_STATE)
    state_specs = [pl.BlockSpec((lt, nb, SSM_BC), one3), pl.BlockSpec((lt, nb, SSM_BC), one3),
                   pl.BlockSpec((lt, nb, SSM_INNER), one3), pl.BlockSpec((nb, SSM_INNER), one),
                   pl.BlockSpec((None, bseq, SSM_INNER, SSM_STATE), lambda i: (layer, i, 0, 0))]
    hn_shape = jax.ShapeDtypeStruct((depth, nb, SSM_INNER, SSM_STATE), F32)
    if new_states is None:
        fill = tuple(s for s in range(depth) if s != layer)
        hn, yr = pl.pallas_call(
            functools.partial(_ssd_s_state_kernel, lt=lt, bseq=bseq, slot=layer, fill_slots=fill),
            grid=(nb // bseq,),
            in_specs=state_specs,
            out_specs=[pl.BlockSpec((depth, bseq, SSM_INNER, SSM_STATE), lambda i: (0, i, 0, 0)),
                       pl.BlockSpec((lt, nb, SSM_INNER), one3)],
            out_shape=[hn_shape, f3(SSM_INNER)],
            compiler_params=_cparams(1),
            name="ssd_sample_state",
        )(c_a, b_a, xw_a, dec_a, h0r)
    else:
        hn, yr = pl.pallas_call(
            functools.partial(_ssd_s_state_kernel_inplace, lt=lt, bseq=bseq),
            grid=(nb // bseq,),
            in_specs=state_specs + [pl.BlockSpec(memory_space=pl.ANY)],
            out_specs=[pl.BlockSpec((None, bseq, SSM_INNER, SSM_STATE), lambda i: (layer, i, 0, 0)),
                       pl.BlockSpec((lt, nb, SSM_INNER), one3)],
            out_shape=[hn_shape, f3(SSM_INNER)],
            input_output_aliases={5: 0},
            compiler_params=_cparams(1),
            name="ssd_sample_state",
        )(c_a, b_a, xw_a, dec_a, h0r, new_states)

    ssm = pl.pallas_call(
        functools.partial(_ssd_s_post_kernel, lt=lt),
        grid=(1,),
        in_specs=([rows(t, SSM_INNER, COL_Z) for t in range(lt)]
                  + [pl.BlockSpec((lt, nb, SSM_INNER), one3)] * 3 + [pl.BlockSpec((1, SSM_INNER), one)]),
        out_specs=pl.BlockSpec((lt * nb, SSM_INNER), one),
        out_shape=jax.ShapeDtypeStruct((lt * nb, SSM_INNER), BF16),
        compiler_params=_cparams(1),
        name="ssd_sample_post",
    )(*([proj] * lt), yd_a, ec_a, yr, norm_g.reshape(1, SSM_INNER))
    return ssm, hn


def _softmax_rows(s):
    p = jnp.exp(s - jnp.max(s, axis=-1, keepdims=True))
    return p * (1.0 / jnp.sum(p, axis=-1, keepdims=True))


def _xattn_kernel(q_ref, k_ref, v_ref, o_ref, *, nh, bseq, tq):
    scale = MEM_HEAD_DIM ** -0.5
    q = q_ref[...].reshape(bseq, tq, q_ref.shape[1])
    for h in range(nh):
        sl = slice(h * MEM_HEAD_DIM, (h + 1) * MEM_HEAD_DIM)
        s = lax.dot_general(q[:, :, sl].astype(BF16), k_ref[:, :, h, :].astype(BF16),
                            (((2,), (2,)), ((0,), (0,))), preferred_element_type=F32) * scale
        o = lax.dot_general(_softmax_rows(s).astype(BF16), v_ref[:, :, h, :].astype(BF16),
                            (((2,), (1,)), ((0,), (0,))), preferred_element_type=F32)
        o_ref[:, sl] = o.reshape(bseq * tq, MEM_HEAD_DIM).astype(o_ref.dtype)


def cross_attention(q, k, v, *, layer, n_seq, seq, tq, bseq, name):
    w = q.shape[1]
    _, _, m, nh, dh = k.shape
    nq = seq // tq
    assert bseq == 1 or nq == 1
    kv_spec = pl.BlockSpec((None, bseq, m, nh, dh), lambda i, n: (layer, i, 0, 0, 0))
    return pl.pallas_call(
        functools.partial(_xattn_kernel, nh=nh, bseq=bseq, tq=tq),
        grid=(n_seq // bseq, nq),
        in_specs=[pl.BlockSpec((bseq * tq, w), lambda i, n: (i * nq + n, 0)), kv_spec, kv_spec],
        out_specs=pl.BlockSpec((bseq * tq, w), lambda i, n: (i * nq + n, 0)),
        out_shape=jax.ShapeDtypeStruct((n_seq * seq, w), F32),
        compiler_params=_cparams(2),
        name=name,
    )(q, k, v)


def _router_kernel(h_ref, w_ref, b_ref, o_ref):
    logits = _dot(h_ref[...].astype(BF16), w_ref[...].astype(BF16)) + b_ref[...]
    lane = lax.broadcasted_iota(jnp.int32, logits.shape, 1)
    lane_f = lane.astype(F32)
    big = float(LANES)
    is_g = lane < N_EGROUPS
    lg = jnp.where(is_g, logits, NEG_BIG)
    mg = jnp.max(lg, axis=-1, keepdims=True)
    zg = jnp.sum(jnp.where(is_g, jnp.exp(lg - mg), 0.0), axis=-1, keepdims=True)
    gi = jnp.min(jnp.where(is_g & (lg == mg), lane_f, big), axis=-1, keepdims=True)
    gw = 1.0 / zg
    lo = N_EGROUPS + gi * EXPERTS_PER_GROUP
    is_e = (lane_f >= lo) & (lane_f < lo + EXPERTS_PER_GROUP)
    le = jnp.where(is_e, logits, NEG_BIG)
    me = jnp.max(le, axis=-1, keepdims=True)
    ee = jnp.where(is_e, jnp.exp(le - me), 0.0)
    pe = ee / jnp.sum(ee, axis=-1, keepdims=True)
    pe = jnp.where(is_e, pe, -1.0)
    p1 = jnp.max(pe, axis=-1, keepdims=True)
    i1 = jnp.min(jnp.where(pe == p1, lane_f, big), axis=-1, keepdims=True)
    pe2 = jnp.where(lane_f == i1, -1.0, pe)
    p2 = jnp.max(pe2, axis=-1, keepdims=True)
    i2 = jnp.min(jnp.where(pe2 == p2, lane_f, big), axis=-1, keepdims=True)
    tot = p1 + p2
    out = jnp.where(lane == 0, i1 - N_EGROUPS,
                    jnp.where(lane == 1, i2 - N_EGROUPS,
                              jnp.where(lane == 2, gw * (p1 / tot),
                                        jnp.where(lane == 3, gw * (p2 / tot), 0.0))))
    o_ref[...] = out


def moe_router(h, w_rg, b_rg, w_re, b_re):
    t, d = h.shape
    tm = _pick(t, (256, 128, 64, 32, 16, 8))
    npad = LANES - N_EGROUPS - N_EXPERTS
    w = jnp.concatenate([w_rg, w_re, jnp.zeros((d, npad), F32)], axis=1)
    b = jnp.concatenate([b_rg, b_re, jnp.zeros((npad,), F32)]).reshape(1, LANES)
    return pl.pallas_call(
        _router_kernel,
        grid=(t // tm,),
        in_specs=[pl.BlockSpec((tm, d), lambda i: (i, 0)),
                  pl.BlockSpec((d, LANES), lambda i: (0, 0)),
                  pl.BlockSpec((1, LANES), lambda i: (0, 0))],
        out_specs=pl.BlockSpec((tm, LANES), lambda i: (i, 0)),
        out_shape=jax.ShapeDtypeStruct((t, LANES), F32),
        compiler_params=_cparams(1),
        name="moe_router",
    )(h, w, b)


def _row_copy(src_hbm, dst, src_row, dst_row, sem):
    return pltpu.make_async_copy(src_hbm.at[pl.ds(src_row, 1)], dst.at[pl.ds(dst_row, 1)], sem)


def _moe_gather_kernel(nrow_ref, tok_ref, h_ref, o_ref, buf, sem, *, tm):
    i = pl.program_id(0)
    n_groups = (nrow_ref[i] + DMA_LOOP_UNROLL - 1) // DMA_LOOP_UNROLL

    @pl.when(i == 0)
    def _():
        buf[...] = jnp.zeros_like(buf)

    def start(gidx, carry):
        for u in range(DMA_LOOP_UNROLL):
            r = gidx * DMA_LOOP_UNROLL + u
            _row_copy(h_ref, buf, tok_ref[0, 0, r], r, sem).start(priority=u % 2)
        return carry

    def wait(gidx, carry):
        for u in range(DMA_LOOP_UNROLL):
            r = gidx * DMA_LOOP_UNROLL + u
            _row_copy(h_ref, buf, tok_ref[0, 0, r], r, sem).wait()
        return carry

    lax.fori_loop(0, n_groups, start, 0)
    lax.fori_loop(0, n_groups, wait, 0)
    o_ref[...] = buf[...].astype(o_ref.dtype)


def moe_gather(h, row_token, tile_rows, tm):
    r_total = row_token.shape[0]
    d = h.shape[1]
    nblk = r_total // tm
    assert tm % DMA_LOOP_UNROLL == 0
    return pl.pallas_call(
        functools.partial(_moe_gather_kernel, tm=tm),
        grid=(nblk,),
        in_specs=[pl.BlockSpec(memory_space=pltpu.SMEM),
                  pl.BlockSpec((1, 1, tm), lambda i: (i, 0, 0), memory_space=pltpu.SMEM),
                  pl.BlockSpec(memory_space=pl.ANY)],
        out_specs=pl.BlockSpec((tm, d), lambda i: (i, 0)),
        out_shape=jax.ShapeDtypeStruct((r_total, d), BF16),
        scratch_shapes=[pltpu.VMEM((tm, d), h.dtype), pltpu.SemaphoreType.DMA(())],
        compiler_params=_cparams(1),
        name="moe_gather",
    )(tile_rows, row_token.reshape(nblk, 1, tm), h)


def _expert_weight_copies(w_refs, bufs, sems, layer, expert, slot):
    return [pltpu.make_async_copy(w.at[layer, expert], buf.at[slot], sems.at[k, slot])
            for k, (w, buf) in enumerate(zip(w_refs, bufs))]


def _expert_weights_step(s, tv_ref, te_ref, tn_ref, ts_ref, w_refs, bufs, w16s, sems, layer):
    @pl.when(s == 0)
    def _():
        for c in _expert_weight_copies(w_refs, bufs, sems, layer, te_ref[0], 0):
            c.start()

    @pl.when(tv_ref[s] == 2)
    def _():
        slot = ts_ref[s]
        for c in _expert_weight_copies(w_refs, bufs, sems, layer, te_ref[s], slot):
            c.wait()

        @pl.when(tn_ref[s] >= 0)
        def _():
            for c in _expert_weight_copies(w_refs, bufs, sems, layer, tn_ref[s], 1 - slot):
                c.start()

        for buf, w16 in zip(bufs, w16s):
            w16[...] = buf[slot].astype(BF16)


def _moe_up_kernel(tv_ref, tc_ref, te_ref, tn_ref, ts_ref, x_ref, wg_ref, wu_ref, o_ref,
                   gbuf, ubuf, wg16, wu16, sems, *, layer):
    s = pl.program_id(0)
    _expert_weights_step(s, tv_ref, te_ref, tn_ref, ts_ref, (wg_ref, wu_ref), (gbuf, ubuf), (wg16, wu16),
                         sems, layer)

    @pl.when(tv_ref[s] > 0)
    def _():
        x = x_ref[...]
        a = _dot(x, wg16[...])
        u = _dot(x, wu16[...])
        o_ref[...] = (_silu(a) * u).astype(o_ref.dtype)

    @pl.when(tv_ref[s] == 0)
    def _():
        o_ref[...] = jnp.zeros_like(o_ref)


def _moe_tables(plan):
    return plan["tile_v"], plan["tile_c"], plan["tile_e"], plan["tile_next"], plan["tile_slot"]


def moe_up(x_sorted, w_gate, w_up, layer, plan, tm):
    r_total, d = x_sorted.shape
    ff = w_gate.shape[-1]
    n_tiles = r_total // tm
    grid_spec = pltpu.PrefetchScalarGridSpec(
        num_scalar_prefetch=5,
        grid=(n_tiles,),
        in_specs=[
            pl.BlockSpec((tm, d), lambda s, tv, tc, te, tn, ts: (tc[s], 0)),
            pl.BlockSpec(memory_space=pl.ANY),
            pl.BlockSpec(memory_space=pl.ANY),
        ],
        out_specs=pl.BlockSpec((tm, ff), lambda s, tv, tc, te, tn, ts: (s, 0)),
        scratch_shapes=[pltpu.VMEM((2, d, ff), F32), pltpu.VMEM((2, d, ff), F32),
                        pltpu.VMEM((d, ff), BF16), pltpu.VMEM((d, ff), BF16),
                        pltpu.SemaphoreType.DMA((2, 2))],
    )
    return pl.pallas_call(
        functools.partial(_moe_up_kernel, layer=layer),
        grid_spec=grid_spec,
        out_shape=jax.ShapeDtypeStruct((r_total, ff), BF16),
        compiler_params=_cparams(1),
        name="moe_up",
    )(*_moe_tables(plan), x_sorted, w_gate, w_up)


def _moe_down_kernel(tv_ref, tc_ref, te_ref, tn_ref, ts_ref, x_ref, w_ref, o_ref, wbuf, w16, sems, *, layer):
    s = pl.program_id(0)
    _expert_weights_step(s, tv_ref, te_ref, tn_ref, ts_ref, (w_ref,), (wbuf,), (w16,), sems, layer)

    @pl.when(tv_ref[s] > 0)
    def _():
        o_ref[...] = _dot(x_ref[...], w16[...])

    @pl.when(tv_ref[s] == 0)
    def _():
        o_ref[...] = jnp.zeros_like(o_ref)


def moe_down(hid, w_down, layer, plan, tm):
    r_total, ff = hid.shape
    d = w_down.shape[-1]
    n_tiles = r_total // tm
    grid_spec = pltpu.PrefetchScalarGridSpec(
        num_scalar_prefetch=5,
        grid=(n_tiles,),
        in_specs=[
            pl.BlockSpec((tm, ff), lambda s, tv, tc, te, tn, ts: (tc[s], 0)),
            pl.BlockSpec(memory_space=pl.ANY),
        ],
        out_specs=pl.BlockSpec((tm, d), lambda s, tv, tc, te, tn, ts: (s, 0)),
        scratch_shapes=[pltpu.VMEM((2, ff, d), F32), pltpu.VMEM((ff, d), BF16),
                        pltpu.SemaphoreType.DMA((1, 2))],
    )
    return pl.pallas_call(
        functools.partial(_moe_down_kernel, layer=layer),
        grid_spec=grid_spec,
        out_shape=jax.ShapeDtypeStruct((r_total, d), F32),
        compiler_params=_cparams(1),
        name="moe_down",
    )(*_moe_tables(plan), hid, w_down)


def _moe_combine_kernel(pos_ref, y_ref, r_ref, h_ref, g_ref, b_ref, o1_ref, o2_ref, ybuf, sem, *,
                        tm, alpha, n_first):
    def start(i, carry):
        _row_copy(y_ref, ybuf.at[0], pos_ref[0, 0, 2 * i], i, sem).start(priority=0)
        _row_copy(y_ref, ybuf.at[1], pos_ref[0, 0, 2 * i + 1], i, sem).start(priority=1)
        return carry

    def wait(i, carry):
        _row_copy(y_ref, ybuf.at[0], pos_ref[0, 0, 2 * i], i, sem).wait()
        _row_copy(y_ref, ybuf.at[1], pos_ref[0, 0, 2 * i + 1], i, sem).wait()
        return carry

    lax.fori_loop(0, tm, start, 0, unroll=DMA_LOOP_UNROLL)
    lax.fori_loop(0, tm, wait, 0, unroll=DMA_LOOP_UNROLL)
    route = r_ref[...]
    ff = ybuf[0] * route[:, 2:3] + ybuf[1] * route[:, 3:4]
    h = _ln_rows(alpha * h_ref[...] + ff, g_ref[...], b_ref[...])
    if n_first is None:
        o1_ref[...] = h
        o2_ref[...] = h.astype(BF16)
    else:
        i = pl.program_id(0)

        @pl.when(i < n_first)
        def _():
            o1_ref[...] = h

        @pl.when(i >= n_first)
        def _():
            o2_ref[...] = h


def moe_combine(y_sorted, pos, route, h, g, b, *, alpha, split_rows=None):
    t, d = h.shape
    tm = _pick(t if split_rows is None else math.gcd(split_rows, t - split_rows), (256, 128, 64, 32, 16, 8))
    nblk = t // tm
    if split_rows is None:
        n_first = None
        out_specs = [pl.BlockSpec((tm, d), lambda i: (i, 0)), pl.BlockSpec((tm, d), lambda i: (i, 0))]
        out_shape = [jax.ShapeDtypeStruct((t, d), F32), jax.ShapeDtypeStruct((t, d), BF16)]
    else:
        n_first = split_rows // tm
        out_specs = [pl.BlockSpec((tm, d), lambda i: (jnp.minimum(i, n_first - 1), 0)),
                     pl.BlockSpec((tm, d), lambda i: (jnp.maximum(i - n_first, 0), 0))]
        out_shape = [jax.ShapeDtypeStruct((split_rows, d), F32),
                     jax.ShapeDtypeStruct((t - split_rows, d), F32)]
    return pl.pallas_call(
        functools.partial(_moe_combine_kernel, tm=tm, alpha=alpha, n_first=n_first),
        grid=(nblk,),
        in_specs=[pl.BlockSpec((1, 1, 2 * tm), lambda i: (i, 0, 0), memory_space=pltpu.SMEM),
                  pl.BlockSpec(memory_space=pl.ANY),
                  pl.BlockSpec((tm, LANES), lambda i: (i, 0)),
                  pl.BlockSpec((tm, d), lambda i: (i, 0)),
                  pl.BlockSpec((1, d), lambda i: (0, 0)),
                  pl.BlockSpec((1, d), lambda i: (0, 0))],
        out_specs=out_specs,
        out_shape=out_shape,
        scratch_shapes=[pltpu.VMEM((2, tm, d), F32), pltpu.SemaphoreType.DMA(())],
        compiler_params=_cparams(1),
        name="moe_combine",
    )(pos.reshape(nblk, 1, 2 * tm), y_sorted, route, h, g.reshape(1, d), b.reshape(1, d))


def moe_plan(route, tm):
    t = route.shape[0]
    eid = route[:, :2].astype(jnp.int32).reshape(-1)
    onehot = (eid[:, None] == jnp.arange(N_EXPERTS, dtype=jnp.int32)[None, :]).astype(jnp.int32)
    csum = jnp.cumsum(onehot, axis=0)
    rank = jnp.sum((csum - onehot) * onehot, axis=1)
    counts = csum[-1]
    tiles_e = (counts + tm - 1) // tm
    tile_end = jnp.cumsum(tiles_e)
    tile_start = tile_end - tiles_e
    n_used = tile_end[-1]
    n_tiles = (2 * t + N_EXPERTS * (tm - 1)) // tm + 1
    r_total = n_tiles * tm
    dest = tile_start[eid] * tm + rank
    row_token = (jnp.arange(r_total, dtype=jnp.int32) % t).at[dest].set(
        jnp.arange(2 * t, dtype=jnp.int32) // 2)
    tile_ids = jnp.arange(n_tiles, dtype=jnp.int32)
    tile_clamped = jnp.minimum(tile_ids, n_used - 1)
    tile_e = jnp.sum(tile_end[None, :] <= tile_clamped[:, None], axis=1).astype(jnp.int32)
    tile_rows = jnp.clip(counts[tile_e] - (tile_ids - tile_start[tile_e]) * tm, 0, tm)
    tile_rows = jnp.where(tile_ids < n_used, tile_rows, 0).astype(jnp.int32)
    tile_first = tile_ids == tile_start[tile_e]
    tile_v = jnp.where(tile_ids < n_used, 1 + tile_first.astype(jnp.int32), 0).astype(jnp.int32)
    e_ids = jnp.arange(N_EXPERTS, dtype=jnp.int32)
    nonempty = tiles_e > 0
    cand = jnp.where(nonempty[None, :] & (e_ids[None, :] > e_ids[:, None]), e_ids[None, :], N_EXPERTS)
    next_e = jnp.min(cand, axis=1)
    next_e = jnp.where(next_e >= N_EXPERTS, -1, next_e).astype(jnp.int32)
    slot_e = ((jnp.cumsum(nonempty.astype(jnp.int32)) - 1) % 2).astype(jnp.int32)
    return dict(row_token=row_token, pos=dest.astype(jnp.int32), tile_rows=tile_rows, tile_v=tile_v,
                tile_c=tile_clamped,
                tile_e=tile_e, tile_next=next_e[tile_e], tile_slot=slot_e[tile_e])


def hierarchical_moe_ln(hf, layer, w_rg, b_rg, w_re, b_re, w_gate, w_up, w_down, ln_g, ln_b, *, alpha,
                        split_rows=None):
    route = moe_router(hf, w_rg, b_rg, w_re, b_re)
    plan = moe_plan(route, MOE_TM)
    x_sorted = moe_gather(hf, plan["row_token"], plan["tile_rows"], MOE_TM)
    hid = moe_up(x_sorted, w_gate, w_up, layer, plan, MOE_TM)
    y_sorted = moe_down(hid, w_down, layer, plan, MOE_TM)
    return moe_combine(y_sorted, plan["pos"], route, hf, ln_g, ln_b, alpha=alpha, split_rows=split_rows)


def _to_seq_major(x_tm, lt, nb, pad_to):
    w = x_tm.shape[1]
    x = jnp.transpose(x_tm.reshape(lt, nb, w), (1, 0, 2))
    x = jnp.pad(x, ((0, 0), (0, pad_to - lt), (0, 0)))
    return x.reshape(nb * pad_to, w)


def _to_time_major(x_sm, lt, nb, pad_to):
    w = x_sm.shape[1]
    x = x_sm.reshape(nb, pad_to, w)[:, :lt]
    return jnp.transpose(x, (1, 0, 2)).reshape(lt * nb, w)


def kernel(x_prompt, x_sample, mem_prompt, cache_swa_k, cache_swa_v, cache_mem_k, cache_mem_v, state_conv, state_ssm, ln_in_g, ln_in_b, w_in, attn_sinks, gm_ln_g, gm_ln_b, gm_ws, gm_bs, conv_w, conv_b, dt_bias, a_log, d_skip, ssm_norm_g, w_pa, w_pb, w_pc, w_o, ln1_g, ln1_b, w_cq, w_ck, w_cv, w_co, ln2_g, ln2_b, w_rg, b_rg, w_re, b_re, w_gate, w_up, w_down, ln3_g, ln3_b):
    bp, seq, d = x_prompt.shape
    nb, lt, _ = x_sample.shape
    depth = w_in.shape[0]
    mem_len = mem_prompt.shape[1]
    past_len = PAST_LEN
    wb = cache_swa_k.shape[2]
    assert wb == WINDOW and seq % CHUNK == 0 and lt <= SUBLANES
    tp, ts = bp * seq, nb * lt
    alpha = (2 * depth) ** 0.25
    qpad = SUBLANES

    xp = x_prompt.reshape(tp, d)
    xs = jnp.transpose(x_sample, (1, 0, 2)).reshape(ts, d)
    hf, hb = ln_in(xp, xs, ln_in_g, ln_in_b)
    cos_t, sin_t = rope_tables(tp, seq, ts, nb, past_len)
    mem_b = mem_prompt.reshape(bp * mem_len, d).astype(BF16)

    in_w = w_in.shape[2]
    assert in_w == Q_W + 2 * KV_W + 2 * GM_W + SSM_INNER + CONV_DIM + SSM_HEADS + 3 * D_MODEL
    n_whole = (in_w // IN_TN) * IN_TN
    w_t = jnp.swapaxes(w_in, 1, 2)
    w_tail = jnp.pad(w_t[:, n_whole:, :], ((0, 0), (0, IN_TN - (in_w - n_whole)), (0, 0)))

    outs = {k: [] for k in ("p_k", "p_v", "p_mk", "p_mv", "p_conv", "p_ssm", "p_gv",
                            "s_k", "s_v", "s_conv", "s_ssm", "s_gv")}
    n_qblk = seq // WINDOW
    s_states = None
    for l in range(depth):
        proj = in_projection(hb, w_t, w_tail, l)
        q_rot, k_rot = rope_qk(proj, cos_t, sin_t)

        kcol, vcol = 0, COL_V // KV_W
        att_p = swa_attention(
            attn_sinks[l], q_rot, k_rot, k_rot, proj, proj,
            n_seq=bp, n_blk=n_qblk, qb=WINDOW, prev_from_block0=False,
            kp_map=lambda i, n: (jnp.maximum(i * n_qblk + n - 1, 0), kcol),
            kc_map=lambda i, n: (i * n_qblk + n, kcol),
            vp_map=lambda i, n: (jnp.maximum(i * n_qblk + n - 1, 0), vcol),
            vc_map=lambda i, n: (i * n_qblk + n, vcol),
            out_dtype=BF16, name="swa_prompt")
        k_s_tm = k_rot[tp:]
        v_s_tm = proj[tp:, COL_V:COL_V + KV_W]
        q_s = _to_seq_major(q_rot[tp:], lt, nb, qpad)
        k_s = _to_seq_major(k_s_tm, lt, nb, qpad)
        v_s = _to_seq_major(v_s_tm, lt, nb, qpad)
        att_s8 = swa_attention_cached(attn_sinks[l], q_s, k_s, v_s, cache_swa_k, cache_swa_v, layer=l,
                                      n_seq=nb, qb=qpad, bseq=_pick(nb, (8, 4, 2, 1)), name="swa_sample")
        att_s = _to_time_major(att_s8, lt, nb, qpad).astype(BF16)
        last_w = lambda a, c0: jnp.stack(
            [a[(i + 1) * seq - WINDOW:(i + 1) * seq, c0:c0 + KV_W] for i in range(bp)]
        ).reshape(bp, WINDOW, N_KV_HEADS, HEAD_DIM)
        outs["p_k"].append(last_w(k_rot, 0))
        outs["p_v"].append(last_w(proj, COL_V))
        k_new = jnp.transpose(k_s_tm.reshape(lt, nb, N_KV_HEADS, HEAD_DIM), (1, 0, 2, 3))
        v_new = jnp.transpose(v_s_tm.reshape(lt, nb, N_KV_HEADS, HEAD_DIM), (1, 0, 2, 3))
        outs["s_k"].append(jnp.concatenate([cache_swa_k[l], k_new], axis=1)[:, -wb:])
        outs["s_v"].append(jnp.concatenate([cache_swa_v[l], v_new], axis=1)[:, -wb:])

        gm_p, vg_last = gmlp_prompt(proj, bp, seq, gm_ws[l], gm_bs[l], gm_ln_g[l], gm_ln_b[l])
        gm_s, vg_s = gmlp_sample(proj, tp, nb, lt, gm_ws[l], gm_bs[l], gm_ln_g[l], gm_ln_b[l])
        outs["p_gv"].append(vg_last.reshape(bp, CHUNK, GM_GROUPS, GM_GROUP_DIM))
        outs["s_gv"].append(jnp.transpose(vg_s.reshape(lt, nb, GM_GROUPS, GM_GROUP_DIM), (1, 0, 2, 3)))

        dtt = jnp.transpose(proj[:tp, COL_DT:COL_DT + SSM_HEADS])
        y_p, st_p = ssd_prompt(proj, dtt, bp, seq, conv_w[l], conv_b[l], dt_bias[l], a_log[l],
                               d_skip[l], ssm_norm_g[l])
        ssm_s, s_states = ssd_sample(proj, tp, nb, lt, state_conv[l], state_ssm, s_states, l, conv_w[l],
                                     conv_b[l], dt_bias[l], a_log[l], d_skip[l], ssm_norm_g[l])
        outs["p_conv"].append(jnp.stack(
            [proj[(i + 1) * seq - (CONV_K - 1):(i + 1) * seq, COL_XS:COL_XS + CONV_DIM] for i in range(bp)]))
        xbc_s = jnp.transpose(proj[tp:, COL_XS:COL_XS + CONV_DIM].reshape(lt, nb, CONV_DIM), (1, 0, 2))
        outs["s_conv"].append(jnp.concatenate([state_conv[l], xbc_s], axis=1)[:, -(CONV_K - 1):])
        outs["p_ssm"].append(st_p.reshape(bp, SSM_HEADS, SSM_HEAD_DIM, SSM_STATE))

        merged = gated_merge(att_p, att_s, gm_p, gm_s, y_p, ssm_s, w_pa[l].astype(BF16),
                             w_pb[l].astype(BF16), w_pc[l].astype(BF16), proj)
        h1f, h1b = matmul_ln(merged, w_o[l].astype(BF16), hf, ln1_g[l], ln1_b[l], alpha=alpha, name="out_proj_ln1")

        qc = matmul(h1b, w_cq[l].astype(BF16), name="xattn_q")
        pmk = matmul(mem_b, w_ck[l].astype(BF16), name="mem_k")
        pmv = matmul(mem_b, w_cv[l].astype(BF16), name="mem_v")
        pmk5 = pmk.reshape(1, bp, mem_len, MEM_HEADS, MEM_HEAD_DIM)
        pmv5 = pmv.reshape(1, bp, mem_len, MEM_HEADS, MEM_HEAD_DIM)
        outs["p_mk"].append(pmk5[0])
        outs["p_mv"].append(pmv5[0])
        tq = _pick(seq, (512, 256, 128))
        o_p = cross_attention(qc, pmk5, pmv5, layer=0, n_seq=bp, seq=seq, tq=tq, bseq=1,
                              name="xattn_prompt")
        qc_s = _to_seq_major(qc[tp:], lt, nb, qpad)
        o_s8 = cross_attention(qc_s, cache_mem_k, cache_mem_v, layer=l, n_seq=nb, seq=qpad, tq=qpad,
                               bseq=_pick(nb, (8, 4, 2, 1)), name="xattn_sample")
        o_all = jnp.concatenate([o_p, _to_time_major(o_s8, lt, nb, qpad)], axis=0).astype(BF16)
        h2f, h2b = matmul_ln(o_all, w_co[l].astype(BF16), h1f, ln2_g[l], ln2_b[l], alpha=alpha, name="xattn_out_ln2")

        hf, hb = hierarchical_moe_ln(h2f, l, w_rg[l], b_rg[l], w_re[l], b_re[l], w_gate, w_up, w_down,
                                     ln3_g[l], ln3_b[l], alpha=alpha,
                                     split_rows=tp if l == depth - 1 else None)

    y_prompt = hf.reshape(bp, seq, d)
    y_sample = jnp.transpose(hb.reshape(lt, nb, d), (1, 0, 2))
    st = lambda k: jnp.stack(outs[k])
    s_ssm = s_states.reshape(depth, nb, SSM_HEADS, SSM_HEAD_DIM, SSM_STATE)
    return (y_prompt, y_sample, st("p_k"), st("p_v"), st("p_mk"), st("p_mv"), st("p_conv"), st("p_ssm"),
            st("p_gv"), st("s_k"), st("s_v"), st("s_conv"), s_ssm, st("s_gv"))
```

```python
import functools
import math

import numpy as np
import jax
import jax.numpy as jnp
from jax import lax
from jax.experimental import pallas as pl
from jax.experimental.pallas import tpu as pltpu

F32 = jnp.float32
BF16 = jnp.bfloat16

D_MODEL = 2048
N_HEADS = 32
N_KV_HEADS = 4
HEAD_DIM = 64
WINDOW = 128
PAST_LEN = 8192
ROPE_THETA = 10000.0
CHUNK = 128
GM_GROUPS = 16
GM_GROUP_DIM = 128
SSM_HEADS = 32
SSM_HEAD_DIM = 64
SSM_GROUPS = 4
SSM_STATE = 128
CONV_K = 4
MEM_HEADS = 4
MEM_HEAD_DIM = 128
N_EGROUPS = 4
EXPERTS_PER_GROUP = 8
N_EXPERTS = N_EGROUPS * EXPERTS_PER_GROUP
EXPERT_FF = D_MODEL // 2
Q_W = N_HEADS * HEAD_DIM
KV_W = N_KV_HEADS * HEAD_DIM
GM_W = GM_GROUPS * GM_GROUP_DIM
SSM_INNER = SSM_HEADS * SSM_HEAD_DIM
SSM_BC = SSM_GROUPS * SSM_STATE
CONV_DIM = SSM_INNER + 2 * SSM_BC
MEM_W = MEM_HEADS * MEM_HEAD_DIM
LN_EPS = 1e-5
NEG_BIG = -1e30

VMEM_LIMIT_BYTES = 52 * 1024 * 1024
LANES = 128
SUBLANES = 8

IN_TN = 512
COL_Q = 0
COL_GU = 2048
COL_GV = 4096
COL_Z = 6144
COL_XS = 8192
COL_BC = 10240
COL_DT = 11264
COL_GATES = COL_DT + SSM_HEADS
COL_K = 17920
COL_V = 18176
PROJ_W = 18432

MOE_TM = 256
DMA_LOOP_UNROLL = 8


def _cparams(n_grid):
    return pltpu.CompilerParams(
        dimension_semantics=("arbitrary",) * n_grid,
        vmem_limit_bytes=VMEM_LIMIT_BYTES,
    )


def _pick(n, prefs):
    for p in prefs:
        if n % p == 0:
            return p
    raise ValueError(f"no tile for {n} in {prefs}")


def _ln_rows(x, g, b):
    mu = jnp.mean(x, axis=-1, keepdims=True)
    xc = x - mu
    var = jnp.mean(xc * xc, axis=-1, keepdims=True)
    return xc * lax.rsqrt(var + LN_EPS) * g + b


def _sigmoid(x):
    return 1.0 / (1.0 + jnp.exp(-x))


def _silu(x):
    return x * _sigmoid(x)


def _softplus(x):
    return jnp.maximum(x, 0.0) + jnp.log1p(jnp.exp(-jnp.abs(x)))


def _gelu(x):
    return jax.nn.gelu(x, approximate=True)


def _split3(x):
    hi = x.astype(BF16)
    r1 = x - hi.astype(F32)
    mid = r1.astype(BF16)
    lo = (r1 - mid.astype(F32)).astype(BF16)
    return hi, mid, lo


def _dot(a, b):
    return jnp.dot(a, b, preferred_element_type=F32)


def _dot_nt(a, b):
    return lax.dot_general(a, b, (((1,), (1,)), ((), ())), preferred_element_type=F32)


def _exact_dot_left(pieces, m):
    acc = _dot(pieces[0], m)
    for p in pieces[1:]:
        acc = acc + _dot(p, m)
    return acc


def _ln_in_kernel(xp_ref, xs_ref, g_ref, b_ref, of_ref, ob_ref, *, n_p):
    i = pl.program_id(0)

    @pl.when(i < n_p)
    def _():
        y = _ln_rows(xp_ref[...], g_ref[...], b_ref[...])
        of_ref[...] = y
        ob_ref[...] = y.astype(BF16)

    @pl.when(i >= n_p)
    def _():
        y = _ln_rows(xs_ref[...], g_ref[...], b_ref[...])
        of_ref[...] = y
        ob_ref[...] = y.astype(BF16)


def ln_in(xp, xs, g, b):
    tp, d = xp.shape
    ts = xs.shape[0]
    tm = _pick(math.gcd(tp, ts), (256, 128, 64, 32, 16, 8))
    n_p, n_s = tp // tm, ts // tm
    t = tp + ts
    return pl.pallas_call(
        functools.partial(_ln_in_kernel, n_p=n_p),
        grid=(n_p + n_s,),
        in_specs=[
            pl.BlockSpec((tm, d), lambda i: (jnp.minimum(i, n_p - 1), 0)),
            pl.BlockSpec((tm, d), lambda i: (jnp.maximum(i - n_p, 0), 0)),
            pl.BlockSpec((1, d), lambda i: (0, 0)),
            pl.BlockSpec((1, d), lambda i: (0, 0)),
        ],
        out_specs=[
            pl.BlockSpec((tm, d), lambda i: (i, 0)),
            pl.BlockSpec((tm, d), lambda i: (i, 0)),
        ],
        out_shape=[jax.ShapeDtypeStruct((t, d), F32), jax.ShapeDtypeStruct((t, d), BF16)],
        compiler_params=_cparams(1),
        name="ln_in",
    )(xp, xs, g.reshape(1, d), b.reshape(1, d))


def _mm_kernel(x_ref, w_ref, o_ref):
    o_ref[...] = _dot(x_ref[...], w_ref[...]).astype(o_ref.dtype)


def matmul(x, w, *, out_dtype=F32, tm_prefs=(1088, 1024, 512, 256, 128, 64, 32, 16, 8),
           tn_prefs=(1280, 1024, 512, 256, 128), name="mm"):
    t, k = x.shape
    n = w.shape[1]
    tm = _pick(t, tm_prefs)
    tn = _pick(n, tn_prefs)
    return pl.pallas_call(
        _mm_kernel,
        grid=(n // tn, t // tm),
        in_specs=[
            pl.BlockSpec((tm, k), lambda j, i: (i, 0)),
            pl.BlockSpec((k, tn), lambda j, i: (0, j)),
        ],
        out_specs=pl.BlockSpec((tm, tn), lambda j, i: (i, j)),
        out_shape=jax.ShapeDtypeStruct((t, n), out_dtype),
        compiler_params=_cparams(2),
        name=name,
    )(x, w)


def _in_proj_kernel(x_ref, w_ref, wt_ref, o_ref, *, n_main):
    j = pl.program_id(1)

    @pl.when(j < n_main)
    def _():
        o_ref[...] = _dot_nt(x_ref[...], w_ref[...].astype(BF16))

    @pl.when(j >= n_main)
    def _():
        o_ref[...] = _dot_nt(x_ref[...], wt_ref[...].astype(BF16))


def in_projection(x, w_t, w_tail_t, layer):
    t, k = x.shape
    n_blk = PROJ_W // IN_TN
    n_main = w_t.shape[1] // IN_TN
    kv_blk = Q_W // IN_TN
    assert 2 * KV_W == IN_TN and COL_K == (n_blk - 1) * IN_TN and n_main == n_blk - 1
    tm = _pick(t, (2176, 1088, 544, 32, 16, 8))

    def out_map(i, j):
        return i, jnp.where(j < kv_blk, j, jnp.where(j == kv_blk, n_blk - 1, j - 1))

    return pl.pallas_call(
        functools.partial(_in_proj_kernel, n_main=n_main),
        grid=(t // tm, n_blk),
        in_specs=[
            pl.BlockSpec((tm, k), lambda i, j: (i, 0)),
            pl.BlockSpec((None, IN_TN, k), lambda i, j: (layer, jnp.minimum(j, n_main - 1), 0)),
            pl.BlockSpec((None, IN_TN, k), lambda i, j: (layer, 0, 0)),
        ],
        out_specs=pl.BlockSpec((tm, IN_TN), out_map),
        out_shape=jax.ShapeDtypeStruct((t, PROJ_W), F32),
        compiler_params=_cparams(2),
        name="in_proj",
    )(x, w_t, w_tail_t)


def _mm_ln_kernel(x_ref, w_ref, r_ref, g_ref, b_ref, of_ref, ob_ref, *, alpha):
    y = _dot(x_ref[...], w_ref[...])
    h = _ln_rows(alpha * r_ref[...] + y, g_ref[...], b_ref[...])
    of_ref[...] = h
    ob_ref[...] = h.astype(BF16)


def matmul_ln(x, w, res, g, b, *, alpha, name="mm_ln"):
    t, k = x.shape
    d = w.shape[1]
    tm = _pick(t, (256, 128, 64, 32, 16, 8))
    return pl.pallas_call(
        functools.partial(_mm_ln_kernel, alpha=alpha),
        grid=(t // tm,),
        in_specs=[
            pl.BlockSpec((tm, k), lambda i: (i, 0)),
            pl.BlockSpec((k, d), lambda i: (0, 0)),
            pl.BlockSpec((tm, d), lambda i: (i, 0)),
            pl.BlockSpec((1, d), lambda i: (0, 0)),
            pl.BlockSpec((1, d), lambda i: (0, 0)),
        ],
        out_specs=[
            pl.BlockSpec((tm, d), lambda i: (i, 0)),
            pl.BlockSpec((tm, d), lambda i: (i, 0)),
        ],
        out_shape=[jax.ShapeDtypeStruct((t, d), F32), jax.ShapeDtypeStruct((t, d), BF16)],
        compiler_params=_cparams(1),
        name=name,
    )(x, w, res, g.reshape(1, d), b.reshape(1, d))


def _merge_kernel(ap_ref, as_ref, bp_ref, bs_ref, cp_ref, cs_ref, wa_ref, wb_ref, wc_ref,
                  ga_ref, ga2_ref, gb_ref, gb2_ref, gc_ref, gc2_ref, o_ref, *, n_p, shift):
    i = pl.program_id(1)
    is_p = i < n_p
    tn = o_ref.shape[1]

    def gate(main_ref, tail_ref):
        win = jnp.concatenate([main_ref[...], tail_ref[...]], axis=1)
        return _sigmoid(win[:, shift:shift + tn])

    xa = jnp.where(is_p, ap_ref[...], as_ref[...])
    xb = jnp.where(is_p, bp_ref[...], bs_ref[...])
    xc = jnp.where(is_p, cp_ref[...], cs_ref[...])
    acc = gate(ga_ref, ga2_ref) * _dot(xa, wa_ref[...])
    acc = acc + gate(gb_ref, gb2_ref) * _dot(xb, wb_ref[...])
    acc = acc + gate(gc_ref, gc2_ref) * _dot(xc, wc_ref[...])
    o_ref[...] = acc.astype(o_ref.dtype)


def gated_merge(att_p, att_s, gm_p, gm_s, ssm_p, ssm_s, w_pa, w_pb, w_pc, proj):
    tp, k = att_p.shape
    ts = att_s.shape[0]
    d = w_pa.shape[1]
    tm = _pick(math.gcd(tp, ts), (256, 128, 64, 32, 16, 8))
    tn = 1024
    n_p, n_s = tp // tm, ts // tm
    shift = COL_GATES - COL_DT
    assert COL_DT % tn == 0 and d % tn == 0 and shift < LANES

    def xp_spec():
        return pl.BlockSpec((tm, k), lambda j, i: (jnp.minimum(i, n_p - 1), 0))

    def xs_spec():
        return pl.BlockSpec((tm, k), lambda j, i: (jnp.maximum(i - n_p, 0), 0))

    def w_spec():
        return pl.BlockSpec((k, tn), lambda j, i: (0, j))

    def g_specs(which):
        base = (COL_DT + which * d) // tn
        tail = (COL_DT + which * d) // LANES
        return [pl.BlockSpec((tm, tn), lambda j, i: (i, base + j)),
                pl.BlockSpec((tm, LANES), lambda j, i: (i, tail + (j + 1) * (tn // LANES)))]

    return pl.pallas_call(
        functools.partial(_merge_kernel, n_p=n_p, shift=shift),
        grid=(d // tn, n_p + n_s),
        in_specs=[xp_spec(), xs_spec(), xp_spec(), xs_spec(), xp_spec(), xs_spec(),
                  w_spec(), w_spec(), w_spec()] + g_specs(0) + g_specs(1) + g_specs(2),
        out_specs=pl.BlockSpec((tm, tn), lambda j, i: (i, j)),
        out_shape=jax.ShapeDtypeStruct((tp + ts, d), BF16),
        compiler_params=_cparams(2),
        name="gated_merge",
    )(att_p, att_s, gm_p, gm_s, ssm_p, ssm_s, w_pa, w_pb, w_pc, *([proj] * 6))


def _rope_block(x, cos, sin_signed, first_half):
    outs = []
    for c in range(x.shape[1] // LANES):
        xc = x[:, c * LANES:(c + 1) * LANES]
        fwd = pltpu.roll(xc, LANES - HEAD_DIM // 2, axis=1)
        bwd = pltpu.roll(xc, HEAD_DIM // 2, axis=1)
        partner = jnp.where(first_half, fwd, bwd)
        outs.append(xc * cos + partner * sin_signed)
    return outs


def _rope_kernel(q_ref, k_ref, cos_ref, sin_ref, qo_ref, ko_ref):
    cos = cos_ref[...]
    sin_signed = sin_ref[...]
    lane = lax.broadcasted_iota(jnp.int32, cos.shape, 1)
    first_half = (lane % HEAD_DIM) < (HEAD_DIM // 2)
    for c, o in enumerate(_rope_block(q_ref[...], cos, sin_signed, first_half)):
        qo_ref[:, c * LANES:(c + 1) * LANES] = o
    for c, o in enumerate(_rope_block(k_ref[...], cos, sin_signed, first_half)):
        ko_ref[:, c * LANES:(c + 1) * LANES] = o


def rope_qk(proj, cos_t, sin_t):
    t = proj.shape[0]
    tm = _pick(t, (256, 128, 64, 32, 16, 8))
    return pl.pallas_call(
        _rope_kernel,
        grid=(t // tm,),
        in_specs=[
            pl.BlockSpec((tm, Q_W), lambda i: (i, COL_Q // Q_W)),
            pl.BlockSpec((tm, KV_W), lambda i: (i, COL_K // KV_W)),
            pl.BlockSpec((tm, LANES), lambda i: (i, 0)),
            pl.BlockSpec((tm, LANES), lambda i: (i, 0)),
        ],
        out_specs=[
            pl.BlockSpec((tm, Q_W), lambda i: (i, 0)),
            pl.BlockSpec((tm, KV_W), lambda i: (i, 0)),
        ],
        out_shape=[jax.ShapeDtypeStruct((t, Q_W), F32), jax.ShapeDtypeStruct((t, KV_W), F32)],
        compiler_params=_cparams(1),
        name="rope_qk",
    )(proj, proj, cos_t, sin_t)


def rope_tables(tp, seq, ts, nb, past_len):
    half = HEAD_DIM // 2
    inv = ROPE_THETA ** (-jnp.arange(half, dtype=F32) / half)
    pos_p = jnp.arange(tp, dtype=jnp.int32) % seq
    pos_s = past_len + jnp.arange(ts, dtype=jnp.int32) // nb
    pos = jnp.concatenate([pos_p, pos_s]).astype(F32)
    ang = pos[:, None] * inv[None, :]
    cos = jnp.tile(jnp.cos(ang), (1, LANES // half))
    sin = jnp.sin(ang)
    sin_signed = jnp.tile(jnp.concatenate([-sin, sin], axis=1), (1, LANES // HEAD_DIM))
    return cos, sin_signed


def _dup_head(slab, g):
    lane = lax.broadcasted_iota(jnp.int32, slab.shape, 1)
    rolled = pltpu.roll(slab, HEAD_DIM, axis=1)
    if g % 2 == 0:
        return jnp.where(lane < HEAD_DIM, slab, rolled)
    return jnp.where(lane < HEAD_DIM, rolled, slab)


def _swa_bias(qb, prev_ok):
    kw = WINDOW + qb
    ii = lax.broadcasted_iota(jnp.int32, (qb, kw), 0)
    jj = lax.broadcasted_iota(jnp.int32, (qb, kw), 1)
    ok_prev = (jj < WINDOW) & (jj > ii)
    if prev_ok is not True:
        ok_prev = ok_prev & prev_ok
    ok = ok_prev | ((jj >= WINDOW) & ((jj - WINDOW) <= ii))
    return jnp.where(ok, 0.0, NEG_BIG)


def _swa_group(q, kdup, vdup, bias, sink_ref, g, qb, store):
    rep = N_HEADS // N_KV_HEADS
    lane = lax.broadcasted_iota(jnp.int32, (qb, LANES), 1)
    scale = HEAD_DIM ** -0.5
    rows = []
    for r in range(rep):
        h = g * rep + r
        qc = q[:, (h // 2) * LANES:(h // 2 + 1) * LANES]
        keep = (lane < HEAD_DIM) if h % 2 == 0 else (lane >= HEAD_DIM)
        rows.append(jnp.where(keep, qc * scale, 0.0))
    s = _dot_nt(jnp.concatenate(rows, axis=0).astype(BF16), kdup)
    ps = []
    for r in range(rep):
        sr = s[r * qb:(r + 1) * qb] + bias
        sink = sink_ref[g * rep + r]
        m = jnp.maximum(jnp.max(sr, axis=-1, keepdims=True), sink)
        p = jnp.exp(sr - m)
        den = jnp.sum(p, axis=-1, keepdims=True) + jnp.exp(sink - m)
        ps.append(p * (1.0 / den))
    o = _dot(jnp.concatenate(ps, axis=0).astype(BF16), vdup)
    for c in range(rep // 2):
        oa = o[(2 * c) * qb:(2 * c + 1) * qb]
        ob = o[(2 * c + 1) * qb:(2 * c + 2) * qb]
        store((g * rep // 2 + c) * LANES, jnp.where(lane < HEAD_DIM, oa, ob))


def _swa_kernel(sink_ref, q_ref, kp_ref, kc_ref, vp_ref, vc_ref, o_ref, *, qb, prev_from_block0):
    n = pl.program_id(1)
    q = q_ref[...]
    kp, kc, vp, vc = kp_ref[...], kc_ref[...], vp_ref[...], vc_ref[...]
    bias = _swa_bias(qb, True if prev_from_block0 else (n > 0))

    def store(col, val):
        o_ref[:, col:col + LANES] = val.astype(o_ref.dtype)

    for g in range(N_KV_HEADS):
        sl = slice((g // 2) * LANES, (g // 2 + 1) * LANES)
        kdup = _dup_head(jnp.concatenate([kp[:, sl], kc[:, sl]], axis=0), g).astype(BF16)
        vdup = _dup_head(jnp.concatenate([vp[:, sl], vc[:, sl]], axis=0), g).astype(BF16)
        _swa_group(q, kdup, vdup, bias, sink_ref, g, qb, store)


def _swa_cache_kernel(sink_ref, q_ref, kc_ref, vc_ref, kp_ref, vp_ref, o_ref, *, qb, bseq):
    rep = N_HEADS // N_KV_HEADS
    bias = jnp.concatenate([_swa_bias(qb, True)] * rep, axis=0)
    q = q_ref[...].reshape(bseq, qb, Q_W)
    kc = kc_ref[...].reshape(bseq, qb, KV_W)
    vc = vc_ref[...].reshape(bseq, qb, KV_W)
    lane = lax.broadcasted_iota(jnp.int32, (bseq, qb, LANES), 2)
    scale = HEAD_DIM ** -0.5
    bdot_nt = lambda a, b: lax.dot_general(a, b, (((2,), (2,)), ((0,), (0,))), preferred_element_type=F32)
    bdot = lambda a, b: lax.dot_general(a, b, (((2,), (1,)), ((0,), (0,))), preferred_element_type=F32)
    for g in range(N_KV_HEADS):
        hs = slice(g * HEAD_DIM, (g + 1) * HEAD_DIM)
        kcat = jnp.concatenate([kp_ref[:, :, g, :], kc[:, :, hs]], axis=1)
        vcat = jnp.concatenate([vp_ref[:, :, g, :], vc[:, :, hs]], axis=1)
        kdup = jnp.concatenate([kcat, kcat], axis=2).astype(BF16)
        vdup = jnp.concatenate([vcat, vcat], axis=2).astype(BF16)
        rows, sinks = [], []
        for r in range(rep):
            h = g * rep + r
            qc = q[:, :, (h // 2) * LANES:(h // 2 + 1) * LANES]
            keep = (lane < HEAD_DIM) if h % 2 == 0 else (lane >= HEAD_DIM)
            rows.append(jnp.where(keep, qc * scale, 0.0))
            sinks.append(jnp.full((qb, 1), sink_ref[h], F32))
        sink = jnp.concatenate(sinks, axis=0)
        s = bdot_nt(jnp.concatenate(rows, axis=1).astype(BF16), kdup) + bias
        m = jnp.maximum(jnp.max(s, axis=-1, keepdims=True), sink)
        p = jnp.exp(s - m)
        den = jnp.sum(p, axis=-1, keepdims=True) + jnp.exp(sink - m)
        o = bdot((p * (1.0 / den)).astype(BF16), vdup)
        for c in range(rep // 2):
            oa = o[:, (2 * c) * qb:(2 * c + 1) * qb]
            ob = o[:, (2 * c + 1) * qb:(2 * c + 2) * qb]
            col = (g * rep // 2 + c) * LANES
            o_ref[:, col:col + LANES] = jnp.where(lane < HEAD_DIM, oa, ob).reshape(bseq * qb, LANES).astype(
                o_ref.dtype)


def swa_attention_cached(sinks, q, kc, vc, cache_k, cache_v, *, layer, n_seq, qb, bseq, name):
    rows = bseq * qb
    cache_spec = pl.BlockSpec((None, bseq, WINDOW, N_KV_HEADS, HEAD_DIM), lambda i: (layer, i, 0, 0, 0))
    return pl.pallas_call(
        functools.partial(_swa_cache_kernel, qb=qb, bseq=bseq),
        grid=(n_seq // bseq,),
        in_specs=[pl.BlockSpec(memory_space=pltpu.SMEM),
                  pl.BlockSpec((rows, Q_W), lambda i: (i, 0)),
                  pl.BlockSpec((rows, KV_W), lambda i: (i, 0)),
                  pl.BlockSpec((rows, KV_W), lambda i: (i, 0)),
                  cache_spec, cache_spec],
        out_specs=pl.BlockSpec((rows, Q_W), lambda i: (i, 0)),
        out_shape=jax.ShapeDtypeStruct((n_seq * qb, Q_W), F32),
        compiler_params=_cparams(1),
        name=name,
    )(sinks, q, kc, vc, cache_k, cache_v)


def swa_attention(sinks, q, kp, kc, vp, vc, *, n_seq, n_blk, qb, prev_from_block0,
                  kp_map, kc_map, vp_map, vc_map, out_dtype, name):
    return pl.pallas_call(
        functools.partial(_swa_kernel, qb=qb, prev_from_block0=prev_from_block0),
        grid=(n_seq, n_blk),
        in_specs=[
            pl.BlockSpec(memory_space=pltpu.SMEM),
            pl.BlockSpec((qb, Q_W), lambda i, n: (i * n_blk + n, 0)),
            pl.BlockSpec((WINDOW, KV_W), kp_map),
            pl.BlockSpec((qb, KV_W), kc_map),
            pl.BlockSpec((WINDOW, KV_W), vp_map),
            pl.BlockSpec((qb, KV_W), vc_map),
        ],
        out_specs=pl.BlockSpec((qb, Q_W), lambda i, n: (i * n_blk + n, 0)),
        out_shape=jax.ShapeDtypeStruct((n_seq * n_blk * qb, Q_W), out_dtype),
        compiler_params=_cparams(2),
        name=name,
    )(sinks, q, kp, kc, vp, vc)


def _gmlp_p_kernel(gu_ref, gv_ref, ws_ref, bst_ref, lg_ref, lb_ref, gm_ref, vg_ref, *, n_chunks):
    n = pl.program_id(1)
    vg = _ln_rows(_gelu(gv_ref[...]), lg_ref[...], lb_ref[...])
    gu = gu_ref[...]
    ri = lax.broadcasted_iota(jnp.int32, (CHUNK, CHUNK), 0)
    ci = lax.broadcasted_iota(jnp.int32, (CHUNK, CHUNK), 1)
    tril = ri >= ci
    bst = bst_ref[...]
    for g in range(GM_GROUPS):
        sl = slice(g * GM_GROUP_DIM, (g + 1) * GM_GROUP_DIM)
        w = jnp.where(tril, ws_ref[g], 0.0).astype(BF16)
        s = _dot(w, vg[:, sl].astype(BF16)) + bst[:, g:g + 1]
        gm_ref[:, sl] = (_gelu(gu[:, sl]) * s).astype(gm_ref.dtype)

    @pl.when(n == n_chunks - 1)
    def _():
        vg_ref[...] = vg


def gmlp_prompt(proj, n_seq, seq, ws, bs, ln_g, ln_b):
    nc = seq // CHUNK
    return pl.pallas_call(
        functools.partial(_gmlp_p_kernel, n_chunks=nc),
        grid=(n_seq, nc),
        in_specs=[
            pl.BlockSpec((CHUNK, GM_W), lambda i, n: (i * nc + n, COL_GU // GM_W)),
            pl.BlockSpec((CHUNK, GM_W), lambda i, n: (i * nc + n, COL_GV // GM_W)),
            pl.BlockSpec((GM_GROUPS, CHUNK, CHUNK), lambda i, n: (0, 0, 0)),
            pl.BlockSpec((CHUNK, GM_GROUPS), lambda i, n: (0, 0)),
            pl.BlockSpec((1, GM_W), lambda i, n: (0, 0)),
            pl.BlockSpec((1, GM_W), lambda i, n: (0, 0)),
        ],
        out_specs=[
            pl.BlockSpec((CHUNK, GM_W), lambda i, n: (i * nc + n, 0)),
            pl.BlockSpec((CHUNK, GM_W), lambda i, n: (i, 0)),
        ],
        out_shape=[jax.ShapeDtypeStruct((n_seq * seq, GM_W), BF16),
                   jax.ShapeDtypeStruct((n_seq * CHUNK, GM_W), F32)],
        compiler_params=_cparams(2),
        name="gmlp_prompt",
    )(proj, proj, ws, bs.T, ln_g.reshape(1, GM_W), ln_b.reshape(1, GM_W))


def _gmlp_s_kernel(*refs, lt):
    gu_refs = refs[:lt]
    gv_refs = refs[lt:2 * lt]
    wrow_ref, brow_ref, lg_ref, lb_ref, gm_ref, vg_ref = refs[2 * lt:]
    nb = gu_refs[0].shape[0]
    vgs = [_ln_rows(_gelu(gv_refs[t][...]), lg_ref[...], lb_ref[...]) for t in range(lt)]
    for i in range(lt):
        s = brow_ref[i:i + 1, :]
        for j in range(i + 1):
            s = s + wrow_ref[i * lt + j:i * lt + j + 1, :] * vgs[j]
        gm_ref[i * nb:(i + 1) * nb, :] = (_gelu(gu_refs[i][...]) * s).astype(gm_ref.dtype)
        vg_ref[i * nb:(i + 1) * nb, :] = vgs[i]


def gmlp_sample(proj, tp, nb, lt, ws, bs, ln_g, ln_b):
    w_small = ws[:, :lt, :lt]
    wrow = jnp.repeat(jnp.transpose(w_small, (1, 2, 0)).reshape(lt * lt, GM_GROUPS), GM_GROUP_DIM, axis=1)
    brow = jnp.repeat(bs[:, :lt].T, GM_GROUP_DIM, axis=1)
    row0 = tp // nb

    def spec(t, col):
        return pl.BlockSpec((nb, GM_W), lambda i: (row0 + t, col // GM_W))

    in_specs = [spec(t, COL_GU) for t in range(lt)] + [spec(t, COL_GV) for t in range(lt)] + [
        pl.BlockSpec((lt * lt, GM_W), lambda i: (0, 0)),
        pl.BlockSpec((lt, GM_W), lambda i: (0, 0)),
        pl.BlockSpec((1, GM_W), lambda i: (0, 0)),
        pl.BlockSpec((1, GM_W), lambda i: (0, 0)),
    ]
    return pl.pallas_call(
        functools.partial(_gmlp_s_kernel, lt=lt),
        grid=(1,),
        in_specs=in_specs,
        out_specs=[pl.BlockSpec((lt * nb, GM_W), lambda i: (0, 0)),
                   pl.BlockSpec((lt * nb, GM_W), lambda i: (0, 0))],
        out_shape=[jax.ShapeDtypeStruct((lt * nb, GM_W), BF16),
                   jax.ShapeDtypeStruct((lt * nb, GM_W), F32)],
        compiler_params=_cparams(1),
        name="gmlp_sample",
    )(*([proj] * (2 * lt)), wrow, brow, ln_g.reshape(1, GM_W), ln_b.reshape(1, GM_W))


def _conv_silu(cur, prev8, w, bias):
    q = cur.shape[0]
    up = jnp.concatenate([prev8, cur], axis=0)
    acc = bias + up[SUBLANES:SUBLANES + q] * w[CONV_K - 1:CONV_K]
    for j in range(CONV_K - 1):
        off = SUBLANES - (CONV_K - 1) + j
        acc = acc + up[off:off + q] * w[j:j + 1]
    return _silu(acc)


def _ssd_p_kernel(z_ref, xs_ref, bc_ref, xsp_ref, bcp_ref, dt_ref, dtt_ref,
                  cwx_ref, cwbc_ref, cbx_ref, cbbc_ref, dtbe_ref, dtbc_ref, aloge_ref, alogc_ref,
                  dske_ref, ng_ref, e_ref, y_ref, st_ref, s_scr, *, n_chunks):
    c = pl.program_id(1)
    q = CHUNK
    rep = SSM_HEADS // SSM_GROUPS
    gw = rep * SSM_HEAD_DIM

    @pl.when(c == 0)
    def _():
        s_scr[...] = jnp.zeros_like(s_scr)

    has_prev = (c > 0).astype(F32)
    xs = _conv_silu(xs_ref[...], xsp_ref[...] * has_prev, cwx_ref[...], cbx_ref[...])
    bcm = _conv_silu(bc_ref[...], bcp_ref[...] * has_prev, cwbc_ref[...], cbbc_ref[...])

    ri = lax.broadcasted_iota(jnp.int32, (q, q), 0)
    ci = lax.broadcasted_iota(jnp.int32, (q, q), 1)
    tril = ri >= ci
    ones_tril = jnp.where(tril, 1.0, 0.0).astype(BF16)
    ones_triu = jnp.where(ri <= ci, 1.0, 0.0).astype(BF16)

    dt_e = _softplus(_exact_dot_left(_split3(dt_ref[...]), e_ref[...]) + dtbe_ref[...])
    a_e = dt_e * (-jnp.exp(aloge_ref[...]))
    a_hi, a_mid, a_lo = _split3(a_e)
    cum_e = _dot(ones_tril, a_hi) + _dot(ones_tril, a_mid) + _dot(ones_tril, a_lo)
    dt_t = _softplus(dtt_ref[...] + dtbc_ref[...])
    a_t = dt_t * (-jnp.exp(alogc_ref[...]))
    cum_t = _exact_dot_left(_split3(a_t), ones_triu)

    xdt = xs * dt_e
    cum_last = cum_e[q - 1:q, :]
    lane = lax.broadcasted_iota(jnp.int32, (q, LANES), 1)
    z = z_ref[...]
    ys = []
    for g in range(SSM_GROUPS):
        gs = slice(g * gw, (g + 1) * gw)
        bg = bcm[:, g * SSM_STATE:(g + 1) * SSM_STATE]
        cg = bcm[:, SSM_BC + g * SSM_STATE:SSM_BC + (g + 1) * SSM_STATE]
        bg16, cg16 = bg.astype(BF16), cg.astype(BF16)
        cb = _dot_nt(cg16, bg16)
        ydiag = []
        for pr in range(rep // 2):
            ms = []
            for hh in (2 * pr, 2 * pr + 1):
                h = g * rep + hh
                col = cum_e[:, h * SSM_HEAD_DIM:h * SSM_HEAD_DIM + 1]
                row = cum_t[h:h + 1, :]
                seg = jnp.where(tril, col - row, NEG_BIG)
                ms.append(cb * jnp.exp(seg))
            lhs = jnp.concatenate(ms, axis=1).astype(BF16)
            xslab = xdt[:, (g * rep + 2 * pr) * SSM_HEAD_DIM:(g * rep + 2 * pr + 2) * SSM_HEAD_DIM]
            xbd = jnp.concatenate([jnp.where(lane < SSM_HEAD_DIM, xslab, 0.0),
                                   jnp.where(lane >= SSM_HEAD_DIM, xslab, 0.0)], axis=0).astype(BF16)
            ydiag.append(_dot(lhs, xbd))
        ydiag = jnp.concatenate(ydiag, axis=1)
        s_old = s_scr[g]
        yoff = _dot(cg16, s_old.astype(BF16)) * jnp.exp(cum_e[:, gs])
        ys.append(ydiag + yoff)
        xw = xdt[:, gs] * jnp.exp(cum_last[:, gs] - cum_e[:, gs])
        s_scr[g] = s_old * jnp.exp(cum_last[:, gs]) + _dot(bg.T.astype(BF16), xw.astype(BF16))
    y = jnp.concatenate(ys, axis=1) + dske_ref[...] * xs
    gated = y * _silu(z)
    out = gated * lax.rsqrt(jnp.mean(gated * gated, axis=-1, keepdims=True) + LN_EPS) * ng_ref[...]
    y_ref[...] = out.astype(y_ref.dtype)

    @pl.when(c == n_chunks - 1)
    def _():
        for g in range(SSM_GROUPS):
            st_ref[0, g * gw:(g + 1) * gw, :] = s_scr[g].T


def _expand_heads(v):
    return jnp.repeat(v.astype(F32), SSM_HEAD_DIM).reshape(1, SSM_INNER)


def _head_expand_matrix():
    e = np.zeros((LANES, SSM_INNER), np.float32)
    for h in range(SSM_HEADS):
        e[h, h * SSM_HEAD_DIM:(h + 1) * SSM_HEAD_DIM] = 1.0
    return jnp.asarray(e, BF16)


def ssd_prompt(proj, dtt, n_seq, seq, conv_w, conv_b, dt_bias, a_log, d_skip, norm_g):
    b = n_seq
    nc = seq // CHUNK
    blk8 = CHUNK // SUBLANES
    const2 = lambda i, c: (0, 0)
    col_bc = COL_BC // (2 * SSM_BC)
    col_xs = COL_XS // SSM_INNER
    col_z = COL_Z // SSM_INNER
    prev_map_x = lambda i, c: (jnp.maximum((i * nc + c) * blk8 - 1, 0), col_xs)
    prev_map_bc = lambda i, c: (jnp.maximum((i * nc + c) * blk8 - 1, 0), col_bc)
    args = (
        proj, proj, proj, proj, proj, proj, dtt,
        conv_w[:, :SSM_INNER], conv_w[:, SSM_INNER:], conv_b[:SSM_INNER].reshape(1, -1),
        conv_b[SSM_INNER:].reshape(1, -1),
        _expand_heads(dt_bias), jnp.broadcast_to(dt_bias.astype(F32)[:, None], (SSM_HEADS, CHUNK)),
        _expand_heads(a_log), jnp.broadcast_to(a_log.astype(F32)[:, None], (SSM_HEADS, CHUNK)),
        _expand_heads(d_skip), norm_g.reshape(1, SSM_INNER), _head_expand_matrix(),
    )
    in_specs = [
        pl.BlockSpec((CHUNK, SSM_INNER), lambda i, c: (i * nc + c, col_z)),
        pl.BlockSpec((CHUNK, SSM_INNER), lambda i, c: (i * nc + c, col_xs)),
        pl.BlockSpec((CHUNK, 2 * SSM_BC), lambda i, c: (i * nc + c, col_bc)),
        pl.BlockSpec((SUBLANES, SSM_INNER), prev_map_x),
        pl.BlockSpec((SUBLANES, 2 * SSM_BC), prev_map_bc),
        pl.BlockSpec((CHUNK, LANES), lambda i, c: (i * nc + c, COL_DT // LANES)),
        pl.BlockSpec((SSM_HEADS, CHUNK), lambda i, c: (0, i * nc + c)),
        pl.BlockSpec((CONV_K, SSM_INNER), const2),
        pl.BlockSpec((CONV_K, 2 * SSM_BC), const2),
        pl.BlockSpec((1, SSM_INNER), const2),
        pl.BlockSpec((1, 2 * SSM_BC), const2),
        pl.BlockSpec((1, SSM_INNER), const2),
        pl.BlockSpec((SSM_HEADS, CHUNK), const2),
        pl.BlockSpec((1, SSM_INNER), const2),
        pl.BlockSpec((SSM_HEADS, CHUNK), const2),
        pl.BlockSpec((1, SSM_INNER), const2),
        pl.BlockSpec((1, SSM_INNER), const2),
        pl.BlockSpec((LANES, SSM_INNER), const2),
    ]
    return pl.pallas_call(
        functools.partial(_ssd_p_kernel, n_chunks=nc),
        grid=(b, nc),
        in_specs=in_specs,
        out_specs=[
            pl.BlockSpec((CHUNK, SSM_INNER), lambda i, c: (i * nc + c, 0)),
            pl.BlockSpec((1, SSM_INNER, SSM_STATE), lambda i, c: (i, 0, 0)),
        ],
        out_shape=[jax.ShapeDtypeStruct((b * seq, SSM_INNER), BF16),
                   jax.ShapeDtypeStruct((b, SSM_INNER, SSM_STATE), F32)],
        scratch_shapes=[pltpu.VMEM((SSM_GROUPS, SSM_STATE, SSM_INNER // SSM_GROUPS), F32)],
        compiler_params=_cparams(2),
        name="ssd_prompt",
    )(*args)


def _group_expand_matrix():
    gw = SSM_INNER // SSM_GROUPS
    m = np.zeros((SSM_BC, SSM_INNER), np.float32)
    for g in range(SSM_GROUPS):
        m[g * SSM_STATE:(g + 1) * SSM_STATE, g * gw:(g + 1) * gw] = 1.0
    return jnp.asarray(m, BF16)


def _ssd_s_pre_kernel(*refs, lt):
    xs_refs = refs[:lt]
    bc_refs = refs[lt:2 * lt]
    dt_refs = refs[2 * lt:3 * lt]
    (cx_ref, cbc_ref, cwx_ref, cwbc_ref, cbx_ref, cbbc_ref, dtbe_ref, aloge_ref, dske_ref, e_ref,
     gmat_ref, c_ref, b_ref, xw_ref, dec_ref, yd_ref, ec_ref) = refs[3 * lt:]
    nprev = CONV_K - 1
    ux = [cx_ref[j] for j in range(nprev)] + [r[...] for r in xs_refs]
    ub = [cbc_ref[j] for j in range(nprev)] + [r[...] for r in bc_refs]
    cwx, cwbc = cwx_ref[...], cwbc_ref[...]
    neg_a = -jnp.exp(aloge_ref[...])
    xs, bm, cm, xdt, cum = [], [], [], [], []
    run = None
    for t in range(lt):
        ax = cbx_ref[...]
        ab = cbbc_ref[...]
        for j in range(CONV_K):
            ax = ax + ux[t + j] * cwx[j:j + 1]
            ab = ab + ub[t + j] * cwbc[j:j + 1]
        x_t = _silu(ax)
        bc_t = _silu(ab)
        dt_e = _softplus(_exact_dot_left(_split3(dt_refs[t][...]), e_ref[...]) + dtbe_ref[...])
        a_t = dt_e * neg_a
        run = a_t if run is None else run + a_t
        xs.append(x_t)
        bm.append(bc_t[:, :SSM_BC])
        cm.append(bc_t[:, SSM_BC:])
        xdt.append(x_t * dt_e)
        cum.append(run)
    for i in range(lt):
        yd = dske_ref[...] * xs[i]
        for j in range(i + 1):
            hi, mid, _ = _split3(cm[i] * bm[j])
            cbe = _dot(hi, gmat_ref[...]) + _dot(mid, gmat_ref[...])
            yd = yd + cbe * jnp.exp(cum[i] - cum[j]) * xdt[j]
        yd_ref[i] = yd
        ec_ref[i] = jnp.exp(cum[i])
        c_ref[i] = cm[i]
        b_ref[i] = bm[i]
        xw_ref[i] = xdt[i] * jnp.exp(cum[lt - 1] - cum[i])
    dec_ref[...] = jnp.exp(cum[lt - 1])


def _rows_block(rows, total):
    c = rows[0].shape[1]
    rid = lax.broadcasted_iota(jnp.int32, (SUBLANES, c), 0)
    acc = jnp.zeros((SUBLANES, c), F32)
    for j, r in enumerate(rows):
        acc = jnp.where(rid == j, jnp.broadcast_to(r, (SUBLANES, c)), acc)
    if total == SUBLANES:
        return acc
    return jnp.concatenate([acc, jnp.zeros((total - SUBLANES, c), F32)], axis=0)


def _ssd_s_state_kernel_inplace(c_ref, b_ref, xw_ref, dec_ref, h0_ref, prev_ref, hn_ref, yr_ref, *, lt, bseq):
    del prev_ref
    _ssd_s_state_kernel(c_ref, b_ref, xw_ref, dec_ref, h0_ref, hn_ref, yr_ref, lt=lt, bseq=bseq)


def _ssd_s_state_kernel(c_ref, b_ref, xw_ref, dec_ref, h0_ref, hn_ref, yr_ref, *, lt, bseq, slot=0,
                        fill_slots=None):
    for bb in range(bseq):
        _ssd_s_state_one(c_ref, b_ref, xw_ref, dec_ref, h0_ref, hn_ref, yr_ref, pl.program_id(0) * bseq + bb,
                         bb, lt, slot, fill_slots)
    if fill_slots is not None:
        for other in fill_slots:
            hn_ref[other] = jnp.zeros(hn_ref.shape[1:], hn_ref.dtype)


def _ssd_s_state_one(c_ref, b_ref, xw_ref, dec_ref, h0_ref, hn_ref, yr_ref, b, bb, lt, slot, fill_slots):
    gw = SSM_INNER // SSM_GROUPS
    c8 = _rows_block([c_ref[i, pl.ds(b, 1), :] for i in range(lt)], SUBLANES).astype(BF16)
    b128 = _rows_block([b_ref[i, pl.ds(b, 1), :] for i in range(lt)], LANES).astype(BF16)
    xaug = _rows_block([xw_ref[i, pl.ds(b, 1), :] for i in range(lt)] + [dec_ref[pl.ds(b, 1), :]], LANES)
    for g in range(SSM_GROUPS):
        hg = h0_ref[bb, g * gw:(g + 1) * gw, :]
        yraw = _dot_nt(c8[:, g * SSM_STATE:(g + 1) * SSM_STATE], hg.astype(BF16))
        for i in range(lt):
            yr_ref[i, pl.ds(b, 1), g * gw:(g + 1) * gw] = yraw[i:i + 1, :]
        tr = xaug[:, g * gw:(g + 1) * gw].T
        s = _dot(tr.astype(BF16), b128[:, g * SSM_STATE:(g + 1) * SSM_STATE])
        new = hg * tr[:, lt:lt + 1] + s
        if fill_slots is None:
            hn_ref[bb, g * gw:(g + 1) * gw, :] = new
        else:
            hn_ref[slot, bb, g * gw:(g + 1) * gw, :] = new


def _ssd_s_post_kernel(*refs, lt):
    z_refs = refs[:lt]
    yd_ref, ec_ref, yr_ref, ng_ref, o_ref = refs[lt:]
    nb = z_refs[0].shape[0]
    for i in range(lt):
        y = yd_ref[i] + ec_ref[i] * yr_ref[i]
        gated = y * _silu(z_refs[i][...])
        out = gated * lax.rsqrt(jnp.mean(gated * gated, axis=-1, keepdims=True) + LN_EPS) * ng_ref[...]
        o_ref[i * nb:(i + 1) * nb, :] = out.astype(o_ref.dtype)


def ssd_sample(proj, tp, nb, lt, conv_state, state_all, new_states, layer, conv_w, conv_b, dt_bias,
               a_log, d_skip, norm_g):
    row0 = tp // nb
    cs = jnp.transpose(conv_state, (1, 0, 2))
    one = lambda i: (0, 0)
    one3 = lambda i: (0, 0, 0)

    def rows(t, width, col):
        return pl.BlockSpec((nb, width), lambda i: (row0 + t, col // width))

    in_specs = ([rows(t, SSM_INNER, COL_XS) for t in range(lt)]
                + [rows(t, 2 * SSM_BC, COL_BC) for t in range(lt)]
                + [pl.BlockSpec((nb, LANES), lambda i, t=t: (row0 + t, COL_DT // LANES)) for t in range(lt)]
                + [pl.BlockSpec((CONV_K - 1, nb, SSM_INNER), one3),
                   pl.BlockSpec((CONV_K - 1, nb, 2 * SSM_BC), one3),
                   pl.BlockSpec((CONV_K, SSM_INNER), one),
                   pl.BlockSpec((CONV_K, 2 * SSM_BC), one),
                   pl.BlockSpec((1, SSM_INNER), one),
                   pl.BlockSpec((1, 2 * SSM_BC), one),
                   pl.BlockSpec((1, SSM_INNER), one),
                   pl.BlockSpec((1, SSM_INNER), one),
                   pl.BlockSpec((1, SSM_INNER), one),
                   pl.BlockSpec((LANES, SSM_INNER), one),
                   pl.BlockSpec((SSM_BC, SSM_INNER), one)])
    f3 = lambda w: jax.ShapeDtypeStruct((lt, nb, w), F32)
    c_a, b_a, xw_a, dec_a, yd_a, ec_a = pl.pallas_call(
        functools.partial(_ssd_s_pre_kernel, lt=lt),
        grid=(1,),
        in_specs=in_specs,
        out_specs=[pl.BlockSpec((lt, nb, SSM_BC), one3), pl.BlockSpec((lt, nb, SSM_BC), one3),
                   pl.BlockSpec((lt, nb, SSM_INNER), one3), pl.BlockSpec((nb, SSM_INNER), one),
                   pl.BlockSpec((lt, nb, SSM_INNER), one3), pl.BlockSpec((lt, nb, SSM_INNER), one3)],
        out_shape=[f3(SSM_BC), f3(SSM_BC), f3(SSM_INNER), jax.ShapeDtypeStruct((nb, SSM_INNER), F32),
                   f3(SSM_INNER), f3(SSM_INNER)],
        compiler_params=_cparams(1),
        name="ssd_sample_pre",
    )(*([proj] * (3 * lt)), cs[:, :, :SSM_INNER], cs[:, :, SSM_INNER:],
      conv_w[:, :SSM_INNER], conv_w[:, SSM_INNER:], conv_b[:SSM_INNER].reshape(1, -1),
      conv_b[SSM_INNER:].reshape(1, -1), _expand_heads(dt_bias), _expand_heads(a_log),
      _expand_heads(d_skip), _head_expand_matrix(), _group_expand_matrix())

    depth = state_all.shape[0]
    bseq = _pick(nb, (4, 2, 1))
    h0r = state_all.reshape(depth, nb, SSM_INNER, SSM_STATE)
    state_specs = [pl.BlockSpec((lt, nb, SSM_BC), one3), pl.BlockSpec((lt, nb, SSM_BC), one3),
                   pl.BlockSpec((lt, nb, SSM_INNER), one3), pl.BlockSpec((nb, SSM_INNER), one),
                   pl.BlockSpec((None, bseq, SSM_INNER, SSM_STATE), lambda i: (layer, i, 0, 0))]
    hn_shape = jax.ShapeDtypeStruct((depth, nb, SSM_INNER, SSM_STATE), F32)
    if new_states is None:
        fill = tuple(s for s in range(depth) if s != layer)
        hn, yr = pl.pallas_call(
            functools.partial(_ssd_s_state_kernel, lt=lt, bseq=bseq, slot=layer, fill_slots=fill),
            grid=(nb // bseq,),
            in_specs=state_specs,
            out_specs=[pl.BlockSpec((depth, bseq, SSM_INNER, SSM_STATE), lambda i: (0, i, 0, 0)),
                       pl.BlockSpec((lt, nb, SSM_INNER), one3)],
            out_shape=[hn_shape, f3(SSM_INNER)],
            compiler_params=_cparams(1),
            name="ssd_sample_state",
        )(c_a, b_a, xw_a, dec_a, h0r)
    else:
        hn, yr = pl.pallas_call(
            functools.partial(_ssd_s_state_kernel_inplace, lt=lt, bseq=bseq),
            grid=(nb // bseq,),
            in_specs=state_specs + [pl.BlockSpec(memory_space=pl.ANY)],
            out_specs=[pl.BlockSpec((None, bseq, SSM_INNER, SSM_STATE), lambda i: (layer, i, 0, 0)),
                       pl.BlockSpec((lt, nb, SSM_INNER), one3)],
            out_shape=[hn_shape, f3(SSM_INNER)],
            input_output_aliases={5: 0},
            compiler_params=_cparams(1),
            name="ssd_sample_state",
        )(c_a, b_a, xw_a, dec_a, h0r, new_states)

    ssm = pl.pallas_call(
        functools.partial(_ssd_s_post_kernel, lt=lt),
        grid=(1,),
        in_specs=([rows(t, SSM_INNER, COL_Z) for t in range(lt)]
                  + [pl.BlockSpec((lt, nb, SSM_INNER), one3)] * 3 + [pl.BlockSpec((1, SSM_INNER), one)]),
        out_specs=pl.BlockSpec((lt * nb, SSM_INNER), one),
        out_shape=jax.ShapeDtypeStruct((lt * nb, SSM_INNER), BF16),
        compiler_params=_cparams(1),
        name="ssd_sample_post",
    )(*([proj] * lt), yd_a, ec_a, yr, norm_g.reshape(1, SSM_INNER))
    return ssm, hn


def _softmax_rows(s):
    p = jnp.exp(s - jnp.max(s, axis=-1, keepdims=True))
    return p * (1.0 / jnp.sum(p, axis=-1, keepdims=True))


def _xattn_kernel(q_ref, k_ref, v_ref, o_ref, *, nh, bseq, tq):
    scale = MEM_HEAD_DIM ** -0.5
    q = q_ref[...].reshape(bseq, tq, q_ref.shape[1])
    for h in range(nh):
        sl = slice(h * MEM_HEAD_DIM, (h + 1) * MEM_HEAD_DIM)
        s = lax.dot_general(q[:, :, sl].astype(BF16), k_ref[:, :, h, :].astype(BF16),
                            (((2,), (2,)), ((0,), (0,))), preferred_element_type=F32) * scale
        o = lax.dot_general(_softmax_rows(s).astype(BF16), v_ref[:, :, h, :].astype(BF16),
                            (((2,), (1,)), ((0,), (0,))), preferred_element_type=F32)
        o_ref[:, sl] = o.reshape(bseq * tq, MEM_HEAD_DIM).astype(o_ref.dtype)


def cross_attention(q, k, v, *, layer, n_seq, seq, tq, bseq, name):
    w = q.shape[1]
    _, _, m, nh, dh = k.shape
    nq = seq // tq
    assert bseq == 1 or nq == 1
    kv_spec = pl.BlockSpec((None, bseq, m, nh, dh), lambda i, n: (layer, i, 0, 0, 0))
    return pl.pallas_call(
        functools.partial(_xattn_kernel, nh=nh, bseq=bseq, tq=tq),
        grid=(n_seq // bseq, nq),
        in_specs=[pl.BlockSpec((bseq * tq, w), lambda i, n: (i * nq + n, 0)), kv_spec, kv_spec],
        out_specs=pl.BlockSpec((bseq * tq, w), lambda i, n: (i * nq + n, 0)),
        out_shape=jax.ShapeDtypeStruct((n_seq * seq, w), F32),
        compiler_params=_cparams(2),
        name=name,
    )(q, k, v)


def _router_kernel(h_ref, w_ref, b_ref, o_ref):
    logits = _dot(h_ref[...].astype(BF16), w_ref[...].astype(BF16)) + b_ref[...]
    lane = lax.broadcasted_iota(jnp.int32, logits.shape, 1)
    lane_f = lane.astype(F32)
    big = float(LANES)
    is_g = lane < N_EGROUPS
    lg = jnp.where(is_g, logits, NEG_BIG)
    mg = jnp.max(lg, axis=-1, keepdims=True)
    zg = jnp.sum(jnp.where(is_g, jnp.exp(lg - mg), 0.0), axis=-1, keepdims=True)
    gi = jnp.min(jnp.where(is_g & (lg == mg), lane_f, big), axis=-1, keepdims=True)
    gw = 1.0 / zg
    lo = N_EGROUPS + gi * EXPERTS_PER_GROUP
    is_e = (lane_f >= lo) & (lane_f < lo + EXPERTS_PER_GROUP)
    le = jnp.where(is_e, logits, NEG_BIG)
    me = jnp.max(le, axis=-1, keepdims=True)
    ee = jnp.where(is_e, jnp.exp(le - me), 0.0)
    pe = ee / jnp.sum(ee, axis=-1, keepdims=True)
    pe = jnp.where(is_e, pe, -1.0)
    p1 = jnp.max(pe, axis=-1, keepdims=True)
    i1 = jnp.min(jnp.where(pe == p1, lane_f, big), axis=-1, keepdims=True)
    pe2 = jnp.where(lane_f == i1, -1.0, pe)
    p2 = jnp.max(pe2, axis=-1, keepdims=True)
    i2 = jnp.min(jnp.where(pe2 == p2, lane_f, big), axis=-1, keepdims=True)
    tot = p1 + p2
    out = jnp.where(lane == 0, i1 - N_EGROUPS,
                    jnp.where(lane == 1, i2 - N_EGROUPS,
                              jnp.where(lane == 2, gw * (p1 / tot),
                                        jnp.where(lane == 3, gw * (p2 / tot), 0.0))))
    o_ref[...] = out


def moe_router(h, w_rg, b_rg, w_re, b_re):
    t, d = h.shape
    tm = _pick(t, (256, 128, 64, 32, 16, 8))
    npad = LANES - N_EGROUPS - N_EXPERTS
    w = jnp.concatenate([w_rg, w_re, jnp.zeros((d, npad), F32)], axis=1)
    b = jnp.concatenate([b_rg, b_re, jnp.zeros((npad,), F32)]).reshape(1, LANES)
    return pl.pallas_call(
        _router_kernel,
        grid=(t // tm,),
        in_specs=[pl.BlockSpec((tm, d), lambda i: (i, 0)),
                  pl.BlockSpec((d, LANES), lambda i: (0, 0)),
                  pl.BlockSpec((1, LANES), lambda i: (0, 0))],
        out_specs=pl.BlockSpec((tm, LANES), lambda i: (i, 0)),
        out_shape=jax.ShapeDtypeStruct((t, LANES), F32),
        compiler_params=_cparams(1),
        name="moe_router",
    )(h, w, b)


def _row_copy(src_hbm, dst, src_row, dst_row, sem):
    return pltpu.make_async_copy(src_hbm.at[pl.ds(src_row, 1)], dst.at[pl.ds(dst_row, 1)], sem)


def _moe_gather_kernel(nrow_ref, tok_ref, h_ref, o_ref, buf, sem, *, tm):
    i = pl.program_id(0)
    n_groups = (nrow_ref[i] + DMA_LOOP_UNROLL - 1) // DMA_LOOP_UNROLL

    @pl.when(i == 0)
    def _():
        buf[...] = jnp.zeros_like(buf)

    def start(gidx, carry):
        for u in range(DMA_LOOP_UNROLL):
            r = gidx * DMA_LOOP_UNROLL + u
            _row_copy(h_ref, buf, tok_ref[0, 0, r], r, sem).start(priority=u % 2)
        return carry

    def wait(gidx, carry):
        for u in range(DMA_LOOP_UNROLL):
            r = gidx * DMA_LOOP_UNROLL + u
            _row_copy(h_ref, buf, tok_ref[0, 0, r], r, sem).wait()
        return carry

    lax.fori_loop(0, n_groups, start, 0)
    lax.fori_loop(0, n_groups, wait, 0)
    o_ref[...] = buf[...].astype(o_ref.dtype)


def moe_gather(h, row_token, tile_rows, tm):
    r_total = row_token.shape[0]
    d = h.shape[1]
    nblk = r_total // tm
    assert tm % DMA_LOOP_UNROLL == 0
    return pl.pallas_call(
        functools.partial(_moe_gather_kernel, tm=tm),
        grid=(nblk,),
        in_specs=[pl.BlockSpec(memory_space=pltpu.SMEM),
                  pl.BlockSpec((1, 1, tm), lambda i: (i, 0, 0), memory_space=pltpu.SMEM),
                  pl.BlockSpec(memory_space=pl.ANY)],
        out_specs=pl.BlockSpec((tm, d), lambda i: (i, 0)),
        out_shape=jax.ShapeDtypeStruct((r_total, d), BF16),
        scratch_shapes=[pltpu.VMEM((tm, d), h.dtype), pltpu.SemaphoreType.DMA(())],
        compiler_params=_cparams(1),
        name="moe_gather",
    )(tile_rows, row_token.reshape(nblk, 1, tm), h)


def _expert_weight_copies(w_refs, bufs, sems, layer, expert, slot):
    return [pltpu.make_async_copy(w.at[layer, expert], buf.at[slot], sems.at[k, slot])
            for k, (w, buf) in enumerate(zip(w_refs, bufs))]


def _expert_weights_step(s, tv_ref, te_ref, tn_ref, ts_ref, w_refs, bufs, w16s, sems, layer):
    @pl.when(s == 0)
    def _():
        for c in _expert_weight_copies(w_refs, bufs, sems, layer, te_ref[0], 0):
            c.start()

    @pl.when(tv_ref[s] == 2)
    def _():
        slot = ts_ref[s]
        for c in _expert_weight_copies(w_refs, bufs, sems, layer, te_ref[s], slot):
            c.wait()

        @pl.when(tn_ref[s] >= 0)
        def _():
            for c in _expert_weight_copies(w_refs, bufs, sems, layer, tn_ref[s], 1 - slot):
                c.start()

        for buf, w16 in zip(bufs, w16s):
            w16[...] = buf[slot].astype(BF16)


def _moe_up_kernel(tv_ref, tc_ref, te_ref, tn_ref, ts_ref, x_ref, wg_ref, wu_ref, o_ref,
                   gbuf, ubuf, wg16, wu16, sems, *, layer):
    s = pl.program_id(0)
    _expert_weights_step(s, tv_ref, te_ref, tn_ref, ts_ref, (wg_ref, wu_ref), (gbuf, ubuf), (wg16, wu16),
                         sems, layer)

    @pl.when(tv_ref[s] > 0)
    def _():
        x = x_ref[...]
        a = _dot(x, wg16[...])
        u = _dot(x, wu16[...])
        o_ref[...] = (_silu(a) * u).astype(o_ref.dtype)

    @pl.when(tv_ref[s] == 0)
    def _():
        o_ref[...] = jnp.zeros_like(o_ref)


def _moe_tables(plan):
    return plan["tile_v"], plan["tile_c"], plan["tile_e"], plan["tile_next"], plan["tile_slot"]


def moe_up(x_sorted, w_gate, w_up, layer, plan, tm):
    r_total, d = x_sorted.shape
    ff = w_gate.shape[-1]
    n_tiles = r_total // tm
    grid_spec = pltpu.PrefetchScalarGridSpec(
        num_scalar_prefetch=5,
        grid=(n_tiles,),
        in_specs=[
            pl.BlockSpec((tm, d), lambda s, tv, tc, te, tn, ts: (tc[s], 0)),
            pl.BlockSpec(memory_space=pl.ANY),
            pl.BlockSpec(memory_space=pl.ANY),
        ],
        out_specs=pl.BlockSpec((tm, ff), lambda s, tv, tc, te, tn, ts: (s, 0)),
        scratch_shapes=[pltpu.VMEM((2, d, ff), F32), pltpu.VMEM((2, d, ff), F32),
                        pltpu.VMEM((d, ff), BF16), pltpu.VMEM((d, ff), BF16),
                        pltpu.SemaphoreType.DMA((2, 2))],
    )
    return pl.pallas_call(
        functools.partial(_moe_up_kernel, layer=layer),
        grid_spec=grid_spec,
        out_shape=jax.ShapeDtypeStruct((r_total, ff), BF16),
        compiler_params=_cparams(1),
        name="moe_up",
    )(*_moe_tables(plan), x_sorted, w_gate, w_up)


def _moe_down_kernel(tv_ref, tc_ref, te_ref, tn_ref, ts_ref, x_ref, w_ref, o_ref, wbuf, w16, sems, *, layer):
    s = pl.program_id(0)
    _expert_weights_step(s, tv_ref, te_ref, tn_ref, ts_ref, (w_ref,), (wbuf,), (w16,), sems, layer)

    @pl.when(tv_ref[s] > 0)
    def _():
        o_ref[...] = _dot(x_ref[...], w16[...])

    @pl.when(tv_ref[s] == 0)
    def _():
        o_ref[...] = jnp.zeros_like(o_ref)


def moe_down(hid, w_down, layer, plan, tm):
    r_total, ff = hid.shape
    d = w_down.shape[-1]
    n_tiles = r_total // tm
    grid_spec = pltpu.PrefetchScalarGridSpec(
        num_scalar_prefetch=5,
        grid=(n_tiles,),
        in_specs=[
            pl.BlockSpec((tm, ff), lambda s, tv, tc, te, tn, ts: (tc[s], 0)),
            pl.BlockSpec(memory_space=pl.ANY),
        ],
        out_specs=pl.BlockSpec((tm, d), lambda s, tv, tc, te, tn, ts: (s, 0)),
        scratch_shapes=[pltpu.VMEM((2, ff, d), F32), pltpu.VMEM((ff, d), BF16),
                        pltpu.SemaphoreType.DMA((1, 2))],
    )
    return pl.pallas_call(
        functools.partial(_moe_down_kernel, layer=layer),
        grid_spec=grid_spec,
        out_shape=jax.ShapeDtypeStruct((r_total, d), F32),
        compiler_params=_cparams(1),
        name="moe_down",
    )(*_moe_tables(plan), hid, w_down)


def _moe_combine_kernel(pos_ref, y_ref, r_ref, h_ref, g_ref, b_ref, o1_ref, o2_ref, ybuf, sem, *,
                        tm, alpha, n_first):
    def start(i, carry):
        _row_copy(y_ref, ybuf.at[0], pos_ref[0, 0, 2 * i], i, sem).start(priority=0)
        _row_copy(y_ref, ybuf.at[1], pos_ref[0, 0, 2 * i + 1], i, sem).start(priority=1)
        return carry

    def wait(i, carry):
        _row_copy(y_ref, ybuf.at[0], pos_ref[0, 0, 2 * i], i, sem).wait()
        _row_copy(y_ref, ybuf.at[1], pos_ref[0, 0, 2 * i + 1], i, sem).wait()
        return carry

    lax.fori_loop(0, tm, start, 0, unroll=DMA_LOOP_UNROLL)
    lax.fori_loop(0, tm, wait, 0, unroll=DMA_LOOP_UNROLL)
    route = r_ref[...]
    ff = ybuf[0] * route[:, 2:3] + ybuf[1] * route[:, 3:4]
    h = _ln_rows(alpha * h_ref[...] + ff, g_ref[...], b_ref[...])
    if n_first is None:
        o1_ref[...] = h
        o2_ref[...] = h.astype(BF16)
    else:
        i = pl.program_id(0)

        @pl.when(i < n_first)
        def _():
            o1_ref[...] = h

        @pl.when(i >= n_first)
        def _():
            o2_ref[...] = h


def moe_combine(y_sorted, pos, route, h, g, b, *, alpha, split_rows=None):
    t, d = h.shape
    tm = _pick(t if split_rows is None else math.gcd(split_rows, t - split_rows), (256, 128, 64, 32, 16, 8))
    nblk = t // tm
    if split_rows is None:
        n_first = None
        out_specs = [pl.BlockSpec((tm, d), lambda i: (i, 0)), pl.BlockSpec((tm, d), lambda i: (i, 0))]
        out_shape = [jax.ShapeDtypeStruct((t, d), F32), jax.ShapeDtypeStruct((t, d), BF16)]
    else:
        n_first = split_rows // tm
        out_specs = [pl.BlockSpec((tm, d), lambda i: (jnp.minimum(i, n_first - 1), 0)),
                     pl.BlockSpec((tm, d), lambda i: (jnp.maximum(i - n_first, 0), 0))]
        out_shape = [jax.ShapeDtypeStruct((split_rows, d), F32),
                     jax.ShapeDtypeStruct((t - split_rows, d), F32)]
    return pl.pallas_call(
        functools.partial(_moe_combine_kernel, tm=tm, alpha=alpha, n_first=n_first),
        grid=(nblk,),
        in_specs=[pl.BlockSpec((1, 1, 2 * tm), lambda i: (i, 0, 0), memory_space=pltpu.SMEM),
                  pl.BlockSpec(memory_space=pl.ANY),
                  pl.BlockSpec((tm, LANES), lambda i: (i, 0)),
                  pl.BlockSpec((tm, d), lambda i: (i, 0)),
                  pl.BlockSpec((1, d), lambda i: (0, 0)),
                  pl.BlockSpec((1, d), lambda i: (0, 0))],
        out_specs=out_specs,
        out_shape=out_shape,
        scratch_shapes=[pltpu.VMEM((2, tm, d), F32), pltpu.SemaphoreType.DMA(())],
        compiler_params=_cparams(1),
        name="moe_combine",
    )(pos.reshape(nblk, 1, 2 * tm), y_sorted, route, h, g.reshape(1, d), b.reshape(1, d))


def moe_plan(route, tm):
    t = route.shape[0]
    eid = route[:, :2].astype(jnp.int32).reshape(-1)
    onehot = (eid[:, None] == jnp.arange(N_EXPERTS, dtype=jnp.int32)[None, :]).astype(jnp.int32)
    csum = jnp.cumsum(onehot, axis=0)
    rank = jnp.sum((csum - onehot) * onehot, axis=1)
    counts = csum[-1]
    tiles_e = (counts + tm - 1) // tm
    tile_end = jnp.cumsum(tiles_e)
    tile_start = tile_end - tiles_e
    n_used = tile_end[-1]
    n_tiles = (2 * t + N_EXPERTS * (tm - 1)) // tm + 1
    r_total = n_tiles * tm
    dest = tile_start[eid] * tm + rank
    row_token = (jnp.arange(r_total, dtype=jnp.int32) % t).at[dest].set(
        jnp.arange(2 * t, dtype=jnp.int32) // 2)
    tile_ids = jnp.arange(n_tiles, dtype=jnp.int32)
    tile_clamped = jnp.minimum(tile_ids, n_used - 1)
    tile_e = jnp.sum(tile_end[None, :] <= tile_clamped[:, None], axis=1).astype(jnp.int32)
    tile_rows = jnp.clip(counts[tile_e] - (tile_ids - tile_start[tile_e]) * tm, 0, tm)
    tile_rows = jnp.where(tile_ids < n_used, tile_rows, 0).astype(jnp.int32)
    tile_first = tile_ids == tile_start[tile_e]
    tile_v = jnp.where(tile_ids < n_used, 1 + tile_first.astype(jnp.int32), 0).astype(jnp.int32)
    e_ids = jnp.arange(N_EXPERTS, dtype=jnp.int32)
    nonempty = tiles_e > 0
    cand = jnp.where(nonempty[None, :] & (e_ids[None, :] > e_ids[:, None]), e_ids[None, :], N_EXPERTS)
    next_e = jnp.min(cand, axis=1)
    next_e = jnp.where(next_e >= N_EXPERTS, -1, next_e).astype(jnp.int32)
    slot_e = ((jnp.cumsum(nonempty.astype(jnp.int32)) - 1) % 2).astype(jnp.int32)
    return dict(row_token=row_token, pos=dest.astype(jnp.int32), tile_rows=tile_rows, tile_v=tile_v,
                tile_c=tile_clamped,
                tile_e=tile_e, tile_next=next_e[tile_e], tile_slot=slot_e[tile_e])


def hierarchical_moe_ln(hf, layer, w_rg, b_rg, w_re, b_re, w_gate, w_up, w_down, ln_g, ln_b, *, alpha,
                        split_rows=None):
    route = moe_router(hf, w_rg, b_rg, w_re, b_re)
    plan = moe_plan(route, MOE_TM)
    x_sorted = moe_gather(hf, plan["row_token"], plan["tile_rows"], MOE_TM)
    hid = moe_up(x_sorted, w_gate, w_up, layer, plan, MOE_TM)
    y_sorted = moe_down(hid, w_down, layer, plan, MOE_TM)
    return moe_combine(y_sorted, plan["pos"], route, hf, ln_g, ln_b, alpha=alpha, split_rows=split_rows)


def _to_seq_major(x_tm, lt, nb, pad_to):
    w = x_tm.shape[1]
    x = jnp.transpose(x_tm.reshape(lt, nb, w), (1, 0, 2))
    x = jnp.pad(x, ((0, 0), (0, pad_to - lt), (0, 0)))
    return x.reshape(nb * pad_to, w)


def _to_time_major(x_sm, lt, nb, pad_to):
    w = x_sm.shape[1]
    x = x_sm.reshape(nb, pad_to, w)[:, :lt]
    return jnp.transpose(x, (1, 0, 2)).reshape(lt * nb, w)


def kernel(x_prompt, x_sample, mem_prompt, cache_swa_k, cache_swa_v, cache_mem_k, cache_mem_v, state_conv, state_ssm, ln_in_g, ln_in_b, w_in, attn_sinks, gm_ln_g, gm_ln_b, gm_ws, gm_bs, conv_w, conv_b, dt_bias, a_log, d_skip, ssm_norm_g, w_pa, w_pb, w_pc, w_o, ln1_g, ln1_b, w_cq, w_ck, w_cv, w_co, ln2_g, ln2_b, w_rg, b_rg, w_re, b_re, w_gate, w_up, w_down, ln3_g, ln3_b):
    bp, seq, d = x_prompt.shape
    nb, lt, _ = x_sample.shape
    depth = w_in.shape[0]
    mem_len = mem_prompt.shape[1]
    past_len = PAST_LEN
    wb = cache_swa_k.shape[2]
    assert wb == WINDOW and seq % CHUNK == 0 and lt <= SUBLANES
    tp, ts = bp * seq, nb * lt
    alpha = (2 * depth) ** 0.25
    qpad = SUBLANES

    xp = x_prompt.reshape(tp, d)
    xs = jnp.transpose(x_sample, (1, 0, 2)).reshape(ts, d)
    hf, hb = ln_in(xp, xs, ln_in_g, ln_in_b)
    cos_t, sin_t = rope_tables(tp, seq, ts, nb, past_len)
    mem_b = mem_prompt.reshape(bp * mem_len, d).astype(BF16)

    in_w = w_in.shape[2]
    assert in_w == Q_W + 2 * KV_W + 2 * GM_W + SSM_INNER + CONV_DIM + SSM_HEADS + 3 * D_MODEL
    n_whole = (in_w // IN_TN) * IN_TN
    w_t = jnp.swapaxes(w_in, 1, 2)
    w_tail = jnp.pad(w_t[:, n_whole:, :], ((0, 0), (0, IN_TN - (in_w - n_whole)), (0, 0)))

    outs = {k: [] for k in ("p_k", "p_v", "p_mk", "p_mv", "p_conv", "p_ssm", "p_gv",
                            "s_k", "s_v", "s_conv", "s_ssm", "s_gv")}
    n_qblk = seq // WINDOW
    s_states = None
    for l in range(depth):
        proj = in_projection(hb, w_t, w_tail, l)
        q_rot, k_rot = rope_qk(proj, cos_t, sin_t)

        kcol, vcol = 0, COL_V // KV_W
        att_p = swa_attention(
            attn_sinks[l], q_rot, k_rot, k_rot, proj, proj,
            n_seq=bp, n_blk=n_qblk, qb=WINDOW, prev_from_block0=False,
            kp_map=lambda i, n: (jnp.maximum(i * n_qblk + n - 1, 0), kcol),
            kc_map=lambda i, n: (i * n_qblk + n, kcol),
            vp_map=lambda i, n: (jnp.maximum(i * n_qblk + n - 1, 0), vcol),
            vc_map=lambda i, n: (i * n_qblk + n, vcol),
            out_dtype=BF16, name="swa_prompt")
        k_s_tm = k_rot[tp:]
        v_s_tm = proj[tp:, COL_V:COL_V + KV_W]
        q_s = _to_seq_major(q_rot[tp:], lt, nb, qpad)
        k_s = _to_seq_major(k_s_tm, lt, nb, qpad)
        v_s = _to_seq_major(v_s_tm, lt, nb, qpad)
        att_s8 = swa_attention_cached(attn_sinks[l], q_s, k_s, v_s, cache_swa_k, cache_swa_v, layer=l,
                                      n_seq=nb, qb=qpad, bseq=_pick(nb, (16, 8, 4, 2, 1)), name="swa_sample")
        att_s = _to_time_major(att_s8, lt, nb, qpad).astype(BF16)
        last_w = lambda a, c0: jnp.stack(
            [a[(i + 1) * seq - WINDOW:(i + 1) * seq, c0:c0 + KV_W] for i in range(bp)]
        ).reshape(bp, WINDOW, N_KV_HEADS, HEAD_DIM)
        outs["p_k"].append(last_w(k_rot, 0))
        outs["p_v"].append(last_w(proj, COL_V))
        k_new = jnp.transpose(k_s_tm.reshape(lt, nb, N_KV_HEADS, HEAD_DIM), (1, 0, 2, 3))
        v_new = jnp.transpose(v_s_tm.reshape(lt, nb, N_KV_HEADS, HEAD_DIM), (1, 0, 2, 3))
        outs["s_k"].append(jnp.concatenate([cache_swa_k[l], k_new], axis=1)[:, -wb:])
        outs["s_v"].append(jnp.concatenate([cache_swa_v[l], v_new], axis=1)[:, -wb:])

        gm_p, vg_last = gmlp_prompt(proj, bp, seq, gm_ws[l], gm_bs[l], gm_ln_g[l], gm_ln_b[l])
        gm_s, vg_s = gmlp_sample(proj, tp, nb, lt, gm_ws[l], gm_bs[l], gm_ln_g[l], gm_ln_b[l])
        outs["p_gv"].append(vg_last.reshape(bp, CHUNK, GM_GROUPS, GM_GROUP_DIM))
        outs["s_gv"].append(jnp.transpose(vg_s.reshape(lt, nb, GM_GROUPS, GM_GROUP_DIM), (1, 0, 2, 3)))

        dtt = jnp.transpose(proj[:tp, COL_DT:COL_DT + SSM_HEADS])
        y_p, st_p = ssd_prompt(proj, dtt, bp, seq, conv_w[l], conv_b[l], dt_bias[l], a_log[l],
                               d_skip[l], ssm_norm_g[l])
        ssm_s, s_states = ssd_sample(proj, tp, nb, lt, state_conv[l], state_ssm, s_states, l, conv_w[l],
                                     conv_b[l], dt_bias[l], a_log[l], d_skip[l], ssm_norm_g[l])
        outs["p_conv"].append(jnp.stack(
            [proj[(i + 1) * seq - (CONV_K - 1):(i + 1) * seq, COL_XS:COL_XS + CONV_DIM] for i in range(bp)]))
        xbc_s = jnp.transpose(proj[tp:, COL_XS:COL_XS + CONV_DIM].reshape(lt, nb, CONV_DIM), (1, 0, 2))
        outs["s_conv"].append(jnp.concatenate([state_conv[l], xbc_s], axis=1)[:, -(CONV_K - 1):])
        outs["p_ssm"].append(st_p.reshape(bp, SSM_HEADS, SSM_HEAD_DIM, SSM_STATE))

        merged = gated_merge(att_p, att_s, gm_p, gm_s, y_p, ssm_s, w_pa[l].astype(BF16),
                             w_pb[l].astype(BF16), w_pc[l].astype(BF16), proj)
        h1f, h1b = matmul_ln(merged, w_o[l].astype(BF16), hf, ln1_g[l], ln1_b[l], alpha=alpha, name="out_proj_ln1")

        qc = matmul(h1b, w_cq[l].astype(BF16), name="xattn_q")
        pmk = matmul(mem_b, w_ck[l].astype(BF16), name="mem_k")
        pmv = matmul(mem_b, w_cv[l].astype(BF16), name="mem_v")
        pmk5 = pmk.reshape(1, bp, mem_len, MEM_HEADS, MEM_HEAD_DIM)
        pmv5 = pmv.reshape(1, bp, mem_len, MEM_HEADS, MEM_HEAD_DIM)
        outs["p_mk"].append(pmk5[0])
        outs["p_mv"].append(pmv5[0])
        tq = _pick(seq, (512, 256, 128))
        o_p = cross_attention(qc, pmk5, pmv5, layer=0, n_seq=bp, seq=seq, tq=tq, bseq=1,
                              name="xattn_prompt")
        qc_s = _to_seq_major(qc[tp:], lt, nb, qpad)
        o_s8 = cross_attention(qc_s, cache_mem_k, cache_mem_v, layer=l, n_seq=nb, seq=qpad, tq=qpad,
                               bseq=_pick(nb, (8, 4, 2, 1)), name="xattn_sample")
        o_all = jnp.concatenate([o_p, _to_time_major(o_s8, lt, nb, qpad)], axis=0).astype(BF16)
        h2f, h2b = matmul_ln(o_all, w_co[l].astype(BF16), h1f, ln2_g[l], ln2_b[l], alpha=alpha, name="xattn_out_ln2")

        hf, hb = hierarchical_moe_ln(h2f, l, w_rg[l], b_rg[l], w_re[l], b_re[l], w_gate, w_up, w_down,
                                     ln3_g[l], ln3_b[l], alpha=alpha,
                                     split_rows=tp if l == depth - 1 else None)

    y_prompt = hf.reshape(bp, seq, d)
    y_sample = jnp.transpose(hb.reshape(lt, nb, d), (1, 0, 2))
    st = lambda k: jnp.stack(outs[k])
    s_ssm = s_states.reshape(depth, nb, SSM_HEADS, SSM_HEAD_DIM, SSM_STATE)
    return (y_prompt, y_sample, st("p_k"), st("p_v"), st("p_mk"), st("p_mv"), st("p_conv"), st("p_ssm"),
            st("p_gv"), st("s_k"), st("s_v"), st("s_conv"), s_ssm, st("s_gv"))
```
